```python
import math
import jax, jax.numpy as jnp
from jax import lax
import numpy as np

D_MODEL = 1024
BATCH = 16
SEQ = 256
DEPTH = 4
DEC_BATCH = 8
DEC_SEQ = 1024
PAST_LEN = 512

GRID_W = 64
N_EVEN = (DEPTH + 1) // 2
N_ODD = DEPTH // 2
S5_WIDTH = D_MODEL // 2
S5_GROUP = 16
S5_GROUPS = S5_WIDTH // S5_GROUP
S5_STATE = 64
MLA_HEADS = 8
MLA_NOPE = 64
MLA_ROPE = 32
MLA_V = 64
MLA_Q_LORA = D_MODEL // 4
MLA_KV_LORA = D_MODEL // 8
MLA_WIDTH = MLA_HEADS * MLA_V
EVEN_IN = S5_WIDTH + MLA_Q_LORA + MLA_KV_LORA + MLA_ROPE
EVEN_OUT = S5_WIDTH + MLA_WIDTH
DIFF_HEADS = 8
DIFF_HD = D_MODEL // (2 * DIFF_HEADS)
DIFF_WIDTH = DIFF_HEADS * 2 * DIFF_HD
N_EXPERTS = 32
TOP_K = 4
D_FF = D_MODEL
SWIGLU_LIMIT = 7.0
SWIGLU_ALPHA = 1.702
ROPE_THETA = 10000.0
Q_BLOCK = 128
EPS = 1e-6

kernel_name = "hybrid_s5_mla_diffattn_moe_diffusion_step"


def rms_norm(x, g):
    xf = x.astype(jnp.float32)
    y = xf * lax.rsqrt(jnp.mean(xf * xf, axis=-1, keepdims=True) + EPS)
    return (y * g.astype(jnp.float32)).astype(x.dtype)


def modulate(h, shift, scale):
    return h * (1.0 + scale[:, None]) + shift[:, None]


def axial_rope(length, dim):
    rows = length // GRID_W
    row = jnp.repeat(jnp.arange(rows, dtype=jnp.float32), GRID_W)
    col = jnp.tile(jnp.arange(GRID_W, dtype=jnp.float32), rows)
    n_freq = dim // 4
    inv = ROPE_THETA ** (-jnp.arange(n_freq, dtype=jnp.float32) / n_freq)
    ang = jnp.concatenate([row[:, None] * inv, col[:, None] * inv], axis=-1)
    return jnp.cos(ang), jnp.sin(ang)


def apply_rope(x, cos, sin):
    shape = (cos.shape[0],) + (1,) * (x.ndim - 3) + (cos.shape[1],)
    cos = cos.reshape(shape).astype(x.dtype)
    sin = sin.reshape(shape).astype(x.dtype)
    xr = x.reshape(x.shape[:-1] + (x.shape[-1] // 2, 2))
    x1, x2 = xr[..., 0], xr[..., 1]
    return jnp.stack([x1 * cos - x2 * sin, x1 * sin + x2 * cos], axis=-1).reshape(x.shape)


def over_query_blocks(fn, q):
    b, lq = q.shape[:2]
    nb = lq // Q_BLOCK
    qb = jnp.moveaxis(q.reshape((b, nb, Q_BLOCK) + q.shape[2:]), 1, 0)
    out = lax.map(fn, qb)
    return jnp.moveaxis(out, 0, 1).reshape((b, lq) + out.shape[3:])


def softmax_attention(q, k, v, scale):
    def block(qb):
        s = jnp.einsum('bqhd,bkhd->bhqk', qb, k).astype(jnp.float32) * scale
        p = jax.nn.softmax(s, axis=-1).astype(v.dtype)
        return jnp.einsum('bhqk,bkhd->bqhd', p, v)
    return over_query_blocks(block, q)


def _complex_linear_step(e1, e2):
    a1r, a1i, b1r, b1i = e1
    a2r, a2i, b2r, b2i = e2
    return (a2r * a1r - a2i * a1i, a2r * a1i + a2i * a1r,
            a2r * b1r - a2i * b1i + b2r, a2r * b1i + a2i * b1r + b2i)


def s5_mixer(u, lam_re, lam_im, log_dt, b_re, b_im, c_re, c_im, d_skip, w_glu, b_glu, h0=None):
    bsz, length, _ = u.shape
    ug = u.reshape(bsz, length, S5_GROUPS, S5_GROUP).astype(jnp.float32)
    y = d_skip.astype(jnp.float32) * ug
    finals = []
    for dr in range(2):
        lr = jnp.minimum(lam_re[dr].astype(jnp.float32), -1e-4)
        li = lam_im[dr].astype(jnp.float32)
        dt = jnp.exp(log_dt[dr].astype(jnp.float32))[:, None]
        mag = jnp.exp(lr * dt)
        ar, ai = mag * jnp.cos(li * dt), mag * jnp.sin(li * dt)
        den = lr * lr + li * li
        fr = ((ar - 1.0) * lr + ai * li) / den
        fi = (ai * lr - (ar - 1.0) * li) / den
        br_ = b_re[dr].astype(jnp.float32)
        bi_ = b_im[dr].astype(jnp.float32)
        bbr = fr[..., None] * br_ - fi[..., None] * bi_
        bbi = fr[..., None] * bi_ + fi[..., None] * br_
        xs = ug if dr == 0 else jnp.flip(ug, axis=1)
        bur = jnp.einsum('blgh,gph->blgp', xs, bbr)
        bui = jnp.einsum('blgh,gph->blgp', xs, bbi)
        if h0 is not None:
            hr0 = h0[:, dr, 0].astype(jnp.float32)
            hi0 = h0[:, dr, 1].astype(jnp.float32)
            bur = bur.at[:, 0].add(ar * hr0 - ai * hi0)
            bui = bui.at[:, 0].add(ar * hi0 + ai * hr0)
        a_r = jnp.broadcast_to(ar, bur.shape)
        a_i = jnp.broadcast_to(ai, bur.shape)
        _, _, hr, hi = lax.associative_scan(_complex_linear_step, (a_r, a_i, bur, bui), axis=1)
        finals.append(jnp.stack([hr[:, -1], hi[:, -1]], axis=1))
        if dr == 1:
            hr, hi = jnp.flip(hr, axis=1), jnp.flip(hi, axis=1)
        y = y + jnp.einsum('blgp,ghp->blgh', hr, c_re[dr].astype(jnp.float32)) \
              - jnp.einsum('blgp,ghp->blgh', hi, c_im[dr].astype(jnp.float32))
    y = jax.nn.gelu(y.reshape(bsz, length, S5_WIDTH)).astype(u.dtype)
    out = y * jax.nn.sigmoid(y @ w_glu + b_glu)
    return out, jnp.stack(finals, axis=1)


def mla_mixer(cq, ckv, kr, g_q, w_uq, g_kv, w_ukv, ctx_lat=None, rope=None):
    bsz, length, _ = cq.shape
    q = (rms_norm(cq, g_q) @ w_uq).reshape(bsz, length, MLA_HEADS, MLA_NOPE + MLA_ROPE)
    ckv = rms_norm(ckv, g_kv)
    if rope is not None:
        q = jnp.concatenate([q[..., :MLA_NOPE], apply_rope(q[..., MLA_NOPE:], *rope)], axis=-1)
        kr = apply_rope(kr, *rope)
    lat = jnp.concatenate([ckv, kr], axis=-1)
    keys = lat if ctx_lat is None else jnp.concatenate([lat, ctx_lat.astype(lat.dtype)], axis=1)
    lk = keys.shape[1]
    kv = (keys[..., :MLA_KV_LORA] @ w_ukv).reshape(bsz, lk, MLA_HEADS, MLA_NOPE + MLA_V)
    k_rope = jnp.broadcast_to(keys[:, :, None, MLA_KV_LORA:], (bsz, lk, MLA_HEADS, MLA_ROPE))
    k = jnp.concatenate([kv[..., :MLA_NOPE], k_rope], axis=-1)
    v = kv[..., MLA_NOPE:]
    o = softmax_attention(q, k, v, (MLA_NOPE + MLA_ROPE) ** -0.5)
    return o.reshape(bsz, length, MLA_WIDTH), lat


def even_mixer(h, w_in, w_out, s5p, mlap, ctx_state=None, ctx_lat=None, rope=None):
    z = h @ w_in
    o1 = S5_WIDTH
    o2 = o1 + MLA_Q_LORA
    o3 = o2 + MLA_KV_LORA
    y_s5, s5_final = s5_mixer(z[..., :o1], *s5p, h0=ctx_state)
    y_mla, lat = mla_mixer(z[..., o1:o2], z[..., o2:o3], z[..., o3:], *mlap, ctx_lat=ctx_lat, rope=rope)
    out = jnp.concatenate([y_s5, y_mla], axis=-1) @ w_out
    return out, s5_final, lat


def diff_mixer(h, w_in, w_out, lam, g_sub, lam_init, ctx_k=None, ctx_v=None, rope=None):
    bsz, length, _ = h.shape
    z = h @ w_in
    q = z[..., :DIFF_WIDTH].reshape(bsz, length, DIFF_HEADS, 2, DIFF_HD)
    k = z[..., DIFF_WIDTH:2 * DIFF_WIDTH].reshape(bsz, length, DIFF_HEADS, 2, DIFF_HD)
    v = z[..., 2 * DIFF_WIDTH:].reshape(bsz, length, DIFF_HEADS, 2 * DIFF_HD)
    if rope is not None:
        q = apply_rope(q, *rope)
        k = apply_rope(k, *rope)
    own_k, own_v = k, v
    if ctx_k is not None:
        k = jnp.concatenate([k, ctx_k.astype(k.dtype)], axis=1)
        v = jnp.concatenate([v, ctx_v.astype(v.dtype)], axis=1)
    lamf = lam.astype(jnp.float32)
    lam_full = jnp.exp(jnp.sum(lamf[0] * lamf[1])) - jnp.exp(jnp.sum(lamf[2] * lamf[3])) + lam_init
    scale = DIFF_HD ** -0.5

    def block(qb):
        s = jnp.einsum('bqhjd,bkhjd->bhjqk', qb, k).astype(jnp.float32) * scale
        p = jax.nn.softmax(s, axis=-1)
        a = (p[:, :, 0] - lam_full * p[:, :, 1]).astype(v.dtype)
        return jnp.einsum('bhqk,bkhe->bqhe', a, v)

    o = over_query_blocks(block, q)
    o = rms_norm(o, g_sub) * (1.0 - lam_init)
    return o.reshape(bsz, length, DIFF_WIDTH) @ w_out, own_k, own_v


def moe(h, w_router, b_router, w_gate_up, b_gate_up, w_down, b_down):
    bsz, length, d = h.shape
    t = h.reshape(bsz * length, d)
    logits = (t @ w_router + b_router).astype(jnp.float32)
    top_v, top_i = lax.top_k(logits, TOP_K)
    gate_w = jax.nn.softmax(top_v, axis=-1)
    combine = jnp.einsum('nk,nke->ne', gate_w,
                         jax.nn.one_hot(top_i, N_EXPERTS, dtype=jnp.float32)).astype(h.dtype)
    out = jnp.zeros_like(t)
    for e in range(N_EXPERTS):
        gu = t @ w_gate_up[e] + b_gate_up[e]
        g = jnp.minimum(gu[:, :D_FF], SWIGLU_LIMIT)
        u = jnp.clip(gu[:, D_FF:], -SWIGLU_LIMIT, SWIGLU_LIMIT)
        act = g * jax.nn.sigmoid(SWIGLU_ALPHA * g) * (u + 1.0)
        out = out + combine[:, e:e + 1] * (act @ w_down[e] + b_down[e])
    return out.reshape(bsz, length, d)


def setup_inputs(seed: int = 0) -> dict:
    key = jax.random.key(seed)
    ks = list(jax.random.split(key, 64))

    def nrm(shape, s):
        return jax.random.normal(ks.pop(), shape, jnp.float32) * s

    def gain(shape):
        return 1.0 + nrm(shape, 0.05)

    n_idx = jnp.arange(S5_STATE, dtype=jnp.float32)
    g2 = (N_EVEN, 2, S5_GROUPS)
    return {
        'x_prompt': nrm((BATCH, SEQ, D_MODEL), 1.0),
        'x_sample': nrm((DEC_BATCH, DEC_SEQ, D_MODEL), 1.0),
        'state_s5': nrm((DEC_BATCH, N_EVEN, 2, 2, S5_GROUPS, S5_STATE), 0.5),
        'cache_mla': nrm((DEC_BATCH, N_EVEN, PAST_LEN, MLA_KV_LORA + MLA_ROPE), 1.0),
        'cache_diff_k': nrm((DEC_BATCH, N_ODD, PAST_LEN, DIFF_HEADS, 2, DIFF_HD), 1.0),
        'cache_diff_v': nrm((DEC_BATCH, N_ODD, PAST_LEN, DIFF_HEADS, 2 * DIFF_HD), 1.0),
        'c': nrm((DEC_BATCH, D_MODEL), 1.0),
        'c_ctx': nrm((D_MODEL,), 1.0),
        'w_mod': nrm((DEPTH, D_MODEL, 6 * D_MODEL), 0.3 * D_MODEL ** -0.5),
        'b_mod': nrm((DEPTH, 6 * D_MODEL), 0.02),
        'g_norm1': gain((DEPTH, D_MODEL)),
        'g_norm2': gain((DEPTH, D_MODEL)),
        'g_final': gain((D_MODEL,)),
        'w_in_even': nrm((N_EVEN, D_MODEL, EVEN_IN), D_MODEL ** -0.5),
        'w_out_even': nrm((N_EVEN, EVEN_OUT, D_MODEL), EVEN_OUT ** -0.5),
        's5_lam_re': -0.5 + nrm(g2 + (S5_STATE,), 0.01),
        's5_lam_im': jnp.pi * n_idx + nrm(g2 + (S5_STATE,), 0.01),
        's5_log_dt': jax.random.uniform(ks.pop(), g2, jnp.float32, math.log(1e-3), math.log(1e-1)),
        's5_b_re': nrm(g2 + (S5_STATE, S5_GROUP), (2 * S5_GROUP) ** -0.5),
        's5_b_im': nrm(g2 + (S5_STATE, S5_GROUP), (2 * S5_GROUP) ** -0.5),
        's5_c_re': nrm(g2 + (S5_GROUP, S5_STATE), (2 * S5_STATE) ** -0.5),
        's5_c_im': nrm(g2 + (S5_GROUP, S5_STATE), (2 * S5_STATE) ** -0.5),
        's5_d': nrm((N_EVEN, S5_GROUPS, S5_GROUP), 1.0),
        's5_w_glu': nrm((N_EVEN, S5_WIDTH, S5_WIDTH), S5_WIDTH ** -0.5),
        's5_b_glu': nrm((N_EVEN, S5_WIDTH), 0.02),
        'mla_g_q': gain((N_EVEN, MLA_Q_LORA)),
        'mla_w_uq': nrm((N_EVEN, MLA_Q_LORA, MLA_HEADS * (MLA_NOPE + MLA_ROPE)), MLA_Q_LORA ** -0.5),
        'mla_g_kv': gain((N_EVEN, MLA_KV_LORA)),
        'mla_w_ukv': nrm((N_EVEN, MLA_KV_LORA, MLA_HEADS * (MLA_NOPE + MLA_V)), MLA_KV_LORA ** -0.5),
        'w_in_odd': nrm((N_ODD, D_MODEL, 3 * DIFF_WIDTH), D_MODEL ** -0.5),
        'w_out_odd': nrm((N_ODD, DIFF_WIDTH, D_MODEL), DIFF_WIDTH ** -0.5),
        'diff_lam': nrm((N_ODD, 4, DIFF_HD), 0.1),
        'diff_g_sub': gain((N_ODD, 2 * DIFF_HD)),
        'w_router': nrm((DEPTH, D_MODEL, N_EXPERTS), D_MODEL ** -0.5),
        'b_router': nrm((DEPTH, N_EXPERTS), 0.01),
        'w_gate_up': nrm((DEPTH, N_EXPERTS, D_MODEL, 2 * D_FF), D_MODEL ** -0.5),
        'b_gate_up': nrm((DEPTH, N_EXPERTS, 2 * D_FF), 0.02),
        'w_down': nrm((DEPTH, N_EXPERTS, D_FF, D_MODEL), D_FF ** -0.5),
        'b_down': nrm((DEPTH, N_EXPERTS, D_MODEL), 0.02),
    }


def reference(x_prompt, x_sample, state_s5, cache_mla, cache_diff_k, cache_diff_v, c, c_ctx,
              w_mod, b_mod, g_norm1, g_norm2, g_final, w_in_even, w_out_even,
              s5_lam_re, s5_lam_im, s5_log_dt, s5_b_re, s5_b_im, s5_c_re, s5_c_im, s5_d, s5_w_glu, s5_b_glu,
              mla_g_q, mla_w_uq, mla_g_kv, mla_w_ukv, w_in_odd, w_out_odd, diff_lam, diff_g_sub,
              w_router, b_router, w_gate_up, b_gate_up, w_down, b_down):
    lat_len = x_sample.shape[1]
    rope_mla = axial_rope(lat_len, MLA_ROPE)
    rope_diff = axial_rope(lat_len, DIFF_HD)
    xp, xs = x_prompt, x_sample
    new_s5, new_mla, new_k, new_v = [], [], [], []
    for l in range(DEPTH):
        mp = jnp.split(jax.nn.silu(c_ctx)[None] @ w_mod[l] + b_mod[l], 6, axis=-1)
        ms = jnp.split(jax.nn.silu(c) @ w_mod[l] + b_mod[l], 6, axis=-1)
        hp = modulate(rms_norm(xp, g_norm1[l]), mp[0], mp[1])
        hs = modulate(rms_norm(xs, g_norm1[l]), ms[0], ms[1])
        i = l // 2
        if l % 2 == 0:
            s5p = (s5_lam_re[i], s5_lam_im[i], s5_log_dt[i], s5_b_re[i], s5_b_im[i],
                   s5_c_re[i], s5_c_im[i], s5_d[i], s5_w_glu[i], s5_b_glu[i])
            mlap = (mla_g_q[i], mla_w_uq[i], mla_g_kv[i], mla_w_ukv[i])
            op, s5_fin, mla_lat = even_mixer(hp, w_in_even[i], w_out_even[i], s5p, mlap)
            os_, _, _ = even_mixer(hs, w_in_even[i], w_out_even[i], s5p, mlap,
                                   ctx_state=state_s5[:, i], ctx_lat=cache_mla[:, i], rope=rope_mla)
            new_s5.append(s5_fin)
            new_mla.append(mla_lat)
        else:
            lam_init = 0.8 - 0.6 * math.exp(-0.3 * l)
            op, kp, vp = diff_mixer(hp, w_in_odd[i], w_out_odd[i], diff_lam[i], diff_g_sub[i], lam_init)
            os_, _, _ = diff_mixer(hs, w_in_odd[i], w_out_odd[i], diff_lam[i], diff_g_sub[i], lam_init,
                                   ctx_k=cache_diff_k[:, i], ctx_v=cache_diff_v[:, i], rope=rope_diff)
            new_k.append(kp)
            new_v.append(vp)
        xp = xp + mp[2][:, None] * op
        xs = xs + ms[2][:, None] * os_
        moe_w = (w_router[l], b_router[l], w_gate_up[l], b_gate_up[l], w_down[l], b_down[l])
        xp = xp + mp[5][:, None] * moe(modulate(rms_norm(xp, g_norm2[l]), mp[3], mp[4]), *moe_w)
        xs = xs + ms[5][:, None] * moe(modulate(rms_norm(xs, g_norm2[l]), ms[3], ms[4]), *moe_w)
    y_prompt = rms_norm(xp, g_final)
    y_sample = rms_norm(xs, g_final)
    return (y_prompt, y_sample, jnp.stack(new_s5, axis=1), jnp.stack(new_mla, axis=1),
            jnp.stack(new_k, axis=1), jnp.stack(new_v, axis=1))
```

```python
import functools
import math

import jax
import jax.numpy as jnp
from jax import lax
from jax.experimental import pallas as pl
from jax.experimental.pallas import tpu as pltpu

D_MODEL = 1024
BATCH = 16
SEQ = 256
DEPTH = 4
DEC_BATCH = 8
DEC_SEQ = 1024
PAST_LEN = 512
GRID_W = 64
N_EVEN = (DEPTH + 1) // 2
N_ODD = DEPTH // 2
S5_WIDTH = D_MODEL // 2
S5_GROUP = 16
S5_GROUPS = S5_WIDTH // S5_GROUP
S5_STATE = 64
MLA_HEADS = 8
MLA_NOPE = 64
MLA_ROPE = 32
MLA_V = 64
MLA_Q_LORA = D_MODEL // 4
MLA_KV_LORA = D_MODEL // 8
MLA_WIDTH = MLA_HEADS * MLA_V
EVEN_IN = S5_WIDTH + MLA_Q_LORA + MLA_KV_LORA + MLA_ROPE
EVEN_OUT = S5_WIDTH + MLA_WIDTH
DIFF_HEADS = 8
DIFF_HD = D_MODEL // (2 * DIFF_HEADS)
DIFF_WIDTH = DIFF_HEADS * 2 * DIFF_HD
N_EXPERTS = 32
TOP_K = 4
D_FF = D_MODEL
SWIGLU_LIMIT = 7.0
SWIGLU_ALPHA = 1.702
ROPE_THETA = 10000.0
Q_BLOCK = 128
EPS = 1e-6

N_PROMPT = BATCH * SEQ
N_SAMPLE = DEC_BATCH * DEC_SEQ
N_TOK = N_PROMPT + N_SAMPLE

LANES = 128
VMEM_LIMIT_BYTES = 56 * 1024 * 1024

TM = 256
N_TILES = N_TOK // TM
R_TILES = N_TOK * TOP_K // TM + N_EXPERTS
R_MAX = R_TILES * TM

F32 = jnp.float32
BF16 = jnp.bfloat16


def _mod_row(i):
    t0 = i * TM
    return jnp.where(t0 < N_PROMPT, 0, 1 + (t0 - N_PROMPT) // DEC_SEQ)


def _router_kernel(x_ref, g_ref, mod_ref, wr_ref, br_ref,
                   h_ref, topi_ref, gate_ref, rank_ref, counts_ref, carry_ref):
    i = pl.program_id(0)

    @pl.when(i == 0)
    def _():
        carry_ref[...] = jnp.zeros_like(carry_ref)

    x = x_ref[...]
    ms = jnp.mean(x * x, axis=-1, keepdims=True)
    y = x * lax.rsqrt(ms + EPS) * g_ref[...]
    shift = mod_ref[0, 3:4, :]
    scale = mod_ref[0, 4:5, :]
    h = y * (1.0 + scale) + shift
    h_ref[...] = h

    logits = jnp.dot(h, wr_ref[...], preferred_element_type=F32,
                     precision=lax.Precision.HIGHEST) + br_ref[...]
    lane_e = lax.broadcasted_iota(jnp.int32, logits.shape, 1)
    work = logits
    vals, idxs = [], []
    sel = jnp.zeros(logits.shape, F32)
    for _ in range(TOP_K):
        m = jnp.max(work, axis=-1, keepdims=True)
        idx = jnp.min(jnp.where(work == m, lane_e, N_EXPERTS), axis=-1, keepdims=True)
        hit = lane_e == idx
        vals.append(m)
        idxs.append(idx)
        sel = jnp.where(hit, 1.0, sel)
        work = jnp.where(hit, -jnp.inf, work)
    es = [jnp.exp(v - vals[0]) for v in vals]
    den = es[0] + es[1] + es[2] + es[3]

    row = lax.broadcasted_iota(jnp.int32, (TM, TM), 0)
    col = lax.broadcasted_iota(jnp.int32, (TM, TM), 1)
    tri = jnp.where(col < row, 1.0, 0.0).astype(BF16)
    before = jnp.dot(tri, sel.astype(BF16), preferred_element_type=F32) + carry_ref[...]
    carry_ref[...] += jnp.sum(sel, axis=0, keepdims=True)
    counts_ref[...] = carry_ref[...].astype(jnp.int32)

    lane = lax.broadcasted_iota(jnp.int32, (TM, LANES), 1)
    topi = jnp.zeros((TM, LANES), jnp.int32)
    gate = jnp.zeros((TM, LANES), F32)
    rank = jnp.zeros((TM, LANES), jnp.int32)
    for k in range(TOP_K):
        rk = jnp.sum(jnp.where(lane_e == idxs[k], before, 0.0), axis=-1, keepdims=True)
        topi = jnp.where(lane == k, idxs[k], topi)
        gate = jnp.where(lane == k, es[k] / den, gate)
        rank = jnp.where(lane == k, rk.astype(jnp.int32), rank)
    topi_ref[...] = topi
    gate_ref[...] = gate
    rank_ref[...] = rank


def _router(x_all, g, mod_tab, w_router, b_router):
    return pl.pallas_call(
        _router_kernel,
        grid=(N_TILES,),
        in_specs=[
            pl.BlockSpec((TM, D_MODEL), lambda i: (i, 0)),
            pl.BlockSpec((1, D_MODEL), lambda i: (0, 0)),
            pl.BlockSpec((1, 6, D_MODEL), lambda i: (_mod_row(i), 0, 0)),
            pl.BlockSpec((D_MODEL, N_EXPERTS), lambda i: (0, 0)),
            pl.BlockSpec((1, N_EXPERTS), lambda i: (0, 0)),
        ],
        out_specs=[
            pl.BlockSpec((TM, D_MODEL), lambda i: (i, 0)),
            pl.BlockSpec((TM, LANES), lambda i: (i, 0)),
            pl.BlockSpec((TM, LANES), lambda i: (i, 0)),
            pl.BlockSpec((TM, LANES), lambda i: (i, 0)),
            pl.BlockSpec((1, N_EXPERTS), lambda i: (0, 0)),
        ],
        out_shape=[
            jax.ShapeDtypeStruct((N_TOK, D_MODEL), F32),
            jax.ShapeDtypeStruct((N_TOK, LANES), jnp.int32),
            jax.ShapeDtypeStruct((N_TOK, LANES), F32),
            jax.ShapeDtypeStruct((N_TOK, LANES), jnp.int32),
            jax.ShapeDtypeStruct((1, N_EXPERTS), jnp.int32),
        ],
        scratch_shapes=[pltpu.VMEM((1, N_EXPERTS), F32)],
        compiler_params=pltpu.CompilerParams(
            dimension_semantics=("arbitrary",), vmem_limit_bytes=VMEM_LIMIT_BYTES),
        name="moe_router",
    )(x_all, g.reshape(1, D_MODEL), mod_tab, w_router, b_router.reshape(1, N_EXPERTS))


def _dispatch_kernel(pos_ref, h_ref, xs_in_ref, xs_ref, pos_smem, sem_idx, sem):
    del xs_in_ref
    cp = pltpu.make_async_copy(pos_ref.at[0, 0], pos_smem, sem_idx)
    cp.start()
    cp.wait()

    def issue(r, carry):
        for k in range(TOP_K):
            p = pos_smem[r * TOP_K + k]
            pltpu.make_async_copy(h_ref.at[pl.ds(r, 1)], xs_ref.at[pl.ds(p, 1)], sem.at[k]).start()
        return carry

    lax.fori_loop(0, TM, issue, 0)
    for k in range(TOP_K):
        pltpu.make_async_copy(h_ref, xs_ref.at[pl.ds(0, TM)], sem.at[k]).wait()


def _dispatch(pos, h, xs_init):
    return pl.pallas_call(
        _dispatch_kernel,
        grid=(N_TILES,),
        in_specs=[
            pl.BlockSpec((1, 1, TM * TOP_K), lambda i: (i, 0, 0)),
            pl.BlockSpec((TM, D_MODEL), lambda i: (i, 0)),
            pl.BlockSpec(memory_space=pl.ANY),
        ],
        out_specs=pl.BlockSpec(memory_space=pl.ANY),
        out_shape=jax.ShapeDtypeStruct((R_MAX, D_MODEL), F32),
        scratch_shapes=[
            pltpu.SMEM((TM * TOP_K,), jnp.int32),
            pltpu.SemaphoreType.DMA,
            pltpu.SemaphoreType.DMA((TOP_K,)),
        ],
        input_output_aliases={2: 0},
        compiler_params=pltpu.CompilerParams(
            dimension_semantics=("arbitrary",), vmem_limit_bytes=VMEM_LIMIT_BYTES),
        name="moe_dispatch",
    )(pos.reshape(N_TILES, 1, TM * TOP_K), h, xs_init)


def _ffn_kernel(te_ref, nu_ref, xs_ref, wgu_ref, bgu_ref, wd_ref, bd_ref, ys_ref, wgu_bf, wd_bf):
    i = pl.program_id(0)

    @pl.when(i < nu_ref[0])
    def _():
        prev = te_ref[jnp.maximum(i - 1, 0)]
        new_expert = jnp.logical_or(i == 0, te_ref[i] != prev)

        @pl.when(new_expert)
        def _():
            wgu_bf[...] = wgu_ref[...].astype(BF16)
            wd_bf[...] = wd_ref[...].astype(BF16)

        x = xs_ref[...].astype(BF16)
        gu = jnp.dot(x, wgu_bf[...], preferred_element_type=F32) + bgu_ref[...]
        g = jnp.minimum(gu[:, :D_FF], SWIGLU_LIMIT)
        u = jnp.clip(gu[:, D_FF:], -SWIGLU_LIMIT, SWIGLU_LIMIT)
        act = g * jax.nn.sigmoid(SWIGLU_ALPHA * g) * (u + 1.0)
        ys_ref[...] = jnp.dot(act.astype(BF16), wd_bf[...], preferred_element_type=F32) + bd_ref[...]

    @pl.when(i >= nu_ref[0])
    def _():
        ys_ref[...] = jnp.zeros_like(ys_ref)


def _ffn(layer, tile_expert, n_used, xs, w_gate_up, b_gate_up, w_down, b_down):
    def row_map(i, te, nu):
        return (jnp.minimum(i, nu[0] - 1), 0)

    def w_map(i, te, nu):
        return (layer, te[i], 0, 0)

    grid_spec = pltpu.PrefetchScalarGridSpec(
        num_scalar_prefetch=2,
        grid=(R_TILES,),
        in_specs=[
            pl.BlockSpec((TM, D_MODEL), row_map),
            pl.BlockSpec((None, None, D_MODEL, 2 * D_FF), w_map),
            pl.BlockSpec((None, None, 1, 2 * D_FF), w_map),
            pl.BlockSpec((None, None, D_FF, D_MODEL), w_map),
            pl.BlockSpec((None, None, 1, D_MODEL), w_map),
        ],
        out_specs=pl.BlockSpec((TM, D_MODEL), lambda i, te, nu: (i, 0)),
        scratch_shapes=[
            pltpu.VMEM((D_MODEL, 2 * D_FF), BF16),
            pltpu.VMEM((D_FF, D_MODEL), BF16),
        ],
    )
    return pl.pallas_call(
        _ffn_kernel,
        grid_spec=grid_spec,
        out_shape=jax.ShapeDtypeStruct((R_MAX, D_MODEL), F32),
        compiler_params=pltpu.CompilerParams(
            dimension_semantics=("arbitrary",), vmem_limit_bytes=VMEM_LIMIT_BYTES),
        name="moe_ffn",
    )(tile_expert, n_used, xs, w_gate_up,
      b_gate_up.reshape(DEPTH, N_EXPERTS, 1, 2 * D_FF), w_down,
      b_down.reshape(DEPTH, N_EXPERTS, 1, D_MODEL))


def _combine_kernel(pos_ref, ys_ref, x_ref, gate_ref, mod_ref, out_ref, buf, pos_smem, sem_idx, sem):
    cp = pltpu.make_async_copy(pos_ref.at[0, 0], pos_smem, sem_idx)
    cp.start()
    cp.wait()

    def issue(r, carry):
        for k in range(TOP_K):
            p = pos_smem[r * TOP_K + k]
            pltpu.make_async_copy(ys_ref.at[pl.ds(p, 1)], buf.at[k, pl.ds(r, 1)], sem.at[k]).start()
        return carry

    lax.fori_loop(0, TM, issue, 0)
    acc = jnp.zeros((TM, D_MODEL), F32)
    for k in range(TOP_K):
        pltpu.make_async_copy(ys_ref.at[pl.ds(0, TM)], buf.at[k], sem.at[k]).wait()
        acc = acc + gate_ref[:, k:k + 1] * buf[k]
    out_ref[...] = x_ref[...] + mod_ref[0, 5:6, :] * acc


def _combine(pos, ys, x_all, gate, mod_tab):
    return pl.pallas_call(
        _combine_kernel,
        grid=(N_TILES,),
        in_specs=[
            pl.BlockSpec((1, 1, TM * TOP_K), lambda i: (i, 0, 0)),
            pl.BlockSpec(memory_space=pl.ANY),
            pl.BlockSpec((TM, D_MODEL), lambda i: (i, 0)),
            pl.BlockSpec((TM, LANES), lambda i: (i, 0)),
            pl.BlockSpec((1, 6, D_MODEL), lambda i: (_mod_row(i), 0, 0)),
        ],
        out_specs=pl.BlockSpec((TM, D_MODEL), lambda i: (i, 0)),
        out_shape=jax.ShapeDtypeStruct((N_TOK, D_MODEL), F32),
        scratch_shapes=[
            pltpu.VMEM((TOP_K, TM, D_MODEL), F32),
            pltpu.SMEM((TM * TOP_K,), jnp.int32),
            pltpu.SemaphoreType.DMA,
            pltpu.SemaphoreType.DMA((TOP_K,)),
        ],
        compiler_params=pltpu.CompilerParams(
            dimension_semantics=("arbitrary",), vmem_limit_bytes=VMEM_LIMIT_BYTES),
        name="moe_combine",
    )(pos.reshape(N_TILES, 1, TM * TOP_K), ys, x_all, gate, mod_tab)


def _moe_layer(layer, x_all, mod_tab, g_norm2, w_router, b_router, w_gate_up, b_gate_up, w_down, b_down):
    h, topi, gate, rank, counts = _router(x_all, g_norm2[layer], mod_tab, w_router[layer], b_router[layer])
    counts = counts[0]
    padded = ((counts + TM - 1) // TM) * TM
    ends = jnp.cumsum(padded)
    starts = ends - padded
    topi4 = topi[:, :TOP_K]
    pos = (starts[topi4] + rank[:, :TOP_K]).astype(jnp.int32)
    n_used = (ends[-1] // TM).astype(jnp.int32)
    tile_start = jnp.arange(R_TILES, dtype=jnp.int32) * TM
    tile_expert = jnp.searchsorted(ends, jnp.minimum(tile_start, ends[-1] - 1), side="right")
    tile_expert = jnp.minimum(tile_expert, N_EXPERTS - 1).astype(jnp.int32)
    xs = _dispatch(pos, h, jnp.zeros((R_MAX, D_MODEL), F32))
    ys = _ffn(layer, tile_expert, n_used.reshape(1), xs, w_gate_up, b_gate_up, w_down, b_down)
    return _combine(pos, ys, x_all, gate, mod_tab)


def _rms_norm(x, g):
    xf = x.astype(F32)
    y = xf * lax.rsqrt(jnp.mean(xf * xf, axis=-1, keepdims=True) + EPS)
    return (y * g.astype(F32)).astype(x.dtype)


def _modulate(h, shift, scale):
    return h * (1.0 + scale[:, None]) + shift[:, None]


def _axial_rope(length, dim):
    rows = length // GRID_W
    row = jnp.repeat(jnp.arange(rows, dtype=F32), GRID_W)
    col = jnp.tile(jnp.arange(GRID_W, dtype=F32), rows)
    n_freq = dim // 4
    inv = ROPE_THETA ** (-jnp.arange(n_freq, dtype=F32) / n_freq)
    ang = jnp.concatenate([row[:, None] * inv, col[:, None] * inv], axis=-1)
    return jnp.cos(ang), jnp.sin(ang)


def _apply_rope(x, cos, sin):
    shape = (cos.shape[0],) + (1,) * (x.ndim - 3) + (cos.shape[1],)
    cos = cos.reshape(shape).astype(x.dtype)
    sin = sin.reshape(shape).astype(x.dtype)
    xr = x.reshape(x.shape[:-1] + (x.shape[-1] // 2, 2))
    x1, x2 = xr[..., 0], xr[..., 1]
    return jnp.stack([x1 * cos - x2 * sin, x1 * sin + x2 * cos], axis=-1).reshape(x.shape)


def _over_query_blocks(fn, q):
    b, lq = q.shape[:2]
    nb = lq // Q_BLOCK
    qb = jnp.moveaxis(q.reshape((b, nb, Q_BLOCK) + q.shape[2:]), 1, 0)
    out = lax.map(fn, qb)
    return jnp.moveaxis(out, 0, 1).reshape((b, lq) + out.shape[3:])


def _softmax_attention(q, k, v, scale):
    def block(qb):
        s = jnp.einsum('bqhd,bkhd->bhqk', qb, k).astype(F32) * scale
        p = jax.nn.softmax(s, axis=-1).astype(v.dtype)
        return jnp.einsum('bhqk,bkhd->bqhd', p, v)
    return _over_query_blocks(block, q)


def _complex_linear_step(e1, e2):
    a1r, a1i, b1r, b1i = e1
    a2r, a2i, b2r, b2i = e2
    return (a2r * a1r - a2i * a1i, a2r * a1i + a2i * a1r,
            a2r * b1r - a2i * b1i + b2r, a2r * b1i + a2i * b1r + b2i)


def _s5_mixer(u, lam_re, lam_im, log_dt, b_re, b_im, c_re, c_im, d_skip, w_glu, b_glu, h0=None):
    bsz, length, _ = u.shape
    ug = u.reshape(bsz, length, S5_GROUPS, S5_GROUP).astype(F32)
    y = d_skip.astype(F32) * ug
    finals = []
    for dr in range(2):
        lr = jnp.minimum(lam_re[dr].astype(F32), -1e-4)
        li = lam_im[dr].astype(F32)
        dt = jnp.exp(log_dt[dr].astype(F32))[:, None]
        mag = jnp.exp(lr * dt)
        ar, ai = mag * jnp.cos(li * dt), mag * jnp.sin(li * dt)
        den = lr * lr + li * li
        fr = ((ar - 1.0) * lr + ai * li) / den
        fi = (ai * lr - (ar - 1.0) * li) / den
        br_ = b_re[dr].astype(F32)
        bi_ = b_im[dr].astype(F32)
        bbr = fr[..., None] * br_ - fi[..., None] * bi_
        bbi = fr[..., None] * bi_ + fi[..., None] * br_
        xs = ug if dr == 0 else jnp.flip(ug, axis=1)
        bur = jnp.einsum('blgh,gph->blgp', xs, bbr)
        bui = jnp.einsum('blgh,gph->blgp', xs, bbi)
        if h0 is not None:
            hr0 = h0[:, dr, 0].astype(F32)
            hi0 = h0[:, dr, 1].astype(F32)
            bur = bur.at[:, 0].add(ar * hr0 - ai * hi0)
            bui = bui.at[:, 0].add(ar * hi0 + ai * hr0)
        a_r = jnp.broadcast_to(ar, bur.shape)
        a_i = jnp.broadcast_to(ai, bur.shape)
        _, _, hr, hi = lax.associative_scan(_complex_linear_step, (a_r, a_i, bur, bui), axis=1)
        finals.append(jnp.stack([hr[:, -1], hi[:, -1]], axis=1))
        if dr == 1:
            hr, hi = jnp.flip(hr, axis=1), jnp.flip(hi, axis=1)
        y = y + jnp.einsum('blgp,ghp->blgh', hr, c_re[dr].astype(F32)) \
              - jnp.einsum('blgp,ghp->blgh', hi, c_im[dr].astype(F32))
    y = jax.nn.gelu(y.reshape(bsz, length, S5_WIDTH)).astype(u.dtype)
    out = y * jax.nn.sigmoid(y @ w_glu + b_glu)
    return out, jnp.stack(finals, axis=1)


def _mla_mixer(cq, ckv, kr, g_q, w_uq, g_kv, w_ukv, ctx_lat=None, rope=None):
    bsz, length, _ = cq.shape
    q = (_rms_norm(cq, g_q) @ w_uq).reshape(bsz, length, MLA_HEADS, MLA_NOPE + MLA_ROPE)
    ckv = _rms_norm(ckv, g_kv)
    if rope is not None:
        q = jnp.concatenate([q[..., :MLA_NOPE], _apply_rope(q[..., MLA_NOPE:], *rope)], axis=-1)
        kr = _apply_rope(kr, *rope)
    lat = jnp.concatenate([ckv, kr], axis=-1)
    keys = lat if ctx_lat is None else jnp.concatenate([lat, ctx_lat.astype(lat.dtype)], axis=1)
    lk = keys.shape[1]
    kv = (keys[..., :MLA_KV_LORA] @ w_ukv).reshape(bsz, lk, MLA_HEADS, MLA_NOPE + MLA_V)
    k_rope = jnp.broadcast_to(keys[:, :, None, MLA_KV_LORA:], (bsz, lk, MLA_HEADS, MLA_ROPE))
    k = jnp.concatenate([kv[..., :MLA_NOPE], k_rope], axis=-1)
    v = kv[..., MLA_NOPE:]
    o = _softmax_attention(q, k, v, (MLA_NOPE + MLA_ROPE) ** -0.5)
    return o.reshape(bsz, length, MLA_WIDTH), lat


def _even_mixer(h, w_in, w_out, s5p, mlap, ctx_state=None, ctx_lat=None, rope=None):
    z = h @ w_in
    o1 = S5_WIDTH
    o2 = o1 + MLA_Q_LORA
    o3 = o2 + MLA_KV_LORA
    y_s5, s5_final = _s5_mixer(z[..., :o1], *s5p, h0=ctx_state)
    y_mla, lat = _mla_mixer(z[..., o1:o2], z[..., o2:o3], z[..., o3:], *mlap, ctx_lat=ctx_lat, rope=rope)
    out = jnp.concatenate([y_s5, y_mla], axis=-1) @ w_out
    return out, s5_final, lat


def _diff_mixer(h, w_in, w_out, lam, g_sub, lam_init, ctx_k=None, ctx_v=None, rope=None):
    bsz, length, _ = h.shape
    z = h @ w_in
    q = z[..., :DIFF_WIDTH].reshape(bsz, length, DIFF_HEADS, 2, DIFF_HD)
    k = z[..., DIFF_WIDTH:2 * DIFF_WIDTH].reshape(bsz, length, DIFF_HEADS, 2, DIFF_HD)
    v = z[..., 2 * DIFF_WIDTH:].reshape(bsz, length, DIFF_HEADS, 2 * DIFF_HD)
    if rope is not None:
        q = _apply_rope(q, *rope)
        k = _apply_rope(k, *rope)
    own_k, own_v = k, v
    if ctx_k is not None:
        k = jnp.concatenate([k, ctx_k.astype(k.dtype)], axis=1)
        v = jnp.concatenate([v, ctx_v.astype(v.dtype)], axis=1)
    lamf = lam.astype(F32)
    lam_full = jnp.exp(jnp.sum(lamf[0] * lamf[1])) - jnp.exp(jnp.sum(lamf[2] * lamf[3])) + lam_init
    scale = DIFF_HD ** -0.5

    def block(qb):
        s = jnp.einsum('bqhjd,bkhjd->bhjqk', qb, k).astype(F32) * scale
        p = jax.nn.softmax(s, axis=-1)
        a = (p[:, :, 0] - lam_full * p[:, :, 1]).astype(v.dtype)
        return jnp.einsum('bhqk,bkhe->bqhe', a, v)

    o = _over_query_blocks(block, q)
    o = _rms_norm(o, g_sub) * (1.0 - lam_init)
    return o.reshape(bsz, length, DIFF_WIDTH) @ w_out, own_k, own_v


def kernel(x_prompt, x_sample, state_s5, cache_mla, cache_diff_k, cache_diff_v, c, c_ctx, w_mod, b_mod, g_norm1, g_norm2, g_final, w_in_even, w_out_even, s5_lam_re, s5_lam_im, s5_log_dt, s5_b_re, s5_b_im, s5_c_re, s5_c_im, s5_d, s5_w_glu, s5_b_glu, mla_g_q, mla_w_uq, mla_g_kv, mla_w_ukv, w_in_odd, w_out_odd, diff_lam, diff_g_sub, w_router, b_router, w_gate_up, b_gate_up, w_down, b_down):
    rope_mla = _axial_rope(DEC_SEQ, MLA_ROPE)
    rope_diff = _axial_rope(DEC_SEQ, DIFF_HD)
    xp, xs = x_prompt, x_sample
    new_s5, new_mla, new_k, new_v = [], [], [], []
    for l in range(DEPTH):
        mod_p = jax.nn.silu(c_ctx)[None] @ w_mod[l] + b_mod[l]
        mod_s = jax.nn.silu(c) @ w_mod[l] + b_mod[l]
        mp = jnp.split(mod_p, 6, axis=-1)
        ms = jnp.split(mod_s, 6, axis=-1)
        mod_tab = jnp.concatenate([mod_p, mod_s], axis=0).reshape(1 + DEC_BATCH, 6, D_MODEL)
        hp = _modulate(_rms_norm(xp, g_norm1[l]), mp[0], mp[1])
        hs = _modulate(_rms_norm(xs, g_norm1[l]), ms[0], ms[1])
        i = l // 2
        if l % 2 == 0:
            s5p = (s5_lam_re[i], s5_lam_im[i], s5_log_dt[i], s5_b_re[i], s5_b_im[i],
                   s5_c_re[i], s5_c_im[i], s5_d[i], s5_w_glu[i], s5_b_glu[i])
            mlap = (mla_g_q[i], mla_w_uq[i], mla_g_kv[i], mla_w_ukv[i])
            op, s5_fin, mla_lat = _even_mixer(hp, w_in_even[i], w_out_even[i], s5p, mlap)
            os_, _, _ = _even_mixer(hs, w_in_even[i], w_out_even[i], s5p, mlap,
                                    ctx_state=state_s5[:, i], ctx_lat=cache_mla[:, i], rope=rope_mla)
            new_s5.append(s5_fin)
            new_mla.append(mla_lat)
        else:
            lam_init = 0.8 - 0.6 * math.exp(-0.3 * l)
            op, kp, vp = _diff_mixer(hp, w_in_odd[i], w_out_odd[i], diff_lam[i], diff_g_sub[i], lam_init)
            os_, _, _ = _diff_mixer(hs, w_in_odd[i], w_out_odd[i], diff_lam[i], diff_g_sub[i], lam_init,
                                    ctx_k=cache_diff_k[:, i], ctx_v=cache_diff_v[:, i], rope=rope_diff)
            new_k.append(kp)
            new_v.append(vp)
        xp = xp + mp[2][:, None] * op
        xs = xs + ms[2][:, None] * os_
        x_all = jnp.concatenate([xp.reshape(N_PROMPT, D_MODEL), xs.reshape(N_SAMPLE, D_MODEL)], axis=0)
        x_all = _moe_layer(l, x_all, mod_tab, g_norm2, w_router, b_router,
                           w_gate_up, b_gate_up, w_down, b_down)
        xp = x_all[:N_PROMPT].reshape(BATCH, SEQ, D_MODEL)
        xs = x_all[N_PROMPT:].reshape(DEC_BATCH, DEC_SEQ, D_MODEL)
    y_prompt = _rms_norm(xp, g_final)
    y_sample = _rms_norm(xs, g_final)
    return (y_prompt, y_sample, jnp.stack(new_s5, axis=1), jnp.stack(new_mla, axis=1),
            jnp.stack(new_k, axis=1), jnp.stack(new_v, axis=1))
```

```python
import functools
import math

import jax
import jax.numpy as jnp
from jax import lax
from jax.experimental import pallas as pl
from jax.experimental.pallas import tpu as pltpu

D_MODEL = 1024
BATCH = 16
SEQ = 256
DEPTH = 4
DEC_BATCH = 8
DEC_SEQ = 1024
PAST_LEN = 512
GRID_W = 64
N_EVEN = (DEPTH + 1) // 2
N_ODD = DEPTH // 2
S5_WIDTH = D_MODEL // 2
S5_GROUP = 16
S5_GROUPS = S5_WIDTH // S5_GROUP
S5_STATE = 64
MLA_HEADS = 8
MLA_NOPE = 64
MLA_ROPE = 32
MLA_V = 64
MLA_Q_LORA = D_MODEL // 4
MLA_KV_LORA = D_MODEL // 8
MLA_WIDTH = MLA_HEADS * MLA_V
EVEN_IN = S5_WIDTH + MLA_Q_LORA + MLA_KV_LORA + MLA_ROPE
EVEN_OUT = S5_WIDTH + MLA_WIDTH
DIFF_HEADS = 8
DIFF_HD = D_MODEL // (2 * DIFF_HEADS)
DIFF_WIDTH = DIFF_HEADS * 2 * DIFF_HD
N_EXPERTS = 32
TOP_K = 4
D_FF = D_MODEL
SWIGLU_LIMIT = 7.0
SWIGLU_ALPHA = 1.702
ROPE_THETA = 10000.0
Q_BLOCK = 128
EPS = 1e-6

N_PROMPT = BATCH * SEQ
N_SAMPLE = DEC_BATCH * DEC_SEQ
N_TOK = N_PROMPT + N_SAMPLE

LANES = 128
VMEM_LIMIT_BYTES = 56 * 1024 * 1024

TM = 256
N_TILES = N_TOK // TM
R_TILES = N_TOK * TOP_K // TM + N_EXPERTS
R_MAX = R_TILES * TM

F32 = jnp.float32
BF16 = jnp.bfloat16


def _mod_row(i):
    t0 = i * TM
    return jnp.where(t0 < N_PROMPT, 0, 1 + (t0 - N_PROMPT) // DEC_SEQ)


def _router_kernel(x_ref, g_ref, mod_ref, wr_ref, br_ref,
                   h_ref, topi_ref, gate_ref, rank_ref, counts_ref, carry_ref):
    i = pl.program_id(0)

    @pl.when(i == 0)
    def _():
        carry_ref[...] = jnp.zeros_like(carry_ref)

    x = x_ref[...]
    ms = jnp.mean(x * x, axis=-1, keepdims=True)
    y = x * lax.rsqrt(ms + EPS) * g_ref[...]
    shift = mod_ref[0, 3:4, :]
    scale = mod_ref[0, 4:5, :]
    h = y * (1.0 + scale) + shift
    h_ref[...] = h

    logits = jnp.dot(h, wr_ref[...], preferred_element_type=F32,
                     precision=lax.Precision.HIGHEST) + br_ref[...]
    lane_e = lax.broadcasted_iota(jnp.int32, logits.shape, 1)
    work = logits
    vals, idxs = [], []
    sel = jnp.zeros(logits.shape, F32)
    for _ in range(TOP_K):
        m = jnp.max(work, axis=-1, keepdims=True)
        idx = jnp.min(jnp.where(work == m, lane_e, N_EXPERTS), axis=-1, keepdims=True)
        hit = lane_e == idx
        vals.append(m)
        idxs.append(idx)
        sel = jnp.where(hit, 1.0, sel)
        work = jnp.where(hit, -jnp.inf, work)
    es = [jnp.exp(v - vals[0]) for v in vals]
    den = es[0] + es[1] + es[2] + es[3]

    row = lax.broadcasted_iota(jnp.int32, (TM, TM), 0)
    col = lax.broadcasted_iota(jnp.int32, (TM, TM), 1)
    tri = jnp.where(col < row, 1.0, 0.0).astype(BF16)
    before = jnp.dot(tri, sel.astype(BF16), preferred_element_type=F32) + carry_ref[...]
    carry_ref[...] += jnp.sum(sel, axis=0, keepdims=True)
    counts_ref[...] = carry_ref[...].astype(jnp.int32)

    lane = lax.broadcasted_iota(jnp.int32, (TM, LANES), 1)
    topi = jnp.zeros((TM, LANES), jnp.int32)
    gate = jnp.zeros((TM, LANES), F32)
    rank = jnp.zeros((TM, LANES), jnp.int32)
    for k in range(TOP_K):
        rk = jnp.sum(jnp.where(lane_e == idxs[k], before, 0.0), axis=-1, keepdims=True)
        topi = jnp.where(lane == k, idxs[k], topi)
        gate = jnp.where(lane == k, es[k] / den, gate)
        rank = jnp.where(lane == k, rk.astype(jnp.int32), rank)
    topi_ref[...] = topi
    gate_ref[...] = gate
    rank_ref[...] = rank


def _router(x_all, g, mod_tab, w_router, b_router):
    return pl.pallas_call(
        _router_kernel,
        grid=(N_TILES,),
        in_specs=[
            pl.BlockSpec((TM, D_MODEL), lambda i: (i, 0)),
            pl.BlockSpec((1, D_MODEL), lambda i: (0, 0)),
            pl.BlockSpec((1, 6, D_MODEL), lambda i: (_mod_row(i), 0, 0)),
            pl.BlockSpec((D_MODEL, N_EXPERTS), lambda i: (0, 0)),
            pl.BlockSpec((1, N_EXPERTS), lambda i: (0, 0)),
        ],
        out_specs=[
            pl.BlockSpec((TM, D_MODEL), lambda i: (i, 0)),
            pl.BlockSpec((TM, LANES), lambda i: (i, 0)),
            pl.BlockSpec((TM, LANES), lambda i: (i, 0)),
            pl.BlockSpec((TM, LANES), lambda i: (i, 0)),
            pl.BlockSpec((1, N_EXPERTS), lambda i: (0, 0)),
        ],
        out_shape=[
            jax.ShapeDtypeStruct((N_TOK, D_MODEL), F32),
            jax.ShapeDtypeStruct((N_TOK, LANES), jnp.int32),
            jax.ShapeDtypeStruct((N_TOK, LANES), F32),
            jax.ShapeDtypeStruct((N_TOK, LANES), jnp.int32),
            jax.ShapeDtypeStruct((1, N_EXPERTS), jnp.int32),
        ],
        scratch_shapes=[pltpu.VMEM((1, N_EXPERTS), F32)],
        compiler_params=pltpu.CompilerParams(
            dimension_semantics=("arbitrary",), vmem_limit_bytes=VMEM_LIMIT_BYTES),
        name="moe_router",
    )(x_all, g.reshape(1, D_MODEL), mod_tab, w_router, b_router.reshape(1, N_EXPERTS))


def _dispatch_kernel(pos_ref, h_ref, xs_in_ref, xs_ref, pos_smem, sem_idx, sem):
    del xs_in_ref
    cp = pltpu.make_async_copy(pos_ref.at[0, 0], pos_smem, sem_idx)
    cp.start()
    cp.wait()

    def issue(r, carry):
        for k in range(TOP_K):
            p = pos_smem[r * TOP_K + k]
            pltpu.make_async_copy(h_ref.at[pl.ds(r, 1)], xs_ref.at[pl.ds(p, 1)], sem.at[k]).start()
        return carry

    lax.fori_loop(0, TM, issue, 0)
    for k in range(TOP_K):
        pltpu.make_async_copy(h_ref, xs_ref.at[pl.ds(0, TM)], sem.at[k]).wait()


def _dispatch(pos, h, xs_init):
    return pl.pallas_call(
        _dispatch_kernel,
        grid=(N_TILES,),
        in_specs=[
            pl.BlockSpec((1, 1, TM * TOP_K), lambda i: (i, 0, 0)),
            pl.BlockSpec((TM, D_MODEL), lambda i: (i, 0)),
            pl.BlockSpec(memory_space=pl.ANY),
        ],
        out_specs=pl.BlockSpec(memory_space=pl.ANY),
        out_shape=jax.ShapeDtypeStruct((R_MAX, D_MODEL), F32),
        scratch_shapes=[
            pltpu.SMEM((TM * TOP_K,), jnp.int32),
            pltpu.SemaphoreType.DMA,
            pltpu.SemaphoreType.DMA((TOP_K,)),
        ],
        input_output_aliases={2: 0},
        compiler_params=pltpu.CompilerParams(
            dimension_semantics=("arbitrary",), vmem_limit_bytes=VMEM_LIMIT_BYTES),
        name="moe_dispatch",
    )(pos.reshape(N_TILES, 1, TM * TOP_K), h, xs_init)


def _ffn_kernel(te_ref, nu_ref, xs_ref, wgu_ref, bgu_ref, wd_ref, bd_ref, ys_ref, wgu_bf, wd_bf):
    i = pl.program_id(0)

    @pl.when(i < nu_ref[0])
    def _():
        prev = te_ref[jnp.maximum(i - 1, 0)]
        new_expert = jnp.logical_or(i == 0, te_ref[i] != prev)

        @pl.when(new_expert)
        def _():
            wgu_bf[...] = wgu_ref[...].astype(BF16)
            wd_bf[...] = wd_ref[...].astype(BF16)

        x = xs_ref[...].astype(BF16)
        gu = jnp.dot(x, wgu_bf[...], preferred_element_type=F32) + bgu_ref[...]
        g = jnp.minimum(gu[:, :D_FF], SWIGLU_LIMIT)
        u = jnp.clip(gu[:, D_FF:], -SWIGLU_LIMIT, SWIGLU_LIMIT)
        act = g * jax.nn.sigmoid(SWIGLU_ALPHA * g) * (u + 1.0)
        ys_ref[...] = jnp.dot(act.astype(BF16), wd_bf[...], preferred_element_type=F32) + bd_ref[...]

    @pl.when(i >= nu_ref[0])
    def _():
        ys_ref[...] = jnp.zeros_like(ys_ref)


def _ffn(layer, tile_expert, n_used, xs, w_gate_up, b_gate_up, w_down, b_down):
    def row_map(i, te, nu):
        return (jnp.minimum(i, nu[0] - 1), 0)

    def w_map(i, te, nu):
        return (layer, te[i], 0, 0)

    grid_spec = pltpu.PrefetchScalarGridSpec(
        num_scalar_prefetch=2,
        grid=(R_TILES,),
        in_specs=[
            pl.BlockSpec((TM, D_MODEL), row_map),
            pl.BlockSpec((None, None, D_MODEL, 2 * D_FF), w_map),
            pl.BlockSpec((None, None, 1, 2 * D_FF), w_map),
            pl.BlockSpec((None, None, D_FF, D_MODEL), w_map),
            pl.BlockSpec((None, None, 1, D_MODEL), w_map),
        ],
        out_specs=pl.BlockSpec((TM, D_MODEL), lambda i, te, nu: (i, 0)),
        scratch_shapes=[
            pltpu.VMEM((D_MODEL, 2 * D_FF), BF16),
            pltpu.VMEM((D_FF, D_MODEL), BF16),
        ],
    )
    return pl.pallas_call(
        _ffn_kernel,
        grid_spec=grid_spec,
        out_shape=jax.ShapeDtypeStruct((R_MAX, D_MODEL), F32),
        compiler_params=pltpu.CompilerParams(
            dimension_semantics=("arbitrary",), vmem_limit_bytes=VMEM_LIMIT_BYTES),
        name="moe_ffn",
    )(tile_expert, n_used, xs, w_gate_up,
      b_gate_up.reshape(DEPTH, N_EXPERTS, 1, 2 * D_FF), w_down,
      b_down.reshape(DEPTH, N_EXPERTS, 1, D_MODEL))


def _combine_kernel(pos_ref, ys_ref, x_ref, gate_ref, mod_ref, out_ref, buf, pos_smem, sem_idx, sem):
    cp = pltpu.make_async_copy(pos_ref.at[0, 0], pos_smem, sem_idx)
    cp.start()
    cp.wait()

    def issue(r, carry):
        for k in range(TOP_K):
            p = pos_smem[r * TOP_K + k]
            pltpu.make_async_copy(ys_ref.at[pl.ds(p, 1)], buf.at[k, pl.ds(r, 1)], sem.at[k]).start()
        return carry

    lax.fori_loop(0, TM, issue, 0)
    acc = jnp.zeros((TM, D_MODEL), F32)
    for k in range(TOP_K):
        pltpu.make_async_copy(ys_ref.at[pl.ds(0, TM)], buf.at[k], sem.at[k]).wait()
        acc = acc + gate_ref[:, k:k + 1] * buf[k]
    out_ref[...] = x_ref[...] + mod_ref[0, 5:6, :] * acc


def _combine(pos, ys, x_all, gate, mod_tab):
    return pl.pallas_call(
        _combine_kernel,
        grid=(N_TILES,),
        in_specs=[
            pl.BlockSpec((1, 1, TM * TOP_K), lambda i: (i, 0, 0)),
            pl.BlockSpec(memory_space=pl.ANY),
            pl.BlockSpec((TM, D_MODEL), lambda i: (i, 0)),
            pl.BlockSpec((TM, LANES), lambda i: (i, 0)),
            pl.BlockSpec((1, 6, D_MODEL), lambda i: (_mod_row(i), 0, 0)),
        ],
        out_specs=pl.BlockSpec((TM, D_MODEL), lambda i: (i, 0)),
        out_shape=jax.ShapeDtypeStruct((N_TOK, D_MODEL), F32),
        scratch_shapes=[
            pltpu.VMEM((TOP_K, TM, D_MODEL), F32),
            pltpu.SMEM((TM * TOP_K,), jnp.int32),
            pltpu.SemaphoreType.DMA,
            pltpu.SemaphoreType.DMA((TOP_K,)),
        ],
        compiler_params=pltpu.CompilerParams(
            dimension_semantics=("arbitrary",), vmem_limit_bytes=VMEM_LIMIT_BYTES),
        name="moe_combine",
    )(pos.reshape(N_TILES, 1, TM * TOP_K), ys, x_all, gate, mod_tab)


def _moe_layer(layer, x_all, mod_tab, g_norm2, w_router, b_router, w_gate_up, b_gate_up, w_down, b_down):
    h, topi, gate, rank, counts = _router(x_all, g_norm2[layer], mod_tab, w_router[layer], b_router[layer])
    counts = counts[0]
    padded = ((counts + TM - 1) // TM) * TM
    ends = jnp.cumsum(padded)
    starts = ends - padded
    topi4 = topi[:, :TOP_K]
    pos = (starts[topi4] + rank[:, :TOP_K]).astype(jnp.int32)
    n_used = (ends[-1] // TM).astype(jnp.int32)
    tile_start = jnp.arange(R_TILES, dtype=jnp.int32) * TM
    tile_start = jnp.minimum(tile_start, ends[-1] - 1)
    tile_expert = jnp.sum((ends[None, :] <= tile_start[:, None]).astype(jnp.int32), axis=1)
    tile_expert = jnp.minimum(tile_expert, N_EXPERTS - 1).astype(jnp.int32)
    xs = _dispatch(pos, h, jnp.zeros((R_MAX, D_MODEL), F32))
    ys = _ffn(layer, tile_expert, n_used.reshape(1), xs, w_gate_up, b_gate_up, w_down, b_down)
    return _combine(pos, ys, x_all, gate, mod_tab)


S5_ROWS = 512
S5_HALF_W = S5_WIDTH // 2
S5_HALF_STATES = (S5_GROUPS // 2) * S5_STATE
S5_COL_CHUNK = 512


def _s5_scan_kernel(u_ref, bmat_ref, cmat_ref, a_ref, h0_ref, y_ref, fin_ref, bu_ref, h_ref, *, bsz, steps):
    d = pl.program_id(0)
    c = pl.program_id(1)
    hs = S5_HALF_STATES

    @pl.when(c == 0)
    def _():
        h_ref[...] = h0_ref[...]

    u = u_ref[...].astype(BF16)
    for hf in range(2):
        bu_ref[...] = jnp.dot(u[:, hf * S5_HALF_W:(hf + 1) * S5_HALF_W], bmat_ref[hf],
                              preferred_element_type=F32)
        for j in range(hs // S5_COL_CHUNK):
            re0 = j * S5_COL_CHUNK
            im0 = hs + j * S5_COL_CHUNK
            ar = jnp.broadcast_to(a_ref[hf, 0:1, re0:re0 + S5_COL_CHUNK], (bsz, S5_COL_CHUNK))
            ai = jnp.broadcast_to(a_ref[hf, 1:2, re0:re0 + S5_COL_CHUNK], (bsz, S5_COL_CHUNK))

            def step(t, carry, re0=re0, im0=im0, ar=ar, ai=ai):
                hr, hi = carry
                te = jnp.where(d == 0, t, steps - 1 - t)
                r0 = pl.multiple_of(te * bsz, bsz)
                br = bu_ref[pl.ds(r0, bsz), re0:re0 + S5_COL_CHUNK]
                bi = bu_ref[pl.ds(r0, bsz), im0:im0 + S5_COL_CHUNK]
                nr = ar * hr - ai * hi + br
                ni = ar * hi + ai * hr + bi
                bu_ref[pl.ds(r0, bsz), re0:re0 + S5_COL_CHUNK] = nr
                bu_ref[pl.ds(r0, bsz), im0:im0 + S5_COL_CHUNK] = ni
                return nr, ni

            hr, hi = lax.fori_loop(
                0, steps, step,
                (h_ref[hf, :, re0:re0 + S5_COL_CHUNK], h_ref[hf, :, im0:im0 + S5_COL_CHUNK]), unroll=4)
            h_ref[hf, :, re0:re0 + S5_COL_CHUNK] = hr
            h_ref[hf, :, im0:im0 + S5_COL_CHUNK] = hi
        y_ref[:, hf * S5_HALF_W:(hf + 1) * S5_HALF_W] = jnp.dot(
            bu_ref[...].astype(BF16), cmat_ref[hf], preferred_element_type=F32)

    @pl.when(c == pl.num_programs(1) - 1)
    def _():
        fin_ref[...] = h_ref[...]


def _s5_scan(u_tm, bmat, cmat, acoef, h0, bsz):
    rows = u_tm.shape[0]
    steps = S5_ROWS // bsz
    n_chunks = rows // S5_ROWS

    def chunk_map(d, c):
        return jnp.where(d == 0, c, n_chunks - 1 - c)

    return pl.pallas_call(
        functools.partial(_s5_scan_kernel, bsz=bsz, steps=steps),
        grid=(2, n_chunks),
        in_specs=[
            pl.BlockSpec((S5_ROWS, S5_WIDTH), lambda d, c: (chunk_map(d, c), 0)),
            pl.BlockSpec((None, 2, S5_HALF_W, 2 * S5_HALF_STATES), lambda d, c: (d, 0, 0, 0)),
            pl.BlockSpec((None, 2, 2 * S5_HALF_STATES, S5_HALF_W), lambda d, c: (d, 0, 0, 0)),
            pl.BlockSpec((None, 2, 2, S5_HALF_STATES), lambda d, c: (d, 0, 0, 0)),
            pl.BlockSpec((None, 2, bsz, 2 * S5_HALF_STATES), lambda d, c: (d, 0, 0, 0)),
        ],
        out_specs=[
            pl.BlockSpec((None, S5_ROWS, S5_WIDTH), lambda d, c: (d, chunk_map(d, c), 0)),
            pl.BlockSpec((None, 2, bsz, 2 * S5_HALF_STATES), lambda d, c: (d, 0, 0, 0)),
        ],
        out_shape=[
            jax.ShapeDtypeStruct((2, rows, S5_WIDTH), F32),
            jax.ShapeDtypeStruct((2, 2, bsz, 2 * S5_HALF_STATES), F32),
        ],
        scratch_shapes=[
            pltpu.VMEM((S5_ROWS, 2 * S5_HALF_STATES), F32),
            pltpu.VMEM((2, bsz, 2 * S5_HALF_STATES), F32),
        ],
        compiler_params=pltpu.CompilerParams(
            dimension_semantics=("arbitrary", "arbitrary"), vmem_limit_bytes=VMEM_LIMIT_BYTES),
        name="s5_scan",
    )(u_tm, bmat, cmat, acoef, h0)


def _s5_discretize(lam_re, lam_im, log_dt, b_re, b_im, c_re, c_im):
    eye = jnp.eye(S5_GROUPS // 2, dtype=F32)
    bmats, cmats, acoefs = [], [], []
    for dr in range(2):
        lr = jnp.minimum(lam_re[dr].astype(F32), -1e-4)
        li = lam_im[dr].astype(F32)
        dt = jnp.exp(log_dt[dr].astype(F32))[:, None]
        mag = jnp.exp(lr * dt)
        ar, ai = mag * jnp.cos(li * dt), mag * jnp.sin(li * dt)
        den = lr * lr + li * li
        fr = ((ar - 1.0) * lr + ai * li) / den
        fi = (ai * lr - (ar - 1.0) * li) / den
        br_ = b_re[dr].astype(F32)
        bi_ = b_im[dr].astype(F32)
        bbr = fr[..., None] * br_ - fi[..., None] * bi_
        bbi = fr[..., None] * bi_ + fi[..., None] * br_
        bm, cm, am = [], [], []
        for hf in range(2):
            g = slice(hf * S5_GROUPS // 2, (hf + 1) * S5_GROUPS // 2)

            def bdiag_in(w):
                return jnp.einsum('ab,aph->ahbp', eye, w[g]).reshape(S5_HALF_W, S5_HALF_STATES)

            def bdiag_out(w):
                return jnp.einsum('ab,ahp->apbh', eye, w[g]).reshape(S5_HALF_STATES, S5_HALF_W)

            bm.append(jnp.concatenate([bdiag_in(bbr), bdiag_in(bbi)], axis=1))
            cm.append(jnp.concatenate([bdiag_out(c_re[dr].astype(F32)),
                                       -bdiag_out(c_im[dr].astype(F32))], axis=0))
            am.append(jnp.stack([ar[g].reshape(-1), ai[g].reshape(-1)]))
        bmats.append(jnp.stack(bm))
        cmats.append(jnp.stack(cm))
        acoefs.append(jnp.stack(am))
    return jnp.stack(bmats).astype(BF16), jnp.stack(cmats).astype(BF16), jnp.stack(acoefs)


def _s5_state_to_kernel(h0):
    bsz = h0.shape[0]
    h = h0.astype(F32).reshape(bsz, 2, 2, 2, S5_HALF_STATES)
    return h.transpose(1, 3, 0, 2, 4).reshape(2, 2, bsz, 2 * S5_HALF_STATES)


def _s5_state_from_kernel(fin):
    bsz = fin.shape[2]
    h = fin.reshape(2, 2, bsz, 2, S5_HALF_STATES).transpose(2, 0, 3, 1, 4)
    return h.reshape(bsz, 2, 2, S5_GROUPS, S5_STATE)


def _rms_norm(x, g):
    xf = x.astype(F32)
    y = xf * lax.rsqrt(jnp.mean(xf * xf, axis=-1, keepdims=True) + EPS)
    return (y * g.astype(F32)).astype(x.dtype)


def _modulate(h, shift, scale):
    return h * (1.0 + scale[:, None]) + shift[:, None]


def _axial_rope(length, dim):
    rows = length // GRID_W
    row = jnp.repeat(jnp.arange(rows, dtype=F32), GRID_W)
    col = jnp.tile(jnp.arange(GRID_W, dtype=F32), rows)
    n_freq = dim // 4
    inv = ROPE_THETA ** (-jnp.arange(n_freq, dtype=F32) / n_freq)
    ang = jnp.concatenate([row[:, None] * inv, col[:, None] * inv], axis=-1)
    return jnp.cos(ang), jnp.sin(ang)


def _apply_rope(x, cos, sin):
    shape = (cos.shape[0],) + (1,) * (x.ndim - 3) + (cos.shape[1],)
    cos = cos.reshape(shape).astype(x.dtype)
    sin = sin.reshape(shape).astype(x.dtype)
    xr = x.reshape(x.shape[:-1] + (x.shape[-1] // 2, 2))
    x1, x2 = xr[..., 0], xr[..., 1]
    return jnp.stack([x1 * cos - x2 * sin, x1 * sin + x2 * cos], axis=-1).reshape(x.shape)


def _over_query_blocks(fn, q):
    b, lq = q.shape[:2]
    nb = lq // Q_BLOCK
    qb = jnp.moveaxis(q.reshape((b, nb, Q_BLOCK) + q.shape[2:]), 1, 0)
    out = lax.map(fn, qb)
    return jnp.moveaxis(out, 0, 1).reshape((b, lq) + out.shape[3:])


def _softmax_attention(q, k, v, scale):
    def block(qb):
        s = jnp.einsum('bqhd,bkhd->bhqk', qb, k).astype(F32) * scale
        p = jax.nn.softmax(s, axis=-1).astype(v.dtype)
        return jnp.einsum('bhqk,bkhd->bqhd', p, v)
    return _over_query_blocks(block, q)


def _s5_mixer(u, lam_re, lam_im, log_dt, b_re, b_im, c_re, c_im, d_skip, w_glu, b_glu, h0=None):
    bsz, length, _ = u.shape
    bmat, cmat, acoef = _s5_discretize(lam_re, lam_im, log_dt, b_re, b_im, c_re, c_im)
    if h0 is None:
        h0k = jnp.zeros((2, 2, bsz, 2 * S5_HALF_STATES), F32)
    else:
        h0k = _s5_state_to_kernel(h0)
    u_tm = u.astype(F32).transpose(1, 0, 2).reshape(length * bsz, S5_WIDTH)
    y_dir, fin = _s5_scan(u_tm, bmat, cmat, acoef, h0k, bsz)
    y_tm = d_skip.astype(F32).reshape(1, S5_WIDTH) * u_tm + y_dir[0] + y_dir[1]
    y = y_tm.reshape(length, bsz, S5_WIDTH).transpose(1, 0, 2)
    y = jax.nn.gelu(y).astype(u.dtype)
    out = y * jax.nn.sigmoid(y @ w_glu + b_glu)
    return out, _s5_state_from_kernel(fin)


def _mla_mixer(cq, ckv, kr, g_q, w_uq, g_kv, w_ukv, ctx_lat=None, rope=None):
    bsz, length, _ = cq.shape
    q = (_rms_norm(cq, g_q) @ w_uq).reshape(bsz, length, MLA_HEADS, MLA_NOPE + MLA_ROPE)
    ckv = _rms_norm(ckv, g_kv)
    if rope is not None:
        q = jnp.concatenate([q[..., :MLA_NOPE], _apply_rope(q[..., MLA_NOPE:], *rope)], axis=-1)
        kr = _apply_rope(kr, *rope)
    lat = jnp.concatenate([ckv, kr], axis=-1)
    keys = lat if ctx_lat is None else jnp.concatenate([lat, ctx_lat.astype(lat.dtype)], axis=1)
    lk = keys.shape[1]
    kv = (keys[..., :MLA_KV_LORA] @ w_ukv).reshape(bsz, lk, MLA_HEADS, MLA_NOPE + MLA_V)
    k_rope = jnp.broadcast_to(keys[:, :, None, MLA_KV_LORA:], (bsz, lk, MLA_HEADS, MLA_ROPE))
    k = jnp.concatenate([kv[..., :MLA_NOPE], k_rope], axis=-1)
    v = kv[..., MLA_NOPE:]
    o = _softmax_attention(q, k, v, (MLA_NOPE + MLA_ROPE) ** -0.5)
    return o.reshape(bsz, length, MLA_WIDTH), lat


def _even_mixer(h, w_in, w_out, s5p, mlap, ctx_state=None, ctx_lat=None, rope=None):
    z = h @ w_in
    o1 = S5_WIDTH
    o2 = o1 + MLA_Q_LORA
    o3 = o2 + MLA_KV_LORA
    y_s5, s5_final = _s5_mixer(z[..., :o1], *s5p, h0=ctx_state)
    y_mla, lat = _mla_mixer(z[..., o1:o2], z[..., o2:o3], z[..., o3:], *mlap, ctx_lat=ctx_lat, rope=rope)
    out = jnp.concatenate([y_s5, y_mla], axis=-1) @ w_out
    return out, s5_final, lat


def _diff_mixer(h, w_in, w_out, lam, g_sub, lam_init, ctx_k=None, ctx_v=None, rope=None):
    bsz, length, _ = h.shape
    z = h @ w_in
    q = z[..., :DIFF_WIDTH].reshape(bsz, length, DIFF_HEADS, 2, DIFF_HD)
    k = z[..., DIFF_WIDTH:2 * DIFF_WIDTH].reshape(bsz, length, DIFF_HEADS, 2, DIFF_HD)
    v = z[..., 2 * DIFF_WIDTH:].reshape(bsz, length, DIFF_HEADS, 2 * DIFF_HD)
    if rope is not None:
        q = _apply_rope(q, *rope)
        k = _apply_rope(k, *rope)
    own_k, own_v = k, v
    if ctx_k is not None:
        k = jnp.concatenate([k, ctx_k.astype(k.dtype)], axis=1)
        v = jnp.concatenate([v, ctx_v.astype(v.dtype)], axis=1)
    lamf = lam.astype(F32)
    lam_full = jnp.exp(jnp.sum(lamf[0] * lamf[1])) - jnp.exp(jnp.sum(lamf[2] * lamf[3])) + lam_init
    scale = DIFF_HD ** -0.5

    def block(qb):
        s = jnp.einsum('bqhjd,bkhjd->bhjqk', qb, k).astype(F32) * scale
        p = jax.nn.softmax(s, axis=-1)
        a = (p[:, :, 0] - lam_full * p[:, :, 1]).astype(v.dtype)
        return jnp.einsum('bhqk,bkhe->bqhe', a, v)

    o = _over_query_blocks(block, q)
    o = _rms_norm(o, g_sub) * (1.0 - lam_init)
    return o.reshape(bsz, length, DIFF_WIDTH) @ w_out, own_k, own_v


def kernel(x_prompt, x_sample, state_s5, cache_mla, cache_diff_k, cache_diff_v, c, c_ctx, w_mod, b_mod, g_norm1, g_norm2, g_final, w_in_even, w_out_even, s5_lam_re, s5_lam_im, s5_log_dt, s5_b_re, s5_b_im, s5_c_re, s5_c_im, s5_d, s5_w_glu, s5_b_glu, mla_g_q, mla_w_uq, mla_g_kv, mla_w_ukv, w_in_odd, w_out_odd, diff_lam, diff_g_sub, w_router, b_router, w_gate_up, b_gate_up, w_down, b_down):
    rope_mla = _axial_rope(DEC_SEQ, MLA_ROPE)
    rope_diff = _axial_rope(DEC_SEQ, DIFF_HD)
    xp, xs = x_prompt, x_sample
    new_s5, new_mla, new_k, new_v = [], [], [], []
    for l in range(DEPTH):
        mod_p = jax.nn.silu(c_ctx)[None] @ w_mod[l] + b_mod[l]
        mod_s = jax.nn.silu(c) @ w_mod[l] + b_mod[l]
        mp = jnp.split(mod_p, 6, axis=-1)
        ms = jnp.split(mod_s, 6, axis=-1)
        mod_tab = jnp.concatenate([mod_p, mod_s], axis=0).reshape(1 + DEC_BATCH, 6, D_MODEL)
        hp = _modulate(_rms_norm(xp, g_norm1[l]), mp[0], mp[1])
        hs = _modulate(_rms_norm(xs, g_norm1[l]), ms[0], ms[1])
        i = l // 2
        if l % 2 == 0:
            s5p = (s5_lam_re[i], s5_lam_im[i], s5_log_dt[i], s5_b_re[i], s5_b_im[i],
                   s5_c_re[i], s5_c_im[i], s5_d[i], s5_w_glu[i], s5_b_glu[i])
            mlap = (mla_g_q[i], mla_w_uq[i], mla_g_kv[i], mla_w_ukv[i])
            op, s5_fin, mla_lat = _even_mixer(hp, w_in_even[i], w_out_even[i], s5p, mlap)
            os_, _, _ = _even_mixer(hs, w_in_even[i], w_out_even[i], s5p, mlap,
                                    ctx_state=state_s5[:, i], ctx_lat=cache_mla[:, i], rope=rope_mla)
            new_s5.append(s5_fin)
            new_mla.append(mla_lat)
        else:
            lam_init = 0.8 - 0.6 * math.exp(-0.3 * l)
            op, kp, vp = _diff_mixer(hp, w_in_odd[i], w_out_odd[i], diff_lam[i], diff_g_sub[i], lam_init)
            os_, _, _ = _diff_mixer(hs, w_in_odd[i], w_out_odd[i], diff_lam[i], diff_g_sub[i], lam_init,
                                    ctx_k=cache_diff_k[:, i], ctx_v=cache_diff_v[:, i], rope=rope_diff)
            new_k.append(kp)
            new_v.append(vp)
        xp = xp + mp[2][:, None] * op
        xs = xs + ms[2][:, None] * os_
        x_all = jnp.concatenate([xp.reshape(N_PROMPT, D_MODEL), xs.reshape(N_SAMPLE, D_MODEL)], axis=0)
        x_all = _moe_layer(l, x_all, mod_tab, g_norm2, w_router, b_router,
                           w_gate_up, b_gate_up, w_down, b_down)
        xp = x_all[:N_PROMPT].reshape(BATCH, SEQ, D_MODEL)
        xs = x_all[N_PROMPT:].reshape(DEC_BATCH, DEC_SEQ, D_MODEL)
    y_prompt = _rms_norm(xp, g_final)
    y_sample = _rms_norm(xs, g_final)
    return (y_prompt, y_sample, jnp.stack(new_s5, axis=1), jnp.stack(new_mla, axis=1),
            jnp.stack(new_k, axis=1), jnp.stack(new_v, axis=1))
```

```python
import functools
import math

import jax
import jax.numpy as jnp
from jax import lax
from jax.experimental import pallas as pl
from jax.experimental.pallas import tpu as pltpu

D_MODEL = 1024
BATCH = 16
SEQ = 256
DEPTH = 4
DEC_BATCH = 8
DEC_SEQ = 1024
PAST_LEN = 512
GRID_W = 64
N_EVEN = (DEPTH + 1) // 2
N_ODD = DEPTH // 2
S5_WIDTH = D_MODEL // 2
S5_GROUP = 16
S5_GROUPS = S5_WIDTH // S5_GROUP
S5_STATE = 64
MLA_HEADS = 8
MLA_NOPE = 64
MLA_ROPE = 32
MLA_V = 64
MLA_Q_LORA = D_MODEL // 4
MLA_KV_LORA = D_MODEL // 8
MLA_WIDTH = MLA_HEADS * MLA_V
EVEN_IN = S5_WIDTH + MLA_Q_LORA + MLA_KV_LORA + MLA_ROPE
EVEN_OUT = S5_WIDTH + MLA_WIDTH
DIFF_HEADS = 8
DIFF_HD = D_MODEL // (2 * DIFF_HEADS)
DIFF_WIDTH = DIFF_HEADS * 2 * DIFF_HD
N_EXPERTS = 32
TOP_K = 4
D_FF = D_MODEL
SWIGLU_LIMIT = 7.0
SWIGLU_ALPHA = 1.702
ROPE_THETA = 10000.0
Q_BLOCK = 128
EPS = 1e-6

N_PROMPT = BATCH * SEQ
N_SAMPLE = DEC_BATCH * DEC_SEQ
N_TOK = N_PROMPT + N_SAMPLE

LANES = 128
VMEM_LIMIT_BYTES = 56 * 1024 * 1024

TM = 256
N_TILES = N_TOK // TM
R_TILES = N_TOK * TOP_K // TM + N_EXPERTS
R_MAX = R_TILES * TM

F32 = jnp.float32
BF16 = jnp.bfloat16


def _mod_row(i):
    t0 = i * TM
    return jnp.where(t0 < N_PROMPT, 0, 1 + (t0 - N_PROMPT) // DEC_SEQ)


def _router_kernel(x_ref, g_ref, mod_ref, wr_ref, br_ref,
                   h_ref, topi_ref, gate_ref, rank_ref, counts_ref, carry_ref):
    i = pl.program_id(0)

    @pl.when(i == 0)
    def _():
        carry_ref[...] = jnp.zeros_like(carry_ref)

    x = x_ref[...]
    ms = jnp.mean(x * x, axis=-1, keepdims=True)
    y = x * lax.rsqrt(ms + EPS) * g_ref[...]
    shift = mod_ref[0, 3:4, :]
    scale = mod_ref[0, 4:5, :]
    h = y * (1.0 + scale) + shift
    h_ref[...] = h

    logits = jnp.dot(h, wr_ref[...], preferred_element_type=F32,
                     precision=lax.Precision.HIGHEST) + br_ref[...]
    lane_e = lax.broadcasted_iota(jnp.int32, logits.shape, 1)
    work = logits
    vals, idxs = [], []
    sel = jnp.zeros(logits.shape, F32)
    for _ in range(TOP_K):
        m = jnp.max(work, axis=-1, keepdims=True)
        idx = jnp.min(jnp.where(work == m, lane_e, N_EXPERTS), axis=-1, keepdims=True)
        hit = lane_e == idx
        vals.append(m)
        idxs.append(idx)
        sel = jnp.where(hit, 1.0, sel)
        work = jnp.where(hit, -jnp.inf, work)
    es = [jnp.exp(v - vals[0]) for v in vals]
    den = es[0] + es[1] + es[2] + es[3]

    row = lax.broadcasted_iota(jnp.int32, (TM, TM), 0)
    col = lax.broadcasted_iota(jnp.int32, (TM, TM), 1)
    tri = jnp.where(col < row, 1.0, 0.0).astype(BF16)
    before = jnp.dot(tri, sel.astype(BF16), preferred_element_type=F32) + carry_ref[...]
    carry_ref[...] += jnp.sum(sel, axis=0, keepdims=True)
    counts_ref[...] = carry_ref[...].astype(jnp.int32)

    lane = lax.broadcasted_iota(jnp.int32, (TM, LANES), 1)
    topi = jnp.zeros((TM, LANES), jnp.int32)
    gate = jnp.zeros((TM, LANES), F32)
    rank = jnp.zeros((TM, LANES), jnp.int32)
    for k in range(TOP_K):
        rk = jnp.sum(jnp.where(lane_e == idxs[k], before, 0.0), axis=-1, keepdims=True)
        topi = jnp.where(lane == k, idxs[k], topi)
        gate = jnp.where(lane == k, es[k] / den, gate)
        rank = jnp.where(lane == k, rk.astype(jnp.int32), rank)
    topi_ref[...] = topi
    gate_ref[...] = gate
    rank_ref[...] = rank


def _router(x_all, g, mod_tab, w_router, b_router):
    return pl.pallas_call(
        _router_kernel,
        grid=(N_TILES,),
        in_specs=[
            pl.BlockSpec((TM, D_MODEL), lambda i: (i, 0)),
            pl.BlockSpec((1, D_MODEL), lambda i: (0, 0)),
            pl.BlockSpec((1, 6, D_MODEL), lambda i: (_mod_row(i), 0, 0)),
            pl.BlockSpec((D_MODEL, N_EXPERTS), lambda i: (0, 0)),
            pl.BlockSpec((1, N_EXPERTS), lambda i: (0, 0)),
        ],
        out_specs=[
            pl.BlockSpec((TM, D_MODEL), lambda i: (i, 0)),
            pl.BlockSpec((TM, LANES), lambda i: (i, 0)),
            pl.BlockSpec((TM, LANES), lambda i: (i, 0)),
            pl.BlockSpec((TM, LANES), lambda i: (i, 0)),
            pl.BlockSpec((1, N_EXPERTS), lambda i: (0, 0)),
        ],
        out_shape=[
            jax.ShapeDtypeStruct((N_TOK, D_MODEL), F32),
            jax.ShapeDtypeStruct((N_TOK, LANES), jnp.int32),
            jax.ShapeDtypeStruct((N_TOK, LANES), F32),
            jax.ShapeDtypeStruct((N_TOK, LANES), jnp.int32),
            jax.ShapeDtypeStruct((1, N_EXPERTS), jnp.int32),
        ],
        scratch_shapes=[pltpu.VMEM((1, N_EXPERTS), F32)],
        compiler_params=pltpu.CompilerParams(
            dimension_semantics=("arbitrary",), vmem_limit_bytes=VMEM_LIMIT_BYTES),
        name="moe_router",
    )(x_all, g.reshape(1, D_MODEL), mod_tab, w_router, b_router.reshape(1, N_EXPERTS))


def _dispatch_kernel(pos_ref, h_ref, xs_in_ref, xs_ref, pos_smem, sem_idx, sem):
    del xs_in_ref
    cp = pltpu.make_async_copy(pos_ref.at[0, 0], pos_smem, sem_idx)
    cp.start()
    cp.wait()

    def issue(r, carry):
        for k in range(TOP_K):
            p = pos_smem[r * TOP_K + k]
            pltpu.make_async_copy(h_ref.at[pl.ds(r, 1)], xs_ref.at[pl.ds(p, 1)], sem.at[k]).start()
        return carry

    lax.fori_loop(0, TM, issue, 0)
    for k in range(TOP_K):
        pltpu.make_async_copy(h_ref, xs_ref.at[pl.ds(0, TM)], sem.at[k]).wait()


def _dispatch(pos, h, xs_init):
    return pl.pallas_call(
        _dispatch_kernel,
        grid=(N_TILES,),
        in_specs=[
            pl.BlockSpec((1, 1, TM * TOP_K), lambda i: (i, 0, 0)),
            pl.BlockSpec((TM, D_MODEL), lambda i: (i, 0)),
            pl.BlockSpec(memory_space=pl.ANY),
        ],
        out_specs=pl.BlockSpec(memory_space=pl.ANY),
        out_shape=jax.ShapeDtypeStruct((R_MAX, D_MODEL), F32),
        scratch_shapes=[
            pltpu.SMEM((TM * TOP_K,), jnp.int32),
            pltpu.SemaphoreType.DMA,
            pltpu.SemaphoreType.DMA((TOP_K,)),
        ],
        input_output_aliases={2: 0},
        compiler_params=pltpu.CompilerParams(
            dimension_semantics=("arbitrary",), vmem_limit_bytes=VMEM_LIMIT_BYTES),
        name="moe_dispatch",
    )(pos.reshape(N_TILES, 1, TM * TOP_K), h, xs_init)


def _ffn_kernel(te_ref, nu_ref, xs_ref, wgu_ref, bgu_ref, wd_ref, bd_ref, ys_ref, wgu_bf, wd_bf):
    i = pl.program_id(0)

    @pl.when(i < nu_ref[0])
    def _():
        prev = te_ref[jnp.maximum(i - 1, 0)]
        new_expert = jnp.logical_or(i == 0, te_ref[i] != prev)

        @pl.when(new_expert)
        def _():
            wgu_bf[...] = wgu_ref[...].astype(BF16)
            wd_bf[...] = wd_ref[...].astype(BF16)

        x = xs_ref[...].astype(BF16)
        gu = jnp.dot(x, wgu_bf[...], preferred_element_type=F32) + bgu_ref[...]
        g = jnp.minimum(gu[:, :D_FF], SWIGLU_LIMIT)
        u = jnp.clip(gu[:, D_FF:], -SWIGLU_LIMIT, SWIGLU_LIMIT)
        act = g * jax.nn.sigmoid(SWIGLU_ALPHA * g) * (u + 1.0)
        ys_ref[...] = jnp.dot(act.astype(BF16), wd_bf[...], preferred_element_type=F32) + bd_ref[...]

    @pl.when(i >= nu_ref[0])
    def _():
        ys_ref[...] = jnp.zeros_like(ys_ref)


def _ffn(layer, tile_expert, n_used, xs, w_gate_up, b_gate_up, w_down, b_down):
    def row_map(i, te, nu):
        return (jnp.maximum(jnp.minimum(i, nu[0] - 1), 0), 0)

    def w_map(i, te, nu):
        return (layer, te[i], 0, 0)

    grid_spec = pltpu.PrefetchScalarGridSpec(
        num_scalar_prefetch=2,
        grid=(R_TILES,),
        in_specs=[
            pl.BlockSpec((TM, D_MODEL), row_map),
            pl.BlockSpec((None, None, D_MODEL, 2 * D_FF), w_map),
            pl.BlockSpec((None, None, 1, 2 * D_FF), w_map),
            pl.BlockSpec((None, None, D_FF, D_MODEL), w_map),
            pl.BlockSpec((None, None, 1, D_MODEL), w_map),
        ],
        out_specs=pl.BlockSpec((TM, D_MODEL), lambda i, te, nu: (i, 0)),
        scratch_shapes=[
            pltpu.VMEM((D_MODEL, 2 * D_FF), BF16),
            pltpu.VMEM((D_FF, D_MODEL), BF16),
        ],
    )
    return pl.pallas_call(
        _ffn_kernel,
        grid_spec=grid_spec,
        out_shape=jax.ShapeDtypeStruct((R_MAX, D_MODEL), F32),
        compiler_params=pltpu.CompilerParams(
            dimension_semantics=("arbitrary",), vmem_limit_bytes=VMEM_LIMIT_BYTES),
        name="moe_ffn",
    )(tile_expert, n_used, xs, w_gate_up,
      b_gate_up.reshape(DEPTH, N_EXPERTS, 1, 2 * D_FF), w_down,
      b_down.reshape(DEPTH, N_EXPERTS, 1, D_MODEL))


def _combine_kernel(pos_ref, ys_ref, x_ref, gate_ref, mod_ref, out_ref, buf, pos_smem, sem_idx, sem):
    cp = pltpu.make_async_copy(pos_ref.at[0, 0], pos_smem, sem_idx)
    cp.start()
    cp.wait()

    def issue(r, carry):
        for k in range(TOP_K):
            p = pos_smem[r * TOP_K + k]
            pltpu.make_async_copy(ys_ref.at[pl.ds(p, 1)], buf.at[k, pl.ds(r, 1)], sem.at[k]).start()
        return carry

    lax.fori_loop(0, TM, issue, 0)
    acc = jnp.zeros((TM, D_MODEL), F32)
    for k in range(TOP_K):
        pltpu.make_async_copy(ys_ref.at[pl.ds(0, TM)], buf.at[k], sem.at[k]).wait()
        acc = acc + gate_ref[:, k:k + 1] * buf[k]
    out_ref[...] = x_ref[...] + mod_ref[0, 5:6, :] * acc


def _combine(pos, ys, x_all, gate, mod_tab):
    return pl.pallas_call(
        _combine_kernel,
        grid=(N_TILES,),
        in_specs=[
            pl.BlockSpec((1, 1, TM * TOP_K), lambda i: (i, 0, 0)),
            pl.BlockSpec(memory_space=pl.ANY),
            pl.BlockSpec((TM, D_MODEL), lambda i: (i, 0)),
            pl.BlockSpec((TM, LANES), lambda i: (i, 0)),
            pl.BlockSpec((1, 6, D_MODEL), lambda i: (_mod_row(i), 0, 0)),
        ],
        out_specs=pl.BlockSpec((TM, D_MODEL), lambda i: (i, 0)),
        out_shape=jax.ShapeDtypeStruct((N_TOK, D_MODEL), F32),
        scratch_shapes=[
            pltpu.VMEM((TOP_K, TM, D_MODEL), F32),
            pltpu.SMEM((TM * TOP_K,), jnp.int32),
            pltpu.SemaphoreType.DMA,
            pltpu.SemaphoreType.DMA((TOP_K,)),
        ],
        compiler_params=pltpu.CompilerParams(
            dimension_semantics=("arbitrary",), vmem_limit_bytes=VMEM_LIMIT_BYTES),
        name="moe_combine",
    )(pos.reshape(N_TILES, 1, TM * TOP_K), ys, x_all, gate, mod_tab)


def _moe_layer(layer, x_all, mod_tab, g_norm2, w_router, b_router, w_gate_up, b_gate_up, w_down, b_down):
    h, topi, gate, rank, counts = _router(x_all, g_norm2[layer], mod_tab, w_router[layer], b_router[layer])
    counts = counts[0]
    padded = ((counts + TM - 1) // TM) * TM
    ends = jnp.cumsum(padded)
    starts = ends - padded
    topi4 = topi[:, :TOP_K]
    pos = (starts[topi4] + rank[:, :TOP_K]).astype(jnp.int32)
    n_used = (ends[-1] // TM).astype(jnp.int32)
    tile_start = jnp.arange(R_TILES, dtype=jnp.int32) * TM
    tile_start = jnp.minimum(tile_start, ends[-1] - 1)
    tile_expert = jnp.sum((ends[None, :] <= tile_start[:, None]).astype(jnp.int32), axis=1)
    tile_expert = jnp.minimum(tile_expert, N_EXPERTS - 1).astype(jnp.int32)
    xs = _dispatch(pos, h, jnp.zeros((R_MAX, D_MODEL), F32))
    ys = _ffn(layer, tile_expert, n_used.reshape(1), xs, w_gate_up, b_gate_up, w_down, b_down)
    return _combine(pos, ys, x_all, gate, mod_tab)


S5_ROWS = 512
S5_HALF_W = S5_WIDTH // 2
S5_HALF_STATES = (S5_GROUPS // 2) * S5_STATE
S5_COL_CHUNK = 512


def _s5_scan_kernel(u_ref, bmat_ref, cmat_ref, a_ref, h0_ref, y_ref, fin_ref, bu_ref, h_ref, *, bsz, steps):
    d = pl.program_id(0)
    c = pl.program_id(1)
    hs = S5_HALF_STATES

    @pl.when(c == 0)
    def _():
        h_ref[...] = h0_ref[...]

    u = u_ref[...].astype(BF16)
    for hf in range(2):
        bu_ref[...] = jnp.dot(u[:, hf * S5_HALF_W:(hf + 1) * S5_HALF_W], bmat_ref[hf],
                              preferred_element_type=F32)
        for j in range(hs // S5_COL_CHUNK):
            re0 = j * S5_COL_CHUNK
            im0 = hs + j * S5_COL_CHUNK
            ar = jnp.broadcast_to(a_ref[hf, 0:1, re0:re0 + S5_COL_CHUNK], (bsz, S5_COL_CHUNK))
            ai = jnp.broadcast_to(a_ref[hf, 1:2, re0:re0 + S5_COL_CHUNK], (bsz, S5_COL_CHUNK))

            def step(t, carry, re0=re0, im0=im0, ar=ar, ai=ai):
                hr, hi = carry
                te = jnp.where(d == 0, t, steps - 1 - t)
                r0 = pl.multiple_of(te * bsz, bsz)
                br = bu_ref[pl.ds(r0, bsz), re0:re0 + S5_COL_CHUNK]
                bi = bu_ref[pl.ds(r0, bsz), im0:im0 + S5_COL_CHUNK]
                nr = ar * hr - ai * hi + br
                ni = ar * hi + ai * hr + bi
                bu_ref[pl.ds(r0, bsz), re0:re0 + S5_COL_CHUNK] = nr
                bu_ref[pl.ds(r0, bsz), im0:im0 + S5_COL_CHUNK] = ni
                return nr, ni

            hr, hi = lax.fori_loop(
                0, steps, step,
                (h_ref[hf, :, re0:re0 + S5_COL_CHUNK], h_ref[hf, :, im0:im0 + S5_COL_CHUNK]), unroll=4)
            h_ref[hf, :, re0:re0 + S5_COL_CHUNK] = hr
            h_ref[hf, :, im0:im0 + S5_COL_CHUNK] = hi
        y_ref[:, hf * S5_HALF_W:(hf + 1) * S5_HALF_W] = jnp.dot(
            bu_ref[...].astype(BF16), cmat_ref[hf], preferred_element_type=F32)

    @pl.when(c == pl.num_programs(1) - 1)
    def _():
        fin_ref[...] = h_ref[...]


def _s5_scan(u_tm, bmat, cmat, acoef, h0, bsz):
    rows = u_tm.shape[0]
    steps = S5_ROWS // bsz
    n_chunks = rows // S5_ROWS

    def chunk_map(d, c):
        return jnp.where(d == 0, c, n_chunks - 1 - c)

    return pl.pallas_call(
        functools.partial(_s5_scan_kernel, bsz=bsz, steps=steps),
        grid=(2, n_chunks),
        in_specs=[
            pl.BlockSpec((S5_ROWS, S5_WIDTH), lambda d, c: (chunk_map(d, c), 0)),
            pl.BlockSpec((None, 2, S5_HALF_W, 2 * S5_HALF_STATES), lambda d, c: (d, 0, 0, 0)),
            pl.BlockSpec((None, 2, 2 * S5_HALF_STATES, S5_HALF_W), lambda d, c: (d, 0, 0, 0)),
            pl.BlockSpec((None, 2, 2, S5_HALF_STATES), lambda d, c: (d, 0, 0, 0)),
            pl.BlockSpec((None, 2, bsz, 2 * S5_HALF_STATES), lambda d, c: (d, 0, 0, 0)),
        ],
        out_specs=[
            pl.BlockSpec((None, S5_ROWS, S5_WIDTH), lambda d, c: (d, chunk_map(d, c), 0)),
            pl.BlockSpec((None, 2, bsz, 2 * S5_HALF_STATES), lambda d, c: (d, 0, 0, 0)),
        ],
        out_shape=[
            jax.ShapeDtypeStruct((2, rows, S5_WIDTH), F32),
            jax.ShapeDtypeStruct((2, 2, bsz, 2 * S5_HALF_STATES), F32),
        ],
        scratch_shapes=[
            pltpu.VMEM((S5_ROWS, 2 * S5_HALF_STATES), F32),
            pltpu.VMEM((2, bsz, 2 * S5_HALF_STATES), F32),
        ],
        compiler_params=pltpu.CompilerParams(
            dimension_semantics=("arbitrary", "arbitrary"), vmem_limit_bytes=VMEM_LIMIT_BYTES),
        name="s5_scan",
    )(u_tm, bmat, cmat, acoef, h0)


def _s5_discretize(lam_re, lam_im, log_dt, b_re, b_im, c_re, c_im):
    eye = jnp.eye(S5_GROUPS // 2, dtype=F32)
    bmats, cmats, acoefs = [], [], []
    for dr in range(2):
        lr = jnp.minimum(lam_re[dr].astype(F32), -1e-4)
        li = lam_im[dr].astype(F32)
        dt = jnp.exp(log_dt[dr].astype(F32))[:, None]
        mag = jnp.exp(lr * dt)
        ar, ai = mag * jnp.cos(li * dt), mag * jnp.sin(li * dt)
        den = lr * lr + li * li
        fr = ((ar - 1.0) * lr + ai * li) / den
        fi = (ai * lr - (ar - 1.0) * li) / den
        br_ = b_re[dr].astype(F32)
        bi_ = b_im[dr].astype(F32)
        bbr = fr[..., None] * br_ - fi[..., None] * bi_
        bbi = fr[..., None] * bi_ + fi[..., None] * br_
        bm, cm, am = [], [], []
        for hf in range(2):
            g = slice(hf * S5_GROUPS // 2, (hf + 1) * S5_GROUPS // 2)

            def bdiag_in(w):
                return jnp.einsum('ab,aph->ahbp', eye, w[g]).reshape(S5_HALF_W, S5_HALF_STATES)

            def bdiag_out(w):
                return jnp.einsum('ab,ahp->apbh', eye, w[g]).reshape(S5_HALF_STATES, S5_HALF_W)

            bm.append(jnp.concatenate([bdiag_in(bbr), bdiag_in(bbi)], axis=1))
            cm.append(jnp.concatenate([bdiag_out(c_re[dr].astype(F32)),
                                       -bdiag_out(c_im[dr].astype(F32))], axis=0))
            am.append(jnp.stack([ar[g].reshape(-1), ai[g].reshape(-1)]))
        bmats.append(jnp.stack(bm))
        cmats.append(jnp.stack(cm))
        acoefs.append(jnp.stack(am))
    return jnp.stack(bmats).astype(BF16), jnp.stack(cmats).astype(BF16), jnp.stack(acoefs)


def _s5_state_to_kernel(h0):
    bsz = h0.shape[0]
    h = h0.astype(F32).reshape(bsz, 2, 2, 2, S5_HALF_STATES)
    return h.transpose(1, 3, 0, 2, 4).reshape(2, 2, bsz, 2 * S5_HALF_STATES)


def _s5_state_from_kernel(fin):
    bsz = fin.shape[2]
    h = fin.reshape(2, 2, bsz, 2, S5_HALF_STATES).transpose(2, 0, 3, 1, 4)
    return h.reshape(bsz, 2, 2, S5_GROUPS, S5_STATE)


TQ = 256
HEAD_LANES = 128


def _dot(a, b):
    return jnp.dot(a, b, preferred_element_type=F32)


def _dot_t(a, b):
    return lax.dot_general(a, b, (((1,), (1,)), ((), ())), preferred_element_type=F32)


def _rms_rows(x, g):
    return x * lax.rsqrt(jnp.mean(x * x, axis=-1, keepdims=True) + EPS) * g


def _group(bsz, length, row0, mod_base, mod_stride):
    return dict(bsz=bsz, length=length, nt=length // TQ, tile0=row0 // TQ,
                mod_base=mod_base, mod_stride=mod_stride)


def _params(n_axes):
    return pltpu.CompilerParams(dimension_semantics=("arbitrary",) * n_axes,
                                vmem_limit_bytes=VMEM_LIMIT_BYTES)


def _axial_rope(length, dim):
    rows = length // GRID_W
    row = jnp.repeat(jnp.arange(rows, dtype=F32), GRID_W)
    col = jnp.tile(jnp.arange(GRID_W, dtype=F32), rows)
    n_freq = dim // 4
    inv = ROPE_THETA ** (-jnp.arange(n_freq, dtype=F32) / n_freq)
    ang = jnp.concatenate([row[:, None] * inv, col[:, None] * inv], axis=-1)
    return jnp.cos(ang), jnp.sin(ang)


def _rope_tables(length, dim, lead, reps):
    cos, sin = _axial_rope(length, dim)
    cos_r = jnp.repeat(cos, 2, axis=-1)
    sin_r = jnp.repeat(sin, 2, axis=-1) * jnp.tile(jnp.array([-1.0, 1.0], F32), dim // 2)
    part = HEAD_LANES // reps
    pad = ((0, 0), (lead, part - lead - dim))
    cos_t = jnp.tile(jnp.pad(cos_r, pad, constant_values=1.0), (1, reps))
    sin_t = jnp.tile(jnp.pad(sin_r, pad), (1, reps))
    return cos_t, sin_t


def _swap_pairs(w):
    return w[:, jnp.arange(w.shape[1]) ^ 1]


def _even_in_kernel(*refs, rope):
    if rope:
        (x_ref, g_ref, mod_ref, win_ref, gq_ref, wuq_ref, gkv_ref, cos_ref, sin_ref,
         u_ref, q_ref, ckv_ref, kr_ref) = refs
    else:
        (x_ref, g_ref, mod_ref, win_ref, gq_ref, wuq_ref, gkv_ref,
         u_ref, q_ref, ckv_ref, kr_ref) = refs
    o1 = S5_WIDTH
    o2 = o1 + MLA_Q_LORA
    o3 = o2 + MLA_KV_LORA
    o4 = o3 + HEAD_LANES
    n_in = o4 + HEAD_LANES if rope else o4
    n_q = MLA_HEADS * HEAD_LANES
    h = _rms_rows(x_ref[...], g_ref[...]) * (1.0 + mod_ref[0, 1:2, :]) + mod_ref[0, 0:1, :]
    z = _dot(h.astype(BF16), win_ref[:, :n_in])
    u_ref[...] = z[:, :o1]
    ckv_ref[...] = _rms_rows(z[:, o2:o3], gkv_ref[...])
    qn = _rms_rows(z[:, o1:o2], gq_ref[...]).astype(BF16)
    if rope:
        q2 = _dot(qn, wuq_ref[...])
        cos = cos_ref[...]
        sin = sin_ref[...]
        for hd in range(MLA_HEADS):
            a = hd * HEAD_LANES
            q_ref[:, a:a + HEAD_LANES] = (q2[:, a:a + HEAD_LANES] * cos
                                          + q2[:, n_q + a:n_q + a + HEAD_LANES] * sin).astype(q_ref.dtype)
        kr_ref[...] = z[:, o3:o4] * cos + z[:, o4:o4 + HEAD_LANES] * sin
    else:
        q_ref[...] = _dot(qn, wuq_ref[:, :n_q]).astype(q_ref.dtype)
        kr_ref[...] = z[:, o3:o4]


def _even_in(x_all, grp, g1, mod_tab, win_aug, g_q, wuq2, g_kv, tables):
    bsz, length, nt = grp["bsz"], grp["length"], grp["nt"]
    rope = tables is not None
    rows = bsz * length

    def tok(b, t):
        return (b * nt + t, 0)

    in_specs = [
        pl.BlockSpec((TQ, D_MODEL), lambda b, t: (grp["tile0"] + b * nt + t, 0)),
        pl.BlockSpec((1, D_MODEL), lambda b, t: (0, 0)),
        pl.BlockSpec((1, 6, D_MODEL), lambda b, t: (grp["mod_base"] + b * grp["mod_stride"], 0, 0)),
        pl.BlockSpec(win_aug.shape, lambda b, t: (0, 0)),
        pl.BlockSpec((1, MLA_Q_LORA), lambda b, t: (0, 0)),
        pl.BlockSpec(wuq2.shape, lambda b, t: (0, 0)),
        pl.BlockSpec((1, MLA_KV_LORA), lambda b, t: (0, 0)),
    ]
    args = [x_all, g1.reshape(1, D_MODEL), mod_tab, win_aug, g_q.reshape(1, -1), wuq2, g_kv.reshape(1, -1)]
    if rope:
        in_specs += [pl.BlockSpec((TQ, HEAD_LANES), lambda b, t: (t, 0))] * 2
        args += list(tables)
    return pl.pallas_call(
        functools.partial(_even_in_kernel, rope=rope),
        grid=(bsz, nt),
        in_specs=in_specs,
        out_specs=[
            pl.BlockSpec((TQ, S5_WIDTH), lambda b, t: (t, b)),
            pl.BlockSpec((TQ, MLA_HEADS * HEAD_LANES), tok),
            pl.BlockSpec((TQ, MLA_KV_LORA), tok),
            pl.BlockSpec((TQ, HEAD_LANES), tok),
        ],
        out_shape=[
            jax.ShapeDtypeStruct((length, bsz * S5_WIDTH), F32),
            jax.ShapeDtypeStruct((rows, MLA_HEADS * HEAD_LANES), BF16),
            jax.ShapeDtypeStruct((rows, MLA_KV_LORA), F32),
            jax.ShapeDtypeStruct((rows, HEAD_LANES), F32),
        ],
        compiler_params=_params(2),
        name="even_in",
    )(*args)


def _kv_expand_kernel(x_ref, w_ref, o_ref):
    o_ref[...] = _dot(x_ref[...].astype(BF16), w_ref[...]).astype(o_ref.dtype)


def _kv_expand(ckv, w_ukv):
    rows = ckv.shape[0]
    tm = 512
    return pl.pallas_call(
        _kv_expand_kernel,
        grid=(rows // tm,),
        in_specs=[pl.BlockSpec((tm, MLA_KV_LORA), lambda i: (i, 0)),
                  pl.BlockSpec(w_ukv.shape, lambda i: (0, 0))],
        out_specs=pl.BlockSpec((tm, w_ukv.shape[1]), lambda i: (i, 0)),
        out_shape=jax.ShapeDtypeStruct((rows, w_ukv.shape[1]), BF16),
        compiler_params=_params(1),
        name="kv_expand",
    )(ckv, w_ukv)


def _softmax_parts(scores):
    m = functools.reduce(jnp.maximum, [jnp.max(s, axis=-1, keepdims=True) for s in scores])
    es = [jnp.exp(s - m) for s in scores]
    den = functools.reduce(jnp.add, [jnp.sum(e, axis=-1, keepdims=True) for e in es])
    inv = 1.0 / den
    return [e * inv for e in es]


def _mla_attn_kernel(*refs, n_seg):
    q_ref = refs[0]
    o_ref = refs[-1]
    q = q_ref[...]
    lane = lax.broadcasted_iota(jnp.int32, q.shape, 1)
    q_nope = jnp.where(lane < MLA_NOPE, q.astype(F32), 0.0).astype(BF16)
    scale = (MLA_NOPE + MLA_ROPE) ** -0.5
    kvs, scores = [], []
    for s in range(n_seg):
        kv = refs[1 + 2 * s][...]
        kr = refs[2 + 2 * s][...].astype(BF16)
        kvs.append(kv)
        scores.append((_dot_t(q_nope, kv) + _dot_t(q, kr)) * scale)
    ps = _softmax_parts(scores)
    o = _dot(ps[0].astype(BF16), kvs[0])
    for s in range(1, n_seg):
        o = o + _dot(ps[s].astype(BF16), kvs[s])
    o_ref[...] = o


def _mla_attn(q, segs, bsz, length):
    nq = length // TQ
    in_specs = [pl.BlockSpec((TQ, HEAD_LANES), lambda b, h, i: (b * nq + i, h))]
    args = [q]
    for kv, kr, lk in segs:
        in_specs += [pl.BlockSpec((lk, HEAD_LANES), lambda b, h, i: (b, h)),
                     pl.BlockSpec((lk, HEAD_LANES), lambda b, h, i: (b, 0))]
        args += [kv, kr]
    return pl.pallas_call(
        functools.partial(_mla_attn_kernel, n_seg=len(segs)),
        grid=(bsz, MLA_HEADS, nq),
        in_specs=in_specs,
        out_specs=pl.BlockSpec((TQ, HEAD_LANES), lambda b, h, i: (b * nq + i, h)),
        out_shape=jax.ShapeDtypeStruct((bsz * length, MLA_HEADS * HEAD_LANES), F32),
        compiler_params=_params(3),
        name="mla_attn",
    )(*args)


def _even_out_kernel(u_ref, y_ref, o_ref, x_ref, mod_ref, d_ref, wglu_ref, bglu_ref, ws5_ref, wmla_ref,
                     out_ref):
    y =jax.nn.gelu(d_ref[...] * u_ref[...] + y_ref[0] + y_ref[1])
    s5 = y * jax.nn.sigmoid(_dot(y.astype(BF16), wglu_ref[...]) + bglu_ref[...])
    mix = _dot(s5.astype(BF16), ws5_ref[...]) + _dot(o_ref[...].astype(BF16), wmla_ref[...])
    out_ref[...] = x_ref[...] + mod_ref[0, 2:3, :] * mix


def _even_out(x_all, grp, mod_tab, u_tm, y_dir, o_mla, d_skip, w_glu, b_glu, w_out_s5, w_out_mla):
    bsz, length, nt = grp["bsz"], grp["length"], grp["nt"]

    def xrow(b, t):
        return (grp["tile0"] + b * nt + t, 0)

    full = lambda b, t: (0, 0)
    return pl.pallas_call(
        _even_out_kernel,
        grid=(bsz, nt),
        in_specs=[
            pl.BlockSpec((TQ, S5_WIDTH), lambda b, t: (t, b)),
            pl.BlockSpec((2, TQ, S5_WIDTH), lambda b, t: (0, t, b)),
            pl.BlockSpec((TQ, MLA_HEADS * HEAD_LANES), lambda b, t: (b * nt + t, 0)),
            pl.BlockSpec((TQ, D_MODEL), xrow),
            pl.BlockSpec((1, 6, D_MODEL), lambda b, t: (grp["mod_base"] + b * grp["mod_stride"], 0, 0)),
            pl.BlockSpec((1, S5_WIDTH), full),
            pl.BlockSpec(w_glu.shape, full),
            pl.BlockSpec((1, S5_WIDTH), full),
            pl.BlockSpec(w_out_s5.shape, full),
            pl.BlockSpec(w_out_mla.shape, full),
        ],
        out_specs=pl.BlockSpec((TQ, D_MODEL), xrow),
        out_shape=jax.ShapeDtypeStruct(x_all.shape, F32),
        input_output_aliases={3: 0},
        compiler_params=_params(2),
        name="even_out",
    )(u_tm, y_dir.reshape(2, length, bsz * S5_WIDTH), o_mla, x_all, mod_tab,
      d_skip.reshape(1, S5_WIDTH), w_glu, b_glu.reshape(1, S5_WIDTH), w_out_s5, w_out_mla)


def _odd_in_kernel(*refs, rope):
    if rope:
        x_ref, g_ref, mod_ref, w_ref, cos_ref, sin_ref, q_ref, k_ref, v_ref = refs
    else:
        x_ref, g_ref, mod_ref, w_ref, q_ref, k_ref, v_ref = refs
    w3 = 3 * DIFF_WIDTH
    h = _rms_rows(x_ref[...], g_ref[...]) * (1.0 + mod_ref[0, 1:2, :]) + mod_ref[0, 0:1, :]
    z = _dot(h.astype(BF16), w_ref[...] if rope else w_ref[:, :w3])
    v_ref[...] = z[:, 2 * DIFF_WIDTH:w3].astype(v_ref.dtype)
    if rope:
        cos = cos_ref[...]
        sin = sin_ref[...]
        for hd in range(DIFF_HEADS):
            a = hd * HEAD_LANES
            q_ref[:, a:a + HEAD_LANES] = (z[:, a:a + HEAD_LANES] * cos
                                          + z[:, w3 + a:w3 + a + HEAD_LANES] * sin).astype(q_ref.dtype)
            b = DIFF_WIDTH + a
            k_ref[:, a:a + HEAD_LANES] = (z[:, b:b + HEAD_LANES] * cos
                                          + z[:, w3 + b:w3 + b + HEAD_LANES] * sin).astype(k_ref.dtype)
    else:
        q_ref[...] = z[:, :DIFF_WIDTH].astype(q_ref.dtype)
        k_ref[...] = z[:, DIFF_WIDTH:2 * DIFF_WIDTH].astype(k_ref.dtype)


def _odd_in(x_all, grp, g1, mod_tab, w_aug, tables, kv_dtype):
    bsz, length, nt = grp["bsz"], grp["length"], grp["nt"]
    rope = tables is not None
    rows = bsz * length

    def tok(b, t):
        return (b * nt + t, 0)

    in_specs = [
        pl.BlockSpec((TQ, D_MODEL), lambda b, t: (grp["tile0"] + b * nt + t, 0)),
        pl.BlockSpec((1, D_MODEL), lambda b, t: (0, 0)),
        pl.BlockSpec((1, 6, D_MODEL), lambda b, t: (grp["mod_base"] + b * grp["mod_stride"], 0, 0)),
        pl.BlockSpec(w_aug.shape, lambda b, t: (0, 0)),
    ]
    args = [x_all, g1.reshape(1, D_MODEL), mod_tab, w_aug]
    if rope:
        in_specs += [pl.BlockSpec((TQ, HEAD_LANES), lambda b, t: (t, 0))] * 2
        args += list(tables)
    return pl.pallas_call(
        functools.partial(_odd_in_kernel, rope=rope),
        grid=(bsz, nt),
        in_specs=in_specs,
        out_specs=[pl.BlockSpec((TQ, DIFF_WIDTH), tok)] * 3,
        out_shape=[
            jax.ShapeDtypeStruct((rows, DIFF_WIDTH), BF16),
            jax.ShapeDtypeStruct((rows, DIFF_WIDTH), kv_dtype),
            jax.ShapeDtypeStruct((rows, DIFF_WIDTH), kv_dtype),
        ],
        compiler_params=_params(2),
        name="odd_in",
    )(*args)


def _diff_attn_kernel(*refs, n_seg, post_scale):
    lam_ref, q_ref = refs[0], refs[1]
    g_ref, o_ref = refs[-2], refs[-1]
    q = q_ref[...].astype(F32)
    lane = lax.broadcasted_iota(jnp.int32, q.shape, 1)
    q0 = jnp.where(lane < DIFF_HD, q, 0.0).astype(BF16)
    q1 = jnp.where(lane >= DIFF_HD, q, 0.0).astype(BF16)
    scale = DIFF_HD ** -0.5
    s0, s1, vs = [], [], []
    for s in range(n_seg):
        k = refs[2 + 2 * s][...].astype(BF16)
        vs.append(refs[3 + 2 * s][...].astype(BF16))
        s0.append(_dot_t(q0, k) * scale)
        s1.append(_dot_t(q1, k) * scale)
    p0 = _softmax_parts(s0)
    p1 = _softmax_parts(s1)
    lam = lam_ref[0]
    o = None
    for s in range(n_seg):
        part = _dot((p0[s] - lam * p1[s]).astype(BF16), vs[s])
        o = part if o is None else o + part
    o_ref[...] = (_rms_rows(o, g_ref[...]) * post_scale).astype(o_ref.dtype)


def _diff_attn(lam_full, q, segs, g_sub, post_scale, bsz, length):
    nq = length // TQ
    in_specs = [pl.BlockSpec(memory_space=pltpu.SMEM),
                pl.BlockSpec((TQ, HEAD_LANES), lambda b, h, i: (b * nq + i, h))]
    args = [lam_full.reshape(1).astype(F32), q]
    for k, v, lk in segs:
        in_specs += [pl.BlockSpec((lk, HEAD_LANES), lambda b, h, i: (b, h))] * 2
        args += [k, v]
    in_specs.append(pl.BlockSpec((1, HEAD_LANES), lambda b, h, i: (0, 0)))
    args.append(g_sub.reshape(1, HEAD_LANES))
    return pl.pallas_call(
        functools.partial(_diff_attn_kernel, n_seg=len(segs), post_scale=post_scale),
        grid=(bsz, DIFF_HEADS, nq),
        in_specs=in_specs,
        out_specs=pl.BlockSpec((TQ, HEAD_LANES), lambda b, h, i: (b * nq + i, h)),
        out_shape=jax.ShapeDtypeStruct((bsz * length, DIFF_WIDTH), BF16),
        compiler_params=_params(3),
        name="diff_attn",
    )(*args)


def _odd_out_kernel(o_ref, x_ref, mod_ref, w_ref, out_ref):
    out_ref[...] = x_ref[...] + mod_ref[0, 2:3, :] * _dot(o_ref[...], w_ref[...])


def _odd_out(x_all, grp, mod_tab, o, w_out):
    bsz, nt = grp["bsz"], grp["nt"]

    def xrow(b, t):
        return (grp["tile0"] + b * nt + t, 0)

    return pl.pallas_call(
        _odd_out_kernel,
        grid=(bsz, nt),
        in_specs=[
            pl.BlockSpec((TQ, DIFF_WIDTH), lambda b, t: (b * nt + t, 0)),
            pl.BlockSpec((TQ, D_MODEL), xrow),
            pl.BlockSpec((1, 6, D_MODEL), lambda b, t: (grp["mod_base"] + b * grp["mod_stride"], 0, 0)),
            pl.BlockSpec(w_out.shape, lambda b, t: (0, 0)),
        ],
        out_specs=pl.BlockSpec((TQ, D_MODEL), xrow),
        out_shape=jax.ShapeDtypeStruct(x_all.shape, F32),
        input_output_aliases={1: 0},
        compiler_params=_params(2),
        name="odd_out",
    )(o, x_all, mod_tab, w_out)


def _final_norm_kernel(x_ref, g_ref, o_ref):
    o_ref[...] = _rms_rows(x_ref[...], g_ref[...])


def _final_norm(x_all, g, row0, rows):
    tm = 512
    return pl.pallas_call(
        _final_norm_kernel,
        grid=(rows // tm,),
        in_specs=[pl.BlockSpec((tm, D_MODEL), lambda i: (row0 // tm + i, 0)),
                  pl.BlockSpec((1, D_MODEL), lambda i: (0, 0))],
        out_specs=pl.BlockSpec((tm, D_MODEL), lambda i: (i, 0)),
        out_shape=jax.ShapeDtypeStruct((rows, D_MODEL), F32),
        compiler_params=_params(1),
        name="final_norm",
    )(x_all, g.reshape(1, D_MODEL))


def _pad_head_lanes(x, lead):
    return jnp.pad(x, ((0, 0), (lead, HEAD_LANES - lead - x.shape[1])))


def _even_weights(w_in, w_out, w_uq, w_ukv, w_glu):
    o3 = S5_WIDTH + MLA_Q_LORA + MLA_KV_LORA
    w_kr = w_in[:, o3:]
    win_aug = jnp.concatenate(
        [w_in[:, :o3], _pad_head_lanes(w_kr, MLA_NOPE), _pad_head_lanes(_swap_pairs(w_kr), MLA_NOPE)], axis=1)
    dq = MLA_NOPE + MLA_ROPE
    plain, swapped = [], []
    for hd in range(MLA_HEADS):
        wn = w_uq[:, hd * dq:hd * dq + MLA_NOPE]
        wr = w_uq[:, hd * dq + MLA_NOPE:(hd + 1) * dq]
        plain.append(jnp.pad(jnp.concatenate([wn, wr], axis=1), ((0, 0), (0, HEAD_LANES - dq))))
        swapped.append(_pad_head_lanes(_swap_pairs(wr), MLA_NOPE))
    wuq2 = jnp.concatenate(plain + swapped, axis=1)
    w_mla = w_out[S5_WIDTH:].reshape(MLA_HEADS, MLA_V, D_MODEL)
    w_out_mla = jnp.pad(w_mla, ((0, 0), (HEAD_LANES - MLA_V, 0), (0, 0))).reshape(MLA_HEADS * HEAD_LANES, D_MODEL)
    return (win_aug.astype(BF16), wuq2.astype(BF16), w_ukv.astype(BF16), w_glu.astype(BF16),
            w_out[:S5_WIDTH].astype(BF16), w_out_mla.astype(BF16))


def _even_layer(x_all, grp, g1, mod_tab, ew, s5m, g_q, g_kv, d_skip, b_glu, h0, ctx, tables):
    win_aug, wuq2, w_ukv, w_glu, w_out_s5, w_out_mla = ew
    bmat, cmat, acoef = s5m
    bsz, length = grp["bsz"], grp["length"]
    u_tm, q, ckv, kr = _even_in(x_all, grp, g1, mod_tab, win_aug, g_q, wuq2, g_kv, tables)
    y_dir, fin = _s5_scan(u_tm.reshape(length * bsz, S5_WIDTH), bmat, cmat, acoef, h0, bsz)
    segs = [(_kv_expand(ckv, w_ukv), kr, length)]
    if ctx is not None:
        ctx_ckv, ctx_kr = ctx
        segs.append((_kv_expand(ctx_ckv, w_ukv), ctx_kr, PAST_LEN))
    o_mla = _mla_attn(q, segs, bsz, length)
    x_all = _even_out(x_all, grp, mod_tab, u_tm, y_dir, o_mla, d_skip, w_glu, b_glu, w_out_s5, w_out_mla)
    return x_all, fin, ckv, kr


def _odd_layer(x_all, grp, g1, mod_tab, w_aug, w_out, lam_full, g_sub, post_scale, ctx, tables, kv_dtype):
    bsz, length = grp["bsz"], grp["length"]
    q, k, v = _odd_in(x_all, grp, g1, mod_tab, w_aug, tables, kv_dtype)
    segs = [(k, v, length)]
    if ctx is not None:
        segs.append((ctx[0], ctx[1], PAST_LEN))
    o = _diff_attn(lam_full, q, segs, g_sub, post_scale, bsz, length)
    return _odd_out(x_all, grp, mod_tab, o, w_out), k, v


def kernel(x_prompt, x_sample, state_s5, cache_mla, cache_diff_k, cache_diff_v, c, c_ctx, w_mod, b_mod, g_norm1, g_norm2, g_final, w_in_even, w_out_even, s5_lam_re, s5_lam_im, s5_log_dt, s5_b_re, s5_b_im, s5_c_re, s5_c_im, s5_d, s5_w_glu, s5_b_glu, mla_g_q, mla_w_uq, mla_g_kv, mla_w_ukv, w_in_odd, w_out_odd, diff_lam, diff_g_sub, w_router, b_router, w_gate_up, b_gate_up, w_down, b_down):
    tab_mla = _rope_tables(DEC_SEQ, MLA_ROPE, MLA_NOPE, 1)
    tab_diff = _rope_tables(DEC_SEQ, DIFF_HD, 0, 2)
    grp_p = _group(BATCH, SEQ, 0, 0, 0)
    grp_s = _group(DEC_BATCH, DEC_SEQ, N_PROMPT, 1, 1)
    x_all = jnp.concatenate([x_prompt.reshape(N_PROMPT, D_MODEL), x_sample.reshape(N_SAMPLE, D_MODEL)], axis=0)
    cond = jax.nn.silu(jnp.concatenate([c_ctx[None], c], axis=0))
    new_s5, new_mla, new_k, new_v = [], [], [], []
    for l in range(DEPTH):
        mod_tab = (cond @ w_mod[l] + b_mod[l]).reshape(1 + DEC_BATCH, 6, D_MODEL)
        i = l // 2
        if l % 2 == 0:
            ew = _even_weights(w_in_even[i], w_out_even[i], mla_w_uq[i], mla_w_ukv[i], s5_w_glu[i])
            s5m = _s5_discretize(s5_lam_re[i], s5_lam_im[i], s5_log_dt[i], s5_b_re[i], s5_b_im[i],
                                 s5_c_re[i], s5_c_im[i])
            common = (ew, s5m, mla_g_q[i], mla_g_kv[i], s5_d[i], s5_b_glu[i])
            h0_p = jnp.zeros((2, 2, BATCH, 2 * S5_HALF_STATES), F32)
            x_all, fin, ckv, kr = _even_layer(x_all, grp_p, g_norm1[l], mod_tab, *common, h0_p, None, None)
            new_s5.append(_s5_state_from_kernel(fin))
            new_mla.append(jnp.concatenate([ckv, kr[:, MLA_NOPE:MLA_NOPE + MLA_ROPE]], axis=1)
                           .reshape(BATCH, SEQ, MLA_KV_LORA + MLA_ROPE))
            lat_ctx = cache_mla[:, i].astype(F32).reshape(DEC_BATCH * PAST_LEN, MLA_KV_LORA + MLA_ROPE)
            ctx = (lat_ctx[:, :MLA_KV_LORA], _pad_head_lanes(lat_ctx[:, MLA_KV_LORA:], MLA_NOPE))
            x_all, _, _, _ = _even_layer(x_all, grp_s, g_norm1[l], mod_tab, *common,
                                         _s5_state_to_kernel(state_s5[:, i]), ctx, tab_mla)
        else:
            lam_init = 0.8 - 0.6 * math.exp(-0.3 * l)
            lamf = diff_lam[i].astype(F32)
            lam_full = jnp.exp(jnp.sum(lamf[0] * lamf[1])) - jnp.exp(jnp.sum(lamf[2] * lamf[3])) + lam_init
            w_qk = w_in_odd[i][:, :2 * DIFF_WIDTH]
            w_aug = jnp.concatenate([w_in_odd[i], _swap_pairs(w_qk)], axis=1).astype(BF16)
            w_out = w_out_odd[i].astype(BF16)
            odd = (w_aug, w_out, lam_full, diff_g_sub[i], 1.0 - lam_init)
            x_all, kp, vp = _odd_layer(x_all, grp_p, g_norm1[l], mod_tab, *odd, None, None, F32)
            new_k.append(kp.reshape(BATCH, SEQ, DIFF_HEADS, 2, DIFF_HD))
            new_v.append(vp.reshape(BATCH, SEQ, DIFF_HEADS, 2 * DIFF_HD))
            ctx = (cache_diff_k[:, i].reshape(DEC_BATCH * PAST_LEN, DIFF_WIDTH),
                   cache_diff_v[:, i].reshape(DEC_BATCH * PAST_LEN, DIFF_WIDTH))
            x_all, _, _ = _odd_layer(x_all, grp_s, g_norm1[l], mod_tab, *odd, ctx, tab_diff, BF16)
        x_all = _moe_layer(l, x_all, mod_tab, g_norm2, w_router, b_router,
                           w_gate_up, b_gate_up, w_down, b_down)
    y_prompt = _final_norm(x_all, g_final, 0, N_PROMPT)
    y_sample = _final_norm(x_all, g_final, N_PROMPT, N_SAMPLE)
    return (y_prompt.reshape(BATCH, SEQ, D_MODEL), y_sample.reshape(DEC_BATCH, DEC_SEQ, D_MODEL),
            jnp.stack(new_s5, axis=1), jnp.stack(new_mla, axis=1),
            jnp.stack(new_k, axis=1), jnp.stack(new_v, axis=1))
```

```python
import functools
import math

import jax
import jax.numpy as jnp
from jax import lax
from jax.experimental import pallas as pl
from jax.experimental.pallas import tpu as pltpu

D_MODEL = 1024
BATCH = 16
SEQ = 256
DEPTH = 4
DEC_BATCH = 8
DEC_SEQ = 1024
PAST_LEN = 512
GRID_W = 64
N_EVEN = (DEPTH + 1) // 2
N_ODD = DEPTH // 2
S5_WIDTH = D_MODEL // 2
S5_GROUP = 16
S5_GROUPS = S5_WIDTH // S5_GROUP
S5_STATE = 64
MLA_HEADS = 8
MLA_NOPE = 64
MLA_ROPE = 32
MLA_V = 64
MLA_Q_LORA = D_MODEL // 4
MLA_KV_LORA = D_MODEL // 8
MLA_WIDTH = MLA_HEADS * MLA_V
EVEN_IN = S5_WIDTH + MLA_Q_LORA + MLA_KV_LORA + MLA_ROPE
EVEN_OUT = S5_WIDTH + MLA_WIDTH
DIFF_HEADS = 8
DIFF_HD = D_MODEL // (2 * DIFF_HEADS)
DIFF_WIDTH = DIFF_HEADS * 2 * DIFF_HD
N_EXPERTS = 32
TOP_K = 4
D_FF = D_MODEL
SWIGLU_LIMIT = 7.0
SWIGLU_ALPHA = 1.702
ROPE_THETA = 10000.0
Q_BLOCK = 128
EPS = 1e-6

N_PROMPT = BATCH * SEQ
N_SAMPLE = DEC_BATCH * DEC_SEQ
N_TOK = N_PROMPT + N_SAMPLE

LANES = 128
VMEM_LIMIT_BYTES = 56 * 1024 * 1024

TM = 256
N_TILES = N_TOK // TM
R_TILES = N_TOK * TOP_K // TM + N_EXPERTS
R_MAX = R_TILES * TM

F32 = jnp.float32
BF16 = jnp.bfloat16


def _mod_row(i):
    t0 = i * TM
    return jnp.where(t0 < N_PROMPT, 0, 1 + (t0 - N_PROMPT) // DEC_SEQ)


def _router_kernel(x_ref, g_ref, mod_ref, wr_ref, br_ref,
                   h_ref, topi_ref, gate_ref, rank_ref, counts_ref, carry_ref):
    i = pl.program_id(0)

    @pl.when(i == 0)
    def _():
        carry_ref[...] = jnp.zeros_like(carry_ref)

    x = x_ref[...]
    ms = jnp.mean(x * x, axis=-1, keepdims=True)
    y = x * lax.rsqrt(ms + EPS) * g_ref[...]
    shift = mod_ref[0, 3:4, :]
    scale = mod_ref[0, 4:5, :]
    h = y * (1.0 + scale) + shift
    h_ref[...] = h

    logits = jnp.dot(h, wr_ref[...], preferred_element_type=F32,
                     precision=lax.Precision.HIGHEST) + br_ref[...]
    lane_e = lax.broadcasted_iota(jnp.int32, logits.shape, 1)
    work = logits
    vals, idxs = [], []
    sel = jnp.zeros(logits.shape, F32)
    for _ in range(TOP_K):
        m = jnp.max(work, axis=-1, keepdims=True)
        idx = jnp.min(jnp.where(work == m, lane_e, N_EXPERTS), axis=-1, keepdims=True)
        hit = lane_e == idx
        vals.append(m)
        idxs.append(idx)
        sel = jnp.where(hit, 1.0, sel)
        work = jnp.where(hit, -jnp.inf, work)
    es = [jnp.exp(v - vals[0]) for v in vals]
    den = es[0] + es[1] + es[2] + es[3]

    row = lax.broadcasted_iota(jnp.int32, (TM, TM), 0)
    col = lax.broadcasted_iota(jnp.int32, (TM, TM), 1)
    tri = jnp.where(col < row, 1.0, 0.0).astype(BF16)
    before = jnp.dot(tri, sel.astype(BF16), preferred_element_type=F32) + carry_ref[...]
    carry_ref[...] += jnp.sum(sel, axis=0, keepdims=True)
    counts_ref[...] = carry_ref[...].astype(jnp.int32)

    lane = lax.broadcasted_iota(jnp.int32, (TM, LANES), 1)
    topi = jnp.zeros((TM, LANES), jnp.int32)
    gate = jnp.zeros((TM, LANES), F32)
    rank = jnp.zeros((TM, LANES), jnp.int32)
    for k in range(TOP_K):
        rk = jnp.sum(jnp.where(lane_e == idxs[k], before, 0.0), axis=-1, keepdims=True)
        topi = jnp.where(lane == k, idxs[k], topi)
        gate = jnp.where(lane == k, es[k] / den, gate)
        rank = jnp.where(lane == k, rk.astype(jnp.int32), rank)
    topi_ref[...] = topi
    gate_ref[...] = gate
    rank_ref[...] = rank


def _router(x_all, g, mod_tab, w_router, b_router):
    return pl.pallas_call(
        _router_kernel,
        grid=(N_TILES,),
        in_specs=[
            pl.BlockSpec((TM, D_MODEL), lambda i: (i, 0)),
            pl.BlockSpec((1, D_MODEL), lambda i: (0, 0)),
            pl.BlockSpec((1, 6, D_MODEL), lambda i: (_mod_row(i), 0, 0)),
            pl.BlockSpec((D_MODEL, N_EXPERTS), lambda i: (0, 0)),
            pl.BlockSpec((1, N_EXPERTS), lambda i: (0, 0)),
        ],
        out_specs=[
            pl.BlockSpec((TM, D_MODEL), lambda i: (i, 0)),
            pl.BlockSpec((TM, LANES), lambda i: (i, 0)),
            pl.BlockSpec((TM, LANES), lambda i: (i, 0)),
            pl.BlockSpec((TM, LANES), lambda i: (i, 0)),
            pl.BlockSpec((1, N_EXPERTS), lambda i: (0, 0)),
        ],
        out_shape=[
            jax.ShapeDtypeStruct((N_TOK, D_MODEL), F32),
            jax.ShapeDtypeStruct((N_TOK, LANES), jnp.int32),
            jax.ShapeDtypeStruct((N_TOK, LANES), F32),
            jax.ShapeDtypeStruct((N_TOK, LANES), jnp.int32),
            jax.ShapeDtypeStruct((1, N_EXPERTS), jnp.int32),
        ],
        scratch_shapes=[pltpu.VMEM((1, N_EXPERTS), F32)],
        compiler_params=pltpu.CompilerParams(
            dimension_semantics=("arbitrary",), vmem_limit_bytes=VMEM_LIMIT_BYTES),
        name="moe_router",
    )(x_all, g.reshape(1, D_MODEL), mod_tab, w_router, b_router.reshape(1, N_EXPERTS))


ISSUE_UNROLL = 4


def _dispatch_kernel(ends_ref, pos_ref, h_ref, xs_ref, zero_buf, pos_smem, sem_idx, sem, sem_zero):
    i = pl.program_id(0)

    @pl.when(i == 0)
    def _():
        zero_buf[...] = jnp.zeros_like(zero_buf)

        for wait in (False, True):
            for e in range(N_EXPERTS):
                start = ends_ref[e - 1] if e > 0 else 0

                @pl.when(ends_ref[e] > start)
                def _(e=e, wait=wait):
                    last = pl.multiple_of(ends_ref[e] - TM, TM)
                    cp = pltpu.make_async_copy(zero_buf, xs_ref.at[pl.ds(last, TM)], sem_zero)
                    if wait:
                        cp.wait()
                    else:
                        cp.start()

            def tail(t, carry, wait=wait):
                cp = pltpu.make_async_copy(zero_buf, xs_ref.at[pl.ds(pl.multiple_of(t * TM, TM), TM)], sem_zero)
                if wait:
                    cp.wait()
                else:
                    cp.start()
                return carry

            lax.fori_loop(ends_ref[N_EXPERTS - 1] // TM, R_TILES, tail, 0)

    cp = pltpu.make_async_copy(pos_ref.at[0, 0], pos_smem, sem_idx)
    cp.start()
    cp.wait()

    def issue(r, carry):
        for k in range(TOP_K):
            p = pos_smem[r * TOP_K + k]
            pltpu.make_async_copy(h_ref.at[pl.ds(r, 1)], xs_ref.at[pl.ds(p, 1)], sem.at[k]).start(priority=k % 2)
        return carry

    lax.fori_loop(0, TM, issue, 0, unroll=ISSUE_UNROLL)
    for k in range(TOP_K):
        pltpu.make_async_copy(h_ref, xs_ref.at[pl.ds(0, TM)], sem.at[k]).wait()


def _dispatch(ends, pos, h):
    grid_spec = pltpu.PrefetchScalarGridSpec(
        num_scalar_prefetch=1,
        grid=(N_TILES,),
        in_specs=[
            pl.BlockSpec((1, 1, TM * TOP_K), lambda i, ends: (i, 0, 0)),
            pl.BlockSpec((TM, D_MODEL), lambda i, ends: (i, 0)),
        ],
        out_specs=pl.BlockSpec(memory_space=pl.ANY),
        scratch_shapes=[
            pltpu.VMEM((TM, D_MODEL), F32),
            pltpu.SMEM((TM * TOP_K,), jnp.int32),
            pltpu.SemaphoreType.DMA,
            pltpu.SemaphoreType.DMA((TOP_K,)),
            pltpu.SemaphoreType.DMA,
        ],
    )
    return pl.pallas_call(
        _dispatch_kernel,
        grid_spec=grid_spec,
        out_shape=jax.ShapeDtypeStruct((R_MAX, D_MODEL), F32),
        compiler_params=pltpu.CompilerParams(
            dimension_semantics=("arbitrary",), vmem_limit_bytes=VMEM_LIMIT_BYTES),
        name="moe_dispatch",
    )(ends, pos.reshape(N_TILES, 1, TM * TOP_K), h)


def _ffn_kernel(te_ref, nu_ref, xs_ref, wgu_ref, bgu_ref, wd_ref, bd_ref, ys_ref, wgu_bf, wd_bf):
    i = pl.program_id(0)

    @pl.when(i < nu_ref[0])
    def _():
        prev = te_ref[jnp.maximum(i - 1, 0)]
        new_expert = jnp.logical_or(i == 0, te_ref[i] != prev)

        @pl.when(new_expert)
        def _():
            wgu_bf[...] = wgu_ref[...].astype(BF16)
            wd_bf[...] = wd_ref[...].astype(BF16)

        x = xs_ref[...].astype(BF16)
        gu = jnp.dot(x, wgu_bf[...], preferred_element_type=F32) + bgu_ref[...]
        g = jnp.minimum(gu[:, :D_FF], SWIGLU_LIMIT)
        u = jnp.clip(gu[:, D_FF:], -SWIGLU_LIMIT, SWIGLU_LIMIT)
        act = g * jax.nn.sigmoid(SWIGLU_ALPHA * g) * (u + 1.0)
        ys_ref[...] = jnp.dot(act.astype(BF16), wd_bf[...], preferred_element_type=F32) + bd_ref[...]

    @pl.when(i >= nu_ref[0])
    def _():
        ys_ref[...] = jnp.zeros_like(ys_ref)


def _ffn(layer, tile_expert, n_used, xs, w_gate_up, b_gate_up, w_down, b_down):
    def row_map(i, te, nu):
        return (jnp.maximum(jnp.minimum(i, nu[0] - 1), 0), 0)

    def w_map(i, te, nu):
        return (layer, te[i], 0, 0)

    grid_spec = pltpu.PrefetchScalarGridSpec(
        num_scalar_prefetch=2,
        grid=(R_TILES,),
        in_specs=[
            pl.BlockSpec((TM, D_MODEL), row_map),
            pl.BlockSpec((None, None, D_MODEL, 2 * D_FF), w_map),
            pl.BlockSpec((None, None, 1, 2 * D_FF), w_map),
            pl.BlockSpec((None, None, D_FF, D_MODEL), w_map),
            pl.BlockSpec((None, None, 1, D_MODEL), w_map),
        ],
        out_specs=pl.BlockSpec((TM, D_MODEL), lambda i, te, nu: (i, 0)),
        scratch_shapes=[
            pltpu.VMEM((D_MODEL, 2 * D_FF), BF16),
            pltpu.VMEM((D_FF, D_MODEL), BF16),
        ],
    )
    return pl.pallas_call(
        _ffn_kernel,
        grid_spec=grid_spec,
        out_shape=jax.ShapeDtypeStruct((R_MAX, D_MODEL), F32),
        compiler_params=pltpu.CompilerParams(
            dimension_semantics=("arbitrary",), vmem_limit_bytes=VMEM_LIMIT_BYTES),
        name="moe_ffn",
    )(tile_expert, n_used, xs, w_gate_up,
      b_gate_up.reshape(DEPTH, N_EXPERTS, 1, 2 * D_FF), w_down,
      b_down.reshape(DEPTH, N_EXPERTS, 1, D_MODEL))


def _combine_kernel(pos_ref, ys_ref, x_ref, gate_ref, mod_ref, out_ref, buf, pos_smem, sem_idx, sem):
    cp = pltpu.make_async_copy(pos_ref.at[0, 0], pos_smem, sem_idx)
    cp.start()
    cp.wait()

    def issue(r, carry):
        for k in range(TOP_K):
            p = pos_smem[r * TOP_K + k]
            pltpu.make_async_copy(ys_ref.at[pl.ds(p, 1)], buf.at[k, pl.ds(r, 1)], sem.at[k]).start(priority=k % 2)
        return carry

    lax.fori_loop(0, TM, issue, 0, unroll=ISSUE_UNROLL)
    acc = jnp.zeros((TM, D_MODEL), F32)
    for k in range(TOP_K):
        pltpu.make_async_copy(ys_ref.at[pl.ds(0, TM)], buf.at[k], sem.at[k]).wait()
        acc = acc + gate_ref[:, k:k + 1] * buf[k]
    out_ref[...] = x_ref[...] + mod_ref[0, 5:6, :] * acc


def _combine(pos, ys, x_all, gate, mod_tab):
    return pl.pallas_call(
        _combine_kernel,
        grid=(N_TILES,),
        in_specs=[
            pl.BlockSpec((1, 1, TM * TOP_K), lambda i: (i, 0, 0)),
            pl.BlockSpec(memory_space=pl.ANY),
            pl.BlockSpec((TM, D_MODEL), lambda i: (i, 0)),
            pl.BlockSpec((TM, LANES), lambda i: (i, 0)),
            pl.BlockSpec((1, 6, D_MODEL), lambda i: (_mod_row(i), 0, 0)),
        ],
        out_specs=pl.BlockSpec((TM, D_MODEL), lambda i: (i, 0)),
        out_shape=jax.ShapeDtypeStruct((N_TOK, D_MODEL), F32),
        scratch_shapes=[
            pltpu.VMEM((TOP_K, TM, D_MODEL), F32),
            pltpu.SMEM((TM * TOP_K,), jnp.int32),
            pltpu.SemaphoreType.DMA,
            pltpu.SemaphoreType.DMA((TOP_K,)),
        ],
        compiler_params=pltpu.CompilerParams(
            dimension_semantics=("arbitrary",), vmem_limit_bytes=VMEM_LIMIT_BYTES),
        name="moe_combine",
    )(pos.reshape(N_TILES, 1, TM * TOP_K), ys, x_all, gate, mod_tab)


def _moe_layer(layer, x_all, mod_tab, g_norm2, w_router, b_router, w_gate_up, b_gate_up, w_down, b_down):
    h, topi, gate, rank, counts = _router(x_all, g_norm2[layer], mod_tab, w_router[layer], b_router[layer])
    counts = counts[0]
    padded = ((counts + TM - 1) // TM) * TM
    ends = jnp.cumsum(padded)
    starts = ends - padded
    topi4 = topi[:, :TOP_K]
    pos = (starts[topi4] + rank[:, :TOP_K]).astype(jnp.int32)
    n_used = (ends[-1] // TM).astype(jnp.int32)
    tile_start = jnp.arange(R_TILES, dtype=jnp.int32) * TM
    tile_start = jnp.minimum(tile_start, ends[-1] - 1)
    tile_expert = jnp.sum((ends[None, :] <= tile_start[:, None]).astype(jnp.int32), axis=1)
    tile_expert = jnp.minimum(tile_expert, N_EXPERTS - 1).astype(jnp.int32)
    xs = _dispatch(ends.astype(jnp.int32), pos, h)
    ys = _ffn(layer, tile_expert, n_used.reshape(1), xs, w_gate_up, b_gate_up, w_down, b_down)
    return _combine(pos, ys, x_all, gate, mod_tab)


S5_ROWS = 512
S5_HALF_W = S5_WIDTH // 2
S5_HALF_STATES = (S5_GROUPS // 2) * S5_STATE
S5_COL_CHUNK = 512


def _s5_scan_kernel(u_ref, bmat_ref, cmat_ref, a_ref, h0_ref, y_ref, fin_ref, bu_ref, h_ref, *, bsz, steps):
    d = pl.program_id(0)
    c = pl.program_id(1)
    hs = S5_HALF_STATES

    @pl.when(c == 0)
    def _():
        h_ref[...] = h0_ref[...]

    u = u_ref[...].astype(BF16)
    for hf in range(2):
        bu_ref[...] = jnp.dot(u[:, hf * S5_HALF_W:(hf + 1) * S5_HALF_W], bmat_ref[hf],
                              preferred_element_type=F32)
        for j in range(hs // S5_COL_CHUNK):
            re0 = j * S5_COL_CHUNK
            im0 = hs + j * S5_COL_CHUNK
            ar = jnp.broadcast_to(a_ref[hf, 0:1, re0:re0 + S5_COL_CHUNK], (bsz, S5_COL_CHUNK))
            ai = jnp.broadcast_to(a_ref[hf, 1:2, re0:re0 + S5_COL_CHUNK], (bsz, S5_COL_CHUNK))

            def step(t, carry, re0=re0, im0=im0, ar=ar, ai=ai):
                hr, hi = carry
                te = jnp.where(d == 0, t, steps - 1 - t)
                r0 = pl.multiple_of(te * bsz, bsz)
                br = bu_ref[pl.ds(r0, bsz), re0:re0 + S5_COL_CHUNK]
                bi = bu_ref[pl.ds(r0, bsz), im0:im0 + S5_COL_CHUNK]
                nr = ar * hr - ai * hi + br
                ni = ar * hi + ai * hr + bi
                bu_ref[pl.ds(r0, bsz), re0:re0 + S5_COL_CHUNK] = nr
                bu_ref[pl.ds(r0, bsz), im0:im0 + S5_COL_CHUNK] = ni
                return nr, ni

            hr, hi = lax.fori_loop(
                0, steps, step,
                (h_ref[hf, :, re0:re0 + S5_COL_CHUNK], h_ref[hf, :, im0:im0 + S5_COL_CHUNK]), unroll=4)
            h_ref[hf, :, re0:re0 + S5_COL_CHUNK] = hr
            h_ref[hf, :, im0:im0 + S5_COL_CHUNK] = hi
        y_ref[:, hf * S5_HALF_W:(hf + 1) * S5_HALF_W] = jnp.dot(
            bu_ref[...].astype(BF16), cmat_ref[hf], preferred_element_type=F32)

    @pl.when(c == pl.num_programs(1) - 1)
    def _():
        fin_ref[...] = h_ref[...]


def _s5_scan(u_tm, bmat, cmat, acoef, h0, bsz):
    rows = u_tm.shape[0]
    steps = S5_ROWS // bsz
    n_chunks = rows // S5_ROWS

    def chunk_map(d, c):
        return jnp.where(d == 0, c, n_chunks - 1 - c)

    return pl.pallas_call(
        functools.partial(_s5_scan_kernel, bsz=bsz, steps=steps),
        grid=(2, n_chunks),
        in_specs=[
            pl.BlockSpec((S5_ROWS, S5_WIDTH), lambda d, c: (chunk_map(d, c), 0)),
            pl.BlockSpec((None, 2, S5_HALF_W, 2 * S5_HALF_STATES), lambda d, c: (d, 0, 0, 0)),
            pl.BlockSpec((None, 2, 2 * S5_HALF_STATES, S5_HALF_W), lambda d, c: (d, 0, 0, 0)),
            pl.BlockSpec((None, 2, 2, S5_HALF_STATES), lambda d, c: (d, 0, 0, 0)),
            pl.BlockSpec((None, 2, bsz, 2 * S5_HALF_STATES), lambda d, c: (d, 0, 0, 0)),
        ],
        out_specs=[
            pl.BlockSpec((None, S5_ROWS, S5_WIDTH), lambda d, c: (d, chunk_map(d, c), 0)),
            pl.BlockSpec((None, 2, bsz, 2 * S5_HALF_STATES), lambda d, c: (d, 0, 0, 0)),
        ],
        out_shape=[
            jax.ShapeDtypeStruct((2, rows, S5_WIDTH), F32),
            jax.ShapeDtypeStruct((2, 2, bsz, 2 * S5_HALF_STATES), F32),
        ],
        scratch_shapes=[
            pltpu.VMEM((S5_ROWS, 2 * S5_HALF_STATES), F32),
            pltpu.VMEM((2, bsz, 2 * S5_HALF_STATES), F32),
        ],
        compiler_params=pltpu.CompilerParams(
            dimension_semantics=("arbitrary", "arbitrary"), vmem_limit_bytes=VMEM_LIMIT_BYTES),
        name="s5_scan",
    )(u_tm, bmat, cmat, acoef, h0)


def _s5_discretize(lam_re, lam_im, log_dt, b_re, b_im, c_re, c_im):
    eye = jnp.eye(S5_GROUPS // 2, dtype=F32)
    bmats, cmats, acoefs = [], [], []
    for dr in range(2):
        lr = jnp.minimum(lam_re[dr].astype(F32), -1e-4)
        li = lam_im[dr].astype(F32)
        dt = jnp.exp(log_dt[dr].astype(F32))[:, None]
        mag = jnp.exp(lr * dt)
        ar, ai = mag * jnp.cos(li * dt), mag * jnp.sin(li * dt)
        den = lr * lr + li * li
        fr = ((ar - 1.0) * lr + ai * li) / den
        fi = (ai * lr - (ar - 1.0) * li) / den
        br_ = b_re[dr].astype(F32)
        bi_ = b_im[dr].astype(F32)
        bbr = fr[..., None] * br_ - fi[..., None] * bi_
        bbi = fr[..., None] * bi_ + fi[..., None] * br_
        bm, cm, am = [], [], []
        for hf in range(2):
            g = slice(hf * S5_GROUPS // 2, (hf + 1) * S5_GROUPS // 2)

            def bdiag_in(w):
                return jnp.einsum('ab,aph->ahbp', eye, w[g]).reshape(S5_HALF_W, S5_HALF_STATES)

            def bdiag_out(w):
                return jnp.einsum('ab,ahp->apbh', eye, w[g]).reshape(S5_HALF_STATES, S5_HALF_W)

            bm.append(jnp.concatenate([bdiag_in(bbr), bdiag_in(bbi)], axis=1))
            cm.append(jnp.concatenate([bdiag_out(c_re[dr].astype(F32)),
                                       -bdiag_out(c_im[dr].astype(F32))], axis=0))
            am.append(jnp.stack([ar[g].reshape(-1), ai[g].reshape(-1)]))
        bmats.append(jnp.stack(bm))
        cmats.append(jnp.stack(cm))
        acoefs.append(jnp.stack(am))
    return jnp.stack(bmats).astype(BF16), jnp.stack(cmats).astype(BF16), jnp.stack(acoefs)


def _s5_state_to_kernel(h0):
    bsz = h0.shape[0]
    h = h0.astype(F32).reshape(bsz, 2, 2, 2, S5_HALF_STATES)
    return h.transpose(1, 3, 0, 2, 4).reshape(2, 2, bsz, 2 * S5_HALF_STATES)


def _s5_state_from_kernel(fin):
    bsz = fin.shape[2]
    h = fin.reshape(2, 2, bsz, 2, S5_HALF_STATES).transpose(2, 0, 3, 1, 4)
    return h.reshape(bsz, 2, 2, S5_GROUPS, S5_STATE)


TQ = 256
HEAD_LANES = 128
MLA_SCALE = (MLA_NOPE + MLA_ROPE) ** -0.5
DIFF_SCALE = DIFF_HD ** -0.5


def _dot(a, b):
    return jnp.dot(a, b, preferred_element_type=F32)


def _dot_t(a, b):
    return lax.dot_general(a, b, (((1,), (1,)), ((), ())), preferred_element_type=F32)


def _rms_rows(x, g):
    return x * lax.rsqrt(jnp.mean(x * x, axis=-1, keepdims=True) + EPS) * g


def _group(bsz, length, row0, mod_base, mod_stride):
    return dict(bsz=bsz, length=length, nt=length // TQ, tile0=row0 // TQ,
                mod_base=mod_base, mod_stride=mod_stride)


def _params(n_axes):
    return pltpu.CompilerParams(dimension_semantics=("arbitrary",) * n_axes,
                                vmem_limit_bytes=VMEM_LIMIT_BYTES)


def _axial_rope(length, dim):
    rows = length // GRID_W
    row = jnp.repeat(jnp.arange(rows, dtype=F32), GRID_W)
    col = jnp.tile(jnp.arange(GRID_W, dtype=F32), rows)
    n_freq = dim // 4
    inv = ROPE_THETA ** (-jnp.arange(n_freq, dtype=F32) / n_freq)
    ang = jnp.concatenate([row[:, None] * inv, col[:, None] * inv], axis=-1)
    return jnp.cos(ang), jnp.sin(ang)


def _rope_tables(length, dim, lead, reps):
    cos, sin = _axial_rope(length, dim)
    cos_r = jnp.repeat(cos, 2, axis=-1)
    sin_r = jnp.repeat(sin, 2, axis=-1) * jnp.tile(jnp.array([-1.0, 1.0], F32), dim // 2)
    part = HEAD_LANES // reps
    pad = ((0, 0), (lead, part - lead - dim))
    cos_t = jnp.tile(jnp.pad(cos_r, pad, constant_values=1.0), (1, reps))
    sin_t = jnp.tile(jnp.pad(sin_r, pad), (1, reps))
    return cos_t, sin_t


def _swap_pairs(w):
    return w[:, jnp.arange(w.shape[1]) ^ 1]


def _even_in_kernel(*refs, rope):
    if rope:
        (x_ref, g_ref, mod_ref, win_ref, gq_ref, wuq_ref, gkv_ref, cos_ref, sin_ref,
         u_ref, q_ref, ckv_ref, kr_ref) = refs
    else:
        (x_ref, g_ref, mod_ref, win_ref, gq_ref, wuq_ref, gkv_ref,
         u_ref, q_ref, ckv_ref, kr_ref) = refs
    o1 = S5_WIDTH
    o2 = o1 + MLA_Q_LORA
    o3 = o2 + MLA_KV_LORA
    o4 = o3 + HEAD_LANES
    n_in = o4 + HEAD_LANES if rope else o4
    n_q = MLA_HEADS * HEAD_LANES
    h = _rms_rows(x_ref[...], g_ref[...]) * (1.0 + mod_ref[0, 1:2, :]) + mod_ref[0, 0:1, :]
    z = _dot(h.astype(BF16), win_ref[:, :n_in])
    u_ref[...] = z[:, :o1]
    ckv_ref[...] = _rms_rows(z[:, o2:o3], gkv_ref[...])
    qn = _rms_rows(z[:, o1:o2], gq_ref[...]).astype(BF16)
    if rope:
        q2 = _dot(qn, wuq_ref[...])
        cos = cos_ref[...]
        sin = sin_ref[...]
        for hd in range(MLA_HEADS):
            a = hd * HEAD_LANES
            q_ref[:, a:a + HEAD_LANES] = ((q2[:, a:a + HEAD_LANES] * cos
                                           + q2[:, n_q + a:n_q + a + HEAD_LANES] * sin) * MLA_SCALE
                                          ).astype(q_ref.dtype)
        kr_ref[...] = z[:, o3:o4] * cos + z[:, o4:o4 + HEAD_LANES] * sin
    else:
        q_ref[...] = (_dot(qn, wuq_ref[:, :n_q]) * MLA_SCALE).astype(q_ref.dtype)
        kr_ref[...] = z[:, o3:o4]


def _even_in(x_all, grp, g1, mod_tab, win_aug, g_q, wuq2, g_kv, tables):
    bsz, length, nt = grp["bsz"], grp["length"], grp["nt"]
    rope = tables is not None
    rows = bsz * length

    def tok(b, t):
        return (b * nt + t, 0)

    in_specs = [
        pl.BlockSpec((TQ, D_MODEL), lambda b, t: (grp["tile0"] + b * nt + t, 0)),
        pl.BlockSpec((1, D_MODEL), lambda b, t: (0, 0)),
        pl.BlockSpec((1, 6, D_MODEL), lambda b, t: (grp["mod_base"] + b * grp["mod_stride"], 0, 0)),
        pl.BlockSpec(win_aug.shape, lambda b, t: (0, 0)),
        pl.BlockSpec((1, MLA_Q_LORA), lambda b, t: (0, 0)),
        pl.BlockSpec(wuq2.shape, lambda b, t: (0, 0)),
        pl.BlockSpec((1, MLA_KV_LORA), lambda b, t: (0, 0)),
    ]
    args = [x_all, g1.reshape(1, D_MODEL), mod_tab, win_aug, g_q.reshape(1, -1), wuq2, g_kv.reshape(1, -1)]
    if rope:
        in_specs += [pl.BlockSpec((TQ, HEAD_LANES), lambda b, t: (t, 0))] * 2
        args += list(tables)
    return pl.pallas_call(
        functools.partial(_even_in_kernel, rope=rope),
        grid=(bsz, nt),
        in_specs=in_specs,
        out_specs=[
            pl.BlockSpec((TQ, S5_WIDTH), lambda b, t: (t, b)),
            pl.BlockSpec((TQ, MLA_HEADS * HEAD_LANES), tok),
            pl.BlockSpec((TQ, MLA_KV_LORA), tok),
            pl.BlockSpec((TQ, HEAD_LANES), tok),
        ],
        out_shape=[
            jax.ShapeDtypeStruct((length, bsz * S5_WIDTH), F32),
            jax.ShapeDtypeStruct((rows, MLA_HEADS * HEAD_LANES), BF16),
            jax.ShapeDtypeStruct((rows, MLA_KV_LORA), F32),
            jax.ShapeDtypeStruct((rows, HEAD_LANES), F32),
        ],
        compiler_params=_params(2),
        name="even_in",
    )(*args)


def _kv_expand_kernel(x_ref, kr_ref, wk_ref, wv_ref, k_ref, v_ref):
    x = x_ref[...].astype(BF16)
    k = _dot(x, wk_ref[...])
    kr = kr_ref[...]
    for hd in range(MLA_HEADS):
        a = hd * HEAD_LANES
        k_ref[:, a:a + HEAD_LANES] = (k[:, a:a + HEAD_LANES] + kr).astype(k_ref.dtype)
    v_ref[...] = _dot(x, wv_ref[...]).astype(v_ref.dtype)


def _kv_expand(ckv, kr, wk, wv):
    rows = ckv.shape[0]
    tm = 512
    width = MLA_HEADS * HEAD_LANES
    return pl.pallas_call(
        _kv_expand_kernel,
        grid=(rows // tm,),
        in_specs=[pl.BlockSpec((tm, MLA_KV_LORA), lambda i: (i, 0)),
                  pl.BlockSpec((tm, HEAD_LANES), lambda i: (i, 0)),
                  pl.BlockSpec(wk.shape, lambda i: (0, 0)),
                  pl.BlockSpec(wv.shape, lambda i: (0, 0))],
        out_specs=[pl.BlockSpec((tm, width), lambda i: (i, 0))] * 2,
        out_shape=[jax.ShapeDtypeStruct((rows, width), BF16)] * 2,
        compiler_params=_params(1),
        name="kv_expand",
    )(ckv, kr, wk, wv)


def _exp_parts(scores):
    m = functools.reduce(jnp.maximum, [jnp.max(s, axis=-1, keepdims=True) for s in scores])
    es = [jnp.exp(s - m) for s in scores]
    den = functools.reduce(jnp.add, [jnp.sum(e, axis=-1, keepdims=True) for e in es])
    return es, 1.0 / den


def _weighted_values(es, vs):
    o = _dot(es[0].astype(BF16), vs[0])
    for e, v in zip(es[1:], vs[1:]):
        o = o + _dot(e.astype(BF16), v)
    return o


def _mla_attn_kernel(*refs, n_seg):
    q = refs[0][...]
    o_ref = refs[-1]
    ks = [refs[1 + 2 * s][...] for s in range(n_seg)]
    vs = [refs[2 + 2 * s][...] for s in range(n_seg)]
    es, inv = _exp_parts([_dot_t(q, k) for k in ks])
    o_ref[...] = (_weighted_values(es, vs) * inv).astype(o_ref.dtype)


def _mla_attn(q, segs, bsz, length):
    nq = length // TQ
    in_specs = [pl.BlockSpec((TQ, HEAD_LANES), lambda b, h, i: (b * nq + i, h))]
    args = [q]
    for k, v, lk in segs:
        in_specs += [pl.BlockSpec((lk, HEAD_LANES), lambda b, h, i: (b, h))] * 2
        args += [k, v]
    return pl.pallas_call(
        functools.partial(_mla_attn_kernel, n_seg=len(segs)),
        grid=(bsz, MLA_HEADS, nq),
        in_specs=in_specs,
        out_specs=pl.BlockSpec((TQ, HEAD_LANES), lambda b, h, i: (b * nq + i, h)),
        out_shape=jax.ShapeDtypeStruct((bsz * length, MLA_HEADS * HEAD_LANES), BF16),
        compiler_params=_params(3),
        name="mla_attn",
    )(*args)


def _even_out_kernel(u_ref, y_ref, o_ref, x_ref, mod_ref, d_ref, wglu_ref, bglu_ref, ws5_ref, wmla_ref,
                     out_ref):
    y =jax.nn.gelu(d_ref[...] * u_ref[...] + y_ref[0] + y_ref[1])
    s5 = y * jax.nn.sigmoid(_dot(y.astype(BF16), wglu_ref[...]) + bglu_ref[...])
    mix = _dot(s5.astype(BF16), ws5_ref[...]) + _dot(o_ref[...].astype(BF16), wmla_ref[...])
    out_ref[...] = x_ref[...] + mod_ref[0, 2:3, :] * mix


def _even_out(x_all, grp, mod_tab, u_tm, y_dir, o_mla, d_skip, w_glu, b_glu, w_out_s5, w_out_mla):
    bsz, length, nt = grp["bsz"], grp["length"], grp["nt"]

    def xrow(b, t):
        return (grp["tile0"] + b * nt + t, 0)

    full = lambda b, t: (0, 0)
    return pl.pallas_call(
        _even_out_kernel,
        grid=(bsz, nt),
        in_specs=[
            pl.BlockSpec((TQ, S5_WIDTH), lambda b, t: (t, b)),
            pl.BlockSpec((2, TQ, S5_WIDTH), lambda b, t: (0, t, b)),
            pl.BlockSpec((TQ, MLA_HEADS * HEAD_LANES), lambda b, t: (b * nt + t, 0)),
            pl.BlockSpec((TQ, D_MODEL), xrow),
            pl.BlockSpec((1, 6, D_MODEL), lambda b, t: (grp["mod_base"] + b * grp["mod_stride"], 0, 0)),
            pl.BlockSpec((1, S5_WIDTH), full),
            pl.BlockSpec(w_glu.shape, full),
            pl.BlockSpec((1, S5_WIDTH), full),
            pl.BlockSpec(w_out_s5.shape, full),
            pl.BlockSpec(w_out_mla.shape, full),
        ],
        out_specs=pl.BlockSpec((TQ, D_MODEL), xrow),
        out_shape=jax.ShapeDtypeStruct(x_all.shape, F32),
        input_output_aliases={3: 0},
        compiler_params=_params(2),
        name="even_out",
    )(u_tm, y_dir.reshape(2, length, bsz * S5_WIDTH), o_mla, x_all, mod_tab,
      d_skip.reshape(1, S5_WIDTH), w_glu, b_glu.reshape(1, S5_WIDTH), w_out_s5, w_out_mla)


def _odd_in_kernel(*refs, rope):
    if rope:
        x_ref, g_ref, mod_ref, w_ref, cos_ref, sin_ref, q_ref, k_ref, v_ref = refs
    else:
        x_ref, g_ref, mod_ref, w_ref, q_ref, k_ref, v_ref = refs
    w3 = 3 * DIFF_WIDTH
    h = _rms_rows(x_ref[...], g_ref[...]) * (1.0 + mod_ref[0, 1:2, :]) + mod_ref[0, 0:1, :]
    z = _dot(h.astype(BF16), w_ref[...] if rope else w_ref[:, :w3])
    v_ref[...] = z[:, 2 * DIFF_WIDTH:w3].astype(v_ref.dtype)
    if rope:
        cos = cos_ref[...]
        sin = sin_ref[...]
        for hd in range(DIFF_HEADS):
            a = hd * HEAD_LANES
            q_ref[:, a:a + HEAD_LANES] = ((z[:, a:a + HEAD_LANES] * cos
                                           + z[:, w3 + a:w3 + a + HEAD_LANES] * sin) * DIFF_SCALE
                                          ).astype(q_ref.dtype)
            b = DIFF_WIDTH + a
            k_ref[:, a:a + HEAD_LANES] = (z[:, b:b + HEAD_LANES] * cos
                                          + z[:, w3 + b:w3 + b + HEAD_LANES] * sin).astype(k_ref.dtype)
    else:
        q_ref[...] = (z[:, :DIFF_WIDTH] * DIFF_SCALE).astype(q_ref.dtype)
        k_ref[...] = z[:, DIFF_WIDTH:2 * DIFF_WIDTH].astype(k_ref.dtype)


def _odd_in(x_all, grp, g1, mod_tab, w_aug, tables, kv_dtype):
    bsz, length, nt = grp["bsz"], grp["length"], grp["nt"]
    rope = tables is not None
    rows = bsz * length

    def tok(b, t):
        return (b * nt + t, 0)

    in_specs = [
        pl.BlockSpec((TQ, D_MODEL), lambda b, t: (grp["tile0"] + b * nt + t, 0)),
        pl.BlockSpec((1, D_MODEL), lambda b, t: (0, 0)),
        pl.BlockSpec((1, 6, D_MODEL), lambda b, t: (grp["mod_base"] + b * grp["mod_stride"], 0, 0)),
        pl.BlockSpec(w_aug.shape, lambda b, t: (0, 0)),
    ]
    args = [x_all, g1.reshape(1, D_MODEL), mod_tab, w_aug]
    if rope:
        in_specs += [pl.BlockSpec((TQ, HEAD_LANES), lambda b, t: (t, 0))] * 2
        args += list(tables)
    return pl.pallas_call(
        functools.partial(_odd_in_kernel, rope=rope),
        grid=(bsz, nt),
        in_specs=in_specs,
        out_specs=[pl.BlockSpec((TQ, DIFF_WIDTH), tok)] * 3,
        out_shape=[
            jax.ShapeDtypeStruct((rows, DIFF_WIDTH), BF16),
            jax.ShapeDtypeStruct((rows, DIFF_WIDTH), kv_dtype),
            jax.ShapeDtypeStruct((rows, DIFF_WIDTH), kv_dtype),
        ],
        compiler_params=_params(2),
        name="odd_in",
    )(*args)


def _diff_attn_kernel(*refs, n_seg, post_scale):
    lam_ref, q_ref = refs[0], refs[1]
    g_ref, o_ref = refs[-2], refs[-1]
    q = q_ref[...].astype(F32)
    lane = lax.broadcasted_iota(jnp.int32, q.shape, 1)
    q0 = jnp.where(lane < DIFF_HD, q, 0.0).astype(BF16)
    q1 = jnp.where(lane >= DIFF_HD, q, 0.0).astype(BF16)
    ks = [refs[2 + 2 * s][...].astype(BF16) for s in range(n_seg)]
    vs = [refs[3 + 2 * s][...].astype(BF16) for s in range(n_seg)]
    e0, inv0 = _exp_parts([_dot_t(q0, k) for k in ks])
    e1, inv1 = _exp_parts([_dot_t(q1, k) for k in ks])
    o = _weighted_values(e0, vs) * inv0 - lam_ref[0] * (_weighted_values(e1, vs) * inv1)
    o_ref[...] = (_rms_rows(o, g_ref[...]) * post_scale).astype(o_ref.dtype)


def _diff_attn(lam_full, q, segs, g_sub, post_scale, bsz, length):
    nq = length // TQ
    in_specs = [pl.BlockSpec(memory_space=pltpu.SMEM),
                pl.BlockSpec((TQ, HEAD_LANES), lambda b, h, i: (b * nq + i, h))]
    args = [lam_full.reshape(1).astype(F32), q]
    for k, v, lk in segs:
        in_specs += [pl.BlockSpec((lk, HEAD_LANES), lambda b, h, i: (b, h))] * 2
        args += [k, v]
    in_specs.append(pl.BlockSpec((1, HEAD_LANES), lambda b, h, i: (0, 0)))
    args.append(g_sub.reshape(1, HEAD_LANES))
    return pl.pallas_call(
        functools.partial(_diff_attn_kernel, n_seg=len(segs), post_scale=post_scale),
        grid=(bsz, DIFF_HEADS, nq),
        in_specs=in_specs,
        out_specs=pl.BlockSpec((TQ, HEAD_LANES), lambda b, h, i: (b * nq + i, h)),
        out_shape=jax.ShapeDtypeStruct((bsz * length, DIFF_WIDTH), BF16),
        compiler_params=_params(3),
        name="diff_attn",
    )(*args)


def _odd_out_kernel(o_ref, x_ref, mod_ref, w_ref, out_ref):
    out_ref[...] = x_ref[...] + mod_ref[0, 2:3, :] * _dot(o_ref[...], w_ref[...])


def _odd_out(x_all, grp, mod_tab, o, w_out):
    bsz, nt = grp["bsz"], grp["nt"]

    def xrow(b, t):
        return (grp["tile0"] + b * nt + t, 0)

    return pl.pallas_call(
        _odd_out_kernel,
        grid=(bsz, nt),
        in_specs=[
            pl.BlockSpec((TQ, DIFF_WIDTH), lambda b, t: (b * nt + t, 0)),
            pl.BlockSpec((TQ, D_MODEL), xrow),
            pl.BlockSpec((1, 6, D_MODEL), lambda b, t: (grp["mod_base"] + b * grp["mod_stride"], 0, 0)),
            pl.BlockSpec(w_out.shape, lambda b, t: (0, 0)),
        ],
        out_specs=pl.BlockSpec((TQ, D_MODEL), xrow),
        out_shape=jax.ShapeDtypeStruct(x_all.shape, F32),
        input_output_aliases={1: 0},
        compiler_params=_params(2),
        name="odd_out",
    )(o, x_all, mod_tab, w_out)


def _final_norm_kernel(x_ref, g_ref, o_ref):
    o_ref[...] = _rms_rows(x_ref[...], g_ref[...])


def _final_norm(x_all, g, row0, rows):
    tm = 512
    return pl.pallas_call(
        _final_norm_kernel,
        grid=(rows // tm,),
        in_specs=[pl.BlockSpec((tm, D_MODEL), lambda i: (row0 // tm + i, 0)),
                  pl.BlockSpec((1, D_MODEL), lambda i: (0, 0))],
        out_specs=pl.BlockSpec((tm, D_MODEL), lambda i: (i, 0)),
        out_shape=jax.ShapeDtypeStruct((rows, D_MODEL), F32),
        compiler_params=_params(1),
        name="final_norm",
    )(x_all, g.reshape(1, D_MODEL))


def _pad_head_lanes(x, lead):
    return jnp.pad(x, ((0, 0), (lead, HEAD_LANES - lead - x.shape[1])))


def _even_weights(w_in, w_out, w_uq, w_ukv, w_glu):
    o3 = S5_WIDTH + MLA_Q_LORA + MLA_KV_LORA
    w_kr = w_in[:, o3:]
    win_aug = jnp.concatenate(
        [w_in[:, :o3], _pad_head_lanes(w_kr, MLA_NOPE), _pad_head_lanes(_swap_pairs(w_kr), MLA_NOPE)], axis=1)
    dq = MLA_NOPE + MLA_ROPE
    plain, swapped = [], []
    for hd in range(MLA_HEADS):
        wn = w_uq[:, hd * dq:hd * dq + MLA_NOPE]
        wr = w_uq[:, hd * dq + MLA_NOPE:(hd + 1) * dq]
        plain.append(jnp.pad(jnp.concatenate([wn, wr], axis=1), ((0, 0), (0, HEAD_LANES - dq))))
        swapped.append(_pad_head_lanes(_swap_pairs(wr), MLA_NOPE))
    wuq2 = jnp.concatenate(plain + swapped, axis=1)
    w_mla = w_out[S5_WIDTH:].reshape(MLA_HEADS, MLA_V, D_MODEL)
    w_out_mla = jnp.pad(w_mla, ((0, 0), (0, HEAD_LANES - MLA_V), (0, 0))).reshape(MLA_HEADS * HEAD_LANES, D_MODEL)
    w_kv = w_ukv.reshape(MLA_KV_LORA, MLA_HEADS, MLA_NOPE + MLA_V)
    wk = jnp.pad(w_kv[:, :, :MLA_NOPE], ((0, 0), (0, 0), (0, HEAD_LANES - MLA_NOPE)))
    wv = jnp.pad(w_kv[:, :, MLA_NOPE:], ((0, 0), (0, 0), (0, HEAD_LANES - MLA_V)))
    w_kv = (wk.reshape(MLA_KV_LORA, -1).astype(BF16), wv.reshape(MLA_KV_LORA, -1).astype(BF16))
    return (win_aug.astype(BF16), wuq2.astype(BF16), w_kv, w_glu.astype(BF16),
            w_out[:S5_WIDTH].astype(BF16), w_out_mla.astype(BF16))


def _even_layer(x_all, grp, g1, mod_tab, ew, s5m, g_q, g_kv, d_skip, b_glu, h0, ctx, tables):
    win_aug, wuq2, w_kv, w_glu, w_out_s5, w_out_mla = ew
    bmat, cmat, acoef = s5m
    bsz, length = grp["bsz"], grp["length"]
    u_tm, q, ckv, kr = _even_in(x_all, grp, g1, mod_tab, win_aug, g_q, wuq2, g_kv, tables)
    y_dir, fin = _s5_scan(u_tm.reshape(length * bsz, S5_WIDTH), bmat, cmat, acoef, h0, bsz)
    segs = [(*_kv_expand(ckv, kr, *w_kv), length)]
    if ctx is not None:
        segs.append((*_kv_expand(*ctx, *w_kv), PAST_LEN))
    o_mla = _mla_attn(q, segs, bsz, length)
    x_all = _even_out(x_all, grp, mod_tab, u_tm, y_dir, o_mla, d_skip, w_glu, b_glu, w_out_s5, w_out_mla)
    return x_all, fin, ckv, kr


def _odd_layer(x_all, grp, g1, mod_tab, w_aug, w_out, lam_full, g_sub, post_scale, ctx, tables, kv_dtype):
    bsz, length = grp["bsz"], grp["length"]
    q, k, v = _odd_in(x_all, grp, g1, mod_tab, w_aug, tables, kv_dtype)
    segs = [(k, v, length)]
    if ctx is not None:
        segs.append((ctx[0], ctx[1], PAST_LEN))
    o = _diff_attn(lam_full, q, segs, g_sub, post_scale, bsz, length)
    return _odd_out(x_all, grp, mod_tab, o, w_out), k, v


def kernel(x_prompt, x_sample, state_s5, cache_mla, cache_diff_k, cache_diff_v, c, c_ctx, w_mod, b_mod, g_norm1, g_norm2, g_final, w_in_even, w_out_even, s5_lam_re, s5_lam_im, s5_log_dt, s5_b_re, s5_b_im, s5_c_re, s5_c_im, s5_d, s5_w_glu, s5_b_glu, mla_g_q, mla_w_uq, mla_g_kv, mla_w_ukv, w_in_odd, w_out_odd, diff_lam, diff_g_sub, w_router, b_router, w_gate_up, b_gate_up, w_down, b_down):
    tab_mla = _rope_tables(DEC_SEQ, MLA_ROPE, MLA_NOPE, 1)
    tab_diff = _rope_tables(DEC_SEQ, DIFF_HD, 0, 2)
    grp_p = _group(BATCH, SEQ, 0, 0, 0)
    grp_s = _group(DEC_BATCH, DEC_SEQ, N_PROMPT, 1, 1)
    x_all = jnp.concatenate([x_prompt.reshape(N_PROMPT, D_MODEL), x_sample.reshape(N_SAMPLE, D_MODEL)], axis=0)
    cond = jax.nn.silu(jnp.concatenate([c_ctx[None], c], axis=0))
    new_s5, new_mla, new_k, new_v = [], [], [], []
    for l in range(DEPTH):
        mod_tab = (cond @ w_mod[l] + b_mod[l]).reshape(1 + DEC_BATCH, 6, D_MODEL)
        i = l // 2
        if l % 2 == 0:
            ew = _even_weights(w_in_even[i], w_out_even[i], mla_w_uq[i], mla_w_ukv[i], s5_w_glu[i])
            s5m = _s5_discretize(s5_lam_re[i], s5_lam_im[i], s5_log_dt[i], s5_b_re[i], s5_b_im[i],
                                 s5_c_re[i], s5_c_im[i])
            common = (ew, s5m, mla_g_q[i], mla_g_kv[i], s5_d[i], s5_b_glu[i])
            h0_p = jnp.zeros((2, 2, BATCH, 2 * S5_HALF_STATES), F32)
            x_all, fin, ckv, kr = _even_layer(x_all, grp_p, g_norm1[l], mod_tab, *common, h0_p, None, None)
            new_s5.append(_s5_state_from_kernel(fin))
            new_mla.append(jnp.concatenate([ckv, kr[:, MLA_NOPE:MLA_NOPE + MLA_ROPE]], axis=1)
                           .reshape(BATCH, SEQ, MLA_KV_LORA + MLA_ROPE))
            lat_ctx = cache_mla[:, i].astype(F32).reshape(DEC_BATCH * PAST_LEN, MLA_KV_LORA + MLA_ROPE)
            ctx = (lat_ctx[:, :MLA_KV_LORA], _pad_head_lanes(lat_ctx[:, MLA_KV_LORA:], MLA_NOPE))
            x_all, _, _, _ = _even_layer(x_all, grp_s, g_norm1[l], mod_tab, *common,
                                         _s5_state_to_kernel(state_s5[:, i]), ctx, tab_mla)
        else:
            lam_init = 0.8 - 0.6 * math.exp(-0.3 * l)
            lamf = diff_lam[i].astype(F32)
            lam_full = jnp.exp(jnp.sum(lamf[0] * lamf[1])) - jnp.exp(jnp.sum(lamf[2] * lamf[3])) + lam_init
            w_qk = w_in_odd[i][:, :2 * DIFF_WIDTH]
            w_aug = jnp.concatenate([w_in_odd[i], _swap_pairs(w_qk)], axis=1).astype(BF16)
            w_out = w_out_odd[i].astype(BF16)
            odd = (w_aug, w_out, lam_full, diff_g_sub[i], 1.0 - lam_init)
            x_all, kp, vp = _odd_layer(x_all, grp_p, g_norm1[l], mod_tab, *odd, None, None, F32)
            new_k.append(kp.reshape(BATCH, SEQ, DIFF_HEADS, 2, DIFF_HD))
            new_v.append(vp.reshape(BATCH, SEQ, DIFF_HEADS, 2 * DIFF_HD))
            ctx = (cache_diff_k[:, i].reshape(DEC_BATCH * PAST_LEN, DIFF_WIDTH).astype(BF16),
                   cache_diff_v[:, i].reshape(DEC_BATCH * PAST_LEN, DIFF_WIDTH).astype(BF16))
            x_all, _, _ = _odd_layer(x_all, grp_s, g_norm1[l], mod_tab, *odd, ctx, tab_diff, BF16)
        x_all = _moe_layer(l, x_all, mod_tab, g_norm2, w_router, b_router,
                           w_gate_up, b_gate_up, w_down, b_down)
    y_prompt = _final_norm(x_all, g_final, 0, N_PROMPT)
    y_sample = _final_norm(x_all, g_final, N_PROMPT, N_SAMPLE)
    return (y_prompt.reshape(BATCH, SEQ, D_MODEL), y_sample.reshape(DEC_BATCH, DEC_SEQ, D_MODEL),
            jnp.stack(new_s5, axis=1), jnp.stack(new_mla, axis=1),
            jnp.stack(new_k, axis=1), jnp.stack(new_v, axis=1))
```

```python
import functools
import math

import jax
import jax.numpy as jnp
from jax import lax
from jax.experimental import pallas as pl
from jax.experimental.pallas import tpu as pltpu

D_MODEL = 1024
BATCH = 16
SEQ = 256
DEPTH = 4
DEC_BATCH = 8
DEC_SEQ = 1024
PAST_LEN = 512
GRID_W = 64
N_EVEN = (DEPTH + 1) // 2
N_ODD = DEPTH // 2
S5_WIDTH = D_MODEL // 2
S5_GROUP = 16
S5_GROUPS = S5_WIDTH // S5_GROUP
S5_STATE = 64
MLA_HEADS = 8
MLA_NOPE = 64
MLA_ROPE = 32
MLA_V = 64
MLA_Q_LORA = D_MODEL // 4
MLA_KV_LORA = D_MODEL // 8
MLA_WIDTH = MLA_HEADS * MLA_V
EVEN_IN = S5_WIDTH + MLA_Q_LORA + MLA_KV_LORA + MLA_ROPE
EVEN_OUT = S5_WIDTH + MLA_WIDTH
DIFF_HEADS = 8
DIFF_HD = D_MODEL // (2 * DIFF_HEADS)
DIFF_WIDTH = DIFF_HEADS * 2 * DIFF_HD
N_EXPERTS = 32
TOP_K = 4
D_FF = D_MODEL
SWIGLU_LIMIT = 7.0
SWIGLU_ALPHA = 1.702
ROPE_THETA = 10000.0
Q_BLOCK = 128
EPS = 1e-6

N_PROMPT = BATCH * SEQ
N_SAMPLE = DEC_BATCH * DEC_SEQ
N_TOK = N_PROMPT + N_SAMPLE

LANES = 128
VMEM_LIMIT_BYTES = 56 * 1024 * 1024

TM = 256
N_TILES = N_TOK // TM
R_TILES = N_TOK * TOP_K // TM + N_EXPERTS
R_MAX = R_TILES * TM
TOPK_ROWS = 8

F32 = jnp.float32
BF16 = jnp.bfloat16


def _mod_row(i):
    t0 = i * TM
    return jnp.where(t0 < N_PROMPT, 0, 1 + (t0 - N_PROMPT) // DEC_SEQ)


def _router_kernel(x_ref, g_ref, mod_ref, wr_ref, br_ref,
                   h_ref, topi_ref, gate_ref, rank_ref, counts_ref, carry_ref):
    i = pl.program_id(0)

    @pl.when(i == 0)
    def _():
        carry_ref[...] = jnp.zeros_like(carry_ref)

    x = x_ref[...]
    ms = jnp.mean(x * x, axis=-1, keepdims=True)
    y = x * lax.rsqrt(ms + EPS) * g_ref[...]
    shift = mod_ref[0, 3:4, :]
    scale = mod_ref[0, 4:5, :]
    h = y * (1.0 + scale) + shift
    h_ref[...] = h

    hi = h.astype(BF16)
    lo = (h - hi.astype(F32)).astype(BF16)
    w_hi = wr_ref[0]
    logits = _dot_t(w_hi, hi) + (_dot_t(wr_ref[1], hi) + _dot_t(w_hi, lo)) + br_ref[...]
    sub_e = lax.broadcasted_iota(jnp.int32, logits.shape, 0)
    work = logits
    vals, hits = [], []
    sel = jnp.zeros(logits.shape, F32)
    for _ in range(TOP_K):
        m = jnp.max(work, axis=0, keepdims=True)
        idx = jnp.min(jnp.where(work == m, sub_e, N_EXPERTS), axis=0, keepdims=True)
        hit = sub_e == idx
        vals.append(m)
        hits.append((hit, idx))
        sel = jnp.where(hit, 1.0, sel)
        work = jnp.where(hit, -jnp.inf, work)
    es = [jnp.exp(v - vals[0]) for v in vals]
    inv = 1.0 / (es[0] + es[1] + es[2] + es[3])

    row = lax.broadcasted_iota(jnp.int32, (TM, TM), 0)
    col = lax.broadcasted_iota(jnp.int32, (TM, TM), 1)
    earlier = jnp.where(row < col, 1.0, 0.0).astype(BF16)
    before = _dot(sel.astype(BF16), earlier) + carry_ref[...]
    carry_ref[...] += jnp.sum(sel, axis=1, keepdims=True)
    counts_ref[...] = carry_ref[...].astype(jnp.int32)

    sub_k = lax.broadcasted_iota(jnp.int32, (TOPK_ROWS, TM), 0)
    topi = jnp.zeros((TOPK_ROWS, TM), jnp.int32)
    gate = jnp.zeros((TOPK_ROWS, TM), F32)
    rank = jnp.zeros((TOPK_ROWS, TM), jnp.int32)
    for k in range(TOP_K):
        hit, idx = hits[k]
        rk = jnp.sum(jnp.where(hit, before, 0.0), axis=0, keepdims=True)
        topi = jnp.where(sub_k == k, idx, topi)
        gate = jnp.where(sub_k == k, es[k] * inv, gate)
        rank = jnp.where(sub_k == k, rk.astype(jnp.int32), rank)
    topi_ref[...] = topi
    gate_ref[...] = gate
    rank_ref[...] = rank


def _router(x_all, g, mod_tab, w_router, b_router):
    w_t = w_router.astype(F32).T
    w_hi = w_t.astype(BF16)
    w_split = jnp.stack([w_hi, (w_t - w_hi.astype(F32)).astype(BF16)])
    return pl.pallas_call(
        _router_kernel,
        grid=(N_TILES,),
        in_specs=[
            pl.BlockSpec((TM, D_MODEL), lambda i: (i, 0)),
            pl.BlockSpec((1, D_MODEL), lambda i: (0, 0)),
            pl.BlockSpec((1, 6, D_MODEL), lambda i: (_mod_row(i), 0, 0)),
            pl.BlockSpec((2, N_EXPERTS, D_MODEL), lambda i: (0, 0, 0)),
            pl.BlockSpec((N_EXPERTS, 1), lambda i: (0, 0)),
        ],
        out_specs=[
            pl.BlockSpec((TM, D_MODEL), lambda i: (i, 0)),
            pl.BlockSpec((TOPK_ROWS, TM), lambda i: (0, i)),
            pl.BlockSpec((TOPK_ROWS, TM), lambda i: (0, i)),
            pl.BlockSpec((TOPK_ROWS, TM), lambda i: (0, i)),
            pl.BlockSpec((N_EXPERTS, 1), lambda i: (0, 0)),
        ],
        out_shape=[
            jax.ShapeDtypeStruct((N_TOK, D_MODEL), F32),
            jax.ShapeDtypeStruct((TOPK_ROWS, N_TOK), jnp.int32),
            jax.ShapeDtypeStruct((TOPK_ROWS, N_TOK), F32),
            jax.ShapeDtypeStruct((TOPK_ROWS, N_TOK), jnp.int32),
            jax.ShapeDtypeStruct((N_EXPERTS, 1), jnp.int32),
        ],
        scratch_shapes=[pltpu.VMEM((N_EXPERTS, 1), F32)],
        compiler_params=pltpu.CompilerParams(
            dimension_semantics=("arbitrary",), vmem_limit_bytes=VMEM_LIMIT_BYTES),
        name="moe_router",
    )(x_all, g.reshape(1, D_MODEL), mod_tab, w_split, b_router.reshape(N_EXPERTS, 1))


ISSUE_UNROLL = 4


def _dispatch_kernel(ends_ref, pos_ref, h_ref, xs_ref, zero_buf, pos_smem, sem_idx, sem, sem_zero):
    i = pl.program_id(0)

    @pl.when(i == 0)
    def _():
        zero_buf[...] = jnp.zeros_like(zero_buf)

        for wait in (False, True):
            for e in range(N_EXPERTS):
                start = ends_ref[e - 1] if e > 0 else 0

                @pl.when(ends_ref[e] > start)
                def _(e=e, wait=wait):
                    last = pl.multiple_of(ends_ref[e] - TM, TM)
                    cp = pltpu.make_async_copy(zero_buf, xs_ref.at[pl.ds(last, TM)], sem_zero)
                    if wait:
                        cp.wait()
                    else:
                        cp.start()

            def tail(t, carry, wait=wait):
                cp = pltpu.make_async_copy(zero_buf, xs_ref.at[pl.ds(pl.multiple_of(t * TM, TM), TM)], sem_zero)
                if wait:
                    cp.wait()
                else:
                    cp.start()
                return carry

            lax.fori_loop(ends_ref[N_EXPERTS - 1] // TM, R_TILES, tail, 0)

    cp = pltpu.make_async_copy(pos_ref, pos_smem, sem_idx)
    cp.start()
    cp.wait()

    def issue(r, carry):
        for k in range(TOP_K):
            p = pos_smem[k, r]
            pltpu.make_async_copy(h_ref.at[pl.ds(r, 1)], xs_ref.at[pl.ds(p, 1)], sem.at[k]).start(priority=k % 2)
        return carry

    lax.fori_loop(0, TM, issue, 0, unroll=ISSUE_UNROLL)
    for k in range(TOP_K):
        pltpu.make_async_copy(h_ref, xs_ref.at[pl.ds(0, TM)], sem.at[k]).wait()


def _dispatch(ends, pos, h):
    grid_spec = pltpu.PrefetchScalarGridSpec(
        num_scalar_prefetch=1,
        grid=(N_TILES,),
        in_specs=[
            pl.BlockSpec((TOPK_ROWS, TM), lambda i, ends: (0, i)),
            pl.BlockSpec((TM, D_MODEL), lambda i, ends: (i, 0)),
        ],
        out_specs=pl.BlockSpec(memory_space=pl.ANY),
        scratch_shapes=[
            pltpu.VMEM((TM, D_MODEL), F32),
            pltpu.SMEM((TOPK_ROWS, TM), jnp.int32),
            pltpu.SemaphoreType.DMA,
            pltpu.SemaphoreType.DMA((TOP_K,)),
            pltpu.SemaphoreType.DMA,
        ],
    )
    return pl.pallas_call(
        _dispatch_kernel,
        grid_spec=grid_spec,
        out_shape=jax.ShapeDtypeStruct((R_MAX, D_MODEL), F32),
        compiler_params=pltpu.CompilerParams(
            dimension_semantics=("arbitrary",), vmem_limit_bytes=VMEM_LIMIT_BYTES),
        name="moe_dispatch",
    )(ends, pos, h)


def _ffn_kernel(te_ref, nu_ref, nx_ref, xs_ref, wgu_hbm, bgu_ref, wd_hbm, bd_ref, ys_ref,
                wgu_f32, wd_f32, wgu_bf, wd_bf, sem, *, layer):
    i = pl.program_id(0)

    def weight_copies(e):
        return (pltpu.make_async_copy(wgu_hbm.at[layer, e], wgu_f32, sem.at[0]),
                pltpu.make_async_copy(wd_hbm.at[layer, e], wd_f32, sem.at[1]))

    @pl.when(i < nu_ref[0])
    def _():
        expert = te_ref[i]
        new_expert = jnp.logical_or(i == 0, expert != te_ref[jnp.maximum(i - 1, 0)])

        @pl.when(i == 0)
        def _():
            for cp in weight_copies(expert):
                cp.start()

        @pl.when(new_expert)
        def _():
            for cp in weight_copies(expert):
                cp.wait()
            wgu_bf[...] = wgu_f32[...].astype(BF16)
            wd_bf[...] = wd_f32[...].astype(BF16)
            nxt = nx_ref[expert]

            @pl.when(nxt >= 0)
            def _():
                for cp in weight_copies(nxt):
                    cp.start()

        x = xs_ref[...].astype(BF16)
        gu = jnp.dot(x, wgu_bf[...], preferred_element_type=F32) + bgu_ref[...]
        g = jnp.minimum(gu[:, :D_FF], SWIGLU_LIMIT)
        u = jnp.clip(gu[:, D_FF:], -SWIGLU_LIMIT, SWIGLU_LIMIT)
        act = g * jax.nn.sigmoid(SWIGLU_ALPHA * g) * (u + 1.0)
        ys_ref[...] = jnp.dot(act.astype(BF16), wd_bf[...], preferred_element_type=F32) + bd_ref[...]

    @pl.when(i >= nu_ref[0])
    def _():
        ys_ref[...] = jnp.zeros_like(ys_ref)


def _ffn(layer, tile_expert, n_used, next_expert, xs, w_gate_up, b_gate_up, w_down, b_down):
    def row_map(i, te, nu, nx):
        return (jnp.maximum(jnp.minimum(i, nu[0] - 1), 0), 0)

    def b_map(i, te, nu, nx):
        return (layer, te[i], 0, 0)

    grid_spec = pltpu.PrefetchScalarGridSpec(
        num_scalar_prefetch=3,
        grid=(R_TILES,),
        in_specs=[
            pl.BlockSpec((TM, D_MODEL), row_map),
            pl.BlockSpec(memory_space=pl.ANY),
            pl.BlockSpec((None, None, 1, 2 * D_FF), b_map),
            pl.BlockSpec(memory_space=pl.ANY),
            pl.BlockSpec((None, None, 1, D_MODEL), b_map),
        ],
        out_specs=pl.BlockSpec((TM, D_MODEL), lambda i, te, nu, nx: (i, 0)),
        scratch_shapes=[
            pltpu.VMEM((D_MODEL, 2 * D_FF), F32),
            pltpu.VMEM((D_FF, D_MODEL), F32),
            pltpu.VMEM((D_MODEL, 2 * D_FF), BF16),
            pltpu.VMEM((D_FF, D_MODEL), BF16),
            pltpu.SemaphoreType.DMA((2,)),
        ],
    )
    return pl.pallas_call(
        functools.partial(_ffn_kernel, layer=layer),
        grid_spec=grid_spec,
        out_shape=jax.ShapeDtypeStruct((R_MAX, D_MODEL), F32),
        compiler_params=pltpu.CompilerParams(
            dimension_semantics=("arbitrary",), vmem_limit_bytes=VMEM_LIMIT_BYTES),
        name="moe_ffn",
    )(tile_expert, n_used, next_expert, xs, w_gate_up,
      b_gate_up.reshape(DEPTH, N_EXPERTS, 1, 2 * D_FF), w_down,
      b_down.reshape(DEPTH, N_EXPERTS, 1, D_MODEL))


def _combine_kernel(pos_ref, ys_ref, x_ref, gate_ref, mod_ref, out_ref, buf, pos_smem, sem_idx, sem):
    cp = pltpu.make_async_copy(pos_ref, pos_smem, sem_idx)
    cp.start()
    cp.wait()

    def issue(r, carry):
        for k in range(TOP_K):
            p = pos_smem[k, r]
            pltpu.make_async_copy(ys_ref.at[pl.ds(p, 1)], buf.at[k, pl.ds(r, 1)], sem.at[k]).start(priority=k % 2)
        return carry

    lax.fori_loop(0, TM, issue, 0, unroll=ISSUE_UNROLL)
    acc = jnp.zeros((TM, D_MODEL), F32)
    for k in range(TOP_K):
        pltpu.make_async_copy(ys_ref.at[pl.ds(0, TM)], buf.at[k], sem.at[k]).wait()
        acc = acc + gate_ref[:, k:k + 1] * buf[k]
    out_ref[...] = x_ref[...] + mod_ref[0, 5:6, :] * acc


def _combine(pos, ys, x_all, gate, mod_tab):
    return pl.pallas_call(
        _combine_kernel,
        grid=(N_TILES,),
        in_specs=[
            pl.BlockSpec((TOPK_ROWS, TM), lambda i: (0, i)),
            pl.BlockSpec(memory_space=pl.ANY),
            pl.BlockSpec((TM, D_MODEL), lambda i: (i, 0)),
            pl.BlockSpec((TM, TOPK_ROWS), lambda i: (i, 0)),
            pl.BlockSpec((1, 6, D_MODEL), lambda i: (_mod_row(i), 0, 0)),
        ],
        out_specs=pl.BlockSpec((TM, D_MODEL), lambda i: (i, 0)),
        out_shape=jax.ShapeDtypeStruct((N_TOK, D_MODEL), F32),
        scratch_shapes=[
            pltpu.VMEM((TOP_K, TM, D_MODEL), F32),
            pltpu.SMEM((TOPK_ROWS, TM), jnp.int32),
            pltpu.SemaphoreType.DMA,
            pltpu.SemaphoreType.DMA((TOP_K,)),
        ],
        compiler_params=pltpu.CompilerParams(
            dimension_semantics=("arbitrary",), vmem_limit_bytes=VMEM_LIMIT_BYTES),
        name="moe_combine",
    )(pos, ys, x_all, gate, mod_tab)


def _moe_layer(layer, x_all, mod_tab, g_norm2, w_router, b_router, w_gate_up, b_gate_up, w_down, b_down):
    h, topi, gate, rank, counts = _router(x_all, g_norm2[layer], mod_tab, w_router[layer], b_router[layer])
    counts = counts[:, 0]
    padded = ((counts + TM - 1) // TM) * TM
    ends = jnp.cumsum(padded)
    starts = ends - padded
    pos = (starts[topi] + rank).astype(jnp.int32)
    gate = gate.T
    n_used = (ends[-1] // TM).astype(jnp.int32)
    order = jnp.arange(N_EXPERTS, dtype=jnp.int32)
    later = jnp.where((padded[None, :] > 0) & (order[None, :] > order[:, None]), order[None, :], N_EXPERTS)
    next_expert = jnp.min(later, axis=1)
    next_expert = jnp.where(next_expert == N_EXPERTS, -1, next_expert).astype(jnp.int32)
    tile_start = jnp.arange(R_TILES, dtype=jnp.int32) * TM
    tile_start = jnp.minimum(tile_start, ends[-1] - 1)
    tile_expert = jnp.sum((ends[None, :] <= tile_start[:, None]).astype(jnp.int32), axis=1)
    tile_expert = jnp.minimum(tile_expert, N_EXPERTS - 1).astype(jnp.int32)
    xs = _dispatch(ends.astype(jnp.int32), pos, h)
    ys = _ffn(layer, tile_expert, n_used.reshape(1), next_expert, xs, w_gate_up, b_gate_up, w_down, b_down)
    return _combine(pos, ys, x_all, gate, mod_tab)


S5_ROWS = 512
S5_HALF_W = S5_WIDTH // 2
S5_HALF_STATES = (S5_GROUPS // 2) * S5_STATE
S5_COL_CHUNK = 512


def _s5_scan_kernel(u_ref, bmat_ref, cmat_ref, a_ref, h0_ref, y_ref, fin_ref, bu_ref, h_ref, *, bsz, steps):
    d = pl.program_id(0)
    c = pl.program_id(1)
    hs = S5_HALF_STATES

    @pl.when(c == 0)
    def _():
        h_ref[...] = h0_ref[...]

    u = u_ref[...].astype(BF16)
    for hf in range(2):
        bu_ref[...] = jnp.dot(u[:, hf * S5_HALF_W:(hf + 1) * S5_HALF_W], bmat_ref[hf],
                              preferred_element_type=F32)
        for j in range(hs // S5_COL_CHUNK):
            re0 = j * S5_COL_CHUNK
            im0 = hs + j * S5_COL_CHUNK
            ar = jnp.broadcast_to(a_ref[hf, 0:1, re0:re0 + S5_COL_CHUNK], (bsz, S5_COL_CHUNK))
            ai = jnp.broadcast_to(a_ref[hf, 1:2, re0:re0 + S5_COL_CHUNK], (bsz, S5_COL_CHUNK))

            def step(t, carry, re0=re0, im0=im0, ar=ar, ai=ai):
                hr, hi = carry
                te = jnp.where(d == 0, t, steps - 1 - t)
                r0 = pl.multiple_of(te * bsz, bsz)
                br = bu_ref[pl.ds(r0, bsz), re0:re0 + S5_COL_CHUNK]
                bi = bu_ref[pl.ds(r0, bsz), im0:im0 + S5_COL_CHUNK]
                nr = ar * hr - ai * hi + br
                ni = ar * hi + ai * hr + bi
                bu_ref[pl.ds(r0, bsz), re0:re0 + S5_COL_CHUNK] = nr
                bu_ref[pl.ds(r0, bsz), im0:im0 + S5_COL_CHUNK] = ni
                return nr, ni

            hr, hi = lax.fori_loop(
                0, steps, step,
                (h_ref[hf, :, re0:re0 + S5_COL_CHUNK], h_ref[hf, :, im0:im0 + S5_COL_CHUNK]), unroll=4)
            h_ref[hf, :, re0:re0 + S5_COL_CHUNK] = hr
            h_ref[hf, :, im0:im0 + S5_COL_CHUNK] = hi
        y_ref[:, hf * S5_HALF_W:(hf + 1) * S5_HALF_W] = jnp.dot(
            bu_ref[...].astype(BF16), cmat_ref[hf], preferred_element_type=F32)

    @pl.when(c == pl.num_programs(1) - 1)
    def _():
        fin_ref[...] = h_ref[...]


def _s5_scan(u_tm, bmat, cmat, acoef, h0, bsz):
    rows = u_tm.shape[0]
    steps = S5_ROWS // bsz
    n_chunks = rows // S5_ROWS

    def chunk_map(d, c):
        return jnp.where(d == 0, c, n_chunks - 1 - c)

    return pl.pallas_call(
        functools.partial(_s5_scan_kernel, bsz=bsz, steps=steps),
        grid=(2, n_chunks),
        in_specs=[
            pl.BlockSpec((S5_ROWS, S5_WIDTH), lambda d, c: (chunk_map(d, c), 0)),
            pl.BlockSpec((None, 2, S5_HALF_W, 2 * S5_HALF_STATES), lambda d, c: (d, 0, 0, 0)),
            pl.BlockSpec((None, 2, 2 * S5_HALF_STATES, S5_HALF_W), lambda d, c: (d, 0, 0, 0)),
            pl.BlockSpec((None, 2, 2, S5_HALF_STATES), lambda d, c: (d, 0, 0, 0)),
            pl.BlockSpec((None, 2, bsz, 2 * S5_HALF_STATES), lambda d, c: (d, 0, 0, 0)),
        ],
        out_specs=[
            pl.BlockSpec((None, S5_ROWS, S5_WIDTH), lambda d, c: (d, chunk_map(d, c), 0)),
            pl.BlockSpec((None, 2, bsz, 2 * S5_HALF_STATES), lambda d, c: (d, 0, 0, 0)),
        ],
        out_shape=[
            jax.ShapeDtypeStruct((2, rows, S5_WIDTH), F32),
            jax.ShapeDtypeStruct((2, 2, bsz, 2 * S5_HALF_STATES), F32),
        ],
        scratch_shapes=[
            pltpu.VMEM((S5_ROWS, 2 * S5_HALF_STATES), F32),
            pltpu.VMEM((2, bsz, 2 * S5_HALF_STATES), F32),
        ],
        compiler_params=pltpu.CompilerParams(
            dimension_semantics=("arbitrary", "arbitrary"), vmem_limit_bytes=VMEM_LIMIT_BYTES),
        name="s5_scan",
    )(u_tm, bmat, cmat, acoef, h0)


def _s5_discretize(lam_re, lam_im, log_dt, b_re, b_im, c_re, c_im):
    eye = jnp.eye(S5_GROUPS // 2, dtype=F32)
    bmats, cmats, acoefs = [], [], []
    for dr in range(2):
        lr = jnp.minimum(lam_re[dr].astype(F32), -1e-4)
        li = lam_im[dr].astype(F32)
        dt = jnp.exp(log_dt[dr].astype(F32))[:, None]
        mag = jnp.exp(lr * dt)
        ar, ai = mag * jnp.cos(li * dt), mag * jnp.sin(li * dt)
        den = lr * lr + li * li
        fr = ((ar - 1.0) * lr + ai * li) / den
        fi = (ai * lr - (ar - 1.0) * li) / den
        br_ = b_re[dr].astype(F32)
        bi_ = b_im[dr].astype(F32)
        bbr = fr[..., None] * br_ - fi[..., None] * bi_
        bbi = fr[..., None] * bi_ + fi[..., None] * br_
        bm, cm, am = [], [], []
        for hf in range(2):
            g = slice(hf * S5_GROUPS // 2, (hf + 1) * S5_GROUPS // 2)

            def bdiag_in(w):
                return jnp.einsum('ab,aph->ahbp', eye, w[g]).reshape(S5_HALF_W, S5_HALF_STATES)

            def bdiag_out(w):
                return jnp.einsum('ab,ahp->apbh', eye, w[g]).reshape(S5_HALF_STATES, S5_HALF_W)

            bm.append(jnp.concatenate([bdiag_in(bbr), bdiag_in(bbi)], axis=1))
            cm.append(jnp.concatenate([bdiag_out(c_re[dr].astype(F32)),
                                       -bdiag_out(c_im[dr].astype(F32))], axis=0))
            am.append(jnp.stack([ar[g].reshape(-1), ai[g].reshape(-1)]))
        bmats.append(jnp.stack(bm))
        cmats.append(jnp.stack(cm))
        acoefs.append(jnp.stack(am))
    return jnp.stack(bmats).astype(BF16), jnp.stack(cmats).astype(BF16), jnp.stack(acoefs)


def _s5_state_to_kernel(h0):
    bsz = h0.shape[0]
    h = h0.astype(F32).reshape(bsz, 2, 2, 2, S5_HALF_STATES)
    return h.transpose(1, 3, 0, 2, 4).reshape(2, 2, bsz, 2 * S5_HALF_STATES)


def _s5_state_from_kernel(fin):
    bsz = fin.shape[2]
    h = fin.reshape(2, 2, bsz, 2, S5_HALF_STATES).transpose(2, 0, 3, 1, 4)
    return h.reshape(bsz, 2, 2, S5_GROUPS, S5_STATE)


TQ = 256
HEAD_LANES = 128
MLA_SCALE = (MLA_NOPE + MLA_ROPE) ** -0.5
DIFF_SCALE = DIFF_HD ** -0.5


def _dot(a, b):
    return jnp.dot(a, b, preferred_element_type=F32)


def _dot_t(a, b):
    return lax.dot_general(a, b, (((1,), (1,)), ((), ())), preferred_element_type=F32)


def _rms_rows(x, g):
    return x * lax.rsqrt(jnp.mean(x * x, axis=-1, keepdims=True) + EPS) * g


def _group(bsz, length, row0, mod_base, mod_stride):
    return dict(bsz=bsz, length=length, nt=length // TQ, tile0=row0 // TQ,
                mod_base=mod_base, mod_stride=mod_stride)


def _params(n_axes):
    return pltpu.CompilerParams(dimension_semantics=("arbitrary",) * n_axes,
                                vmem_limit_bytes=VMEM_LIMIT_BYTES)


def _axial_rope(length, dim):
    rows = length // GRID_W
    row = jnp.repeat(jnp.arange(rows, dtype=F32), GRID_W)
    col = jnp.tile(jnp.arange(GRID_W, dtype=F32), rows)
    n_freq = dim // 4
    inv = ROPE_THETA ** (-jnp.arange(n_freq, dtype=F32) / n_freq)
    ang = jnp.concatenate([row[:, None] * inv, col[:, None] * inv], axis=-1)
    return jnp.cos(ang), jnp.sin(ang)


def _rope_tables(length, dim, lead, reps):
    cos, sin = _axial_rope(length, dim)
    cos_r = jnp.repeat(cos, 2, axis=-1)
    sin_r = jnp.repeat(sin, 2, axis=-1) * jnp.tile(jnp.array([-1.0, 1.0], F32), dim // 2)
    part = HEAD_LANES // reps
    pad = ((0, 0), (lead, part - lead - dim))
    cos_t = jnp.tile(jnp.pad(cos_r, pad, constant_values=1.0), (1, reps))
    sin_t = jnp.tile(jnp.pad(sin_r, pad), (1, reps))
    return cos_t, sin_t


def _swap_pairs(w):
    return w[:, jnp.arange(w.shape[1]) ^ 1]


def _even_in_kernel(*refs, rope):
    if rope:
        (x_ref, g_ref, mod_ref, win_ref, gq_ref, wuq_ref, gkv_ref, cos_ref, sin_ref,
         u_ref, q_ref, ckv_ref, kr_ref) = refs
    else:
        (x_ref, g_ref, mod_ref, win_ref, gq_ref, wuq_ref, gkv_ref,
         u_ref, q_ref, ckv_ref, kr_ref) = refs
    o1 = S5_WIDTH
    o2 = o1 + MLA_Q_LORA
    o3 = o2 + MLA_KV_LORA
    o4 = o3 + HEAD_LANES
    n_in = o4 + HEAD_LANES if rope else o4
    n_q = MLA_HEADS * HEAD_LANES
    h = _rms_rows(x_ref[...], g_ref[...]) * (1.0 + mod_ref[0, 1:2, :]) + mod_ref[0, 0:1, :]
    z = _dot(h.astype(BF16), win_ref[:, :n_in])
    u_ref[...] = z[:, :o1]
    ckv_ref[...] = _rms_rows(z[:, o2:o3], gkv_ref[...])
    qn = _rms_rows(z[:, o1:o2], gq_ref[...]).astype(BF16)
    if rope:
        q2 = _dot(qn, wuq_ref[...])
        cos = cos_ref[...]
        sin = sin_ref[...]
        for hd in range(MLA_HEADS):
            a = hd * HEAD_LANES
            q_ref[:, a:a + HEAD_LANES] = ((q2[:, a:a + HEAD_LANES] * cos
                                           + q2[:, n_q + a:n_q + a + HEAD_LANES] * sin) * MLA_SCALE
                                          ).astype(q_ref.dtype)
        kr_ref[...] = z[:, o3:o4] * cos + z[:, o4:o4 + HEAD_LANES] * sin
    else:
        q_ref[...] = (_dot(qn, wuq_ref[:, :n_q]) * MLA_SCALE).astype(q_ref.dtype)
        kr_ref[...] = z[:, o3:o4]


def _even_in(x_all, grp, g1, mod_tab, win_aug, g_q, wuq2, g_kv, tables):
    bsz, length, nt = grp["bsz"], grp["length"], grp["nt"]
    rope = tables is not None
    rows = bsz * length

    def tok(b, t):
        return (b * nt + t, 0)

    in_specs = [
        pl.BlockSpec((TQ, D_MODEL), lambda b, t: (grp["tile0"] + b * nt + t, 0)),
        pl.BlockSpec((1, D_MODEL), lambda b, t: (0, 0)),
        pl.BlockSpec((1, 6, D_MODEL), lambda b, t: (grp["mod_base"] + b * grp["mod_stride"], 0, 0)),
        pl.BlockSpec(win_aug.shape, lambda b, t: (0, 0)),
        pl.BlockSpec((1, MLA_Q_LORA), lambda b, t: (0, 0)),
        pl.BlockSpec(wuq2.shape, lambda b, t: (0, 0)),
        pl.BlockSpec((1, MLA_KV_LORA), lambda b, t: (0, 0)),
    ]
    args = [x_all, g1.reshape(1, D_MODEL), mod_tab, win_aug, g_q.reshape(1, -1), wuq2, g_kv.reshape(1, -1)]
    if rope:
        in_specs += [pl.BlockSpec((TQ, HEAD_LANES), lambda b, t: (t, 0))] * 2
        args += list(tables)
    return pl.pallas_call(
        functools.partial(_even_in_kernel, rope=rope),
        grid=(bsz, nt),
        in_specs=in_specs,
        out_specs=[
            pl.BlockSpec((TQ, S5_WIDTH), lambda b, t: (t, b)),
            pl.BlockSpec((TQ, MLA_HEADS * HEAD_LANES), tok),
            pl.BlockSpec((TQ, MLA_KV_LORA), tok),
            pl.BlockSpec((TQ, HEAD_LANES), tok),
        ],
        out_shape=[
            jax.ShapeDtypeStruct((length, bsz * S5_WIDTH), F32),
            jax.ShapeDtypeStruct((rows, MLA_HEADS * HEAD_LANES), BF16),
            jax.ShapeDtypeStruct((rows, MLA_KV_LORA), F32),
            jax.ShapeDtypeStruct((rows, HEAD_LANES), F32),
        ],
        compiler_params=_params(2),
        name="even_in",
    )(*args)


def _kv_expand_kernel(x_ref, kr_ref, wk_ref, wv_ref, k_ref, v_ref):
    x = x_ref[...].astype(BF16)
    k = _dot(x, wk_ref[...])
    kr = kr_ref[...]
    for hd in range(MLA_HEADS):
        a = hd * HEAD_LANES
        k_ref[:, a:a + HEAD_LANES] = (k[:, a:a + HEAD_LANES] + kr).astype(k_ref.dtype)
    v_ref[...] = _dot(x, wv_ref[...]).astype(v_ref.dtype)


def _kv_expand(ckv, kr, wk, wv):
    rows = ckv.shape[0]
    tm = 512
    width = MLA_HEADS * HEAD_LANES
    return pl.pallas_call(
        _kv_expand_kernel,
        grid=(rows // tm,),
        in_specs=[pl.BlockSpec((tm, MLA_KV_LORA), lambda i: (i, 0)),
                  pl.BlockSpec((tm, HEAD_LANES), lambda i: (i, 0)),
                  pl.BlockSpec(wk.shape, lambda i: (0, 0)),
                  pl.BlockSpec(wv.shape, lambda i: (0, 0))],
        out_specs=[pl.BlockSpec((tm, width), lambda i: (i, 0))] * 2,
        out_shape=[jax.ShapeDtypeStruct((rows, width), BF16)] * 2,
        compiler_params=_params(1),
        name="kv_expand",
    )(ckv, kr, wk, wv)


def _exp_parts(scores):
    m = functools.reduce(jnp.maximum, [jnp.max(s, axis=-1, keepdims=True) for s in scores])
    es = [jnp.exp(s - m) for s in scores]
    den = functools.reduce(jnp.add, [jnp.sum(e, axis=-1, keepdims=True) for e in es])
    return es, 1.0 / den


def _weighted_values(es, vs):
    o = _dot(es[0].astype(BF16), vs[0])
    for e, v in zip(es[1:], vs[1:]):
        o = o + _dot(e.astype(BF16), v)
    return o


def _mla_attn_kernel(*refs, n_seg):
    q = refs[0][...]
    o_ref = refs[-1]
    ks = [refs[1 + 2 * s][...] for s in range(n_seg)]
    vs = [refs[2 + 2 * s][...] for s in range(n_seg)]
    es, inv = _exp_parts([_dot_t(q, k) for k in ks])
    o_ref[...] = (_weighted_values(es, vs) * inv).astype(o_ref.dtype)


def _mla_attn(q, segs, bsz, length):
    nq = length // TQ
    in_specs = [pl.BlockSpec((TQ, HEAD_LANES), lambda b, h, i: (b * nq + i, h))]
    args = [q]
    for k, v, lk in segs:
        in_specs += [pl.BlockSpec((lk, HEAD_LANES), lambda b, h, i: (b, h))] * 2
        args += [k, v]
    return pl.pallas_call(
        functools.partial(_mla_attn_kernel, n_seg=len(segs)),
        grid=(bsz, MLA_HEADS, nq),
        in_specs=in_specs,
        out_specs=pl.BlockSpec((TQ, HEAD_LANES), lambda b, h, i: (b * nq + i, h)),
        out_shape=jax.ShapeDtypeStruct((bsz * length, MLA_HEADS * HEAD_LANES), BF16),
        compiler_params=_params(3),
        name="mla_attn",
    )(*args)


def _even_out_kernel(u_ref, y_ref, o_ref, x_ref, mod_ref, d_ref, wglu_ref, bglu_ref, ws5_ref, wmla_ref,
                     out_ref):
    y =jax.nn.gelu(d_ref[...] * u_ref[...] + y_ref[0] + y_ref[1])
    s5 = y * jax.nn.sigmoid(_dot(y.astype(BF16), wglu_ref[...]) + bglu_ref[...])
    mix = _dot(s5.astype(BF16), ws5_ref[...]) + _dot(o_ref[...].astype(BF16), wmla_ref[...])
    out_ref[...] = x_ref[...] + mod_ref[0, 2:3, :] * mix


def _even_out(x_all, grp, mod_tab, u_tm, y_dir, o_mla, d_skip, w_glu, b_glu, w_out_s5, w_out_mla):
    bsz, length, nt = grp["bsz"], grp["length"], grp["nt"]

    def xrow(b, t):
        return (grp["tile0"] + b * nt + t, 0)

    full = lambda b, t: (0, 0)
    return pl.pallas_call(
        _even_out_kernel,
        grid=(bsz, nt),
        in_specs=[
            pl.BlockSpec((TQ, S5_WIDTH), lambda b, t: (t, b)),
            pl.BlockSpec((2, TQ, S5_WIDTH), lambda b, t: (0, t, b)),
            pl.BlockSpec((TQ, MLA_HEADS * HEAD_LANES), lambda b, t: (b * nt + t, 0)),
            pl.BlockSpec((TQ, D_MODEL), xrow),
            pl.BlockSpec((1, 6, D_MODEL), lambda b, t: (grp["mod_base"] + b * grp["mod_stride"], 0, 0)),
            pl.BlockSpec((1, S5_WIDTH), full),
            pl.BlockSpec(w_glu.shape, full),
            pl.BlockSpec((1, S5_WIDTH), full),
            pl.BlockSpec(w_out_s5.shape, full),
            pl.BlockSpec(w_out_mla.shape, full),
        ],
        out_specs=pl.BlockSpec((TQ, D_MODEL), xrow),
        out_shape=jax.ShapeDtypeStruct(x_all.shape, F32),
        input_output_aliases={3: 0},
        compiler_params=_params(2),
        name="even_out",
    )(u_tm, y_dir.reshape(2, length, bsz * S5_WIDTH), o_mla, x_all, mod_tab,
      d_skip.reshape(1, S5_WIDTH), w_glu, b_glu.reshape(1, S5_WIDTH), w_out_s5, w_out_mla)


def _odd_in_kernel(*refs, rope):
    if rope:
        x_ref, g_ref, mod_ref, w_ref, cos_ref, sin_ref, q_ref, k_ref, v_ref = refs
    else:
        x_ref, g_ref, mod_ref, w_ref, q_ref, k_ref, v_ref = refs
    w3 = 3 * DIFF_WIDTH
    h = _rms_rows(x_ref[...], g_ref[...]) * (1.0 + mod_ref[0, 1:2, :]) + mod_ref[0, 0:1, :]
    z = _dot(h.astype(BF16), w_ref[...] if rope else w_ref[:, :w3])
    v_ref[...] = z[:, 2 * DIFF_WIDTH:w3].astype(v_ref.dtype)
    if rope:
        cos = cos_ref[...]
        sin = sin_ref[...]
        for hd in range(DIFF_HEADS):
            a = hd * HEAD_LANES
            q_ref[:, a:a + HEAD_LANES] = ((z[:, a:a + HEAD_LANES] * cos
                                           + z[:, w3 + a:w3 + a + HEAD_LANES] * sin) * DIFF_SCALE
                                          ).astype(q_ref.dtype)
            b = DIFF_WIDTH + a
            k_ref[:, a:a + HEAD_LANES] = (z[:, b:b + HEAD_LANES] * cos
                                          + z[:, w3 + b:w3 + b + HEAD_LANES] * sin).astype(k_ref.dtype)
    else:
        q_ref[...] = (z[:, :DIFF_WIDTH] * DIFF_SCALE).astype(q_ref.dtype)
        k_ref[...] = z[:, DIFF_WIDTH:2 * DIFF_WIDTH].astype(k_ref.dtype)


def _odd_in(x_all, grp, g1, mod_tab, w_aug, tables, kv_dtype):
    bsz, length, nt = grp["bsz"], grp["length"], grp["nt"]
    rope = tables is not None
    rows = bsz * length

    def tok(b, t):
        return (b * nt + t, 0)

    in_specs = [
        pl.BlockSpec((TQ, D_MODEL), lambda b, t: (grp["tile0"] + b * nt + t, 0)),
        pl.BlockSpec((1, D_MODEL), lambda b, t: (0, 0)),
        pl.BlockSpec((1, 6, D_MODEL), lambda b, t: (grp["mod_base"] + b * grp["mod_stride"], 0, 0)),
        pl.BlockSpec(w_aug.shape, lambda b, t: (0, 0)),
    ]
    args = [x_all, g1.reshape(1, D_MODEL), mod_tab, w_aug]
    if rope:
        in_specs += [pl.BlockSpec((TQ, HEAD_LANES), lambda b, t: (t, 0))] * 2
        args += list(tables)
    return pl.pallas_call(
        functools.partial(_odd_in_kernel, rope=rope),
        grid=(bsz, nt),
        in_specs=in_specs,
        out_specs=[pl.BlockSpec((TQ, DIFF_WIDTH), tok)] * 3,
        out_shape=[
            jax.ShapeDtypeStruct((rows, DIFF_WIDTH), BF16),
            jax.ShapeDtypeStruct((rows, DIFF_WIDTH), kv_dtype),
            jax.ShapeDtypeStruct((rows, DIFF_WIDTH), kv_dtype),
        ],
        compiler_params=_params(2),
        name="odd_in",
    )(*args)


def _diff_attn_kernel(*refs, n_seg, post_scale):
    lam_ref, q_ref = refs[0], refs[1]
    g_ref, o_ref = refs[-2], refs[-1]
    q = q_ref[...].astype(F32)
    lane = lax.broadcasted_iota(jnp.int32, q.shape, 1)
    q0 = jnp.where(lane < DIFF_HD, q, 0.0).astype(BF16)
    q1 = jnp.where(lane >= DIFF_HD, q, 0.0).astype(BF16)
    ks = [refs[2 + 2 * s][...].astype(BF16) for s in range(n_seg)]
    vs = [refs[3 + 2 * s][...].astype(BF16) for s in range(n_seg)]
    e0, inv0 = _exp_parts([_dot_t(q0, k) for k in ks])
    e1, inv1 = _exp_parts([_dot_t(q1, k) for k in ks])
    o = _weighted_values(e0, vs) * inv0 - lam_ref[0] * (_weighted_values(e1, vs) * inv1)
    o_ref[...] = (_rms_rows(o, g_ref[...]) * post_scale).astype(o_ref.dtype)


def _diff_attn(lam_full, q, segs, g_sub, post_scale, bsz, length):
    nq = length // TQ
    in_specs = [pl.BlockSpec(memory_space=pltpu.SMEM),
                pl.BlockSpec((TQ, HEAD_LANES), lambda b, h, i: (b * nq + i, h))]
    args = [lam_full.reshape(1).astype(F32), q]
    for k, v, lk in segs:
        in_specs += [pl.BlockSpec((lk, HEAD_LANES), lambda b, h, i: (b, h))] * 2
        args += [k, v]
    in_specs.append(pl.BlockSpec((1, HEAD_LANES), lambda b, h, i: (0, 0)))
    args.append(g_sub.reshape(1, HEAD_LANES))
    return pl.pallas_call(
        functools.partial(_diff_attn_kernel, n_seg=len(segs), post_scale=post_scale),
        grid=(bsz, DIFF_HEADS, nq),
        in_specs=in_specs,
        out_specs=pl.BlockSpec((TQ, HEAD_LANES), lambda b, h, i: (b * nq + i, h)),
        out_shape=jax.ShapeDtypeStruct((bsz * length, DIFF_WIDTH), BF16),
        compiler_params=_params(3),
        name="diff_attn",
    )(*args)


def _odd_out_kernel(o_ref, x_ref, mod_ref, w_ref, out_ref):
    out_ref[...] = x_ref[...] + mod_ref[0, 2:3, :] * _dot(o_ref[...], w_ref[...])


def _odd_out(x_all, grp, mod_tab, o, w_out):
    bsz, nt = grp["bsz"], grp["nt"]

    def xrow(b, t):
        return (grp["tile0"] + b * nt + t, 0)

    return pl.pallas_call(
        _odd_out_kernel,
        grid=(bsz, nt),
        in_specs=[
            pl.BlockSpec((TQ, DIFF_WIDTH), lambda b, t: (b * nt + t, 0)),
            pl.BlockSpec((TQ, D_MODEL), xrow),
            pl.BlockSpec((1, 6, D_MODEL), lambda b, t: (grp["mod_base"] + b * grp["mod_stride"], 0, 0)),
            pl.BlockSpec(w_out.shape, lambda b, t: (0, 0)),
        ],
        out_specs=pl.BlockSpec((TQ, D_MODEL), xrow),
        out_shape=jax.ShapeDtypeStruct(x_all.shape, F32),
        input_output_aliases={1: 0},
        compiler_params=_params(2),
        name="odd_out",
    )(o, x_all, mod_tab, w_out)


def _final_norm_kernel(x_ref, g_ref, o_ref):
    o_ref[...] = _rms_rows(x_ref[...], g_ref[...])


def _final_norm(x_all, g, row0, rows):
    tm = 512
    return pl.pallas_call(
        _final_norm_kernel,
        grid=(rows // tm,),
        in_specs=[pl.BlockSpec((tm, D_MODEL), lambda i: (row0 // tm + i, 0)),
                  pl.BlockSpec((1, D_MODEL), lambda i: (0, 0))],
        out_specs=pl.BlockSpec((tm, D_MODEL), lambda i: (i, 0)),
        out_shape=jax.ShapeDtypeStruct((rows, D_MODEL), F32),
        compiler_params=_params(1),
        name="final_norm",
    )(x_all, g.reshape(1, D_MODEL))


def _pad_head_lanes(x, lead):
    return jnp.pad(x, ((0, 0), (lead, HEAD_LANES - lead - x.shape[1])))


def _even_weights(w_in, w_out, w_uq, w_ukv, w_glu):
    o3 = S5_WIDTH + MLA_Q_LORA + MLA_KV_LORA
    w_kr = w_in[:, o3:]
    win_aug = jnp.concatenate(
        [w_in[:, :o3], _pad_head_lanes(w_kr, MLA_NOPE), _pad_head_lanes(_swap_pairs(w_kr), MLA_NOPE)], axis=1)
    dq = MLA_NOPE + MLA_ROPE
    plain, swapped = [], []
    for hd in range(MLA_HEADS):
        wn = w_uq[:, hd * dq:hd * dq + MLA_NOPE]
        wr = w_uq[:, hd * dq + MLA_NOPE:(hd + 1) * dq]
        plain.append(jnp.pad(jnp.concatenate([wn, wr], axis=1), ((0, 0), (0, HEAD_LANES - dq))))
        swapped.append(_pad_head_lanes(_swap_pairs(wr), MLA_NOPE))
    wuq2 = jnp.concatenate(plain + swapped, axis=1)
    w_mla = w_out[S5_WIDTH:].reshape(MLA_HEADS, MLA_V, D_MODEL)
    w_out_mla = jnp.pad(w_mla, ((0, 0), (0, HEAD_LANES - MLA_V), (0, 0))).reshape(MLA_HEADS * HEAD_LANES, D_MODEL)
    w_kv = w_ukv.reshape(MLA_KV_LORA, MLA_HEADS, MLA_NOPE + MLA_V)
    wk = jnp.pad(w_kv[:, :, :MLA_NOPE], ((0, 0), (0, 0), (0, HEAD_LANES - MLA_NOPE)))
    wv = jnp.pad(w_kv[:, :, MLA_NOPE:], ((0, 0), (0, 0), (0, HEAD_LANES - MLA_V)))
    w_kv = (wk.reshape(MLA_KV_LORA, -1).astype(BF16), wv.reshape(MLA_KV_LORA, -1).astype(BF16))
    return (win_aug.astype(BF16), wuq2.astype(BF16), w_kv, w_glu.astype(BF16),
            w_out[:S5_WIDTH].astype(BF16), w_out_mla.astype(BF16))


def _even_layer(x_all, grp, g1, mod_tab, ew, s5m, g_q, g_kv, d_skip, b_glu, h0, ctx, tables):
    win_aug, wuq2, w_kv, w_glu, w_out_s5, w_out_mla = ew
    bmat, cmat, acoef = s5m
    bsz, length = grp["bsz"], grp["length"]
    u_tm, q, ckv, kr = _even_in(x_all, grp, g1, mod_tab, win_aug, g_q, wuq2, g_kv, tables)
    y_dir, fin = _s5_scan(u_tm.reshape(length * bsz, S5_WIDTH), bmat, cmat, acoef, h0, bsz)
    segs = [(*_kv_expand(ckv, kr, *w_kv), length)]
    if ctx is not None:
        segs.append((*_kv_expand(*ctx, *w_kv), PAST_LEN))
    o_mla = _mla_attn(q, segs, bsz, length)
    x_all = _even_out(x_all, grp, mod_tab, u_tm, y_dir, o_mla, d_skip, w_glu, b_glu, w_out_s5, w_out_mla)
    return x_all, fin, ckv, kr


def _odd_layer(x_all, grp, g1, mod_tab, w_aug, w_out, lam_full, g_sub, post_scale, ctx, tables, kv_dtype):
    bsz, length = grp["bsz"], grp["length"]
    q, k, v = _odd_in(x_all, grp, g1, mod_tab, w_aug, tables, kv_dtype)
    segs = [(k, v, length)]
    if ctx is not None:
        segs.append((ctx[0], ctx[1], PAST_LEN))
    o = _diff_attn(lam_full, q, segs, g_sub, post_scale, bsz, length)
    return _odd_out(x_all, grp, mod_tab, o, w_out), k, v


def kernel(x_prompt, x_sample, state_s5, cache_mla, cache_diff_k, cache_diff_v, c, c_ctx, w_mod, b_mod, g_norm1, g_norm2, g_final, w_in_even, w_out_even, s5_lam_re, s5_lam_im, s5_log_dt, s5_b_re, s5_b_im, s5_c_re, s5_c_im, s5_d, s5_w_glu, s5_b_glu, mla_g_q, mla_w_uq, mla_g_kv, mla_w_ukv, w_in_odd, w_out_odd, diff_lam, diff_g_sub, w_router, b_router, w_gate_up, b_gate_up, w_down, b_down):
    tab_mla = _rope_tables(DEC_SEQ, MLA_ROPE, MLA_NOPE, 1)
    tab_diff = _rope_tables(DEC_SEQ, DIFF_HD, 0, 2)
    grp_p = _group(BATCH, SEQ, 0, 0, 0)
    grp_s = _group(DEC_BATCH, DEC_SEQ, N_PROMPT, 1, 1)
    x_all = jnp.concatenate([x_prompt.reshape(N_PROMPT, D_MODEL), x_sample.reshape(N_SAMPLE, D_MODEL)], axis=0)
    cond = jax.nn.silu(jnp.concatenate([c_ctx[None], c], axis=0))
    new_s5, new_mla, new_k, new_v = [], [], [], []
    for l in range(DEPTH):
        mod_tab = (cond @ w_mod[l] + b_mod[l]).reshape(1 + DEC_BATCH, 6, D_MODEL)
        i = l // 2
        if l % 2 == 0:
            ew = _even_weights(w_in_even[i], w_out_even[i], mla_w_uq[i], mla_w_ukv[i], s5_w_glu[i])
            s5m = _s5_discretize(s5_lam_re[i], s5_lam_im[i], s5_log_dt[i], s5_b_re[i], s5_b_im[i],
                                 s5_c_re[i], s5_c_im[i])
            common = (ew, s5m, mla_g_q[i], mla_g_kv[i], s5_d[i], s5_b_glu[i])
            h0_p = jnp.zeros((2, 2, BATCH, 2 * S5_HALF_STATES), F32)
            x_all, fin, ckv, kr = _even_layer(x_all, grp_p, g_norm1[l], mod_tab, *common, h0_p, None, None)
            new_s5.append(_s5_state_from_kernel(fin))
            new_mla.append(jnp.concatenate([ckv, kr[:, MLA_NOPE:MLA_NOPE + MLA_ROPE]], axis=1)
                           .reshape(BATCH, SEQ, MLA_KV_LORA + MLA_ROPE))
            lat_ctx = cache_mla[:, i].astype(F32).reshape(DEC_BATCH * PAST_LEN, MLA_KV_LORA + MLA_ROPE)
            ctx = (lat_ctx[:, :MLA_KV_LORA], _pad_head_lanes(lat_ctx[:, MLA_KV_LORA:], MLA_NOPE))
            x_all, _, _, _ = _even_layer(x_all, grp_s, g_norm1[l], mod_tab, *common,
                                         _s5_state_to_kernel(state_s5[:, i]), ctx, tab_mla)
        else:
            lam_init = 0.8 - 0.6 * math.exp(-0.3 * l)
            lamf = diff_lam[i].astype(F32)
            lam_full = jnp.exp(jnp.sum(lamf[0] * lamf[1])) - jnp.exp(jnp.sum(lamf[2] * lamf[3])) + lam_init
            w_qk = w_in_odd[i][:, :2 * DIFF_WIDTH]
            w_aug = jnp.concatenate([w_in_odd[i], _swap_pairs(w_qk)], axis=1).astype(BF16)
            w_out = w_out_odd[i].astype(BF16)
            odd = (w_aug, w_out, lam_full, diff_g_sub[i], 1.0 - lam_init)
            x_all, kp, vp = _odd_layer(x_all, grp_p, g_norm1[l], mod_tab, *odd, None, None, F32)
            new_k.append(kp.reshape(BATCH, SEQ, DIFF_HEADS, 2, DIFF_HD))
            new_v.append(vp.reshape(BATCH, SEQ, DIFF_HEADS, 2 * DIFF_HD))
            ctx = (cache_diff_k[:, i].reshape(DEC_BATCH * PAST_LEN, DIFF_WIDTH).astype(BF16),
                   cache_diff_v[:, i].reshape(DEC_BATCH * PAST_LEN, DIFF_WIDTH).astype(BF16))
            x_all, _, _ = _odd_layer(x_all, grp_s, g_norm1[l], mod_tab, *odd, ctx, tab_diff, BF16)
        x_all = _moe_layer(l, x_all, mod_tab, g_norm2, w_router, b_router,
                           w_gate_up, b_gate_up, w_down, b_down)
    y_prompt = _final_norm(x_all, g_final, 0, N_PROMPT)
    y_sample = _final_norm(x_all, g_final, N_PROMPT, N_SAMPLE)
    return (y_prompt.reshape(BATCH, SEQ, D_MODEL), y_sample.reshape(DEC_BATCH, DEC_SEQ, D_MODEL),
            jnp.stack(new_s5, axis=1), jnp.stack(new_mla, axis=1),
            jnp.stack(new_k, axis=1), jnp.stack(new_v, axis=1))
```

```python
import functools
import math

import jax
import jax.numpy as jnp
from jax import lax
from jax.experimental import pallas as pl
from jax.experimental.pallas import tpu as pltpu

D_MODEL = 1024
BATCH = 16
SEQ = 256
DEPTH = 4
DEC_BATCH = 8
DEC_SEQ = 1024
PAST_LEN = 512
GRID_W = 64
N_EVEN = (DEPTH + 1) // 2
N_ODD = DEPTH // 2
S5_WIDTH = D_MODEL // 2
S5_GROUP = 16
S5_GROUPS = S5_WIDTH // S5_GROUP
S5_STATE = 64
MLA_HEADS = 8
MLA_NOPE = 64
MLA_ROPE = 32
MLA_V = 64
MLA_Q_LORA = D_MODEL // 4
MLA_KV_LORA = D_MODEL // 8
MLA_WIDTH = MLA_HEADS * MLA_V
EVEN_IN = S5_WIDTH + MLA_Q_LORA + MLA_KV_LORA + MLA_ROPE
EVEN_OUT = S5_WIDTH + MLA_WIDTH
DIFF_HEADS = 8
DIFF_HD = D_MODEL // (2 * DIFF_HEADS)
DIFF_WIDTH = DIFF_HEADS * 2 * DIFF_HD
N_EXPERTS = 32
TOP_K = 4
D_FF = D_MODEL
SWIGLU_LIMIT = 7.0
SWIGLU_ALPHA = 1.702
ROPE_THETA = 10000.0
Q_BLOCK = 128
EPS = 1e-6

N_PROMPT = BATCH * SEQ
N_SAMPLE = DEC_BATCH * DEC_SEQ
N_TOK = N_PROMPT + N_SAMPLE

LANES = 128
VMEM_LIMIT_BYTES = 56 * 1024 * 1024

TM = 256
N_TILES = N_TOK // TM
R_TILES = N_TOK * TOP_K // TM + N_EXPERTS
R_MAX = R_TILES * TM
TOPK_ROWS = 8

F32 = jnp.float32
BF16 = jnp.bfloat16


def _mod_row(i):
    t0 = i * TM
    return jnp.where(t0 < N_PROMPT, 0, 1 + (t0 - N_PROMPT) // DEC_SEQ)


def _router_kernel(x_ref, g_ref, mod_ref, wr_ref, br_ref,
                   h_ref, topi_ref, gate_ref, rank_ref, counts_ref, carry_ref):
    i = pl.program_id(0)

    @pl.when(i == 0)
    def _():
        carry_ref[...] = jnp.zeros_like(carry_ref)

    x = x_ref[...]
    ms = jnp.mean(x * x, axis=-1, keepdims=True)
    y = x * lax.rsqrt(ms + EPS) * g_ref[...]
    shift = mod_ref[0, 3:4, :]
    scale = mod_ref[0, 4:5, :]
    h = y * (1.0 + scale) + shift
    h_ref[...] = h

    hi = h.astype(BF16)
    lo = (h - hi.astype(F32)).astype(BF16)
    w_hi = wr_ref[0]
    logits = _dot_t(w_hi, hi) + (_dot_t(wr_ref[1], hi) + _dot_t(w_hi, lo)) + br_ref[...]
    sub_e = lax.broadcasted_iota(jnp.int32, logits.shape, 0)
    work = logits
    vals, hits = [], []
    sel = jnp.zeros(logits.shape, F32)
    for _ in range(TOP_K):
        m = jnp.max(work, axis=0, keepdims=True)
        idx = jnp.min(jnp.where(work == m, sub_e, N_EXPERTS), axis=0, keepdims=True)
        hit = sub_e == idx
        vals.append(m)
        hits.append((hit, idx))
        sel = jnp.where(hit, 1.0, sel)
        work = jnp.where(hit, -jnp.inf, work)
    es = [jnp.exp(v - vals[0]) for v in vals]
    inv = 1.0 / (es[0] + es[1] + es[2] + es[3])

    row = lax.broadcasted_iota(jnp.int32, (TM, TM), 0)
    col = lax.broadcasted_iota(jnp.int32, (TM, TM), 1)
    earlier = jnp.where(row < col, 1.0, 0.0).astype(BF16)
    before = _dot(sel.astype(BF16), earlier) + carry_ref[...]
    carry_ref[...] += jnp.sum(sel, axis=1, keepdims=True)
    counts_ref[...] = carry_ref[...].astype(jnp.int32)

    sub_k = lax.broadcasted_iota(jnp.int32, (TOPK_ROWS, TM), 0)
    topi = jnp.zeros((TOPK_ROWS, TM), jnp.int32)
    gate = jnp.zeros((TOPK_ROWS, TM), F32)
    rank = jnp.zeros((TOPK_ROWS, TM), jnp.int32)
    for k in range(TOP_K):
        hit, idx = hits[k]
        rk = jnp.sum(jnp.where(hit, before, 0.0), axis=0, keepdims=True)
        topi = jnp.where(sub_k == k, idx, topi)
        gate = jnp.where(sub_k == k, es[k] * inv, gate)
        rank = jnp.where(sub_k == k, rk.astype(jnp.int32), rank)
    topi_ref[...] = topi
    gate_ref[...] = gate
    rank_ref[...] = rank


def _router(x_all, g, mod_tab, w_router, b_router):
    w_t = w_router.astype(F32).T
    w_hi = w_t.astype(BF16)
    w_split = jnp.stack([w_hi, (w_t - w_hi.astype(F32)).astype(BF16)])
    return pl.pallas_call(
        _router_kernel,
        grid=(N_TILES,),
        in_specs=[
            pl.BlockSpec((TM, D_MODEL), lambda i: (i, 0)),
            pl.BlockSpec((1, D_MODEL), lambda i: (0, 0)),
            pl.BlockSpec((1, 6, D_MODEL), lambda i: (_mod_row(i), 0, 0)),
            pl.BlockSpec((2, N_EXPERTS, D_MODEL), lambda i: (0, 0, 0)),
            pl.BlockSpec((N_EXPERTS, 1), lambda i: (0, 0)),
        ],
        out_specs=[
            pl.BlockSpec((TM, D_MODEL), lambda i: (i, 0)),
            pl.BlockSpec((TOPK_ROWS, TM), lambda i: (0, i)),
            pl.BlockSpec((TOPK_ROWS, TM), lambda i: (0, i)),
            pl.BlockSpec((TOPK_ROWS, TM), lambda i: (0, i)),
            pl.BlockSpec((N_EXPERTS, 1), lambda i: (0, 0)),
        ],
        out_shape=[
            jax.ShapeDtypeStruct((N_TOK, D_MODEL), F32),
            jax.ShapeDtypeStruct((TOPK_ROWS, N_TOK), jnp.int32),
            jax.ShapeDtypeStruct((TOPK_ROWS, N_TOK), F32),
            jax.ShapeDtypeStruct((TOPK_ROWS, N_TOK), jnp.int32),
            jax.ShapeDtypeStruct((N_EXPERTS, 1), jnp.int32),
        ],
        scratch_shapes=[pltpu.VMEM((N_EXPERTS, 1), F32)],
        compiler_params=pltpu.CompilerParams(
            dimension_semantics=("arbitrary",), vmem_limit_bytes=VMEM_LIMIT_BYTES),
        name="moe_router",
    )(x_all, g.reshape(1, D_MODEL), mod_tab, w_split, b_router.reshape(N_EXPERTS, 1))


ISSUE_UNROLL = 4


def _dispatch_kernel(ends_ref, pos_ref, h_ref, xs_ref, zero_buf, pos_smem, sem_idx, sem, sem_zero):
    i = pl.program_id(0)

    @pl.when(i == 0)
    def _():
        zero_buf[...] = jnp.zeros_like(zero_buf)

        for wait in (False, True):
            for e in range(N_EXPERTS):
                start = ends_ref[e - 1] if e > 0 else 0

                @pl.when(ends_ref[e] > start)
                def _(e=e, wait=wait):
                    last = pl.multiple_of(ends_ref[e] - TM, TM)
                    cp = pltpu.make_async_copy(zero_buf, xs_ref.at[pl.ds(last, TM)], sem_zero)
                    if wait:
                        cp.wait()
                    else:
                        cp.start()

            def tail(t, carry, wait=wait):
                cp = pltpu.make_async_copy(zero_buf, xs_ref.at[pl.ds(pl.multiple_of(t * TM, TM), TM)], sem_zero)
                if wait:
                    cp.wait()
                else:
                    cp.start()
                return carry

            lax.fori_loop(ends_ref[N_EXPERTS - 1] // TM, R_TILES, tail, 0)

    cp = pltpu.make_async_copy(pos_ref, pos_smem, sem_idx)
    cp.start()
    cp.wait()

    def issue(r, carry):
        for k in range(TOP_K):
            p = pos_smem[k, r]
            pltpu.make_async_copy(h_ref.at[pl.ds(r, 1)], xs_ref.at[pl.ds(p, 1)], sem.at[k]).start(priority=k % 2)
        return carry

    lax.fori_loop(0, TM, issue, 0, unroll=ISSUE_UNROLL)
    for k in range(TOP_K):
        pltpu.make_async_copy(h_ref, xs_ref.at[pl.ds(0, TM)], sem.at[k]).wait()


def _dispatch(ends, pos, h):
    grid_spec = pltpu.PrefetchScalarGridSpec(
        num_scalar_prefetch=1,
        grid=(N_TILES,),
        in_specs=[
            pl.BlockSpec((TOPK_ROWS, TM), lambda i, ends: (0, i)),
            pl.BlockSpec((TM, D_MODEL), lambda i, ends: (i, 0)),
        ],
        out_specs=pl.BlockSpec(memory_space=pl.ANY),
        scratch_shapes=[
            pltpu.VMEM((TM, D_MODEL), F32),
            pltpu.SMEM((TOPK_ROWS, TM), jnp.int32),
            pltpu.SemaphoreType.DMA,
            pltpu.SemaphoreType.DMA((TOP_K,)),
            pltpu.SemaphoreType.DMA,
        ],
    )
    return pl.pallas_call(
        _dispatch_kernel,
        grid_spec=grid_spec,
        out_shape=jax.ShapeDtypeStruct((R_MAX, D_MODEL), F32),
        compiler_params=pltpu.CompilerParams(
            dimension_semantics=("arbitrary",), vmem_limit_bytes=VMEM_LIMIT_BYTES),
        name="moe_dispatch",
    )(ends, pos, h)


def _ffn_kernel(te_ref, nu_ref, nx_ref, xs_ref, wgu_hbm, bgu_ref, wd_hbm, bd_ref, ys_ref,
                wgu_f32, wd_f32, wgu_bf, wd_bf, sem, *, layer):
    i = pl.program_id(0)

    def weight_copies(e):
        return (pltpu.make_async_copy(wgu_hbm.at[layer, e], wgu_f32, sem.at[0]),
                pltpu.make_async_copy(wd_hbm.at[layer, e], wd_f32, sem.at[1]))

    @pl.when(i < nu_ref[0])
    def _():
        expert = te_ref[i]
        new_expert = jnp.logical_or(i == 0, expert != te_ref[jnp.maximum(i - 1, 0)])

        @pl.when(i == 0)
        def _():
            for cp in weight_copies(expert):
                cp.start()

        @pl.when(new_expert)
        def _():
            for cp in weight_copies(expert):
                cp.wait()
            wgu_bf[...] = wgu_f32[...].astype(BF16)
            wd_bf[...] = wd_f32[...].astype(BF16)
            nxt = nx_ref[expert]

            @pl.when(nxt >= 0)
            def _():
                for cp in weight_copies(nxt):
                    cp.start()

        x = xs_ref[...].astype(BF16)
        gu = jnp.dot(x, wgu_bf[...], preferred_element_type=F32) + bgu_ref[...]
        g = jnp.minimum(gu[:, :D_FF], SWIGLU_LIMIT)
        u = jnp.clip(gu[:, D_FF:], -SWIGLU_LIMIT, SWIGLU_LIMIT)
        act = g * jax.nn.sigmoid(SWIGLU_ALPHA * g) * (u + 1.0)
        ys_ref[...] = jnp.dot(act.astype(BF16), wd_bf[...], preferred_element_type=F32) + bd_ref[...]

    @pl.when(i >= nu_ref[0])
    def _():
        ys_ref[...] = jnp.zeros_like(ys_ref)


def _ffn(layer, tile_expert, n_used, next_expert, xs, w_gate_up, b_gate_up, w_down, b_down):
    def row_map(i, te, nu, nx):
        return (jnp.maximum(jnp.minimum(i, nu[0] - 1), 0), 0)

    def b_map(i, te, nu, nx):
        return (layer, te[i], 0, 0)

    grid_spec = pltpu.PrefetchScalarGridSpec(
        num_scalar_prefetch=3,
        grid=(R_TILES,),
        in_specs=[
            pl.BlockSpec((TM, D_MODEL), row_map),
            pl.BlockSpec(memory_space=pl.ANY),
            pl.BlockSpec((None, None, 1, 2 * D_FF), b_map),
            pl.BlockSpec(memory_space=pl.ANY),
            pl.BlockSpec((None, None, 1, D_MODEL), b_map),
        ],
        out_specs=pl.BlockSpec((TM, D_MODEL), lambda i, te, nu, nx: (i, 0)),
        scratch_shapes=[
            pltpu.VMEM((D_MODEL, 2 * D_FF), F32),
            pltpu.VMEM((D_FF, D_MODEL), F32),
            pltpu.VMEM((D_MODEL, 2 * D_FF), BF16),
            pltpu.VMEM((D_FF, D_MODEL), BF16),
            pltpu.SemaphoreType.DMA((2,)),
        ],
    )
    return pl.pallas_call(
        functools.partial(_ffn_kernel, layer=layer),
        grid_spec=grid_spec,
        out_shape=jax.ShapeDtypeStruct((R_MAX, D_MODEL), F32),
        compiler_params=pltpu.CompilerParams(
            dimension_semantics=("arbitrary",), vmem_limit_bytes=VMEM_LIMIT_BYTES),
        name="moe_ffn",
    )(tile_expert, n_used, next_expert, xs, w_gate_up,
      b_gate_up.reshape(DEPTH, N_EXPERTS, 1, 2 * D_FF), w_down,
      b_down.reshape(DEPTH, N_EXPERTS, 1, D_MODEL))


def _combine_kernel(pos_ref, ys_ref, x_ref, gate_ref, mod_ref, out_ref, buf, pos_smem, sem_idx, sem):
    cp = pltpu.make_async_copy(pos_ref, pos_smem, sem_idx)
    cp.start()
    cp.wait()

    def issue(r, carry):
        for k in range(TOP_K):
            p = pos_smem[k, r]
            pltpu.make_async_copy(ys_ref.at[pl.ds(p, 1)], buf.at[k, pl.ds(r, 1)], sem.at[k]).start(priority=k % 2)
        return carry

    lax.fori_loop(0, TM, issue, 0, unroll=ISSUE_UNROLL)
    acc = jnp.zeros((TM, D_MODEL), F32)
    for k in range(TOP_K):
        pltpu.make_async_copy(ys_ref.at[pl.ds(0, TM)], buf.at[k], sem.at[k]).wait()
        acc = acc + gate_ref[:, k:k + 1] * buf[k]
    out_ref[...] = x_ref[...] + mod_ref[0, 5:6, :] * acc


def _combine(pos, ys, x_all, gate, mod_tab):
    return pl.pallas_call(
        _combine_kernel,
        grid=(N_TILES,),
        in_specs=[
            pl.BlockSpec((TOPK_ROWS, TM), lambda i: (0, i)),
            pl.BlockSpec(memory_space=pl.ANY),
            pl.BlockSpec((TM, D_MODEL), lambda i: (i, 0)),
            pl.BlockSpec((TM, TOPK_ROWS), lambda i: (i, 0)),
            pl.BlockSpec((1, 6, D_MODEL), lambda i: (_mod_row(i), 0, 0)),
        ],
        out_specs=pl.BlockSpec((TM, D_MODEL), lambda i: (i, 0)),
        out_shape=jax.ShapeDtypeStruct((N_TOK, D_MODEL), F32),
        scratch_shapes=[
            pltpu.VMEM((TOP_K, TM, D_MODEL), F32),
            pltpu.SMEM((TOPK_ROWS, TM), jnp.int32),
            pltpu.SemaphoreType.DMA,
            pltpu.SemaphoreType.DMA((TOP_K,)),
        ],
        compiler_params=pltpu.CompilerParams(
            dimension_semantics=("arbitrary",), vmem_limit_bytes=VMEM_LIMIT_BYTES),
        name="moe_combine",
    )(pos, ys, x_all, gate, mod_tab)


def _moe_layer(layer, x_all, mod_tab, g_norm2, w_router, b_router, w_gate_up, b_gate_up, w_down, b_down):
    h, topi, gate, rank, counts = _router(x_all, g_norm2[layer], mod_tab, w_router[layer], b_router[layer])
    counts = counts[:, 0]
    padded = ((counts + TM - 1) // TM) * TM
    ends = jnp.cumsum(padded)
    starts = ends - padded
    order = jnp.arange(N_EXPERTS, dtype=jnp.int32)
    first_row = jnp.sum(jnp.where(topi[None] == order[:, None, None], starts[:, None, None], 0), axis=0)
    pos = (first_row + rank).astype(jnp.int32)
    gate = gate.T
    n_used = (ends[-1] // TM).astype(jnp.int32)
    later = jnp.where((padded[None, :] > 0) & (order[None, :] > order[:, None]), order[None, :], N_EXPERTS)
    next_expert = jnp.min(later, axis=1)
    next_expert = jnp.where(next_expert == N_EXPERTS, -1, next_expert).astype(jnp.int32)
    tile_start = jnp.arange(R_TILES, dtype=jnp.int32) * TM
    tile_start = jnp.minimum(tile_start, ends[-1] - 1)
    tile_expert = jnp.sum((ends[None, :] <= tile_start[:, None]).astype(jnp.int32), axis=1)
    tile_expert = jnp.minimum(tile_expert, N_EXPERTS - 1).astype(jnp.int32)
    xs = _dispatch(ends.astype(jnp.int32), pos, h)
    ys = _ffn(layer, tile_expert, n_used.reshape(1), next_expert, xs, w_gate_up, b_gate_up, w_down, b_down)
    return _combine(pos, ys, x_all, gate, mod_tab)


S5_ROWS = 512
S5_HALF_W = S5_WIDTH // 2
S5_HALF_STATES = (S5_GROUPS // 2) * S5_STATE
S5_COL_CHUNK = 512


def _s5_scan_kernel(u_ref, bmat_ref, cmat_ref, a_ref, h0_ref, y_ref, fin_ref, bu_ref, h_ref, *, bsz, steps):
    d = pl.program_id(0)
    c = pl.program_id(1)
    hs = S5_HALF_STATES

    @pl.when(c == 0)
    def _():
        h_ref[...] = h0_ref[...]

    u = u_ref[...].astype(BF16)
    for hf in range(2):
        bu_ref[...] = jnp.dot(u[:, hf * S5_HALF_W:(hf + 1) * S5_HALF_W], bmat_ref[hf],
                              preferred_element_type=F32)
        for j in range(hs // S5_COL_CHUNK):
            re0 = j * S5_COL_CHUNK
            im0 = hs + j * S5_COL_CHUNK
            ar = jnp.broadcast_to(a_ref[hf, 0:1, re0:re0 + S5_COL_CHUNK], (bsz, S5_COL_CHUNK))
            ai = jnp.broadcast_to(a_ref[hf, 1:2, re0:re0 + S5_COL_CHUNK], (bsz, S5_COL_CHUNK))

            def step(t, carry, re0=re0, im0=im0, ar=ar, ai=ai):
                hr, hi = carry
                te = jnp.where(d == 0, t, steps - 1 - t)
                r0 = pl.multiple_of(te * bsz, bsz)
                br = bu_ref[pl.ds(r0, bsz), re0:re0 + S5_COL_CHUNK]
                bi = bu_ref[pl.ds(r0, bsz), im0:im0 + S5_COL_CHUNK]
                nr = ar * hr - ai * hi + br
                ni = ar * hi + ai * hr + bi
                bu_ref[pl.ds(r0, bsz), re0:re0 + S5_COL_CHUNK] = nr
                bu_ref[pl.ds(r0, bsz), im0:im0 + S5_COL_CHUNK] = ni
                return nr, ni

            hr, hi = lax.fori_loop(
                0, steps, step,
                (h_ref[hf, :, re0:re0 + S5_COL_CHUNK], h_ref[hf, :, im0:im0 + S5_COL_CHUNK]), unroll=4)
            h_ref[hf, :, re0:re0 + S5_COL_CHUNK] = hr
            h_ref[hf, :, im0:im0 + S5_COL_CHUNK] = hi
        y_ref[:, hf * S5_HALF_W:(hf + 1) * S5_HALF_W] = jnp.dot(
            bu_ref[...].astype(BF16), cmat_ref[hf], preferred_element_type=F32)

    @pl.when(c == pl.num_programs(1) - 1)
    def _():
        fin_ref[...] = h_ref[...]


def _s5_scan(u_tm, bmat, cmat, acoef, h0, bsz):
    rows = u_tm.shape[0]
    steps = S5_ROWS // bsz
    n_chunks = rows // S5_ROWS

    def chunk_map(d, c):
        return jnp.where(d == 0, c, n_chunks - 1 - c)

    return pl.pallas_call(
        functools.partial(_s5_scan_kernel, bsz=bsz, steps=steps),
        grid=(2, n_chunks),
        in_specs=[
            pl.BlockSpec((S5_ROWS, S5_WIDTH), lambda d, c: (chunk_map(d, c), 0)),
            pl.BlockSpec((None, 2, S5_HALF_W, 2 * S5_HALF_STATES), lambda d, c: (d, 0, 0, 0)),
            pl.BlockSpec((None, 2, 2 * S5_HALF_STATES, S5_HALF_W), lambda d, c: (d, 0, 0, 0)),
            pl.BlockSpec((None, 2, 2, S5_HALF_STATES), lambda d, c: (d, 0, 0, 0)),
            pl.BlockSpec((None, 2, bsz, 2 * S5_HALF_STATES), lambda d, c: (d, 0, 0, 0)),
        ],
        out_specs=[
            pl.BlockSpec((None, S5_ROWS, S5_WIDTH), lambda d, c: (d, chunk_map(d, c), 0)),
            pl.BlockSpec((None, 2, bsz, 2 * S5_HALF_STATES), lambda d, c: (d, 0, 0, 0)),
        ],
        out_shape=[
            jax.ShapeDtypeStruct((2, rows, S5_WIDTH), F32),
            jax.ShapeDtypeStruct((2, 2, bsz, 2 * S5_HALF_STATES), F32),
        ],
        scratch_shapes=[
            pltpu.VMEM((S5_ROWS, 2 * S5_HALF_STATES), F32),
            pltpu.VMEM((2, bsz, 2 * S5_HALF_STATES), F32),
        ],
        compiler_params=pltpu.CompilerParams(
            dimension_semantics=("arbitrary", "arbitrary"), vmem_limit_bytes=VMEM_LIMIT_BYTES),
        name="s5_scan",
    )(u_tm, bmat, cmat, acoef, h0)


def _s5_discretize(lam_re, lam_im, log_dt, b_re, b_im, c_re, c_im):
    eye = jnp.eye(S5_GROUPS // 2, dtype=F32)
    bmats, cmats, acoefs = [], [], []
    for dr in range(2):
        lr = jnp.minimum(lam_re[dr].astype(F32), -1e-4)
        li = lam_im[dr].astype(F32)
        dt = jnp.exp(log_dt[dr].astype(F32))[:, None]
        mag = jnp.exp(lr * dt)
        ar, ai = mag * jnp.cos(li * dt), mag * jnp.sin(li * dt)
        den = lr * lr + li * li
        fr = ((ar - 1.0) * lr + ai * li) / den
        fi = (ai * lr - (ar - 1.0) * li) / den
        br_ = b_re[dr].astype(F32)
        bi_ = b_im[dr].astype(F32)
        bbr = fr[..., None] * br_ - fi[..., None] * bi_
        bbi = fr[..., None] * bi_ + fi[..., None] * br_
        bm, cm, am = [], [], []
        for hf in range(2):
            g = slice(hf * S5_GROUPS // 2, (hf + 1) * S5_GROUPS // 2)

            def bdiag_in(w):
                return jnp.einsum('ab,aph->ahbp', eye, w[g]).reshape(S5_HALF_W, S5_HALF_STATES)

            def bdiag_out(w):
                return jnp.einsum('ab,ahp->apbh', eye, w[g]).reshape(S5_HALF_STATES, S5_HALF_W)

            bm.append(jnp.concatenate([bdiag_in(bbr), bdiag_in(bbi)], axis=1))
            cm.append(jnp.concatenate([bdiag_out(c_re[dr].astype(F32)),
                                       -bdiag_out(c_im[dr].astype(F32))], axis=0))
            am.append(jnp.stack([ar[g].reshape(-1), ai[g].reshape(-1)]))
        bmats.append(jnp.stack(bm))
        cmats.append(jnp.stack(cm))
        acoefs.append(jnp.stack(am))
    return jnp.stack(bmats).astype(BF16), jnp.stack(cmats).astype(BF16), jnp.stack(acoefs)


def _s5_state_to_kernel(h0):
    bsz = h0.shape[0]
    h = h0.astype(F32).reshape(bsz, 2, 2, 2, S5_HALF_STATES)
    return h.transpose(1, 3, 0, 2, 4).reshape(2, 2, bsz, 2 * S5_HALF_STATES)


def _s5_state_from_kernel(fin):
    bsz = fin.shape[2]
    h = fin.reshape(2, 2, bsz, 2, S5_HALF_STATES).transpose(2, 0, 3, 1, 4)
    return h.reshape(bsz, 2, 2, S5_GROUPS, S5_STATE)


TQ = 256
HEAD_LANES = 128
MLA_SCALE = (MLA_NOPE + MLA_ROPE) ** -0.5
DIFF_SCALE = DIFF_HD ** -0.5


def _dot(a, b):
    return jnp.dot(a, b, preferred_element_type=F32)


def _dot_t(a, b):
    return lax.dot_general(a, b, (((1,), (1,)), ((), ())), preferred_element_type=F32)


def _rms_rows(x, g):
    return x * lax.rsqrt(jnp.mean(x * x, axis=-1, keepdims=True) + EPS) * g


def _group(bsz, length, row0, mod_base, mod_stride):
    return dict(bsz=bsz, length=length, nt=length // TQ, tile0=row0 // TQ,
                mod_base=mod_base, mod_stride=mod_stride)


def _params(n_axes):
    return pltpu.CompilerParams(dimension_semantics=("arbitrary",) * n_axes,
                                vmem_limit_bytes=VMEM_LIMIT_BYTES)


def _axial_rope(length, dim):
    rows = length // GRID_W
    row = jnp.repeat(jnp.arange(rows, dtype=F32), GRID_W)
    col = jnp.tile(jnp.arange(GRID_W, dtype=F32), rows)
    n_freq = dim // 4
    inv = ROPE_THETA ** (-jnp.arange(n_freq, dtype=F32) / n_freq)
    ang = jnp.concatenate([row[:, None] * inv, col[:, None] * inv], axis=-1)
    return jnp.cos(ang), jnp.sin(ang)


def _rope_tables(length, dim, lead, reps):
    cos, sin = _axial_rope(length, dim)
    cos_r = jnp.repeat(cos, 2, axis=-1)
    sin_r = jnp.repeat(sin, 2, axis=-1) * jnp.tile(jnp.array([-1.0, 1.0], F32), dim // 2)
    part = HEAD_LANES // reps
    pad = ((0, 0), (lead, part - lead - dim))
    cos_t = jnp.tile(jnp.pad(cos_r, pad, constant_values=1.0), (1, reps))
    sin_t = jnp.tile(jnp.pad(sin_r, pad), (1, reps))
    return cos_t, sin_t


def _swap_pairs(w):
    return w[:, jnp.arange(w.shape[1]) ^ 1]


def _even_in_kernel(*refs, rope):
    if rope:
        (x_ref, g_ref, mod_ref, win_ref, gq_ref, wuq_ref, gkv_ref, cos_ref, sin_ref,
         u_ref, q_ref, ckv_ref, kr_ref) = refs
    else:
        (x_ref, g_ref, mod_ref, win_ref, gq_ref, wuq_ref, gkv_ref,
         u_ref, q_ref, ckv_ref, kr_ref) = refs
    o1 = S5_WIDTH
    o2 = o1 + MLA_Q_LORA
    o3 = o2 + MLA_KV_LORA
    o4 = o3 + HEAD_LANES
    n_in = o4 + HEAD_LANES if rope else o4
    n_q = MLA_HEADS * HEAD_LANES
    h = _rms_rows(x_ref[...], g_ref[...]) * (1.0 + mod_ref[0, 1:2, :]) + mod_ref[0, 0:1, :]
    z = _dot(h.astype(BF16), win_ref[:, :n_in])
    u_ref[...] = z[:, :o1]
    ckv_ref[...] = _rms_rows(z[:, o2:o3], gkv_ref[...])
    qn = _rms_rows(z[:, o1:o2], gq_ref[...]).astype(BF16)
    if rope:
        q2 = _dot(qn, wuq_ref[...])
        cos = cos_ref[...]
        sin = sin_ref[...]
        for hd in range(MLA_HEADS):
            a = hd * HEAD_LANES
            q_ref[:, a:a + HEAD_LANES] = ((q2[:, a:a + HEAD_LANES] * cos
                                           + q2[:, n_q + a:n_q + a + HEAD_LANES] * sin) * MLA_SCALE
                                          ).astype(q_ref.dtype)
        kr_ref[...] = z[:, o3:o4] * cos + z[:, o4:o4 + HEAD_LANES] * sin
    else:
        q_ref[...] = (_dot(qn, wuq_ref[:, :n_q]) * MLA_SCALE).astype(q_ref.dtype)
        kr_ref[...] = z[:, o3:o4]


def _even_in(x_all, grp, g1, mod_tab, win_aug, g_q, wuq2, g_kv, tables):
    bsz, length, nt = grp["bsz"], grp["length"], grp["nt"]
    rope = tables is not None
    rows = bsz * length

    def tok(b, t):
        return (b * nt + t, 0)

    in_specs = [
        pl.BlockSpec((TQ, D_MODEL), lambda b, t: (grp["tile0"] + b * nt + t, 0)),
        pl.BlockSpec((1, D_MODEL), lambda b, t: (0, 0)),
        pl.BlockSpec((1, 6, D_MODEL), lambda b, t: (grp["mod_base"] + b * grp["mod_stride"], 0, 0)),
        pl.BlockSpec(win_aug.shape, lambda b, t: (0, 0)),
        pl.BlockSpec((1, MLA_Q_LORA), lambda b, t: (0, 0)),
        pl.BlockSpec(wuq2.shape, lambda b, t: (0, 0)),
        pl.BlockSpec((1, MLA_KV_LORA), lambda b, t: (0, 0)),
    ]
    args = [x_all, g1.reshape(1, D_MODEL), mod_tab, win_aug, g_q.reshape(1, -1), wuq2, g_kv.reshape(1, -1)]
    if rope:
        in_specs += [pl.BlockSpec((TQ, HEAD_LANES), lambda b, t: (t, 0))] * 2
        args += list(tables)
    return pl.pallas_call(
        functools.partial(_even_in_kernel, rope=rope),
        grid=(bsz, nt),
        in_specs=in_specs,
        out_specs=[
            pl.BlockSpec((TQ, S5_WIDTH), lambda b, t: (t, b)),
            pl.BlockSpec((TQ, MLA_HEADS * HEAD_LANES), tok),
            pl.BlockSpec((TQ, MLA_KV_LORA), tok),
            pl.BlockSpec((TQ, HEAD_LANES), tok),
        ],
        out_shape=[
            jax.ShapeDtypeStruct((length, bsz * S5_WIDTH), F32),
            jax.ShapeDtypeStruct((rows, MLA_HEADS * HEAD_LANES), BF16),
            jax.ShapeDtypeStruct((rows, MLA_KV_LORA), F32),
            jax.ShapeDtypeStruct((rows, HEAD_LANES), F32),
        ],
        compiler_params=_params(2),
        name="even_in",
    )(*args)


def _kv_expand_kernel(x_ref, kr_ref, wk_ref, wv_ref, k_ref, v_ref):
    x = x_ref[...].astype(BF16)
    k = _dot(x, wk_ref[...])
    kr = kr_ref[...]
    for hd in range(MLA_HEADS):
        a = hd * HEAD_LANES
        k_ref[:, a:a + HEAD_LANES] = (k[:, a:a + HEAD_LANES] + kr).astype(k_ref.dtype)
    v_ref[...] = _dot(x, wv_ref[...]).astype(v_ref.dtype)


def _kv_expand(ckv, kr, wk, wv):
    rows = ckv.shape[0]
    tm = 512
    width = MLA_HEADS * HEAD_LANES
    return pl.pallas_call(
        _kv_expand_kernel,
        grid=(rows // tm,),
        in_specs=[pl.BlockSpec((tm, MLA_KV_LORA), lambda i: (i, 0)),
                  pl.BlockSpec((tm, HEAD_LANES), lambda i: (i, 0)),
                  pl.BlockSpec(wk.shape, lambda i: (0, 0)),
                  pl.BlockSpec(wv.shape, lambda i: (0, 0))],
        out_specs=[pl.BlockSpec((tm, width), lambda i: (i, 0))] * 2,
        out_shape=[jax.ShapeDtypeStruct((rows, width), BF16)] * 2,
        compiler_params=_params(1),
        name="kv_expand",
    )(ckv, kr, wk, wv)


def _exp_parts(scores):
    m = functools.reduce(jnp.maximum, [jnp.max(s, axis=-1, keepdims=True) for s in scores])
    es = [jnp.exp(s - m) for s in scores]
    den = functools.reduce(jnp.add, [jnp.sum(e, axis=-1, keepdims=True) for e in es])
    return es, 1.0 / den


def _weighted_values(es, vs):
    o = _dot(es[0].astype(BF16), vs[0])
    for e, v in zip(es[1:], vs[1:]):
        o = o + _dot(e.astype(BF16), v)
    return o


def _mla_attn_kernel(*refs, n_seg):
    q = refs[0][...]
    o_ref = refs[-1]
    ks = [refs[1 + 2 * s][...] for s in range(n_seg)]
    vs = [refs[2 + 2 * s][...] for s in range(n_seg)]
    es, inv = _exp_parts([_dot_t(q, k) for k in ks])
    o_ref[...] = (_weighted_values(es, vs) * inv).astype(o_ref.dtype)


def _mla_attn(q, segs, bsz, length):
    nq = length // TQ
    in_specs = [pl.BlockSpec((TQ, HEAD_LANES), lambda b, h, i: (b * nq + i, h))]
    args = [q]
    for k, v, lk in segs:
        in_specs += [pl.BlockSpec((lk, HEAD_LANES), lambda b, h, i: (b, h))] * 2
        args += [k, v]
    return pl.pallas_call(
        functools.partial(_mla_attn_kernel, n_seg=len(segs)),
        grid=(bsz, MLA_HEADS, nq),
        in_specs=in_specs,
        out_specs=pl.BlockSpec((TQ, HEAD_LANES), lambda b, h, i: (b * nq + i, h)),
        out_shape=jax.ShapeDtypeStruct((bsz * length, MLA_HEADS * HEAD_LANES), BF16),
        compiler_params=_params(3),
        name="mla_attn",
    )(*args)


def _even_out_kernel(u_ref, y_ref, o_ref, x_ref, mod_ref, d_ref, wglu_ref, bglu_ref, ws5_ref, wmla_ref,
                     out_ref):
    y =jax.nn.gelu(d_ref[...] * u_ref[...] + y_ref[0] + y_ref[1])
    s5 = y * jax.nn.sigmoid(_dot(y.astype(BF16), wglu_ref[...]) + bglu_ref[...])
    mix = _dot(s5.astype(BF16), ws5_ref[...]) + _dot(o_ref[...].astype(BF16), wmla_ref[...])
    out_ref[...] = x_ref[...] + mod_ref[0, 2:3, :] * mix


def _even_out(x_all, grp, mod_tab, u_tm, y_dir, o_mla, d_skip, w_glu, b_glu, w_out_s5, w_out_mla):
    bsz, length, nt = grp["bsz"], grp["length"], grp["nt"]

    def xrow(b, t):
        return (grp["tile0"] + b * nt + t, 0)

    full = lambda b, t: (0, 0)
    return pl.pallas_call(
        _even_out_kernel,
        grid=(bsz, nt),
        in_specs=[
            pl.BlockSpec((TQ, S5_WIDTH), lambda b, t: (t, b)),
            pl.BlockSpec((2, TQ, S5_WIDTH), lambda b, t: (0, t, b)),
            pl.BlockSpec((TQ, MLA_HEADS * HEAD_LANES), lambda b, t: (b * nt + t, 0)),
            pl.BlockSpec((TQ, D_MODEL), xrow),
            pl.BlockSpec((1, 6, D_MODEL), lambda b, t: (grp["mod_base"] + b * grp["mod_stride"], 0, 0)),
            pl.BlockSpec((1, S5_WIDTH), full),
            pl.BlockSpec(w_glu.shape, full),
            pl.BlockSpec((1, S5_WIDTH), full),
            pl.BlockSpec(w_out_s5.shape, full),
            pl.BlockSpec(w_out_mla.shape, full),
        ],
        out_specs=pl.BlockSpec((TQ, D_MODEL), xrow),
        out_shape=jax.ShapeDtypeStruct(x_all.shape, F32),
        input_output_aliases={3: 0},
        compiler_params=_params(2),
        name="even_out",
    )(u_tm, y_dir.reshape(2, length, bsz * S5_WIDTH), o_mla, x_all, mod_tab,
      d_skip.reshape(1, S5_WIDTH), w_glu, b_glu.reshape(1, S5_WIDTH), w_out_s5, w_out_mla)


def _odd_in_kernel(*refs, rope):
    if rope:
        x_ref, g_ref, mod_ref, w_ref, cos_ref, sin_ref, q_ref, k_ref, v_ref = refs
    else:
        x_ref, g_ref, mod_ref, w_ref, q_ref, k_ref, v_ref = refs
    w3 = 3 * DIFF_WIDTH
    h = _rms_rows(x_ref[...], g_ref[...]) * (1.0 + mod_ref[0, 1:2, :]) + mod_ref[0, 0:1, :]
    z = _dot(h.astype(BF16), w_ref[...] if rope else w_ref[:, :w3])
    v_ref[...] = z[:, 2 * DIFF_WIDTH:w3].astype(v_ref.dtype)
    if rope:
        cos = cos_ref[...]
        sin = sin_ref[...]
        for hd in range(DIFF_HEADS):
            a = hd * HEAD_LANES
            q_ref[:, a:a + HEAD_LANES] = ((z[:, a:a + HEAD_LANES] * cos
                                           + z[:, w3 + a:w3 + a + HEAD_LANES] * sin) * DIFF_SCALE
                                          ).astype(q_ref.dtype)
            b = DIFF_WIDTH + a
            k_ref[:, a:a + HEAD_LANES] = (z[:, b:b + HEAD_LANES] * cos
                                          + z[:, w3 + b:w3 + b + HEAD_LANES] * sin).astype(k_ref.dtype)
    else:
        q_ref[...] = (z[:, :DIFF_WIDTH] * DIFF_SCALE).astype(q_ref.dtype)
        k_ref[...] = z[:, DIFF_WIDTH:2 * DIFF_WIDTH].astype(k_ref.dtype)


def _odd_in(x_all, grp, g1, mod_tab, w_aug, tables, kv_dtype):
    bsz, length, nt = grp["bsz"], grp["length"], grp["nt"]
    rope = tables is not None
    rows = bsz * length

    def tok(b, t):
        return (b * nt + t, 0)

    in_specs = [
        pl.BlockSpec((TQ, D_MODEL), lambda b, t: (grp["tile0"] + b * nt + t, 0)),
        pl.BlockSpec((1, D_MODEL), lambda b, t: (0, 0)),
        pl.BlockSpec((1, 6, D_MODEL), lambda b, t: (grp["mod_base"] + b * grp["mod_stride"], 0, 0)),
        pl.BlockSpec(w_aug.shape, lambda b, t: (0, 0)),
    ]
    args = [x_all, g1.reshape(1, D_MODEL), mod_tab, w_aug]
    if rope:
        in_specs += [pl.BlockSpec((TQ, HEAD_LANES), lambda b, t: (t, 0))] * 2
        args += list(tables)
    return pl.pallas_call(
        functools.partial(_odd_in_kernel, rope=rope),
        grid=(bsz, nt),
        in_specs=in_specs,
        out_specs=[pl.BlockSpec((TQ, DIFF_WIDTH), tok)] * 3,
        out_shape=[
            jax.ShapeDtypeStruct((rows, DIFF_WIDTH), BF16),
            jax.ShapeDtypeStruct((rows, DIFF_WIDTH), kv_dtype),
            jax.ShapeDtypeStruct((rows, DIFF_WIDTH), kv_dtype),
        ],
        compiler_params=_params(2),
        name="odd_in",
    )(*args)


def _diff_attn_kernel(*refs, n_seg, post_scale):
    lam_ref, q_ref = refs[0], refs[1]
    g_ref, o_ref = refs[-2], refs[-1]
    q = q_ref[...].astype(F32)
    lane = lax.broadcasted_iota(jnp.int32, q.shape, 1)
    q0 = jnp.where(lane < DIFF_HD, q, 0.0).astype(BF16)
    q1 = jnp.where(lane >= DIFF_HD, q, 0.0).astype(BF16)
    ks = [refs[2 + 2 * s][...].astype(BF16) for s in range(n_seg)]
    vs = [refs[3 + 2 * s][...].astype(BF16) for s in range(n_seg)]
    e0, inv0 = _exp_parts([_dot_t(q0, k) for k in ks])
    e1, inv1 = _exp_parts([_dot_t(q1, k) for k in ks])
    o = _weighted_values(e0, vs) * inv0 - lam_ref[0] * (_weighted_values(e1, vs) * inv1)
    o_ref[...] = (_rms_rows(o, g_ref[...]) * post_scale).astype(o_ref.dtype)


def _diff_attn(lam_full, q, segs, g_sub, post_scale, bsz, length):
    nq = length // TQ
    in_specs = [pl.BlockSpec(memory_space=pltpu.SMEM),
                pl.BlockSpec((TQ, HEAD_LANES), lambda b, h, i: (b * nq + i, h))]
    args = [lam_full.reshape(1).astype(F32), q]
    for k, v, lk in segs:
        in_specs += [pl.BlockSpec((lk, HEAD_LANES), lambda b, h, i: (b, h))] * 2
        args += [k, v]
    in_specs.append(pl.BlockSpec((1, HEAD_LANES), lambda b, h, i: (0, 0)))
    args.append(g_sub.reshape(1, HEAD_LANES))
    return pl.pallas_call(
        functools.partial(_diff_attn_kernel, n_seg=len(segs), post_scale=post_scale),
        grid=(bsz, DIFF_HEADS, nq),
        in_specs=in_specs,
        out_specs=pl.BlockSpec((TQ, HEAD_LANES), lambda b, h, i: (b * nq + i, h)),
        out_shape=jax.ShapeDtypeStruct((bsz * length, DIFF_WIDTH), BF16),
        compiler_params=_params(3),
        name="diff_attn",
    )(*args)


def _odd_out_kernel(o_ref, x_ref, mod_ref, w_ref, out_ref):
    out_ref[...] = x_ref[...] + mod_ref[0, 2:3, :] * _dot(o_ref[...], w_ref[...])


def _odd_out(x_all, grp, mod_tab, o, w_out):
    bsz, nt = grp["bsz"], grp["nt"]

    def xrow(b, t):
        return (grp["tile0"] + b * nt + t, 0)

    return pl.pallas_call(
        _odd_out_kernel,
        grid=(bsz, nt),
        in_specs=[
            pl.BlockSpec((TQ, DIFF_WIDTH), lambda b, t: (b * nt + t, 0)),
            pl.BlockSpec((TQ, D_MODEL), xrow),
            pl.BlockSpec((1, 6, D_MODEL), lambda b, t: (grp["mod_base"] + b * grp["mod_stride"], 0, 0)),
            pl.BlockSpec(w_out.shape, lambda b, t: (0, 0)),
        ],
        out_specs=pl.BlockSpec((TQ, D_MODEL), xrow),
        out_shape=jax.ShapeDtypeStruct(x_all.shape, F32),
        input_output_aliases={1: 0},
        compiler_params=_params(2),
        name="odd_out",
    )(o, x_all, mod_tab, w_out)


def _final_norm_kernel(x_ref, g_ref, o_ref):
    o_ref[...] = _rms_rows(x_ref[...], g_ref[...])


def _final_norm(x_all, g, row0, rows):
    tm = 512
    return pl.pallas_call(
        _final_norm_kernel,
        grid=(rows // tm,),
        in_specs=[pl.BlockSpec((tm, D_MODEL), lambda i: (row0 // tm + i, 0)),
                  pl.BlockSpec((1, D_MODEL), lambda i: (0, 0))],
        out_specs=pl.BlockSpec((tm, D_MODEL), lambda i: (i, 0)),
        out_shape=jax.ShapeDtypeStruct((rows, D_MODEL), F32),
        compiler_params=_params(1),
        name="final_norm",
    )(x_all, g.reshape(1, D_MODEL))


def _pad_head_lanes(x, lead):
    return jnp.pad(x, ((0, 0), (lead, HEAD_LANES - lead - x.shape[1])))


def _even_weights(w_in, w_out, w_uq, w_ukv, w_glu):
    o3 = S5_WIDTH + MLA_Q_LORA + MLA_KV_LORA
    w_kr = w_in[:, o3:]
    win_aug = jnp.concatenate(
        [w_in[:, :o3], _pad_head_lanes(w_kr, MLA_NOPE), _pad_head_lanes(_swap_pairs(w_kr), MLA_NOPE)], axis=1)
    dq = MLA_NOPE + MLA_ROPE
    plain, swapped = [], []
    for hd in range(MLA_HEADS):
        wn = w_uq[:, hd * dq:hd * dq + MLA_NOPE]
        wr = w_uq[:, hd * dq + MLA_NOPE:(hd + 1) * dq]
        plain.append(jnp.pad(jnp.concatenate([wn, wr], axis=1), ((0, 0), (0, HEAD_LANES - dq))))
        swapped.append(_pad_head_lanes(_swap_pairs(wr), MLA_NOPE))
    wuq2 = jnp.concatenate(plain + swapped, axis=1)
    w_mla = w_out[S5_WIDTH:].reshape(MLA_HEADS, MLA_V, D_MODEL)
    w_out_mla = jnp.pad(w_mla, ((0, 0), (0, HEAD_LANES - MLA_V), (0, 0))).reshape(MLA_HEADS * HEAD_LANES, D_MODEL)
    w_kv = w_ukv.reshape(MLA_KV_LORA, MLA_HEADS, MLA_NOPE + MLA_V)
    wk = jnp.pad(w_kv[:, :, :MLA_NOPE], ((0, 0), (0, 0), (0, HEAD_LANES - MLA_NOPE)))
    wv = jnp.pad(w_kv[:, :, MLA_NOPE:], ((0, 0), (0, 0), (0, HEAD_LANES - MLA_V)))
    w_kv = (wk.reshape(MLA_KV_LORA, -1).astype(BF16), wv.reshape(MLA_KV_LORA, -1).astype(BF16))
    return (win_aug.astype(BF16), wuq2.astype(BF16), w_kv, w_glu.astype(BF16),
            w_out[:S5_WIDTH].astype(BF16), w_out_mla.astype(BF16))


def _even_layer(x_all, grp, g1, mod_tab, ew, s5m, g_q, g_kv, d_skip, b_glu, h0, ctx, tables):
    win_aug, wuq2, w_kv, w_glu, w_out_s5, w_out_mla = ew
    bmat, cmat, acoef = s5m
    bsz, length = grp["bsz"], grp["length"]
    u_tm, q, ckv, kr = _even_in(x_all, grp, g1, mod_tab, win_aug, g_q, wuq2, g_kv, tables)
    y_dir, fin = _s5_scan(u_tm.reshape(length * bsz, S5_WIDTH), bmat, cmat, acoef, h0, bsz)
    segs = [(*_kv_expand(ckv, kr, *w_kv), length)]
    if ctx is not None:
        segs.append((*_kv_expand(*ctx, *w_kv), PAST_LEN))
    o_mla = _mla_attn(q, segs, bsz, length)
    x_all = _even_out(x_all, grp, mod_tab, u_tm, y_dir, o_mla, d_skip, w_glu, b_glu, w_out_s5, w_out_mla)
    return x_all, fin, ckv, kr


def _odd_layer(x_all, grp, g1, mod_tab, w_aug, w_out, lam_full, g_sub, post_scale, ctx, tables, kv_dtype):
    bsz, length = grp["bsz"], grp["length"]
    q, k, v = _odd_in(x_all, grp, g1, mod_tab, w_aug, tables, kv_dtype)
    segs = [(k, v, length)]
    if ctx is not None:
        segs.append((ctx[0], ctx[1], PAST_LEN))
    o = _diff_attn(lam_full, q, segs, g_sub, post_scale, bsz, length)
    return _odd_out(x_all, grp, mod_tab, o, w_out), k, v


def kernel(x_prompt, x_sample, state_s5, cache_mla, cache_diff_k, cache_diff_v, c, c_ctx, w_mod, b_mod, g_norm1, g_norm2, g_final, w_in_even, w_out_even, s5_lam_re, s5_lam_im, s5_log_dt, s5_b_re, s5_b_im, s5_c_re, s5_c_im, s5_d, s5_w_glu, s5_b_glu, mla_g_q, mla_w_uq, mla_g_kv, mla_w_ukv, w_in_odd, w_out_odd, diff_lam, diff_g_sub, w_router, b_router, w_gate_up, b_gate_up, w_down, b_down):
    tab_mla = _rope_tables(DEC_SEQ, MLA_ROPE, MLA_NOPE, 1)
    tab_diff = _rope_tables(DEC_SEQ, DIFF_HD, 0, 2)
    grp_p = _group(BATCH, SEQ, 0, 0, 0)
    grp_s = _group(DEC_BATCH, DEC_SEQ, N_PROMPT, 1, 1)
    x_all = jnp.concatenate([x_prompt.reshape(N_PROMPT, D_MODEL), x_sample.reshape(N_SAMPLE, D_MODEL)], axis=0)
    cond = jax.nn.silu(jnp.concatenate([c_ctx[None], c], axis=0))
    new_s5, new_mla, new_k, new_v = [], [], [], []
    for l in range(DEPTH):
        mod_tab = (cond @ w_mod[l] + b_mod[l]).reshape(1 + DEC_BATCH, 6, D_MODEL)
        i = l // 2
        if l % 2 == 0:
            ew = _even_weights(w_in_even[i], w_out_even[i], mla_w_uq[i], mla_w_ukv[i], s5_w_glu[i])
            s5m = _s5_discretize(s5_lam_re[i], s5_lam_im[i], s5_log_dt[i], s5_b_re[i], s5_b_im[i],
                                 s5_c_re[i], s5_c_im[i])
            common = (ew, s5m, mla_g_q[i], mla_g_kv[i], s5_d[i], s5_b_glu[i])
            h0_p = jnp.zeros((2, 2, BATCH, 2 * S5_HALF_STATES), F32)
            x_all, fin, ckv, kr = _even_layer(x_all, grp_p, g_norm1[l], mod_tab, *common, h0_p, None, None)
            new_s5.append(_s5_state_from_kernel(fin))
            new_mla.append(jnp.concatenate([ckv, kr[:, MLA_NOPE:MLA_NOPE + MLA_ROPE]], axis=1)
                           .reshape(BATCH, SEQ, MLA_KV_LORA + MLA_ROPE))
            lat_ctx = cache_mla[:, i].astype(F32).reshape(DEC_BATCH * PAST_LEN, MLA_KV_LORA + MLA_ROPE)
            ctx = (lat_ctx[:, :MLA_KV_LORA], _pad_head_lanes(lat_ctx[:, MLA_KV_LORA:], MLA_NOPE))
            x_all, _, _, _ = _even_layer(x_all, grp_s, g_norm1[l], mod_tab, *common,
                                         _s5_state_to_kernel(state_s5[:, i]), ctx, tab_mla)
        else:
            lam_init = 0.8 - 0.6 * math.exp(-0.3 * l)
            lamf = diff_lam[i].astype(F32)
            lam_full = jnp.exp(jnp.sum(lamf[0] * lamf[1])) - jnp.exp(jnp.sum(lamf[2] * lamf[3])) + lam_init
            w_qk = w_in_odd[i][:, :2 * DIFF_WIDTH]
            w_aug = jnp.concatenate([w_in_odd[i], _swap_pairs(w_qk)], axis=1).astype(BF16)
            w_out = w_out_odd[i].astype(BF16)
            odd = (w_aug, w_out, lam_full, diff_g_sub[i], 1.0 - lam_init)
            x_all, kp, vp = _odd_layer(x_all, grp_p, g_norm1[l], mod_tab, *odd, None, None, F32)
            new_k.append(kp.reshape(BATCH, SEQ, DIFF_HEADS, 2, DIFF_HD))
            new_v.append(vp.reshape(BATCH, SEQ, DIFF_HEADS, 2 * DIFF_HD))
            ctx = (cache_diff_k[:, i].reshape(DEC_BATCH * PAST_LEN, DIFF_WIDTH).astype(BF16),
                   cache_diff_v[:, i].reshape(DEC_BATCH * PAST_LEN, DIFF_WIDTH).astype(BF16))
            x_all, _, _ = _odd_layer(x_all, grp_s, g_norm1[l], mod_tab, *odd, ctx, tab_diff, BF16)
        x_all = _moe_layer(l, x_all, mod_tab, g_norm2, w_router, b_router,
                           w_gate_up, b_gate_up, w_down, b_down)
    y_prompt = _final_norm(x_all, g_final, 0, N_PROMPT)
    y_sample = _final_norm(x_all, g_final, N_PROMPT, N_SAMPLE)
    return (y_prompt.reshape(BATCH, SEQ, D_MODEL), y_sample.reshape(DEC_BATCH, DEC_SEQ, D_MODEL),
            jnp.stack(new_s5, axis=1), jnp.stack(new_mla, axis=1),
            jnp.stack(new_k, axis=1), jnp.stack(new_v, axis=1))
```

```python
import functools
import math

import jax
import jax.numpy as jnp
from jax import lax
from jax.experimental import pallas as pl
from jax.experimental.pallas import tpu as pltpu

D_MODEL = 1024
BATCH = 16
SEQ = 256
DEPTH = 4
DEC_BATCH = 8
DEC_SEQ = 1024
PAST_LEN = 512
GRID_W = 64
N_EVEN = (DEPTH + 1) // 2
N_ODD = DEPTH // 2
S5_WIDTH = D_MODEL // 2
S5_GROUP = 16
S5_GROUPS = S5_WIDTH // S5_GROUP
S5_STATE = 64
MLA_HEADS = 8
MLA_NOPE = 64
MLA_ROPE = 32
MLA_V = 64
MLA_Q_LORA = D_MODEL // 4
MLA_KV_LORA = D_MODEL // 8
MLA_WIDTH = MLA_HEADS * MLA_V
EVEN_IN = S5_WIDTH + MLA_Q_LORA + MLA_KV_LORA + MLA_ROPE
EVEN_OUT = S5_WIDTH + MLA_WIDTH
DIFF_HEADS = 8
DIFF_HD = D_MODEL // (2 * DIFF_HEADS)
DIFF_WIDTH = DIFF_HEADS * 2 * DIFF_HD
N_EXPERTS = 32
TOP_K = 4
D_FF = D_MODEL
SWIGLU_LIMIT = 7.0
SWIGLU_ALPHA = 1.702
ROPE_THETA = 10000.0
Q_BLOCK = 128
EPS = 1e-6

N_PROMPT = BATCH * SEQ
N_SAMPLE = DEC_BATCH * DEC_SEQ
N_TOK = N_PROMPT + N_SAMPLE

LANES = 128
VMEM_LIMIT_BYTES = 56 * 1024 * 1024

TM = 256
N_TILES = N_TOK // TM
R_TILES = N_TOK * TOP_K // TM + N_EXPERTS
R_MAX = R_TILES * TM
TOPK_ROWS = 8

F32 = jnp.float32
BF16 = jnp.bfloat16


def _mod_row(i):
    t0 = i * TM
    return jnp.where(t0 < N_PROMPT, 0, 1 + (t0 - N_PROMPT) // DEC_SEQ)


def _router_kernel(x_ref, g_ref, mod_ref, wr_ref, br_ref,
                   h_ref, topi_ref, gate_ref, rank_ref, counts_ref, carry_ref):
    i = pl.program_id(0)

    @pl.when(i == 0)
    def _():
        carry_ref[...] = jnp.zeros_like(carry_ref)

    x = x_ref[...]
    ms = jnp.mean(x * x, axis=-1, keepdims=True)
    y = x * lax.rsqrt(ms + EPS) * g_ref[...]
    shift = mod_ref[0, 3:4, :]
    scale = mod_ref[0, 4:5, :]
    h = y * (1.0 + scale) + shift
    h_ref[...] = h

    hi = h.astype(BF16)
    lo = (h - hi.astype(F32)).astype(BF16)
    w_hi = wr_ref[0]
    logits = _dot_t(w_hi, hi) + (_dot_t(wr_ref[1], hi) + _dot_t(w_hi, lo)) + br_ref[...]
    sub_e = lax.broadcasted_iota(jnp.int32, logits.shape, 0)
    work = logits
    vals, hits = [], []
    sel = jnp.zeros(logits.shape, F32)
    for _ in range(TOP_K):
        m = jnp.max(work, axis=0, keepdims=True)
        idx = jnp.min(jnp.where(work == m, sub_e, N_EXPERTS), axis=0, keepdims=True)
        hit = sub_e == idx
        vals.append(m)
        hits.append((hit, idx))
        sel = jnp.where(hit, 1.0, sel)
        work = jnp.where(hit, -jnp.inf, work)
    es = [jnp.exp(v - vals[0]) for v in vals]
    inv = 1.0 / (es[0] + es[1] + es[2] + es[3])

    row = lax.broadcasted_iota(jnp.int32, (TM, TM), 0)
    col = lax.broadcasted_iota(jnp.int32, (TM, TM), 1)
    earlier = jnp.where(row < col, 1.0, 0.0).astype(BF16)
    before = _dot(sel.astype(BF16), earlier) + carry_ref[...]
    carry_ref[...] += jnp.sum(sel, axis=1, keepdims=True)
    counts_ref[...] = carry_ref[...].astype(jnp.int32)

    sub_k = lax.broadcasted_iota(jnp.int32, (TOPK_ROWS, TM), 0)
    topi = jnp.zeros((TOPK_ROWS, TM), jnp.int32)
    gate = jnp.zeros((TOPK_ROWS, TM), F32)
    rank = jnp.zeros((TOPK_ROWS, TM), jnp.int32)
    for k in range(TOP_K):
        hit, idx = hits[k]
        rk = jnp.sum(jnp.where(hit, before, 0.0), axis=0, keepdims=True)
        topi = jnp.where(sub_k == k, idx, topi)
        gate = jnp.where(sub_k == k, es[k] * inv, gate)
        rank = jnp.where(sub_k == k, rk.astype(jnp.int32), rank)
    topi_ref[...] = topi
    gate_ref[...] = gate
    rank_ref[...] = rank


def _router(x_all, g, mod_tab, w_router, b_router):
    w_t = w_router.astype(F32).T
    w_hi = w_t.astype(BF16)
    w_split = jnp.stack([w_hi, (w_t - w_hi.astype(F32)).astype(BF16)])
    return pl.pallas_call(
        _router_kernel,
        grid=(N_TILES,),
        in_specs=[
            pl.BlockSpec((TM, D_MODEL), lambda i: (i, 0)),
            pl.BlockSpec((1, D_MODEL), lambda i: (0, 0)),
            pl.BlockSpec((1, 6, D_MODEL), lambda i: (_mod_row(i), 0, 0)),
            pl.BlockSpec((2, N_EXPERTS, D_MODEL), lambda i: (0, 0, 0)),
            pl.BlockSpec((N_EXPERTS, 1), lambda i: (0, 0)),
        ],
        out_specs=[
            pl.BlockSpec((TM, D_MODEL), lambda i: (i, 0)),
            pl.BlockSpec((TOPK_ROWS, TM), lambda i: (0, i)),
            pl.BlockSpec((TOPK_ROWS, TM), lambda i: (0, i)),
            pl.BlockSpec((TOPK_ROWS, TM), lambda i: (0, i)),
            pl.BlockSpec((N_EXPERTS, 1), lambda i: (0, 0)),
        ],
        out_shape=[
            jax.ShapeDtypeStruct((N_TOK, D_MODEL), F32),
            jax.ShapeDtypeStruct((TOPK_ROWS, N_TOK), jnp.int32),
            jax.ShapeDtypeStruct((TOPK_ROWS, N_TOK), F32),
            jax.ShapeDtypeStruct((TOPK_ROWS, N_TOK), jnp.int32),
            jax.ShapeDtypeStruct((N_EXPERTS, 1), jnp.int32),
        ],
        scratch_shapes=[pltpu.VMEM((N_EXPERTS, 1), F32)],
        compiler_params=pltpu.CompilerParams(
            dimension_semantics=("arbitrary",), vmem_limit_bytes=VMEM_LIMIT_BYTES),
        name="moe_router",
    )(x_all, g.reshape(1, D_MODEL), mod_tab, w_split, b_router.reshape(N_EXPERTS, 1))


ISSUE_UNROLL = 4


def _dispatch_kernel(ends_ref, pos_ref, h_ref, xs_ref, zero_buf, pos_smem, sem_idx, sem, sem_zero):
    i = pl.program_id(0)

    @pl.when(i == 0)
    def _():
        zero_buf[...] = jnp.zeros_like(zero_buf)

        for wait in (False, True):
            for e in range(N_EXPERTS):
                start = ends_ref[e - 1] if e > 0 else 0

                @pl.when(ends_ref[e] > start)
                def _(e=e, wait=wait):
                    last = pl.multiple_of(ends_ref[e] - TM, TM)
                    cp = pltpu.make_async_copy(zero_buf, xs_ref.at[pl.ds(last, TM)], sem_zero)
                    if wait:
                        cp.wait()
                    else:
                        cp.start()

            def tail(t, carry, wait=wait):
                cp = pltpu.make_async_copy(zero_buf, xs_ref.at[pl.ds(pl.multiple_of(t * TM, TM), TM)], sem_zero)
                if wait:
                    cp.wait()
                else:
                    cp.start()
                return carry

            lax.fori_loop(ends_ref[N_EXPERTS - 1] // TM, R_TILES, tail, 0)

    cp = pltpu.make_async_copy(pos_ref, pos_smem, sem_idx)
    cp.start()
    cp.wait()

    def issue(r, carry):
        for k in range(TOP_K):
            p = pos_smem[k, r]
            pltpu.make_async_copy(h_ref.at[pl.ds(r, 1)], xs_ref.at[pl.ds(p, 1)], sem.at[k]).start(priority=k % 2)
        return carry

    lax.fori_loop(0, TM, issue, 0, unroll=ISSUE_UNROLL)
    for k in range(TOP_K):
        pltpu.make_async_copy(h_ref, xs_ref.at[pl.ds(0, TM)], sem.at[k]).wait()


def _dispatch(ends, pos, h):
    grid_spec = pltpu.PrefetchScalarGridSpec(
        num_scalar_prefetch=1,
        grid=(N_TILES,),
        in_specs=[
            pl.BlockSpec((TOPK_ROWS, TM), lambda i, ends: (0, i)),
            pl.BlockSpec((TM, D_MODEL), lambda i, ends: (i, 0)),
        ],
        out_specs=pl.BlockSpec(memory_space=pl.ANY),
        scratch_shapes=[
            pltpu.VMEM((TM, D_MODEL), F32),
            pltpu.SMEM((TOPK_ROWS, TM), jnp.int32),
            pltpu.SemaphoreType.DMA,
            pltpu.SemaphoreType.DMA((TOP_K,)),
            pltpu.SemaphoreType.DMA,
        ],
    )
    return pl.pallas_call(
        _dispatch_kernel,
        grid_spec=grid_spec,
        out_shape=jax.ShapeDtypeStruct((R_MAX, D_MODEL), F32),
        compiler_params=pltpu.CompilerParams(
            dimension_semantics=("arbitrary",), vmem_limit_bytes=VMEM_LIMIT_BYTES),
        name="moe_dispatch",
    )(ends, pos, h)


def _ffn_kernel(te_ref, nu_ref, nx_ref, xs_ref, wgu_hbm, bgu_ref, wd_hbm, bd_ref, ys_ref,
                wgu_f32, wd_f32, wgu_bf, wd_bf, sem, *, layer):
    i = pl.program_id(0)

    def weight_copies(e):
        return (pltpu.make_async_copy(wgu_hbm.at[layer, e], wgu_f32, sem.at[0]),
                pltpu.make_async_copy(wd_hbm.at[layer, e], wd_f32, sem.at[1]))

    @pl.when(i < nu_ref[0])
    def _():
        expert = te_ref[i]
        new_expert = jnp.logical_or(i == 0, expert != te_ref[jnp.maximum(i - 1, 0)])

        @pl.when(i == 0)
        def _():
            for cp in weight_copies(expert):
                cp.start()

        @pl.when(new_expert)
        def _():
            for cp in weight_copies(expert):
                cp.wait()
            wgu_bf[...] = wgu_f32[...].astype(BF16)
            wd_bf[...] = wd_f32[...].astype(BF16)
            nxt = nx_ref[expert]

            @pl.when(nxt >= 0)
            def _():
                for cp in weight_copies(nxt):
                    cp.start()

        x = xs_ref[...].astype(BF16)
        gu = jnp.dot(x, wgu_bf[...], preferred_element_type=F32) + bgu_ref[...]
        g = jnp.minimum(gu[:, :D_FF], SWIGLU_LIMIT)
        u = jnp.clip(gu[:, D_FF:], -SWIGLU_LIMIT, SWIGLU_LIMIT)
        act = g * jax.nn.sigmoid(SWIGLU_ALPHA * g) * (u + 1.0)
        ys_ref[...] = jnp.dot(act.astype(BF16), wd_bf[...], preferred_element_type=F32) + bd_ref[...]

    @pl.when(i >= nu_ref[0])
    def _():
        ys_ref[...] = jnp.zeros_like(ys_ref)


def _ffn(layer, tile_expert, n_used, next_expert, xs, w_gate_up, b_gate_up, w_down, b_down):
    def row_map(i, te, nu, nx):
        return (jnp.maximum(jnp.minimum(i, nu[0] - 1), 0), 0)

    def b_map(i, te, nu, nx):
        return (layer, te[i], 0, 0)

    grid_spec = pltpu.PrefetchScalarGridSpec(
        num_scalar_prefetch=3,
        grid=(R_TILES,),
        in_specs=[
            pl.BlockSpec((TM, D_MODEL), row_map),
            pl.BlockSpec(memory_space=pl.ANY),
            pl.BlockSpec((None, None, 1, 2 * D_FF), b_map),
            pl.BlockSpec(memory_space=pl.ANY),
            pl.BlockSpec((None, None, 1, D_MODEL), b_map),
        ],
        out_specs=pl.BlockSpec((TM, D_MODEL), lambda i, te, nu, nx: (i, 0)),
        scratch_shapes=[
            pltpu.VMEM((D_MODEL, 2 * D_FF), F32),
            pltpu.VMEM((D_FF, D_MODEL), F32),
            pltpu.VMEM((D_MODEL, 2 * D_FF), BF16),
            pltpu.VMEM((D_FF, D_MODEL), BF16),
            pltpu.SemaphoreType.DMA((2,)),
        ],
    )
    return pl.pallas_call(
        functools.partial(_ffn_kernel, layer=layer),
        grid_spec=grid_spec,
        out_shape=jax.ShapeDtypeStruct((R_MAX, D_MODEL), F32),
        compiler_params=pltpu.CompilerParams(
            dimension_semantics=("arbitrary",), vmem_limit_bytes=VMEM_LIMIT_BYTES),
        name="moe_ffn",
    )(tile_expert, n_used, next_expert, xs, w_gate_up,
      b_gate_up.reshape(DEPTH, N_EXPERTS, 1, 2 * D_FF), w_down,
      b_down.reshape(DEPTH, N_EXPERTS, 1, D_MODEL))


def _combine_kernel(pos_ref, ys_ref, x_ref, gate_ref, mod_ref, out_ref, buf, pos_smem, sem_idx, sem):
    cp = pltpu.make_async_copy(pos_ref, pos_smem, sem_idx)
    cp.start()
    cp.wait()

    def issue(r, carry):
        for k in range(TOP_K):
            p = pos_smem[k, r]
            pltpu.make_async_copy(ys_ref.at[pl.ds(p, 1)], buf.at[k, pl.ds(r, 1)], sem.at[k]).start(priority=k % 2)
        return carry

    lax.fori_loop(0, TM, issue, 0, unroll=ISSUE_UNROLL)
    acc = jnp.zeros((TM, D_MODEL), F32)
    for k in range(TOP_K):
        pltpu.make_async_copy(ys_ref.at[pl.ds(0, TM)], buf.at[k], sem.at[k]).wait()
        acc = acc + gate_ref[:, k:k + 1] * buf[k]
    out_ref[...] = x_ref[...] + mod_ref[0, 5:6, :] * acc


def _combine(pos, ys, x_all, gate, mod_tab):
    return pl.pallas_call(
        _combine_kernel,
        grid=(N_TILES,),
        in_specs=[
            pl.BlockSpec((TOPK_ROWS, TM), lambda i: (0, i)),
            pl.BlockSpec(memory_space=pl.ANY),
            pl.BlockSpec((TM, D_MODEL), lambda i: (i, 0)),
            pl.BlockSpec((TM, TOPK_ROWS), lambda i: (i, 0)),
            pl.BlockSpec((1, 6, D_MODEL), lambda i: (_mod_row(i), 0, 0)),
        ],
        out_specs=pl.BlockSpec((TM, D_MODEL), lambda i: (i, 0)),
        out_shape=jax.ShapeDtypeStruct((N_TOK, D_MODEL), F32),
        scratch_shapes=[
            pltpu.VMEM((TOP_K, TM, D_MODEL), F32),
            pltpu.SMEM((TOPK_ROWS, TM), jnp.int32),
            pltpu.SemaphoreType.DMA,
            pltpu.SemaphoreType.DMA((TOP_K,)),
        ],
        compiler_params=pltpu.CompilerParams(
            dimension_semantics=("arbitrary",), vmem_limit_bytes=VMEM_LIMIT_BYTES),
        name="moe_combine",
    )(pos, ys, x_all, gate, mod_tab)


CODE_SHIFT = 16
DST_MASK = (1 << CODE_SHIFT) - 1
assert TOP_K * N_TOK + N_EXPERTS * TM <= DST_MASK
PAD_BASE = TOP_K * N_TOK
OUT_ROWS = PAD_BASE + N_EXPERTS * TM


def _invert_kernel(ends_ref, counts_ref, pos_ref, codes_ref, pos_smem, sem):
    i = pl.program_id(0)

    @pl.when(i == 0)
    def _():
        for e in range(N_EXPERTS):
            def mark_padding(r, carry, e=e):
                codes_ref[r] = PAD_BASE + e * TM + (r & (TM - 1))
                return carry

            start = ends_ref[e - 1] if e > 0 else 0
            lax.fori_loop(start + counts_ref[e], ends_ref[e], mark_padding, 0)

        def mark_unused(r, carry):
            codes_ref[r] = PAD_BASE
            return carry

        lax.fori_loop(ends_ref[N_EXPERTS - 1], R_MAX, mark_unused, 0)

    cp = pltpu.make_async_copy(pos_ref, pos_smem, sem)
    cp.start()
    cp.wait()

    def body(j, carry):
        for u in range(ISSUE_UNROLL):
            r = j * ISSUE_UNROLL + u
            for k in range(TOP_K):
                codes_ref[pos_smem[k, r]] = ((i * TM + r) << CODE_SHIFT) | ((i * TOP_K + k) * TM + r)
        return carry

    lax.fori_loop(0, TM // ISSUE_UNROLL, body, 0)


def _invert(ends, counts, pos):
    grid_spec = pltpu.PrefetchScalarGridSpec(
        num_scalar_prefetch=2,
        grid=(N_TILES,),
        in_specs=[pl.BlockSpec((TOPK_ROWS, TM), lambda i, ends, counts: (0, i))],
        out_specs=pl.BlockSpec(memory_space=pltpu.SMEM),
        scratch_shapes=[pltpu.SMEM((TOPK_ROWS, TM), jnp.int32), pltpu.SemaphoreType.DMA],
    )
    return pl.pallas_call(
        _invert_kernel,
        grid_spec=grid_spec,
        out_shape=jax.ShapeDtypeStruct((R_MAX,), jnp.int32),
        compiler_params=_params(1),
        name="moe_invert",
    )(ends, counts, pos)


def _ffn_fused_kernel(te_ref, nu_ref, nx_ref, codes_ref, h_hbm, wgu_hbm, bgu_ref, wd_hbm, bd_ref, out_hbm,
                      xbuf, ybuf, wgu_f32, wd_f32, wgu_bf, wd_bf, wsem, gsem, ssem, *, layer):
    i = pl.program_id(0)
    n_used = nu_ref[0]

    def weight_copies(e):
        return (pltpu.make_async_copy(wgu_hbm.at[layer, e], wgu_f32, wsem.at[0]),
                pltpu.make_async_copy(wd_hbm.at[layer, e], wd_f32, wsem.at[1]))

    def start_gather(tile, slot):
        base = tile * TM
        for r in range(TM):
            token = codes_ref[base + r] >> CODE_SHIFT
            pltpu.make_async_copy(h_hbm.at[pl.ds(token, 1)], xbuf.at[slot, pl.ds(r, 1)],
                                  gsem.at[slot]).start(priority=r % 2)

    def wait_gather(slot):
        pltpu.make_async_copy(h_hbm.at[pl.ds(0, TM)], xbuf.at[slot], gsem.at[slot]).wait()

    def start_scatter(tile, slot):
        base = tile * TM
        for r in range(TM):
            pltpu.make_async_copy(ybuf.at[slot, pl.ds(r, 1)], out_hbm.at[pl.ds(codes_ref[base + r] & DST_MASK, 1)],
                                  ssem.at[slot]).start(priority=r % 2)

    def wait_scatter(slot):
        pltpu.make_async_copy(ybuf.at[slot], out_hbm.at[pl.ds(0, TM)], ssem.at[slot]).wait()

    @pl.when(i == 0)
    def _():
        ybuf[1] = jnp.zeros((TM, D_MODEL), F32)
        for wait in (False, True):
            for e in range(N_EXPERTS):
                cp = pltpu.make_async_copy(ybuf.at[1], out_hbm.at[pl.ds(PAD_BASE + e * TM, TM)], ssem.at[1])
                if wait:
                    cp.wait()
                else:
                    cp.start()
        start_gather(0, 0)

    def tile_step(slot):
        other = 1 - slot
        wait_gather(slot)

        @pl.when(i >= 2)
        def _():
            wait_scatter(slot)

        start_gather(jnp.minimum(i + 1, R_TILES - 1), other)

        x = xbuf[slot].astype(BF16)
        gu = jnp.dot(x, wgu_bf[...], preferred_element_type=F32) + bgu_ref[...]
        g = jnp.minimum(gu[:, :D_FF], SWIGLU_LIMIT)
        u = jnp.clip(gu[:, D_FF:], -SWIGLU_LIMIT, SWIGLU_LIMIT)
        act = g * jax.nn.sigmoid(SWIGLU_ALPHA * g) * (u + 1.0)
        ybuf[slot] = jnp.dot(act.astype(BF16), wd_bf[...], preferred_element_type=F32) + bd_ref[...]
        start_scatter(i, slot)

        @pl.when(i == n_used - 1)
        def _():
            wait_gather(other)
            wait_scatter(slot)

            @pl.when(i >= 1)
            def _():
                wait_scatter(other)

    @pl.when(i < n_used)
    def _():
        expert = te_ref[i]
        new_expert = jnp.logical_or(i == 0, expert != te_ref[jnp.maximum(i - 1, 0)])

        @pl.when(i == 0)
        def _():
            for cp in weight_copies(expert):
                cp.start()

        @pl.when(new_expert)
        def _():
            for cp in weight_copies(expert):
                cp.wait()
            wgu_bf[...] = wgu_f32[...].astype(BF16)
            wd_bf[...] = wd_f32[...].astype(BF16)
            nxt = nx_ref[expert]

            @pl.when(nxt >= 0)
            def _():
                for cp in weight_copies(nxt):
                    cp.start()

        for slot in range(2):
            pl.when(i % 2 == slot)(functools.partial(tile_step, slot))


def _ffn_fused(layer, tile_expert, n_used, next_expert, codes, h, w_gate_up, b_gate_up, w_down, b_down):
    def b_map(i, te, nu, nx, codes):
        return (layer, te[i], 0, 0)

    grid_spec = pltpu.PrefetchScalarGridSpec(
        num_scalar_prefetch=4,
        grid=(R_TILES,),
        in_specs=[
            pl.BlockSpec(memory_space=pl.ANY),
            pl.BlockSpec(memory_space=pl.ANY),
            pl.BlockSpec((None, None, 1, 2 * D_FF), b_map),
            pl.BlockSpec(memory_space=pl.ANY),
            pl.BlockSpec((None, None, 1, D_MODEL), b_map),
        ],
        out_specs=pl.BlockSpec(memory_space=pl.ANY),
        scratch_shapes=[
            pltpu.VMEM((2, TM, D_MODEL), F32),
            pltpu.VMEM((2, TM, D_MODEL), F32),
            pltpu.VMEM((D_MODEL, 2 * D_FF), F32),
            pltpu.VMEM((D_FF, D_MODEL), F32),
            pltpu.VMEM((D_MODEL, 2 * D_FF), BF16),
            pltpu.VMEM((D_FF, D_MODEL), BF16),
            pltpu.SemaphoreType.DMA((2,)),
            pltpu.SemaphoreType.DMA((2,)),
            pltpu.SemaphoreType.DMA((2,)),
        ],
    )
    return pl.pallas_call(
        functools.partial(_ffn_fused_kernel, layer=layer),
        grid_spec=grid_spec,
        out_shape=jax.ShapeDtypeStruct((OUT_ROWS, D_MODEL), F32),
        compiler_params=_params(1),
        name="moe_ffn",
    )(tile_expert, n_used, next_expert, codes, h, w_gate_up,
      b_gate_up.reshape(DEPTH, N_EXPERTS, 1, 2 * D_FF), w_down,
      b_down.reshape(DEPTH, N_EXPERTS, 1, D_MODEL))


def _combine_sum_kernel(y_ref, x_ref, gate_ref, mod_ref, out_ref):
    acc = gate_ref[:, 0:1] * y_ref[0:TM, :]
    for k in range(1, TOP_K):
        acc = acc + gate_ref[:, k:k + 1] * y_ref[k * TM:(k + 1) * TM, :]
    out_ref[...] = x_ref[...] + mod_ref[0, 5:6, :] * acc


def _combine_sum(out4, x_all, gate, mod_tab):
    return pl.pallas_call(
        _combine_sum_kernel,
        grid=(N_TILES,),
        in_specs=[pl.BlockSpec((TOP_K * TM, D_MODEL), lambda i: (i, 0)),
                  pl.BlockSpec((TM, D_MODEL), lambda i: (i, 0)),
                  pl.BlockSpec((TM, TOPK_ROWS), lambda i: (i, 0)),
                  pl.BlockSpec((1, 6, D_MODEL), lambda i: (_mod_row(i), 0, 0))],
        out_specs=pl.BlockSpec((TM, D_MODEL), lambda i: (i, 0)),
        out_shape=jax.ShapeDtypeStruct((N_TOK, D_MODEL), F32),
        compiler_params=_params(1),
        name="moe_combine",
    )(out4, x_all, gate, mod_tab)


def _moe_layer(layer, x_all, mod_tab, g_norm2, w_router, b_router, w_gate_up, b_gate_up, w_down, b_down):
    h, topi, gate, rank, counts = _router(x_all, g_norm2[layer], mod_tab, w_router[layer], b_router[layer])
    counts = counts[:, 0]
    padded = ((counts + TM - 1) // TM) * TM
    ends = jnp.cumsum(padded)
    starts = ends - padded
    order = jnp.arange(N_EXPERTS, dtype=jnp.int32)
    first_row = jnp.sum(jnp.where(topi[None] == order[:, None, None], starts[:, None, None], 0), axis=0)
    pos = (first_row + rank).astype(jnp.int32)
    gate = gate.T
    n_used = (ends[-1] // TM).astype(jnp.int32)
    later = jnp.where((padded[None, :] > 0) & (order[None, :] > order[:, None]), order[None, :], N_EXPERTS)
    next_expert = jnp.min(later, axis=1)
    next_expert = jnp.where(next_expert == N_EXPERTS, -1, next_expert).astype(jnp.int32)
    tile_start = jnp.arange(R_TILES, dtype=jnp.int32) * TM
    tile_start = jnp.minimum(tile_start, ends[-1] - 1)
    tile_expert = jnp.sum((ends[None, :] <= tile_start[:, None]).astype(jnp.int32), axis=1)
    tile_expert = jnp.minimum(tile_expert, N_EXPERTS - 1).astype(jnp.int32)
    codes = _invert(ends.astype(jnp.int32), counts.astype(jnp.int32), pos)
    out4 = _ffn_fused(layer, tile_expert, n_used.reshape(1), next_expert, codes, h,
                      w_gate_up, b_gate_up, w_down, b_down)
    return _combine_sum(out4, x_all, gate, mod_tab)


S5_ROWS = 512
S5_HALF_W = S5_WIDTH // 2
S5_HALF_STATES = (S5_GROUPS // 2) * S5_STATE
S5_COL_CHUNK = 512


def _s5_scan_kernel(u_ref, bmat_ref, cmat_ref, a_ref, h0_ref, y_ref, fin_ref, bu_ref, h_ref, *, bsz, steps):
    d = pl.program_id(0)
    c = pl.program_id(1)
    hs = S5_HALF_STATES

    @pl.when(c == 0)
    def _():
        h_ref[...] = h0_ref[...]

    u = u_ref[...].astype(BF16)
    for hf in range(2):
        bu_ref[...] = jnp.dot(u[:, hf * S5_HALF_W:(hf + 1) * S5_HALF_W], bmat_ref[hf],
                              preferred_element_type=F32)
        for j in range(hs // S5_COL_CHUNK):
            re0 = j * S5_COL_CHUNK
            im0 = hs + j * S5_COL_CHUNK
            ar = jnp.broadcast_to(a_ref[hf, 0:1, re0:re0 + S5_COL_CHUNK], (bsz, S5_COL_CHUNK))
            ai = jnp.broadcast_to(a_ref[hf, 1:2, re0:re0 + S5_COL_CHUNK], (bsz, S5_COL_CHUNK))

            def step(t, carry, re0=re0, im0=im0, ar=ar, ai=ai):
                hr, hi = carry
                te = jnp.where(d == 0, t, steps - 1 - t)
                r0 = pl.multiple_of(te * bsz, bsz)
                br = bu_ref[pl.ds(r0, bsz), re0:re0 + S5_COL_CHUNK]
                bi = bu_ref[pl.ds(r0, bsz), im0:im0 + S5_COL_CHUNK]
                nr = ar * hr - ai * hi + br
                ni = ar * hi + ai * hr + bi
                bu_ref[pl.ds(r0, bsz), re0:re0 + S5_COL_CHUNK] = nr
                bu_ref[pl.ds(r0, bsz), im0:im0 + S5_COL_CHUNK] = ni
                return nr, ni

            hr, hi = lax.fori_loop(
                0, steps, step,
                (h_ref[hf, :, re0:re0 + S5_COL_CHUNK], h_ref[hf, :, im0:im0 + S5_COL_CHUNK]), unroll=4)
            h_ref[hf, :, re0:re0 + S5_COL_CHUNK] = hr
            h_ref[hf, :, im0:im0 + S5_COL_CHUNK] = hi
        y_ref[:, hf * S5_HALF_W:(hf + 1) * S5_HALF_W] = jnp.dot(
            bu_ref[...].astype(BF16), cmat_ref[hf], preferred_element_type=F32)

    @pl.when(c == pl.num_programs(1) - 1)
    def _():
        fin_ref[...] = h_ref[...]


def _s5_scan(u_tm, bmat, cmat, acoef, h0, bsz):
    rows = u_tm.shape[0]
    steps = S5_ROWS // bsz
    n_chunks = rows // S5_ROWS

    def chunk_map(d, c):
        return jnp.where(d == 0, c, n_chunks - 1 - c)

    return pl.pallas_call(
        functools.partial(_s5_scan_kernel, bsz=bsz, steps=steps),
        grid=(2, n_chunks),
        in_specs=[
            pl.BlockSpec((S5_ROWS, S5_WIDTH), lambda d, c: (chunk_map(d, c), 0)),
            pl.BlockSpec((None, 2, S5_HALF_W, 2 * S5_HALF_STATES), lambda d, c: (d, 0, 0, 0)),
            pl.BlockSpec((None, 2, 2 * S5_HALF_STATES, S5_HALF_W), lambda d, c: (d, 0, 0, 0)),
            pl.BlockSpec((None, 2, 2, S5_HALF_STATES), lambda d, c: (d, 0, 0, 0)),
            pl.BlockSpec((None, 2, bsz, 2 * S5_HALF_STATES), lambda d, c: (d, 0, 0, 0)),
        ],
        out_specs=[
            pl.BlockSpec((None, S5_ROWS, S5_WIDTH), lambda d, c: (d, chunk_map(d, c), 0)),
            pl.BlockSpec((None, 2, bsz, 2 * S5_HALF_STATES), lambda d, c: (d, 0, 0, 0)),
        ],
        out_shape=[
            jax.ShapeDtypeStruct((2, rows, S5_WIDTH), F32),
            jax.ShapeDtypeStruct((2, 2, bsz, 2 * S5_HALF_STATES), F32),
        ],
        scratch_shapes=[
            pltpu.VMEM((S5_ROWS, 2 * S5_HALF_STATES), F32),
            pltpu.VMEM((2, bsz, 2 * S5_HALF_STATES), F32),
        ],
        compiler_params=pltpu.CompilerParams(
            dimension_semantics=("arbitrary", "arbitrary"), vmem_limit_bytes=VMEM_LIMIT_BYTES),
        name="s5_scan",
    )(u_tm, bmat, cmat, acoef, h0)


def _s5_discretize(lam_re, lam_im, log_dt, b_re, b_im, c_re, c_im):
    eye = jnp.eye(S5_GROUPS // 2, dtype=F32)
    bmats, cmats, acoefs = [], [], []
    for dr in range(2):
        lr = jnp.minimum(lam_re[dr].astype(F32), -1e-4)
        li = lam_im[dr].astype(F32)
        dt = jnp.exp(log_dt[dr].astype(F32))[:, None]
        mag = jnp.exp(lr * dt)
        ar, ai = mag * jnp.cos(li * dt), mag * jnp.sin(li * dt)
        den = lr * lr + li * li
        fr = ((ar - 1.0) * lr + ai * li) / den
        fi = (ai * lr - (ar - 1.0) * li) / den
        br_ = b_re[dr].astype(F32)
        bi_ = b_im[dr].astype(F32)
        bbr = fr[..., None] * br_ - fi[..., None] * bi_
        bbi = fr[..., None] * bi_ + fi[..., None] * br_
        bm, cm, am = [], [], []
        for hf in range(2):
            g = slice(hf * S5_GROUPS // 2, (hf + 1) * S5_GROUPS // 2)

            def bdiag_in(w):
                return jnp.einsum('ab,aph->ahbp', eye, w[g]).reshape(S5_HALF_W, S5_HALF_STATES)

            def bdiag_out(w):
                return jnp.einsum('ab,ahp->apbh', eye, w[g]).reshape(S5_HALF_STATES, S5_HALF_W)

            bm.append(jnp.concatenate([bdiag_in(bbr), bdiag_in(bbi)], axis=1))
            cm.append(jnp.concatenate([bdiag_out(c_re[dr].astype(F32)),
                                       -bdiag_out(c_im[dr].astype(F32))], axis=0))
            am.append(jnp.stack([ar[g].reshape(-1), ai[g].reshape(-1)]))
        bmats.append(jnp.stack(bm))
        cmats.append(jnp.stack(cm))
        acoefs.append(jnp.stack(am))
    return jnp.stack(bmats).astype(BF16), jnp.stack(cmats).astype(BF16), jnp.stack(acoefs)


def _s5_state_to_kernel(h0):
    bsz = h0.shape[0]
    h = h0.astype(F32).reshape(bsz, 2, 2, 2, S5_HALF_STATES)
    return h.transpose(1, 3, 0, 2, 4).reshape(2, 2, bsz, 2 * S5_HALF_STATES)


def _s5_state_from_kernel(fin):
    bsz = fin.shape[2]
    h = fin.reshape(2, 2, bsz, 2, S5_HALF_STATES).transpose(2, 0, 3, 1, 4)
    return h.reshape(bsz, 2, 2, S5_GROUPS, S5_STATE)


TQ = 256
HEAD_LANES = 128
MLA_SCALE = (MLA_NOPE + MLA_ROPE) ** -0.5
DIFF_SCALE = DIFF_HD ** -0.5


def _dot(a, b):
    return jnp.dot(a, b, preferred_element_type=F32)


def _dot_t(a, b):
    return lax.dot_general(a, b, (((1,), (1,)), ((), ())), preferred_element_type=F32)


def _rms_rows(x, g):
    return x * lax.rsqrt(jnp.mean(x * x, axis=-1, keepdims=True) + EPS) * g


def _group(bsz, length, row0, mod_base, mod_stride):
    return dict(bsz=bsz, length=length, nt=length // TQ, tile0=row0 // TQ,
                mod_base=mod_base, mod_stride=mod_stride)


def _params(n_axes):
    return pltpu.CompilerParams(dimension_semantics=("arbitrary",) * n_axes,
                                vmem_limit_bytes=VMEM_LIMIT_BYTES)


def _axial_rope(length, dim):
    rows = length // GRID_W
    row = jnp.repeat(jnp.arange(rows, dtype=F32), GRID_W)
    col = jnp.tile(jnp.arange(GRID_W, dtype=F32), rows)
    n_freq = dim // 4
    inv = ROPE_THETA ** (-jnp.arange(n_freq, dtype=F32) / n_freq)
    ang = jnp.concatenate([row[:, None] * inv, col[:, None] * inv], axis=-1)
    return jnp.cos(ang), jnp.sin(ang)


def _rope_tables(length, dim, lead, reps):
    cos, sin = _axial_rope(length, dim)
    cos_r = jnp.repeat(cos, 2, axis=-1)
    sin_r = jnp.repeat(sin, 2, axis=-1) * jnp.tile(jnp.array([-1.0, 1.0], F32), dim // 2)
    part = HEAD_LANES // reps
    pad = ((0, 0), (lead, part - lead - dim))
    cos_t = jnp.tile(jnp.pad(cos_r, pad, constant_values=1.0), (1, reps))
    sin_t = jnp.tile(jnp.pad(sin_r, pad), (1, reps))
    return cos_t, sin_t


def _swap_pairs(w):
    return w[:, jnp.arange(w.shape[1]) ^ 1]


def _even_in_kernel(*refs, rope):
    if rope:
        (x_ref, g_ref, mod_ref, win_ref, gq_ref, wuq_ref, gkv_ref, cos_ref, sin_ref,
         u_ref, q_ref, ckv_ref, kr_ref) = refs
    else:
        (x_ref, g_ref, mod_ref, win_ref, gq_ref, wuq_ref, gkv_ref,
         u_ref, q_ref, ckv_ref, kr_ref) = refs
    o1 = S5_WIDTH
    o2 = o1 + MLA_Q_LORA
    o3 = o2 + MLA_KV_LORA
    o4 = o3 + HEAD_LANES
    n_in = o4 + HEAD_LANES if rope else o4
    n_q = MLA_HEADS * HEAD_LANES
    h = _rms_rows(x_ref[...], g_ref[...]) * (1.0 + mod_ref[0, 1:2, :]) + mod_ref[0, 0:1, :]
    z = _dot(h.astype(BF16), win_ref[:, :n_in])
    u_ref[...] = z[:, :o1]
    ckv_ref[...] = _rms_rows(z[:, o2:o3], gkv_ref[...])
    qn = _rms_rows(z[:, o1:o2], gq_ref[...]).astype(BF16)
    if rope:
        q2 = _dot(qn, wuq_ref[...])
        cos = cos_ref[...]
        sin = sin_ref[...]
        for hd in range(MLA_HEADS):
            a = hd * HEAD_LANES
            q_ref[:, a:a + HEAD_LANES] = ((q2[:, a:a + HEAD_LANES] * cos
                                           + q2[:, n_q + a:n_q + a + HEAD_LANES] * sin) * MLA_SCALE
                                          ).astype(q_ref.dtype)
        kr_ref[...] = z[:, o3:o4] * cos + z[:, o4:o4 + HEAD_LANES] * sin
    else:
        q_ref[...] = (_dot(qn, wuq_ref[:, :n_q]) * MLA_SCALE).astype(q_ref.dtype)
        kr_ref[...] = z[:, o3:o4]


def _even_in(x_all, grp, g1, mod_tab, win_aug, g_q, wuq2, g_kv, tables):
    bsz, length, nt = grp["bsz"], grp["length"], grp["nt"]
    rope = tables is not None
    rows = bsz * length

    def tok(b, t):
        return (b * nt + t, 0)

    in_specs = [
        pl.BlockSpec((TQ, D_MODEL), lambda b, t: (grp["tile0"] + b * nt + t, 0)),
        pl.BlockSpec((1, D_MODEL), lambda b, t: (0, 0)),
        pl.BlockSpec((1, 6, D_MODEL), lambda b, t: (grp["mod_base"] + b * grp["mod_stride"], 0, 0)),
        pl.BlockSpec(win_aug.shape, lambda b, t: (0, 0)),
        pl.BlockSpec((1, MLA_Q_LORA), lambda b, t: (0, 0)),
        pl.BlockSpec(wuq2.shape, lambda b, t: (0, 0)),
        pl.BlockSpec((1, MLA_KV_LORA), lambda b, t: (0, 0)),
    ]
    args = [x_all, g1.reshape(1, D_MODEL), mod_tab, win_aug, g_q.reshape(1, -1), wuq2, g_kv.reshape(1, -1)]
    if rope:
        in_specs += [pl.BlockSpec((TQ, HEAD_LANES), lambda b, t: (t, 0))] * 2
        args += list(tables)
    return pl.pallas_call(
        functools.partial(_even_in_kernel, rope=rope),
        grid=(bsz, nt),
        in_specs=in_specs,
        out_specs=[
            pl.BlockSpec((TQ, S5_WIDTH), lambda b, t: (t, b)),
            pl.BlockSpec((TQ, MLA_HEADS * HEAD_LANES), tok),
            pl.BlockSpec((TQ, MLA_KV_LORA), tok),
            pl.BlockSpec((TQ, HEAD_LANES), tok),
        ],
        out_shape=[
            jax.ShapeDtypeStruct((length, bsz * S5_WIDTH), F32),
            jax.ShapeDtypeStruct((rows, MLA_HEADS * HEAD_LANES), BF16),
            jax.ShapeDtypeStruct((rows, MLA_KV_LORA), F32),
            jax.ShapeDtypeStruct((rows, HEAD_LANES), F32),
        ],
        compiler_params=_params(2),
        name="even_in",
    )(*args)


def _kv_expand_kernel(x_ref, kr_ref, wk_ref, wv_ref, k_ref, v_ref):
    x = x_ref[...].astype(BF16)
    k = _dot(x, wk_ref[...])
    kr = kr_ref[...]
    for hd in range(MLA_HEADS):
        a = hd * HEAD_LANES
        k_ref[:, a:a + HEAD_LANES] = (k[:, a:a + HEAD_LANES] + kr).astype(k_ref.dtype)
    v_ref[...] = _dot(x, wv_ref[...]).astype(v_ref.dtype)


def _kv_expand(ckv, kr, wk, wv):
    rows = ckv.shape[0]
    tm = 512
    width = MLA_HEADS * HEAD_LANES
    return pl.pallas_call(
        _kv_expand_kernel,
        grid=(rows // tm,),
        in_specs=[pl.BlockSpec((tm, MLA_KV_LORA), lambda i: (i, 0)),
                  pl.BlockSpec((tm, HEAD_LANES), lambda i: (i, 0)),
                  pl.BlockSpec(wk.shape, lambda i: (0, 0)),
                  pl.BlockSpec(wv.shape, lambda i: (0, 0))],
        out_specs=[pl.BlockSpec((tm, width), lambda i: (i, 0))] * 2,
        out_shape=[jax.ShapeDtypeStruct((rows, width), BF16)] * 2,
        compiler_params=_params(1),
        name="kv_expand",
    )(ckv, kr, wk, wv)


def _exp_parts(scores):
    m = functools.reduce(jnp.maximum, [jnp.max(s, axis=-1, keepdims=True) for s in scores])
    es = [jnp.exp(s - m) for s in scores]
    den = functools.reduce(jnp.add, [jnp.sum(e, axis=-1, keepdims=True) for e in es])
    return es, 1.0 / den


def _weighted_values(es, vs):
    o = _dot(es[0].astype(BF16), vs[0])
    for e, v in zip(es[1:], vs[1:]):
        o = o + _dot(e.astype(BF16), v)
    return o


def _mla_attn_kernel(*refs, n_seg):
    q = refs[0][...]
    o_ref = refs[-1]
    ks = [refs[1 + 2 * s][...] for s in range(n_seg)]
    vs = [refs[2 + 2 * s][...] for s in range(n_seg)]
    es, inv = _exp_parts([_dot_t(q, k) for k in ks])
    o_ref[...] = (_weighted_values(es, vs) * inv).astype(o_ref.dtype)


def _mla_attn(q, segs, bsz, length):
    nq = length // TQ
    in_specs = [pl.BlockSpec((TQ, HEAD_LANES), lambda b, h, i: (b * nq + i, h))]
    args = [q]
    for k, v, lk in segs:
        in_specs += [pl.BlockSpec((lk, HEAD_LANES), lambda b, h, i: (b, h))] * 2
        args += [k, v]
    return pl.pallas_call(
        functools.partial(_mla_attn_kernel, n_seg=len(segs)),
        grid=(bsz, MLA_HEADS, nq),
        in_specs=in_specs,
        out_specs=pl.BlockSpec((TQ, HEAD_LANES), lambda b, h, i: (b * nq + i, h)),
        out_shape=jax.ShapeDtypeStruct((bsz * length, MLA_HEADS * HEAD_LANES), BF16),
        compiler_params=_params(3),
        name="mla_attn",
    )(*args)


def _even_out_kernel(u_ref, y_ref, o_ref, x_ref, mod_ref, d_ref, wglu_ref, bglu_ref, ws5_ref, wmla_ref,
                     out_ref):
    y =jax.nn.gelu(d_ref[...] * u_ref[...] + y_ref[0] + y_ref[1])
    s5 = y * jax.nn.sigmoid(_dot(y.astype(BF16), wglu_ref[...]) + bglu_ref[...])
    mix = _dot(s5.astype(BF16), ws5_ref[...]) + _dot(o_ref[...].astype(BF16), wmla_ref[...])
    out_ref[...] = x_ref[...] + mod_ref[0, 2:3, :] * mix


def _even_out(x_all, grp, mod_tab, u_tm, y_dir, o_mla, d_skip, w_glu, b_glu, w_out_s5, w_out_mla):
    bsz, length, nt = grp["bsz"], grp["length"], grp["nt"]

    def xrow(b, t):
        return (grp["tile0"] + b * nt + t, 0)

    full = lambda b, t: (0, 0)
    return pl.pallas_call(
        _even_out_kernel,
        grid=(bsz, nt),
        in_specs=[
            pl.BlockSpec((TQ, S5_WIDTH), lambda b, t: (t, b)),
            pl.BlockSpec((2, TQ, S5_WIDTH), lambda b, t: (0, t, b)),
            pl.BlockSpec((TQ, MLA_HEADS * HEAD_LANES), lambda b, t: (b * nt + t, 0)),
            pl.BlockSpec((TQ, D_MODEL), xrow),
            pl.BlockSpec((1, 6, D_MODEL), lambda b, t: (grp["mod_base"] + b * grp["mod_stride"], 0, 0)),
            pl.BlockSpec((1, S5_WIDTH), full),
            pl.BlockSpec(w_glu.shape, full),
            pl.BlockSpec((1, S5_WIDTH), full),
            pl.BlockSpec(w_out_s5.shape, full),
            pl.BlockSpec(w_out_mla.shape, full),
        ],
        out_specs=pl.BlockSpec((TQ, D_MODEL), xrow),
        out_shape=jax.ShapeDtypeStruct(x_all.shape, F32),
        input_output_aliases={3: 0},
        compiler_params=_params(2),
        name="even_out",
    )(u_tm, y_dir.reshape(2, length, bsz * S5_WIDTH), o_mla, x_all, mod_tab,
      d_skip.reshape(1, S5_WIDTH), w_glu, b_glu.reshape(1, S5_WIDTH), w_out_s5, w_out_mla)


def _odd_in_kernel(*refs, rope):
    if rope:
        x_ref, g_ref, mod_ref, w_ref, cos_ref, sin_ref, q_ref, k_ref, v_ref = refs
    else:
        x_ref, g_ref, mod_ref, w_ref, q_ref, k_ref, v_ref = refs
    w3 = 3 * DIFF_WIDTH
    h = _rms_rows(x_ref[...], g_ref[...]) * (1.0 + mod_ref[0, 1:2, :]) + mod_ref[0, 0:1, :]
    z = _dot(h.astype(BF16), w_ref[...] if rope else w_ref[:, :w3])
    v_ref[...] = z[:, 2 * DIFF_WIDTH:w3].astype(v_ref.dtype)
    if rope:
        cos = cos_ref[...]
        sin = sin_ref[...]
        for hd in range(DIFF_HEADS):
            a = hd * HEAD_LANES
            q_ref[:, a:a + HEAD_LANES] = ((z[:, a:a + HEAD_LANES] * cos
                                           + z[:, w3 + a:w3 + a + HEAD_LANES] * sin) * DIFF_SCALE
                                          ).astype(q_ref.dtype)
            b = DIFF_WIDTH + a
            k_ref[:, a:a + HEAD_LANES] = (z[:, b:b + HEAD_LANES] * cos
                                          + z[:, w3 + b:w3 + b + HEAD_LANES] * sin).astype(k_ref.dtype)
    else:
        q_ref[...] = (z[:, :DIFF_WIDTH] * DIFF_SCALE).astype(q_ref.dtype)
        k_ref[...] = z[:, DIFF_WIDTH:2 * DIFF_WIDTH].astype(k_ref.dtype)


def _odd_in(x_all, grp, g1, mod_tab, w_aug, tables, kv_dtype):
    bsz, length, nt = grp["bsz"], grp["length"], grp["nt"]
    rope = tables is not None
    rows = bsz * length

    def tok(b, t):
        return (b * nt + t, 0)

    in_specs = [
        pl.BlockSpec((TQ, D_MODEL), lambda b, t: (grp["tile0"] + b * nt + t, 0)),
        pl.BlockSpec((1, D_MODEL), lambda b, t: (0, 0)),
        pl.BlockSpec((1, 6, D_MODEL), lambda b, t: (grp["mod_base"] + b * grp["mod_stride"], 0, 0)),
        pl.BlockSpec(w_aug.shape, lambda b, t: (0, 0)),
    ]
    args = [x_all, g1.reshape(1, D_MODEL), mod_tab, w_aug]
    if rope:
        in_specs += [pl.BlockSpec((TQ, HEAD_LANES), lambda b, t: (t, 0))] * 2
        args += list(tables)
    return pl.pallas_call(
        functools.partial(_odd_in_kernel, rope=rope),
        grid=(bsz, nt),
        in_specs=in_specs,
        out_specs=[pl.BlockSpec((TQ, DIFF_WIDTH), tok)] * 3,
        out_shape=[
            jax.ShapeDtypeStruct((rows, DIFF_WIDTH), BF16),
            jax.ShapeDtypeStruct((rows, DIFF_WIDTH), kv_dtype),
            jax.ShapeDtypeStruct((rows, DIFF_WIDTH), kv_dtype),
        ],
        compiler_params=_params(2),
        name="odd_in",
    )(*args)


def _diff_attn_kernel(*refs, n_seg, post_scale):
    lam_ref, q_ref = refs[0], refs[1]
    g_ref, o_ref = refs[-2], refs[-1]
    q = q_ref[...].astype(F32)
    lane = lax.broadcasted_iota(jnp.int32, q.shape, 1)
    q0 = jnp.where(lane < DIFF_HD, q, 0.0).astype(BF16)
    q1 = jnp.where(lane >= DIFF_HD, q, 0.0).astype(BF16)
    ks = [refs[2 + 2 * s][...].astype(BF16) for s in range(n_seg)]
    vs = [refs[3 + 2 * s][...].astype(BF16) for s in range(n_seg)]
    e0, inv0 = _exp_parts([_dot_t(q0, k) for k in ks])
    e1, inv1 = _exp_parts([_dot_t(q1, k) for k in ks])
    o = _weighted_values(e0, vs) * inv0 - lam_ref[0] * (_weighted_values(e1, vs) * inv1)
    o_ref[...] = (_rms_rows(o, g_ref[...]) * post_scale).astype(o_ref.dtype)


def _diff_attn(lam_full, q, segs, g_sub, post_scale, bsz, length):
    nq = length // TQ
    in_specs = [pl.BlockSpec(memory_space=pltpu.SMEM),
                pl.BlockSpec((TQ, HEAD_LANES), lambda b, h, i: (b * nq + i, h))]
    args = [lam_full.reshape(1).astype(F32), q]
    for k, v, lk in segs:
        in_specs += [pl.BlockSpec((lk, HEAD_LANES), lambda b, h, i: (b, h))] * 2
        args += [k, v]
    in_specs.append(pl.BlockSpec((1, HEAD_LANES), lambda b, h, i: (0, 0)))
    args.append(g_sub.reshape(1, HEAD_LANES))
    return pl.pallas_call(
        functools.partial(_diff_attn_kernel, n_seg=len(segs), post_scale=post_scale),
        grid=(bsz, DIFF_HEADS, nq),
        in_specs=in_specs,
        out_specs=pl.BlockSpec((TQ, HEAD_LANES), lambda b, h, i: (b * nq + i, h)),
        out_shape=jax.ShapeDtypeStruct((bsz * length, DIFF_WIDTH), BF16),
        compiler_params=_params(3),
        name="diff_attn",
    )(*args)


def _odd_out_kernel(o_ref, x_ref, mod_ref, w_ref, out_ref):
    out_ref[...] = x_ref[...] + mod_ref[0, 2:3, :] * _dot(o_ref[...], w_ref[...])


def _odd_out(x_all, grp, mod_tab, o, w_out):
    bsz, nt = grp["bsz"], grp["nt"]

    def xrow(b, t):
        return (grp["tile0"] + b * nt + t, 0)

    return pl.pallas_call(
        _odd_out_kernel,
        grid=(bsz, nt),
        in_specs=[
            pl.BlockSpec((TQ, DIFF_WIDTH), lambda b, t: (b * nt + t, 0)),
            pl.BlockSpec((TQ, D_MODEL), xrow),
            pl.BlockSpec((1, 6, D_MODEL), lambda b, t: (grp["mod_base"] + b * grp["mod_stride"], 0, 0)),
            pl.BlockSpec(w_out.shape, lambda b, t: (0, 0)),
        ],
        out_specs=pl.BlockSpec((TQ, D_MODEL), xrow),
        out_shape=jax.ShapeDtypeStruct(x_all.shape, F32),
        input_output_aliases={1: 0},
        compiler_params=_params(2),
        name="odd_out",
    )(o, x_all, mod_tab, w_out)


def _final_norm_kernel(x_ref, g_ref, o_ref):
    o_ref[...] = _rms_rows(x_ref[...], g_ref[...])


def _final_norm(x_all, g, row0, rows):
    tm = 512
    return pl.pallas_call(
        _final_norm_kernel,
        grid=(rows // tm,),
        in_specs=[pl.BlockSpec((tm, D_MODEL), lambda i: (row0 // tm + i, 0)),
                  pl.BlockSpec((1, D_MODEL), lambda i: (0, 0))],
        out_specs=pl.BlockSpec((tm, D_MODEL), lambda i: (i, 0)),
        out_shape=jax.ShapeDtypeStruct((rows, D_MODEL), F32),
        compiler_params=_params(1),
        name="final_norm",
    )(x_all, g.reshape(1, D_MODEL))


def _pad_head_lanes(x, lead):
    return jnp.pad(x, ((0, 0), (lead, HEAD_LANES - lead - x.shape[1])))


def _even_weights(w_in, w_out, w_uq, w_ukv, w_glu):
    o3 = S5_WIDTH + MLA_Q_LORA + MLA_KV_LORA
    w_kr = w_in[:, o3:]
    win_aug = jnp.concatenate(
        [w_in[:, :o3], _pad_head_lanes(w_kr, MLA_NOPE), _pad_head_lanes(_swap_pairs(w_kr), MLA_NOPE)], axis=1)
    dq = MLA_NOPE + MLA_ROPE
    plain, swapped = [], []
    for hd in range(MLA_HEADS):
        wn = w_uq[:, hd * dq:hd * dq + MLA_NOPE]
        wr = w_uq[:, hd * dq + MLA_NOPE:(hd + 1) * dq]
        plain.append(jnp.pad(jnp.concatenate([wn, wr], axis=1), ((0, 0), (0, HEAD_LANES - dq))))
        swapped.append(_pad_head_lanes(_swap_pairs(wr), MLA_NOPE))
    wuq2 = jnp.concatenate(plain + swapped, axis=1)
    w_mla = w_out[S5_WIDTH:].reshape(MLA_HEADS, MLA_V, D_MODEL)
    w_out_mla = jnp.pad(w_mla, ((0, 0), (0, HEAD_LANES - MLA_V), (0, 0))).reshape(MLA_HEADS * HEAD_LANES, D_MODEL)
    w_kv = w_ukv.reshape(MLA_KV_LORA, MLA_HEADS, MLA_NOPE + MLA_V)
    wk = jnp.pad(w_kv[:, :, :MLA_NOPE], ((0, 0), (0, 0), (0, HEAD_LANES - MLA_NOPE)))
    wv = jnp.pad(w_kv[:, :, MLA_NOPE:], ((0, 0), (0, 0), (0, HEAD_LANES - MLA_V)))
    w_kv = (wk.reshape(MLA_KV_LORA, -1).astype(BF16), wv.reshape(MLA_KV_LORA, -1).astype(BF16))
    return (win_aug.astype(BF16), wuq2.astype(BF16), w_kv, w_glu.astype(BF16),
            w_out[:S5_WIDTH].astype(BF16), w_out_mla.astype(BF16))


def _even_layer(x_all, grp, g1, mod_tab, ew, s5m, g_q, g_kv, d_skip, b_glu, h0, ctx, tables):
    win_aug, wuq2, w_kv, w_glu, w_out_s5, w_out_mla = ew
    bmat, cmat, acoef = s5m
    bsz, length = grp["bsz"], grp["length"]
    u_tm, q, ckv, kr = _even_in(x_all, grp, g1, mod_tab, win_aug, g_q, wuq2, g_kv, tables)
    y_dir, fin = _s5_scan(u_tm.reshape(length * bsz, S5_WIDTH), bmat, cmat, acoef, h0, bsz)
    segs = [(*_kv_expand(ckv, kr, *w_kv), length)]
    if ctx is not None:
        segs.append((*_kv_expand(*ctx, *w_kv), PAST_LEN))
    o_mla = _mla_attn(q, segs, bsz, length)
    x_all = _even_out(x_all, grp, mod_tab, u_tm, y_dir, o_mla, d_skip, w_glu, b_glu, w_out_s5, w_out_mla)
    return x_all, fin, ckv, kr


def _odd_layer(x_all, grp, g1, mod_tab, w_aug, w_out, lam_full, g_sub, post_scale, ctx, tables, kv_dtype):
    bsz, length = grp["bsz"], grp["length"]
    q, k, v = _odd_in(x_all, grp, g1, mod_tab, w_aug, tables, kv_dtype)
    segs = [(k, v, length)]
    if ctx is not None:
        segs.append((ctx[0], ctx[1], PAST_LEN))
    o = _diff_attn(lam_full, q, segs, g_sub, post_scale, bsz, length)
    return _odd_out(x_all, grp, mod_tab, o, w_out), k, v


def kernel(x_prompt, x_sample, state_s5, cache_mla, cache_diff_k, cache_diff_v, c, c_ctx, w_mod, b_mod, g_norm1, g_norm2, g_final, w_in_even, w_out_even, s5_lam_re, s5_lam_im, s5_log_dt, s5_b_re, s5_b_im, s5_c_re, s5_c_im, s5_d, s5_w_glu, s5_b_glu, mla_g_q, mla_w_uq, mla_g_kv, mla_w_ukv, w_in_odd, w_out_odd, diff_lam, diff_g_sub, w_router, b_router, w_gate_up, b_gate_up, w_down, b_down):
    tab_mla = _rope_tables(DEC_SEQ, MLA_ROPE, MLA_NOPE, 1)
    tab_diff = _rope_tables(DEC_SEQ, DIFF_HD, 0, 2)
    grp_p = _group(BATCH, SEQ, 0, 0, 0)
    grp_s = _group(DEC_BATCH, DEC_SEQ, N_PROMPT, 1, 1)
    x_all = jnp.concatenate([x_prompt.reshape(N_PROMPT, D_MODEL), x_sample.reshape(N_SAMPLE, D_MODEL)], axis=0)
    cond = jax.nn.silu(jnp.concatenate([c_ctx[None], c], axis=0))
    new_s5, new_mla, new_k, new_v = [], [], [], []
    for l in range(DEPTH):
        mod_tab = (cond @ w_mod[l] + b_mod[l]).reshape(1 + DEC_BATCH, 6, D_MODEL)
        i = l // 2
        if l % 2 == 0:
            ew = _even_weights(w_in_even[i], w_out_even[i], mla_w_uq[i], mla_w_ukv[i], s5_w_glu[i])
            s5m = _s5_discretize(s5_lam_re[i], s5_lam_im[i], s5_log_dt[i], s5_b_re[i], s5_b_im[i],
                                 s5_c_re[i], s5_c_im[i])
            common = (ew, s5m, mla_g_q[i], mla_g_kv[i], s5_d[i], s5_b_glu[i])
            h0_p = jnp.zeros((2, 2, BATCH, 2 * S5_HALF_STATES), F32)
            x_all, fin, ckv, kr = _even_layer(x_all, grp_p, g_norm1[l], mod_tab, *common, h0_p, None, None)
            new_s5.append(_s5_state_from_kernel(fin))
            new_mla.append(jnp.concatenate([ckv, kr[:, MLA_NOPE:MLA_NOPE + MLA_ROPE]], axis=1)
                           .reshape(BATCH, SEQ, MLA_KV_LORA + MLA_ROPE))
            lat_ctx = cache_mla[:, i].astype(F32).reshape(DEC_BATCH * PAST_LEN, MLA_KV_LORA + MLA_ROPE)
            ctx = (lat_ctx[:, :MLA_KV_LORA], _pad_head_lanes(lat_ctx[:, MLA_KV_LORA:], MLA_NOPE))
            x_all, _, _, _ = _even_layer(x_all, grp_s, g_norm1[l], mod_tab, *common,
                                         _s5_state_to_kernel(state_s5[:, i]), ctx, tab_mla)
        else:
            lam_init = 0.8 - 0.6 * math.exp(-0.3 * l)
            lamf = diff_lam[i].astype(F32)
            lam_full = jnp.exp(jnp.sum(lamf[0] * lamf[1])) - jnp.exp(jnp.sum(lamf[2] * lamf[3])) + lam_init
            w_qk = w_in_odd[i][:, :2 * DIFF_WIDTH]
            w_aug = jnp.concatenate([w_in_odd[i], _swap_pairs(w_qk)], axis=1).astype(BF16)
            w_out = w_out_odd[i].astype(BF16)
            odd = (w_aug, w_out, lam_full, diff_g_sub[i], 1.0 - lam_init)
            x_all, kp, vp = _odd_layer(x_all, grp_p, g_norm1[l], mod_tab, *odd, None, None, F32)
            new_k.append(kp.reshape(BATCH, SEQ, DIFF_HEADS, 2, DIFF_HD))
            new_v.append(vp.reshape(BATCH, SEQ, DIFF_HEADS, 2 * DIFF_HD))
            ctx = (cache_diff_k[:, i].reshape(DEC_BATCH * PAST_LEN, DIFF_WIDTH).astype(BF16),
                   cache_diff_v[:, i].reshape(DEC_BATCH * PAST_LEN, DIFF_WIDTH).astype(BF16))
            x_all, _, _ = _odd_layer(x_all, grp_s, g_norm1[l], mod_tab, *odd, ctx, tab_diff, BF16)
        x_all = _moe_layer(l, x_all, mod_tab, g_norm2, w_router, b_router,
                           w_gate_up, b_gate_up, w_down, b_down)
    y_prompt = _final_norm(x_all, g_final, 0, N_PROMPT)
    y_sample = _final_norm(x_all, g_final, N_PROMPT, N_SAMPLE)
    return (y_prompt.reshape(BATCH, SEQ, D_MODEL), y_sample.reshape(DEC_BATCH, DEC_SEQ, D_MODEL),
            jnp.stack(new_s5, axis=1), jnp.stack(new_mla, axis=1),
            jnp.stack(new_k, axis=1), jnp.stack(new_v, axis=1))
```

```python
import functools
import math

import jax
import jax.numpy as jnp
from jax import lax
from jax.experimental import pallas as pl
from jax.experimental.pallas import tpu as pltpu

D_MODEL = 1024
BATCH = 16
SEQ = 256
DEPTH = 4
DEC_BATCH = 8
DEC_SEQ = 1024
PAST_LEN = 512
GRID_W = 64
N_EVEN = (DEPTH + 1) // 2
N_ODD = DEPTH // 2
S5_WIDTH = D_MODEL // 2
S5_GROUP = 16
S5_GROUPS = S5_WIDTH // S5_GROUP
S5_STATE = 64
MLA_HEADS = 8
MLA_NOPE = 64
MLA_ROPE = 32
MLA_V = 64
MLA_Q_LORA = D_MODEL // 4
MLA_KV_LORA = D_MODEL // 8
MLA_WIDTH = MLA_HEADS * MLA_V
EVEN_IN = S5_WIDTH + MLA_Q_LORA + MLA_KV_LORA + MLA_ROPE
EVEN_OUT = S5_WIDTH + MLA_WIDTH
DIFF_HEADS = 8
DIFF_HD = D_MODEL // (2 * DIFF_HEADS)
DIFF_WIDTH = DIFF_HEADS * 2 * DIFF_HD
N_EXPERTS = 32
TOP_K = 4
D_FF = D_MODEL
SWIGLU_LIMIT = 7.0
SWIGLU_ALPHA = 1.702
ROPE_THETA = 10000.0
Q_BLOCK = 128
EPS = 1e-6

N_PROMPT = BATCH * SEQ
N_SAMPLE = DEC_BATCH * DEC_SEQ
N_TOK = N_PROMPT + N_SAMPLE

LANES = 128
VMEM_LIMIT_BYTES = 56 * 1024 * 1024

TM = 256
N_TILES = N_TOK // TM
R_TILES = N_TOK * TOP_K // TM + N_EXPERTS
R_MAX = R_TILES * TM
TOPK_ROWS = 8

F32 = jnp.float32
BF16 = jnp.bfloat16


def _mod_row(i):
    t0 = i * TM
    return jnp.where(t0 < N_PROMPT, 0, 1 + (t0 - N_PROMPT) // DEC_SEQ)


def _router_kernel(x_ref, g_ref, mod_ref, wr_ref, br_ref,
                   h_ref, topi_ref, gate_ref, rank_ref, counts_ref, carry_ref):
    i = pl.program_id(0)

    @pl.when(i == 0)
    def _():
        carry_ref[...] = jnp.zeros_like(carry_ref)

    x = x_ref[...]
    ms = jnp.mean(x * x, axis=-1, keepdims=True)
    y = x * lax.rsqrt(ms + EPS) * g_ref[...]
    shift = mod_ref[0, 3:4, :]
    scale = mod_ref[0, 4:5, :]
    h = y * (1.0 + scale) + shift
    h_ref[...] = h

    hi = h.astype(BF16)
    lo = (h - hi.astype(F32)).astype(BF16)
    w_hi = wr_ref[0]
    logits = _dot_t(w_hi, hi) + (_dot_t(wr_ref[1], hi) + _dot_t(w_hi, lo)) + br_ref[...]
    sub_e = lax.broadcasted_iota(jnp.int32, logits.shape, 0)
    work = logits
    vals, hits = [], []
    sel = jnp.zeros(logits.shape, F32)
    for _ in range(TOP_K):
        m = jnp.max(work, axis=0, keepdims=True)
        idx = jnp.min(jnp.where(work == m, sub_e, N_EXPERTS), axis=0, keepdims=True)
        hit = sub_e == idx
        vals.append(m)
        hits.append((hit, idx))
        sel = jnp.where(hit, 1.0, sel)
        work = jnp.where(hit, -jnp.inf, work)
    es = [jnp.exp(v - vals[0]) for v in vals]
    inv = 1.0 / (es[0] + es[1] + es[2] + es[3])

    row = lax.broadcasted_iota(jnp.int32, (TM, TM), 0)
    col = lax.broadcasted_iota(jnp.int32, (TM, TM), 1)
    earlier = jnp.where(row < col, 1.0, 0.0).astype(BF16)
    before = _dot(sel.astype(BF16), earlier) + carry_ref[...]
    carry_ref[...] += jnp.sum(sel, axis=1, keepdims=True)
    counts_ref[...] = carry_ref[...].astype(jnp.int32)

    sub_k = lax.broadcasted_iota(jnp.int32, (TOPK_ROWS, TM), 0)
    topi = jnp.zeros((TOPK_ROWS, TM), jnp.int32)
    gate = jnp.zeros((TOPK_ROWS, TM), F32)
    rank = jnp.zeros((TOPK_ROWS, TM), jnp.int32)
    for k in range(TOP_K):
        hit, idx = hits[k]
        rk = jnp.sum(jnp.where(hit, before, 0.0), axis=0, keepdims=True)
        topi = jnp.where(sub_k == k, idx, topi)
        gate = jnp.where(sub_k == k, es[k] * inv, gate)
        rank = jnp.where(sub_k == k, rk.astype(jnp.int32), rank)
    topi_ref[...] = topi
    gate_ref[...] = gate
    rank_ref[...] = rank


def _router(x_all, g, mod_tab, w_router, b_router):
    w_t = w_router.astype(F32).T
    w_hi = w_t.astype(BF16)
    w_split = jnp.stack([w_hi, (w_t - w_hi.astype(F32)).astype(BF16)])
    return pl.pallas_call(
        _router_kernel,
        grid=(N_TILES,),
        in_specs=[
            pl.BlockSpec((TM, D_MODEL), lambda i: (i, 0)),
            pl.BlockSpec((1, D_MODEL), lambda i: (0, 0)),
            pl.BlockSpec((1, 6, D_MODEL), lambda i: (_mod_row(i), 0, 0)),
            pl.BlockSpec((2, N_EXPERTS, D_MODEL), lambda i: (0, 0, 0)),
            pl.BlockSpec((N_EXPERTS, 1), lambda i: (0, 0)),
        ],
        out_specs=[
            pl.BlockSpec((TM, D_MODEL), lambda i: (i, 0)),
            pl.BlockSpec((TOPK_ROWS, TM), lambda i: (0, i)),
            pl.BlockSpec((TOPK_ROWS, TM), lambda i: (0, i)),
            pl.BlockSpec((TOPK_ROWS, TM), lambda i: (0, i)),
            pl.BlockSpec((N_EXPERTS, 1), lambda i: (0, 0)),
        ],
        out_shape=[
            jax.ShapeDtypeStruct((N_TOK, D_MODEL), F32),
            jax.ShapeDtypeStruct((TOPK_ROWS, N_TOK), jnp.int32),
            jax.ShapeDtypeStruct((TOPK_ROWS, N_TOK), F32),
            jax.ShapeDtypeStruct((TOPK_ROWS, N_TOK), jnp.int32),
            jax.ShapeDtypeStruct((N_EXPERTS, 1), jnp.int32),
        ],
        scratch_shapes=[pltpu.VMEM((N_EXPERTS, 1), F32)],
        compiler_params=pltpu.CompilerParams(
            dimension_semantics=("arbitrary",), vmem_limit_bytes=VMEM_LIMIT_BYTES),
        name="moe_router",
    )(x_all, g.reshape(1, D_MODEL), mod_tab, w_split, b_router.reshape(N_EXPERTS, 1))


ISSUE_UNROLL = 4


def _dispatch_kernel(ends_ref, pos_ref, h_ref, xs_ref, zero_buf, pos_smem, sem_idx, sem, sem_zero):
    i = pl.program_id(0)

    @pl.when(i == 0)
    def _():
        zero_buf[...] = jnp.zeros_like(zero_buf)

        for wait in (False, True):
            for e in range(N_EXPERTS):
                start = ends_ref[e - 1] if e > 0 else 0

                @pl.when(ends_ref[e] > start)
                def _(e=e, wait=wait):
                    last = pl.multiple_of(ends_ref[e] - TM, TM)
                    cp = pltpu.make_async_copy(zero_buf, xs_ref.at[pl.ds(last, TM)], sem_zero)
                    if wait:
                        cp.wait()
                    else:
                        cp.start()

            def tail(t, carry, wait=wait):
                cp = pltpu.make_async_copy(zero_buf, xs_ref.at[pl.ds(pl.multiple_of(t * TM, TM), TM)], sem_zero)
                if wait:
                    cp.wait()
                else:
                    cp.start()
                return carry

            lax.fori_loop(ends_ref[N_EXPERTS - 1] // TM, R_TILES, tail, 0)

    cp = pltpu.make_async_copy(pos_ref, pos_smem, sem_idx)
    cp.start()
    cp.wait()

    def issue(r, carry):
        for k in range(TOP_K):
            p = pos_smem[k, r]
            pltpu.make_async_copy(h_ref.at[pl.ds(r, 1)], xs_ref.at[pl.ds(p, 1)], sem.at[k]).start(priority=k % 2)
        return carry

    lax.fori_loop(0, TM, issue, 0, unroll=ISSUE_UNROLL)
    for k in range(TOP_K):
        pltpu.make_async_copy(h_ref, xs_ref.at[pl.ds(0, TM)], sem.at[k]).wait()


def _dispatch(ends, pos, h):
    grid_spec = pltpu.PrefetchScalarGridSpec(
        num_scalar_prefetch=1,
        grid=(N_TILES,),
        in_specs=[
            pl.BlockSpec((TOPK_ROWS, TM), lambda i, ends: (0, i)),
            pl.BlockSpec((TM, D_MODEL), lambda i, ends: (i, 0)),
        ],
        out_specs=pl.BlockSpec(memory_space=pl.ANY),
        scratch_shapes=[
            pltpu.VMEM((TM, D_MODEL), F32),
            pltpu.SMEM((TOPK_ROWS, TM), jnp.int32),
            pltpu.SemaphoreType.DMA,
            pltpu.SemaphoreType.DMA((TOP_K,)),
            pltpu.SemaphoreType.DMA,
        ],
    )
    return pl.pallas_call(
        _dispatch_kernel,
        grid_spec=grid_spec,
        out_shape=jax.ShapeDtypeStruct((R_MAX, D_MODEL), F32),
        compiler_params=pltpu.CompilerParams(
            dimension_semantics=("arbitrary",), vmem_limit_bytes=VMEM_LIMIT_BYTES),
        name="moe_dispatch",
    )(ends, pos, h)


def _ffn_kernel(te_ref, nu_ref, nx_ref, xs_ref, wgu_hbm, bgu_ref, wd_hbm, bd_ref, ys_ref,
                wgu_f32, wd_f32, wgu_bf, wd_bf, sem, *, layer):
    i = pl.program_id(0)

    def weight_copies(e):
        return (pltpu.make_async_copy(wgu_hbm.at[layer, e], wgu_f32, sem.at[0]),
                pltpu.make_async_copy(wd_hbm.at[layer, e], wd_f32, sem.at[1]))

    @pl.when(i < nu_ref[0])
    def _():
        expert = te_ref[i]
        new_expert = jnp.logical_or(i == 0, expert != te_ref[jnp.maximum(i - 1, 0)])

        @pl.when(i == 0)
        def _():
            for cp in weight_copies(expert):
                cp.start()

        @pl.when(new_expert)
        def _():
            for cp in weight_copies(expert):
                cp.wait()
            wgu_bf[...] = wgu_f32[...].astype(BF16)
            wd_bf[...] = wd_f32[...].astype(BF16)
            nxt = nx_ref[expert]

            @pl.when(nxt >= 0)
            def _():
                for cp in weight_copies(nxt):
                    cp.start()

        x = xs_ref[...].astype(BF16)
        gu = jnp.dot(x, wgu_bf[...], preferred_element_type=F32) + bgu_ref[...]
        g = jnp.minimum(gu[:, :D_FF], SWIGLU_LIMIT)
        u = jnp.clip(gu[:, D_FF:], -SWIGLU_LIMIT, SWIGLU_LIMIT)
        act = g * jax.nn.sigmoid(SWIGLU_ALPHA * g) * (u + 1.0)
        ys_ref[...] = jnp.dot(act.astype(BF16), wd_bf[...], preferred_element_type=F32) + bd_ref[...]

    @pl.when(i >= nu_ref[0])
    def _():
        ys_ref[...] = jnp.zeros_like(ys_ref)


def _ffn(layer, tile_expert, n_used, next_expert, xs, w_gate_up, b_gate_up, w_down, b_down):
    def row_map(i, te, nu, nx):
        return (jnp.maximum(jnp.minimum(i, nu[0] - 1), 0), 0)

    def b_map(i, te, nu, nx):
        return (layer, te[i], 0, 0)

    grid_spec = pltpu.PrefetchScalarGridSpec(
        num_scalar_prefetch=3,
        grid=(R_TILES,),
        in_specs=[
            pl.BlockSpec((TM, D_MODEL), row_map),
            pl.BlockSpec(memory_space=pl.ANY),
            pl.BlockSpec((None, None, 1, 2 * D_FF), b_map),
            pl.BlockSpec(memory_space=pl.ANY),
            pl.BlockSpec((None, None, 1, D_MODEL), b_map),
        ],
        out_specs=pl.BlockSpec((TM, D_MODEL), lambda i, te, nu, nx: (i, 0)),
        scratch_shapes=[
            pltpu.VMEM((D_MODEL, 2 * D_FF), F32),
            pltpu.VMEM((D_FF, D_MODEL), F32),
            pltpu.VMEM((D_MODEL, 2 * D_FF), BF16),
            pltpu.VMEM((D_FF, D_MODEL), BF16),
            pltpu.SemaphoreType.DMA((2,)),
        ],
    )
    return pl.pallas_call(
        functools.partial(_ffn_kernel, layer=layer),
        grid_spec=grid_spec,
        out_shape=jax.ShapeDtypeStruct((R_MAX, D_MODEL), F32),
        compiler_params=pltpu.CompilerParams(
            dimension_semantics=("arbitrary",), vmem_limit_bytes=VMEM_LIMIT_BYTES),
        name="moe_ffn",
    )(tile_expert, n_used, next_expert, xs, w_gate_up,
      b_gate_up.reshape(DEPTH, N_EXPERTS, 1, 2 * D_FF), w_down,
      b_down.reshape(DEPTH, N_EXPERTS, 1, D_MODEL))


def _combine_kernel(pos_ref, pos_next_ref, ys_ref, x_ref, gate_ref, mod_ref, out_ref, buf, pos_smem, sem_idx, sem):
    i = pl.program_id(0)
    slot = i % 2

    def gather_tile(tile_pos_ref, s):
        cp = pltpu.make_async_copy(tile_pos_ref, pos_smem, sem_idx)
        cp.start()
        cp.wait()

        def issue(r, carry):
            for k in range(TOP_K):
                p = pos_smem[k, r]
                pltpu.make_async_copy(ys_ref.at[pl.ds(p, 1)], buf.at[s, k, pl.ds(r, 1)],
                                      sem.at[s, k]).start(priority=k % 2)
            return carry

        lax.fori_loop(0, TM, issue, 0, unroll=ISSUE_UNROLL)

    @pl.when(i == 0)
    def _():
        gather_tile(pos_ref, 0)

    @pl.when(i + 1 < N_TILES)
    def _():
        gather_tile(pos_next_ref, 1 - slot)

    acc = jnp.zeros((TM, D_MODEL), F32)
    for k in range(TOP_K):
        pltpu.make_async_copy(ys_ref.at[pl.ds(0, TM)], buf.at[slot, k], sem.at[slot, k]).wait()
        acc = acc + gate_ref[:, k:k + 1] * buf[slot, k]
    out_ref[...] = x_ref[...] + mod_ref[0, 5:6, :] * acc


def _combine(pos, ys, x_all, gate, mod_tab):
    return pl.pallas_call(
        _combine_kernel,
        grid=(N_TILES,),
        in_specs=[
            pl.BlockSpec((TOPK_ROWS, TM), lambda i: (0, i)),
            pl.BlockSpec((TOPK_ROWS, TM), lambda i: (0, jnp.minimum(i + 1, N_TILES - 1))),
            pl.BlockSpec(memory_space=pl.ANY),
            pl.BlockSpec((TM, D_MODEL), lambda i: (i, 0)),
            pl.BlockSpec((TM, TOPK_ROWS), lambda i: (i, 0)),
            pl.BlockSpec((1, 6, D_MODEL), lambda i: (_mod_row(i), 0, 0)),
        ],
        out_specs=pl.BlockSpec((TM, D_MODEL), lambda i: (i, 0)),
        out_shape=jax.ShapeDtypeStruct((N_TOK, D_MODEL), F32),
        scratch_shapes=[
            pltpu.VMEM((2, TOP_K, TM, D_MODEL), F32),
            pltpu.SMEM((TOPK_ROWS, TM), jnp.int32),
            pltpu.SemaphoreType.DMA,
            pltpu.SemaphoreType.DMA((2, TOP_K)),
        ],
        compiler_params=pltpu.CompilerParams(
            dimension_semantics=("arbitrary",), vmem_limit_bytes=VMEM_LIMIT_BYTES),
        name="moe_combine",
    )(pos, pos, ys, x_all, gate, mod_tab)


def _moe_layer(layer, x_all, mod_tab, g_norm2, w_router, b_router, w_gate_up, b_gate_up, w_down, b_down):
    h, topi, gate, rank, counts = _router(x_all, g_norm2[layer], mod_tab, w_router[layer], b_router[layer])
    counts = counts[:, 0]
    padded = ((counts + TM - 1) // TM) * TM
    ends = jnp.cumsum(padded)
    starts = ends - padded
    order = jnp.arange(N_EXPERTS, dtype=jnp.int32)
    first_row = jnp.sum(jnp.where(topi[None] == order[:, None, None], starts[:, None, None], 0), axis=0)
    pos = (first_row + rank).astype(jnp.int32)
    gate = gate.T
    n_used = (ends[-1] // TM).astype(jnp.int32)
    later = jnp.where((padded[None, :] > 0) & (order[None, :] > order[:, None]), order[None, :], N_EXPERTS)
    next_expert = jnp.min(later, axis=1)
    next_expert = jnp.where(next_expert == N_EXPERTS, -1, next_expert).astype(jnp.int32)
    tile_start = jnp.arange(R_TILES, dtype=jnp.int32) * TM
    tile_start = jnp.minimum(tile_start, ends[-1] - 1)
    tile_expert = jnp.sum((ends[None, :] <= tile_start[:, None]).astype(jnp.int32), axis=1)
    tile_expert = jnp.minimum(tile_expert, N_EXPERTS - 1).astype(jnp.int32)
    xs = _dispatch(ends.astype(jnp.int32), pos, h)
    ys = _ffn(layer, tile_expert, n_used.reshape(1), next_expert, xs, w_gate_up, b_gate_up, w_down, b_down)
    return _combine(pos, ys, x_all, gate, mod_tab)


S5_ROWS = 512
S5_HALF_W = S5_WIDTH // 2
S5_HALF_STATES = (S5_GROUPS // 2) * S5_STATE
S5_COL_CHUNK = 512


def _s5_scan_kernel(u_ref, bmat_ref, cmat_ref, a_ref, h0_ref, y_ref, fin_ref, bu_ref, h_ref, *, bsz, steps):
    d = pl.program_id(0)
    c = pl.program_id(1)
    hs = S5_HALF_STATES

    @pl.when(c == 0)
    def _():
        h_ref[...] = h0_ref[...]

    u = u_ref[...].astype(BF16)
    for hf in range(2):
        bu_ref[...] = jnp.dot(u[:, hf * S5_HALF_W:(hf + 1) * S5_HALF_W], bmat_ref[hf],
                              preferred_element_type=F32)
        for j in range(hs // S5_COL_CHUNK):
            re0 = j * S5_COL_CHUNK
            im0 = hs + j * S5_COL_CHUNK
            ar = jnp.broadcast_to(a_ref[hf, 0:1, re0:re0 + S5_COL_CHUNK], (bsz, S5_COL_CHUNK))
            ai = jnp.broadcast_to(a_ref[hf, 1:2, re0:re0 + S5_COL_CHUNK], (bsz, S5_COL_CHUNK))

            def step(t, carry, re0=re0, im0=im0, ar=ar, ai=ai):
                hr, hi = carry
                te = jnp.where(d == 0, t, steps - 1 - t)
                r0 = pl.multiple_of(te * bsz, bsz)
                br = bu_ref[pl.ds(r0, bsz), re0:re0 + S5_COL_CHUNK]
                bi = bu_ref[pl.ds(r0, bsz), im0:im0 + S5_COL_CHUNK]
                nr = ar * hr - ai * hi + br
                ni = ar * hi + ai * hr + bi
                bu_ref[pl.ds(r0, bsz), re0:re0 + S5_COL_CHUNK] = nr
                bu_ref[pl.ds(r0, bsz), im0:im0 + S5_COL_CHUNK] = ni
                return nr, ni

            hr, hi = lax.fori_loop(
                0, steps, step,
                (h_ref[hf, :, re0:re0 + S5_COL_CHUNK], h_ref[hf, :, im0:im0 + S5_COL_CHUNK]), unroll=4)
            h_ref[hf, :, re0:re0 + S5_COL_CHUNK] = hr
            h_ref[hf, :, im0:im0 + S5_COL_CHUNK] = hi
        y_ref[:, hf * S5_HALF_W:(hf + 1) * S5_HALF_W] = jnp.dot(
            bu_ref[...].astype(BF16), cmat_ref[hf], preferred_element_type=F32)

    @pl.when(c == pl.num_programs(1) - 1)
    def _():
        fin_ref[...] = h_ref[...]


def _s5_scan(u_tm, bmat, cmat, acoef, h0, bsz):
    rows = u_tm.shape[0]
    steps = S5_ROWS // bsz
    n_chunks = rows // S5_ROWS

    def chunk_map(d, c):
        return jnp.where(d == 0, c, n_chunks - 1 - c)

    return pl.pallas_call(
        functools.partial(_s5_scan_kernel, bsz=bsz, steps=steps),
        grid=(2, n_chunks),
        in_specs=[
            pl.BlockSpec((S5_ROWS, S5_WIDTH), lambda d, c: (chunk_map(d, c), 0)),
            pl.BlockSpec((None, 2, S5_HALF_W, 2 * S5_HALF_STATES), lambda d, c: (d, 0, 0, 0)),
            pl.BlockSpec((None, 2, 2 * S5_HALF_STATES, S5_HALF_W), lambda d, c: (d, 0, 0, 0)),
            pl.BlockSpec((None, 2, 2, S5_HALF_STATES), lambda d, c: (d, 0, 0, 0)),
            pl.BlockSpec((None, 2, bsz, 2 * S5_HALF_STATES), lambda d, c: (d, 0, 0, 0)),
        ],
        out_specs=[
            pl.BlockSpec((None, S5_ROWS, S5_WIDTH), lambda d, c: (d, chunk_map(d, c), 0)),
            pl.BlockSpec((None, 2, bsz, 2 * S5_HALF_STATES), lambda d, c: (d, 0, 0, 0)),
        ],
        out_shape=[
            jax.ShapeDtypeStruct((2, rows, S5_WIDTH), F32),
            jax.ShapeDtypeStruct((2, 2, bsz, 2 * S5_HALF_STATES), F32),
        ],
        scratch_shapes=[
            pltpu.VMEM((S5_ROWS, 2 * S5_HALF_STATES), F32),
            pltpu.VMEM((2, bsz, 2 * S5_HALF_STATES), F32),
        ],
        compiler_params=pltpu.CompilerParams(
            dimension_semantics=("arbitrary", "arbitrary"), vmem_limit_bytes=VMEM_LIMIT_BYTES),
        name="s5_scan",
    )(u_tm, bmat, cmat, acoef, h0)


def _s5_discretize(lam_re, lam_im, log_dt, b_re, b_im, c_re, c_im):
    eye = jnp.eye(S5_GROUPS // 2, dtype=F32)
    bmats, cmats, acoefs = [], [], []
    for dr in range(2):
        lr = jnp.minimum(lam_re[dr].astype(F32), -1e-4)
        li = lam_im[dr].astype(F32)
        dt = jnp.exp(log_dt[dr].astype(F32))[:, None]
        mag = jnp.exp(lr * dt)
        ar, ai = mag * jnp.cos(li * dt), mag * jnp.sin(li * dt)
        den = lr * lr + li * li
        fr = ((ar - 1.0) * lr + ai * li) / den
        fi = (ai * lr - (ar - 1.0) * li) / den
        br_ = b_re[dr].astype(F32)
        bi_ = b_im[dr].astype(F32)
        bbr = fr[..., None] * br_ - fi[..., None] * bi_
        bbi = fr[..., None] * bi_ + fi[..., None] * br_
        bm, cm, am = [], [], []
        for hf in range(2):
            g = slice(hf * S5_GROUPS // 2, (hf + 1) * S5_GROUPS // 2)

            def bdiag_in(w):
                return jnp.einsum('ab,aph->ahbp', eye, w[g]).reshape(S5_HALF_W, S5_HALF_STATES)

            def bdiag_out(w):
                return jnp.einsum('ab,ahp->apbh', eye, w[g]).reshape(S5_HALF_STATES, S5_HALF_W)

            bm.append(jnp.concatenate([bdiag_in(bbr), bdiag_in(bbi)], axis=1))
            cm.append(jnp.concatenate([bdiag_out(c_re[dr].astype(F32)),
                                       -bdiag_out(c_im[dr].astype(F32))], axis=0))
            am.append(jnp.stack([ar[g].reshape(-1), ai[g].reshape(-1)]))
        bmats.append(jnp.stack(bm))
        cmats.append(jnp.stack(cm))
        acoefs.append(jnp.stack(am))
    return jnp.stack(bmats).astype(BF16), jnp.stack(cmats).astype(BF16), jnp.stack(acoefs)


def _s5_state_to_kernel(h0):
    bsz = h0.shape[0]
    h = h0.astype(F32).reshape(bsz, 2, 2, 2, S5_HALF_STATES)
    return h.transpose(1, 3, 0, 2, 4).reshape(2, 2, bsz, 2 * S5_HALF_STATES)


def _s5_state_from_kernel(fin):
    bsz = fin.shape[2]
    h = fin.reshape(2, 2, bsz, 2, S5_HALF_STATES).transpose(2, 0, 3, 1, 4)
    return h.reshape(bsz, 2, 2, S5_GROUPS, S5_STATE)


TQ = 256
HEAD_LANES = 128
MLA_SCALE = (MLA_NOPE + MLA_ROPE) ** -0.5
DIFF_SCALE = DIFF_HD ** -0.5


def _dot(a, b):
    return jnp.dot(a, b, preferred_element_type=F32)


def _dot_t(a, b):
    return lax.dot_general(a, b, (((1,), (1,)), ((), ())), preferred_element_type=F32)


def _rms_rows(x, g):
    return x * lax.rsqrt(jnp.mean(x * x, axis=-1, keepdims=True) + EPS) * g


def _group(bsz, length, row0, mod_base, mod_stride):
    return dict(bsz=bsz, length=length, nt=length // TQ, tile0=row0 // TQ,
                mod_base=mod_base, mod_stride=mod_stride)


def _params(n_axes):
    return pltpu.CompilerParams(dimension_semantics=("arbitrary",) * n_axes,
                                vmem_limit_bytes=VMEM_LIMIT_BYTES)


def _axial_rope(length, dim):
    rows = length // GRID_W
    row = jnp.repeat(jnp.arange(rows, dtype=F32), GRID_W)
    col = jnp.tile(jnp.arange(GRID_W, dtype=F32), rows)
    n_freq = dim // 4
    inv = ROPE_THETA ** (-jnp.arange(n_freq, dtype=F32) / n_freq)
    ang = jnp.concatenate([row[:, None] * inv, col[:, None] * inv], axis=-1)
    return jnp.cos(ang), jnp.sin(ang)


def _rope_tables(length, dim, lead, reps):
    cos, sin = _axial_rope(length, dim)
    cos_r = jnp.repeat(cos, 2, axis=-1)
    sin_r = jnp.repeat(sin, 2, axis=-1) * jnp.tile(jnp.array([-1.0, 1.0], F32), dim // 2)
    part = HEAD_LANES // reps
    pad = ((0, 0), (lead, part - lead - dim))
    cos_t = jnp.tile(jnp.pad(cos_r, pad, constant_values=1.0), (1, reps))
    sin_t = jnp.tile(jnp.pad(sin_r, pad), (1, reps))
    return cos_t, sin_t


def _swap_pairs(w):
    return w[:, jnp.arange(w.shape[1]) ^ 1]


def _even_in_kernel(*refs, rope):
    if rope:
        (x_ref, g_ref, mod_ref, win_ref, gq_ref, wuq_ref, gkv_ref, cos_ref, sin_ref,
         u_ref, q_ref, ckv_ref, kr_ref) = refs
    else:
        (x_ref, g_ref, mod_ref, win_ref, gq_ref, wuq_ref, gkv_ref,
         u_ref, q_ref, ckv_ref, kr_ref) = refs
    o1 = S5_WIDTH
    o2 = o1 + MLA_Q_LORA
    o3 = o2 + MLA_KV_LORA
    o4 = o3 + HEAD_LANES
    n_in = o4 + HEAD_LANES if rope else o4
    n_q = MLA_HEADS * HEAD_LANES
    h = _rms_rows(x_ref[...], g_ref[...]) * (1.0 + mod_ref[0, 1:2, :]) + mod_ref[0, 0:1, :]
    z = _dot(h.astype(BF16), win_ref[:, :n_in])
    u_ref[...] = z[:, :o1]
    ckv_ref[...] = _rms_rows(z[:, o2:o3], gkv_ref[...])
    qn = _rms_rows(z[:, o1:o2], gq_ref[...]).astype(BF16)
    if rope:
        q2 = _dot(qn, wuq_ref[...])
        cos = cos_ref[...]
        sin = sin_ref[...]
        for hd in range(MLA_HEADS):
            a = hd * HEAD_LANES
            q_ref[:, a:a + HEAD_LANES] = ((q2[:, a:a + HEAD_LANES] * cos
                                           + q2[:, n_q + a:n_q + a + HEAD_LANES] * sin) * MLA_SCALE
                                          ).astype(q_ref.dtype)
        kr_ref[...] = z[:, o3:o4] * cos + z[:, o4:o4 + HEAD_LANES] * sin
    else:
        q_ref[...] = (_dot(qn, wuq_ref[:, :n_q]) * MLA_SCALE).astype(q_ref.dtype)
        kr_ref[...] = z[:, o3:o4]


def _even_in(x_all, grp, g1, mod_tab, win_aug, g_q, wuq2, g_kv, tables):
    bsz, length, nt = grp["bsz"], grp["length"], grp["nt"]
    rope = tables is not None
    rows = bsz * length

    def tok(b, t):
        return (b * nt + t, 0)

    in_specs = [
        pl.BlockSpec((TQ, D_MODEL), lambda b, t: (grp["tile0"] + b * nt + t, 0)),
        pl.BlockSpec((1, D_MODEL), lambda b, t: (0, 0)),
        pl.BlockSpec((1, 6, D_MODEL), lambda b, t: (grp["mod_base"] + b * grp["mod_stride"], 0, 0)),
        pl.BlockSpec(win_aug.shape, lambda b, t: (0, 0)),
        pl.BlockSpec((1, MLA_Q_LORA), lambda b, t: (0, 0)),
        pl.BlockSpec(wuq2.shape, lambda b, t: (0, 0)),
        pl.BlockSpec((1, MLA_KV_LORA), lambda b, t: (0, 0)),
    ]
    args = [x_all, g1.reshape(1, D_MODEL), mod_tab, win_aug, g_q.reshape(1, -1), wuq2, g_kv.reshape(1, -1)]
    if rope:
        in_specs += [pl.BlockSpec((TQ, HEAD_LANES), lambda b, t: (t, 0))] * 2
        args += list(tables)
    return pl.pallas_call(
        functools.partial(_even_in_kernel, rope=rope),
        grid=(bsz, nt),
        in_specs=in_specs,
        out_specs=[
            pl.BlockSpec((TQ, S5_WIDTH), lambda b, t: (t, b)),
            pl.BlockSpec((TQ, MLA_HEADS * HEAD_LANES), tok),
            pl.BlockSpec((TQ, MLA_KV_LORA), tok),
            pl.BlockSpec((TQ, HEAD_LANES), tok),
        ],
        out_shape=[
            jax.ShapeDtypeStruct((length, bsz * S5_WIDTH), F32),
            jax.ShapeDtypeStruct((rows, MLA_HEADS * HEAD_LANES), BF16),
            jax.ShapeDtypeStruct((rows, MLA_KV_LORA), F32),
            jax.ShapeDtypeStruct((rows, HEAD_LANES), F32),
        ],
        compiler_params=_params(2),
        name="even_in",
    )(*args)


def _kv_expand_kernel(x_ref, kr_ref, wk_ref, wv_ref, k_ref, v_ref):
    x = x_ref[...].astype(BF16)
    k = _dot(x, wk_ref[...])
    kr = kr_ref[...]
    for hd in range(MLA_HEADS):
        a = hd * HEAD_LANES
        k_ref[:, a:a + HEAD_LANES] = (k[:, a:a + HEAD_LANES] + kr).astype(k_ref.dtype)
    v_ref[...] = _dot(x, wv_ref[...]).astype(v_ref.dtype)


def _kv_expand(ckv, kr, wk, wv):
    rows = ckv.shape[0]
    tm = 512
    width = MLA_HEADS * HEAD_LANES
    return pl.pallas_call(
        _kv_expand_kernel,
        grid=(rows // tm,),
        in_specs=[pl.BlockSpec((tm, MLA_KV_LORA), lambda i: (i, 0)),
                  pl.BlockSpec((tm, HEAD_LANES), lambda i: (i, 0)),
                  pl.BlockSpec(wk.shape, lambda i: (0, 0)),
                  pl.BlockSpec(wv.shape, lambda i: (0, 0))],
        out_specs=[pl.BlockSpec((tm, width), lambda i: (i, 0))] * 2,
        out_shape=[jax.ShapeDtypeStruct((rows, width), BF16)] * 2,
        compiler_params=_params(1),
        name="kv_expand",
    )(ckv, kr, wk, wv)


def _exp_parts(scores):
    m = functools.reduce(jnp.maximum, [jnp.max(s, axis=-1, keepdims=True) for s in scores])
    es = [jnp.exp(s - m) for s in scores]
    den = functools.reduce(jnp.add, [jnp.sum(e, axis=-1, keepdims=True) for e in es])
    return es, 1.0 / den


def _weighted_values(es, vs):
    o = _dot(es[0].astype(BF16), vs[0])
    for e, v in zip(es[1:], vs[1:]):
        o = o + _dot(e.astype(BF16), v)
    return o


def _mla_attn_kernel(*refs, n_seg):
    q = refs[0][...]
    o_ref = refs[-1]
    ks = [refs[1 + 2 * s][...] for s in range(n_seg)]
    vs = [refs[2 + 2 * s][...] for s in range(n_seg)]
    es, inv = _exp_parts([_dot_t(q, k) for k in ks])
    o_ref[...] = (_weighted_values(es, vs) * inv).astype(o_ref.dtype)


def _mla_attn(q, segs, bsz, length):
    nq = length // TQ
    in_specs = [pl.BlockSpec((TQ, HEAD_LANES), lambda b, h, i: (b * nq + i, h))]
    args = [q]
    for k, v, lk in segs:
        in_specs += [pl.BlockSpec((lk, HEAD_LANES), lambda b, h, i: (b, h))] * 2
        args += [k, v]
    return pl.pallas_call(
        functools.partial(_mla_attn_kernel, n_seg=len(segs)),
        grid=(bsz, MLA_HEADS, nq),
        in_specs=in_specs,
        out_specs=pl.BlockSpec((TQ, HEAD_LANES), lambda b, h, i: (b * nq + i, h)),
        out_shape=jax.ShapeDtypeStruct((bsz * length, MLA_HEADS * HEAD_LANES), BF16),
        compiler_params=_params(3),
        name="mla_attn",
    )(*args)


def _even_out_kernel(u_ref, y_ref, o_ref, x_ref, mod_ref, d_ref, wglu_ref, bglu_ref, ws5_ref, wmla_ref,
                     out_ref):
    y =jax.nn.gelu(d_ref[...] * u_ref[...] + y_ref[0] + y_ref[1])
    s5 = y * jax.nn.sigmoid(_dot(y.astype(BF16), wglu_ref[...]) + bglu_ref[...])
    mix = _dot(s5.astype(BF16), ws5_ref[...]) + _dot(o_ref[...].astype(BF16), wmla_ref[...])
    out_ref[...] = x_ref[...] + mod_ref[0, 2:3, :] * mix


def _even_out(x_all, grp, mod_tab, u_tm, y_dir, o_mla, d_skip, w_glu, b_glu, w_out_s5, w_out_mla):
    bsz, length, nt = grp["bsz"], grp["length"], grp["nt"]

    def xrow(b, t):
        return (grp["tile0"] + b * nt + t, 0)

    full = lambda b, t: (0, 0)
    return pl.pallas_call(
        _even_out_kernel,
        grid=(bsz, nt),
        in_specs=[
            pl.BlockSpec((TQ, S5_WIDTH), lambda b, t: (t, b)),
            pl.BlockSpec((2, TQ, S5_WIDTH), lambda b, t: (0, t, b)),
            pl.BlockSpec((TQ, MLA_HEADS * HEAD_LANES), lambda b, t: (b * nt + t, 0)),
            pl.BlockSpec((TQ, D_MODEL), xrow),
            pl.BlockSpec((1, 6, D_MODEL), lambda b, t: (grp["mod_base"] + b * grp["mod_stride"], 0, 0)),
            pl.BlockSpec((1, S5_WIDTH), full),
            pl.BlockSpec(w_glu.shape, full),
            pl.BlockSpec((1, S5_WIDTH), full),
            pl.BlockSpec(w_out_s5.shape, full),
            pl.BlockSpec(w_out_mla.shape, full),
        ],
        out_specs=pl.BlockSpec((TQ, D_MODEL), xrow),
        out_shape=jax.ShapeDtypeStruct(x_all.shape, F32),
        input_output_aliases={3: 0},
        compiler_params=_params(2),
        name="even_out",
    )(u_tm, y_dir.reshape(2, length, bsz * S5_WIDTH), o_mla, x_all, mod_tab,
      d_skip.reshape(1, S5_WIDTH), w_glu, b_glu.reshape(1, S5_WIDTH), w_out_s5, w_out_mla)


def _odd_in_kernel(*refs, rope):
    if rope:
        x_ref, g_ref, mod_ref, w_ref, cos_ref, sin_ref, q_ref, k_ref, v_ref = refs
    else:
        x_ref, g_ref, mod_ref, w_ref, q_ref, k_ref, v_ref = refs
    w3 = 3 * DIFF_WIDTH
    h = _rms_rows(x_ref[...], g_ref[...]) * (1.0 + mod_ref[0, 1:2, :]) + mod_ref[0, 0:1, :]
    z = _dot(h.astype(BF16), w_ref[...] if rope else w_ref[:, :w3])
    v_ref[...] = z[:, 2 * DIFF_WIDTH:w3].astype(v_ref.dtype)
    if rope:
        cos = cos_ref[...]
        sin = sin_ref[...]
        for hd in range(DIFF_HEADS):
            a = hd * HEAD_LANES
            q_ref[:, a:a + HEAD_LANES] = ((z[:, a:a + HEAD_LANES] * cos
                                           + z[:, w3 + a:w3 + a + HEAD_LANES] * sin) * DIFF_SCALE
                                          ).astype(q_ref.dtype)
            b = DIFF_WIDTH + a
            k_ref[:, a:a + HEAD_LANES] = (z[:, b:b + HEAD_LANES] * cos
                                          + z[:, w3 + b:w3 + b + HEAD_LANES] * sin).astype(k_ref.dtype)
    else:
        q_ref[...] = (z[:, :DIFF_WIDTH] * DIFF_SCALE).astype(q_ref.dtype)
        k_ref[...] = z[:, DIFF_WIDTH:2 * DIFF_WIDTH].astype(k_ref.dtype)


def _odd_in(x_all, grp, g1, mod_tab, w_aug, tables, kv_dtype):
    bsz, length, nt = grp["bsz"], grp["length"], grp["nt"]
    rope = tables is not None
    rows = bsz * length

    def tok(b, t):
        return (b * nt + t, 0)

    in_specs = [
        pl.BlockSpec((TQ, D_MODEL), lambda b, t: (grp["tile0"] + b * nt + t, 0)),
        pl.BlockSpec((1, D_MODEL), lambda b, t: (0, 0)),
        pl.BlockSpec((1, 6, D_MODEL), lambda b, t: (grp["mod_base"] + b * grp["mod_stride"], 0, 0)),
        pl.BlockSpec(w_aug.shape, lambda b, t: (0, 0)),
    ]
    args = [x_all, g1.reshape(1, D_MODEL), mod_tab, w_aug]
    if rope:
        in_specs += [pl.BlockSpec((TQ, HEAD_LANES), lambda b, t: (t, 0))] * 2
        args += list(tables)
    return pl.pallas_call(
        functools.partial(_odd_in_kernel, rope=rope),
        grid=(bsz, nt),
        in_specs=in_specs,
        out_specs=[pl.BlockSpec((TQ, DIFF_WIDTH), tok)] * 3,
        out_shape=[
            jax.ShapeDtypeStruct((rows, DIFF_WIDTH), BF16),
            jax.ShapeDtypeStruct((rows, DIFF_WIDTH), kv_dtype),
            jax.ShapeDtypeStruct((rows, DIFF_WIDTH), kv_dtype),
        ],
        compiler_params=_params(2),
        name="odd_in",
    )(*args)


def _diff_attn_kernel(*refs, n_seg, post_scale):
    lam_ref, q_ref = refs[0], refs[1]
    g_ref, o_ref = refs[-2], refs[-1]
    q = q_ref[...].astype(F32)
    lane = lax.broadcasted_iota(jnp.int32, q.shape, 1)
    q0 = jnp.where(lane < DIFF_HD, q, 0.0).astype(BF16)
    q1 = jnp.where(lane >= DIFF_HD, q, 0.0).astype(BF16)
    ks = [refs[2 + 2 * s][...].astype(BF16) for s in range(n_seg)]
    vs = [refs[3 + 2 * s][...].astype(BF16) for s in range(n_seg)]
    e0, inv0 = _exp_parts([_dot_t(q0, k) for k in ks])
    e1, inv1 = _exp_parts([_dot_t(q1, k) for k in ks])
    o = _weighted_values(e0, vs) * inv0 - lam_ref[0] * (_weighted_values(e1, vs) * inv1)
    o_ref[...] = (_rms_rows(o, g_ref[...]) * post_scale).astype(o_ref.dtype)


def _diff_attn(lam_full, q, segs, g_sub, post_scale, bsz, length):
    nq = length // TQ
    in_specs = [pl.BlockSpec(memory_space=pltpu.SMEM),
                pl.BlockSpec((TQ, HEAD_LANES), lambda b, h, i: (b * nq + i, h))]
    args = [lam_full.reshape(1).astype(F32), q]
    for k, v, lk in segs:
        in_specs += [pl.BlockSpec((lk, HEAD_LANES), lambda b, h, i: (b, h))] * 2
        args += [k, v]
    in_specs.append(pl.BlockSpec((1, HEAD_LANES), lambda b, h, i: (0, 0)))
    args.append(g_sub.reshape(1, HEAD_LANES))
    return pl.pallas_call(
        functools.partial(_diff_attn_kernel, n_seg=len(segs), post_scale=post_scale),
        grid=(bsz, DIFF_HEADS, nq),
        in_specs=in_specs,
        out_specs=pl.BlockSpec((TQ, HEAD_LANES), lambda b, h, i: (b * nq + i, h)),
        out_shape=jax.ShapeDtypeStruct((bsz * length, DIFF_WIDTH), BF16),
        compiler_params=_params(3),
        name="diff_attn",
    )(*args)


def _odd_out_kernel(o_ref, x_ref, mod_ref, w_ref, out_ref):
    out_ref[...] = x_ref[...] + mod_ref[0, 2:3, :] * _dot(o_ref[...], w_ref[...])


def _odd_out(x_all, grp, mod_tab, o, w_out):
    bsz, nt = grp["bsz"], grp["nt"]

    def xrow(b, t):
        return (grp["tile0"] + b * nt + t, 0)

    return pl.pallas_call(
        _odd_out_kernel,
        grid=(bsz, nt),
        in_specs=[
            pl.BlockSpec((TQ, DIFF_WIDTH), lambda b, t: (b * nt + t, 0)),
            pl.BlockSpec((TQ, D_MODEL), xrow),
            pl.BlockSpec((1, 6, D_MODEL), lambda b, t: (grp["mod_base"] + b * grp["mod_stride"], 0, 0)),
            pl.BlockSpec(w_out.shape, lambda b, t: (0, 0)),
        ],
        out_specs=pl.BlockSpec((TQ, D_MODEL), xrow),
        out_shape=jax.ShapeDtypeStruct(x_all.shape, F32),
        input_output_aliases={1: 0},
        compiler_params=_params(2),
        name="odd_out",
    )(o, x_all, mod_tab, w_out)


def _final_norm_kernel(x_ref, g_ref, o_ref):
    o_ref[...] = _rms_rows(x_ref[...], g_ref[...])


def _final_norm(x_all, g, row0, rows):
    tm = 512
    return pl.pallas_call(
        _final_norm_kernel,
        grid=(rows // tm,),
        in_specs=[pl.BlockSpec((tm, D_MODEL), lambda i: (row0 // tm + i, 0)),
                  pl.BlockSpec((1, D_MODEL), lambda i: (0, 0))],
        out_specs=pl.BlockSpec((tm, D_MODEL), lambda i: (i, 0)),
        out_shape=jax.ShapeDtypeStruct((rows, D_MODEL), F32),
        compiler_params=_params(1),
        name="final_norm",
    )(x_all, g.reshape(1, D_MODEL))


MOD_ROWS = 16
MOD_COL_TILE = 1536


def _adaln_kernel(c_ref, w_ref, b_ref, o_ref):
    cond = jax.nn.silu(c_ref[...])
    o_ref[...] = jnp.dot(cond, w_ref[...], preferred_element_type=F32,
                         precision=lax.Precision.HIGHEST) + b_ref[...]


def _adaln(cond_rows, w_mod, b_mod):
    return pl.pallas_call(
        _adaln_kernel,
        grid=(DEPTH, 6 * D_MODEL // MOD_COL_TILE),
        in_specs=[pl.BlockSpec((MOD_ROWS, D_MODEL), lambda l, j: (0, 0)),
                  pl.BlockSpec((None, D_MODEL, MOD_COL_TILE), lambda l, j: (l, 0, j)),
                  pl.BlockSpec((None, 1, MOD_COL_TILE), lambda l, j: (l, 0, j))],
        out_specs=pl.BlockSpec((None, MOD_ROWS, MOD_COL_TILE), lambda l, j: (l, 0, j)),
        out_shape=jax.ShapeDtypeStruct((DEPTH, MOD_ROWS, 6 * D_MODEL), F32),
        compiler_params=_params(2),
        name="adaln",
    )(cond_rows, w_mod, b_mod.reshape(DEPTH, 1, 6 * D_MODEL))


def _pad_head_lanes(x, lead):
    return jnp.pad(x, ((0, 0), (lead, HEAD_LANES - lead - x.shape[1])))


def _even_weights(w_in, w_out, w_uq, w_ukv, w_glu):
    o3 = S5_WIDTH + MLA_Q_LORA + MLA_KV_LORA
    w_kr = w_in[:, o3:]
    win_aug = jnp.concatenate(
        [w_in[:, :o3], _pad_head_lanes(w_kr, MLA_NOPE), _pad_head_lanes(_swap_pairs(w_kr), MLA_NOPE)], axis=1)
    dq = MLA_NOPE + MLA_ROPE
    plain, swapped = [], []
    for hd in range(MLA_HEADS):
        wn = w_uq[:, hd * dq:hd * dq + MLA_NOPE]
        wr = w_uq[:, hd * dq + MLA_NOPE:(hd + 1) * dq]
        plain.append(jnp.pad(jnp.concatenate([wn, wr], axis=1), ((0, 0), (0, HEAD_LANES - dq))))
        swapped.append(_pad_head_lanes(_swap_pairs(wr), MLA_NOPE))
    wuq2 = jnp.concatenate(plain + swapped, axis=1)
    w_mla = w_out[S5_WIDTH:].reshape(MLA_HEADS, MLA_V, D_MODEL)
    w_out_mla = jnp.pad(w_mla, ((0, 0), (0, HEAD_LANES - MLA_V), (0, 0))).reshape(MLA_HEADS * HEAD_LANES, D_MODEL)
    w_kv = w_ukv.reshape(MLA_KV_LORA, MLA_HEADS, MLA_NOPE + MLA_V)
    wk = jnp.pad(w_kv[:, :, :MLA_NOPE], ((0, 0), (0, 0), (0, HEAD_LANES - MLA_NOPE)))
    wv = jnp.pad(w_kv[:, :, MLA_NOPE:], ((0, 0), (0, 0), (0, HEAD_LANES - MLA_V)))
    w_kv = (wk.reshape(MLA_KV_LORA, -1).astype(BF16), wv.reshape(MLA_KV_LORA, -1).astype(BF16))
    return (win_aug.astype(BF16), wuq2.astype(BF16), w_kv, w_glu.astype(BF16),
            w_out[:S5_WIDTH].astype(BF16), w_out_mla.astype(BF16))


def _even_layer(x_all, grp, g1, mod_tab, ew, s5m, g_q, g_kv, d_skip, b_glu, h0, ctx, tables):
    win_aug, wuq2, w_kv, w_glu, w_out_s5, w_out_mla = ew
    bmat, cmat, acoef = s5m
    bsz, length = grp["bsz"], grp["length"]
    u_tm, q, ckv, kr = _even_in(x_all, grp, g1, mod_tab, win_aug, g_q, wuq2, g_kv, tables)
    y_dir, fin = _s5_scan(u_tm.reshape(length * bsz, S5_WIDTH), bmat, cmat, acoef, h0, bsz)
    segs = [(*_kv_expand(ckv, kr, *w_kv), length)]
    if ctx is not None:
        segs.append((*_kv_expand(*ctx, *w_kv), PAST_LEN))
    o_mla = _mla_attn(q, segs, bsz, length)
    x_all = _even_out(x_all, grp, mod_tab, u_tm, y_dir, o_mla, d_skip, w_glu, b_glu, w_out_s5, w_out_mla)
    return x_all, fin, ckv, kr


def _odd_layer(x_all, grp, g1, mod_tab, w_aug, w_out, lam_full, g_sub, post_scale, ctx, tables, kv_dtype):
    bsz, length = grp["bsz"], grp["length"]
    q, k, v = _odd_in(x_all, grp, g1, mod_tab, w_aug, tables, kv_dtype)
    segs = [(k, v, length)]
    if ctx is not None:
        segs.append((ctx[0], ctx[1], PAST_LEN))
    o = _diff_attn(lam_full, q, segs, g_sub, post_scale, bsz, length)
    return _odd_out(x_all, grp, mod_tab, o, w_out), k, v


def kernel(x_prompt, x_sample, state_s5, cache_mla, cache_diff_k, cache_diff_v, c, c_ctx, w_mod, b_mod, g_norm1, g_norm2, g_final, w_in_even, w_out_even, s5_lam_re, s5_lam_im, s5_log_dt, s5_b_re, s5_b_im, s5_c_re, s5_c_im, s5_d, s5_w_glu, s5_b_glu, mla_g_q, mla_w_uq, mla_g_kv, mla_w_ukv, w_in_odd, w_out_odd, diff_lam, diff_g_sub, w_router, b_router, w_gate_up, b_gate_up, w_down, b_down):
    tab_mla = _rope_tables(DEC_SEQ, MLA_ROPE, MLA_NOPE, 1)
    tab_diff = _rope_tables(DEC_SEQ, DIFF_HD, 0, 2)
    grp_p = _group(BATCH, SEQ, 0, 0, 0)
    grp_s = _group(DEC_BATCH, DEC_SEQ, N_PROMPT, 1, 1)
    x_all = jnp.concatenate([x_prompt.reshape(N_PROMPT, D_MODEL), x_sample.reshape(N_SAMPLE, D_MODEL)], axis=0)
    cond = jnp.concatenate([c_ctx[None], c, jnp.zeros((MOD_ROWS - 1 - DEC_BATCH, D_MODEL), c.dtype)], axis=0)
    mods = _adaln(cond.astype(F32), w_mod, b_mod)
    new_s5, new_mla, new_k, new_v = [], [], [], []
    for l in range(DEPTH):
        mod_tab = mods[l].reshape(MOD_ROWS, 6, D_MODEL)
        i = l // 2
        if l % 2 == 0:
            ew = _even_weights(w_in_even[i], w_out_even[i], mla_w_uq[i], mla_w_ukv[i], s5_w_glu[i])
            s5m = _s5_discretize(s5_lam_re[i], s5_lam_im[i], s5_log_dt[i], s5_b_re[i], s5_b_im[i],
                                 s5_c_re[i], s5_c_im[i])
            common = (ew, s5m, mla_g_q[i], mla_g_kv[i], s5_d[i], s5_b_glu[i])
            h0_p = jnp.zeros((2, 2, BATCH, 2 * S5_HALF_STATES), F32)
            x_all, fin, ckv, kr = _even_layer(x_all, grp_p, g_norm1[l], mod_tab, *common, h0_p, None, None)
            new_s5.append(_s5_state_from_kernel(fin))
            new_mla.append(jnp.concatenate([ckv, kr[:, MLA_NOPE:MLA_NOPE + MLA_ROPE]], axis=1)
                           .reshape(BATCH, SEQ, MLA_KV_LORA + MLA_ROPE))
            lat_ctx = cache_mla[:, i].astype(F32).reshape(DEC_BATCH * PAST_LEN, MLA_KV_LORA + MLA_ROPE)
            ctx = (lat_ctx[:, :MLA_KV_LORA], _pad_head_lanes(lat_ctx[:, MLA_KV_LORA:], MLA_NOPE))
            x_all, _, _, _ = _even_layer(x_all, grp_s, g_norm1[l], mod_tab, *common,
                                         _s5_state_to_kernel(state_s5[:, i]), ctx, tab_mla)
        else:
            lam_init = 0.8 - 0.6 * math.exp(-0.3 * l)
            lamf = diff_lam[i].astype(F32)
            lam_full = jnp.exp(jnp.sum(lamf[0] * lamf[1])) - jnp.exp(jnp.sum(lamf[2] * lamf[3])) + lam_init
            w_qk = w_in_odd[i][:, :2 * DIFF_WIDTH]
            w_aug = jnp.concatenate([w_in_odd[i], _swap_pairs(w_qk)], axis=1).astype(BF16)
            w_out = w_out_odd[i].astype(BF16)
            odd = (w_aug, w_out, lam_full, diff_g_sub[i], 1.0 - lam_init)
            x_all, kp, vp = _odd_layer(x_all, grp_p, g_norm1[l], mod_tab, *odd, None, None, F32)
            new_k.append(kp.reshape(BATCH, SEQ, DIFF_HEADS, 2, DIFF_HD))
            new_v.append(vp.reshape(BATCH, SEQ, DIFF_HEADS, 2 * DIFF_HD))
            ctx = (cache_diff_k[:, i].reshape(DEC_BATCH * PAST_LEN, DIFF_WIDTH).astype(BF16),
                   cache_diff_v[:, i].reshape(DEC_BATCH * PAST_LEN, DIFF_WIDTH).astype(BF16))
            x_all, _, _ = _odd_layer(x_all, grp_s, g_norm1[l], mod_tab, *odd, ctx, tab_diff, BF16)
        x_all = _moe_layer(l, x_all, mod_tab, g_norm2, w_router, b_router,
                           w_gate_up, b_gate_up, w_down, b_down)
    y_prompt = _final_norm(x_all, g_final, 0, N_PROMPT)
    y_sample = _final_norm(x_all, g_final, N_PROMPT, N_SAMPLE)
    return (y_prompt.reshape(BATCH, SEQ, D_MODEL), y_sample.reshape(DEC_BATCH, DEC_SEQ, D_MODEL),
            jnp.stack(new_s5, axis=1), jnp.stack(new_mla, axis=1),
            jnp.stack(new_k, axis=1), jnp.stack(new_v, axis=1))
```

```python
import functools
import math

import jax
import jax.numpy as jnp
from jax import lax
from jax.experimental import pallas as pl
from jax.experimental.pallas import tpu as pltpu

D_MODEL = 1024
BATCH = 16
SEQ = 256
DEPTH = 4
DEC_BATCH = 8
DEC_SEQ = 1024
PAST_LEN = 512
GRID_W = 64
N_EVEN = (DEPTH + 1) // 2
N_ODD = DEPTH // 2
S5_WIDTH = D_MODEL // 2
S5_GROUP = 16
S5_GROUPS = S5_WIDTH // S5_GROUP
S5_STATE = 64
MLA_HEADS = 8
MLA_NOPE = 64
MLA_ROPE = 32
MLA_V = 64
MLA_Q_LORA = D_MODEL // 4
MLA_KV_LORA = D_MODEL // 8
MLA_WIDTH = MLA_HEADS * MLA_V
EVEN_IN = S5_WIDTH + MLA_Q_LORA + MLA_KV_LORA + MLA_ROPE
EVEN_OUT = S5_WIDTH + MLA_WIDTH
DIFF_HEADS = 8
DIFF_HD = D_MODEL // (2 * DIFF_HEADS)
DIFF_WIDTH = DIFF_HEADS * 2 * DIFF_HD
N_EXPERTS = 32
TOP_K = 4
D_FF = D_MODEL
SWIGLU_LIMIT = 7.0
SWIGLU_ALPHA = 1.702
ROPE_THETA = 10000.0
Q_BLOCK = 128
EPS = 1e-6

N_PROMPT = BATCH * SEQ
N_SAMPLE = DEC_BATCH * DEC_SEQ
N_TOK = N_PROMPT + N_SAMPLE

LANES = 128
VMEM_LIMIT_BYTES = 56 * 1024 * 1024

TM = 256
N_TILES = N_TOK // TM
R_TILES = N_TOK * TOP_K // TM + N_EXPERTS
R_MAX = R_TILES * TM
TOPK_ROWS = 8

F32 = jnp.float32
BF16 = jnp.bfloat16


def _mod_row(i):
    t0 = i * TM
    return jnp.where(t0 < N_PROMPT, 0, 1 + (t0 - N_PROMPT) // DEC_SEQ)


def _router_kernel(x_ref, g_ref, mod_ref, wr_ref, br_ref,
                   h_ref, topi_ref, gate_ref, rank_ref, counts_ref, carry_ref):
    i = pl.program_id(0)

    @pl.when(i == 0)
    def _():
        carry_ref[...] = jnp.zeros_like(carry_ref)

    x = x_ref[...]
    ms = jnp.mean(x * x, axis=-1, keepdims=True)
    y = x * lax.rsqrt(ms + EPS) * g_ref[...]
    shift = mod_ref[0, 3:4, :]
    scale = mod_ref[0, 4:5, :]
    h = y * (1.0 + scale) + shift
    h_ref[...] = h

    hi = h.astype(BF16)
    lo = (h - hi.astype(F32)).astype(BF16)
    w_hi = wr_ref[0]
    logits = _dot_t(w_hi, hi) + (_dot_t(wr_ref[1], hi) + _dot_t(w_hi, lo)) + br_ref[...]
    sub_e = lax.broadcasted_iota(jnp.int32, logits.shape, 0)
    work = logits
    vals, hits = [], []
    sel = jnp.zeros(logits.shape, F32)
    for _ in range(TOP_K):
        m = jnp.max(work, axis=0, keepdims=True)
        idx = jnp.min(jnp.where(work == m, sub_e, N_EXPERTS), axis=0, keepdims=True)
        hit = sub_e == idx
        vals.append(m)
        hits.append((hit, idx))
        sel = jnp.where(hit, 1.0, sel)
        work = jnp.where(hit, -jnp.inf, work)
    es = [jnp.exp(v - vals[0]) for v in vals]
    inv = 1.0 / (es[0] + es[1] + es[2] + es[3])

    row = lax.broadcasted_iota(jnp.int32, (TM, TM), 0)
    col = lax.broadcasted_iota(jnp.int32, (TM, TM), 1)
    earlier = jnp.where(row < col, 1.0, 0.0).astype(BF16)
    before = _dot(sel.astype(BF16), earlier) + carry_ref[...]
    carry_ref[...] += jnp.sum(sel, axis=1, keepdims=True)
    counts_ref[...] = carry_ref[...].astype(jnp.int32)

    sub_k = lax.broadcasted_iota(jnp.int32, (TOPK_ROWS, TM), 0)
    topi = jnp.zeros((TOPK_ROWS, TM), jnp.int32)
    gate = jnp.zeros((TOPK_ROWS, TM), F32)
    rank = jnp.zeros((TOPK_ROWS, TM), jnp.int32)
    for k in range(TOP_K):
        hit, idx = hits[k]
        rk = jnp.sum(jnp.where(hit, before, 0.0), axis=0, keepdims=True)
        topi = jnp.where(sub_k == k, idx, topi)
        gate = jnp.where(sub_k == k, es[k] * inv, gate)
        rank = jnp.where(sub_k == k, rk.astype(jnp.int32), rank)
    topi_ref[...] = topi
    gate_ref[...] = gate
    rank_ref[...] = rank


def _router(x_all, g, mod_tab, w_router, b_router):
    w_t = w_router.astype(F32).T
    w_hi = w_t.astype(BF16)
    w_split = jnp.stack([w_hi, (w_t - w_hi.astype(F32)).astype(BF16)])
    return pl.pallas_call(
        _router_kernel,
        grid=(N_TILES,),
        in_specs=[
            pl.BlockSpec((TM, D_MODEL), lambda i: (i, 0)),
            pl.BlockSpec((1, D_MODEL), lambda i: (0, 0)),
            pl.BlockSpec((1, 6, D_MODEL), lambda i: (_mod_row(i), 0, 0)),
            pl.BlockSpec((2, N_EXPERTS, D_MODEL), lambda i: (0, 0, 0)),
            pl.BlockSpec((N_EXPERTS, 1), lambda i: (0, 0)),
        ],
        out_specs=[
            pl.BlockSpec((TM, D_MODEL), lambda i: (i, 0)),
            pl.BlockSpec((TOPK_ROWS, TM), lambda i: (0, i)),
            pl.BlockSpec((TOPK_ROWS, TM), lambda i: (0, i)),
            pl.BlockSpec((TOPK_ROWS, TM), lambda i: (0, i)),
            pl.BlockSpec((N_EXPERTS, 1), lambda i: (0, 0)),
        ],
        out_shape=[
            jax.ShapeDtypeStruct((N_TOK, D_MODEL), F32),
            jax.ShapeDtypeStruct((TOPK_ROWS, N_TOK), jnp.int32),
            jax.ShapeDtypeStruct((TOPK_ROWS, N_TOK), F32),
            jax.ShapeDtypeStruct((TOPK_ROWS, N_TOK), jnp.int32),
            jax.ShapeDtypeStruct((N_EXPERTS, 1), jnp.int32),
        ],
        scratch_shapes=[pltpu.VMEM((N_EXPERTS, 1), F32)],
        compiler_params=pltpu.CompilerParams(
            dimension_semantics=("arbitrary",), vmem_limit_bytes=VMEM_LIMIT_BYTES),
        name="moe_router",
    )(x_all, g.reshape(1, D_MODEL), mod_tab, w_split, b_router.reshape(N_EXPERTS, 1))


ISSUE_UNROLL = 4


def _dispatch_kernel(ends_ref, pos_ref, h_ref, xs_ref, zero_buf, pos_smem, sem_idx, sem, sem_zero):
    i = pl.program_id(0)

    @pl.when(i == 0)
    def _():
        zero_buf[...] = jnp.zeros_like(zero_buf)

        for wait in (False, True):
            for e in range(N_EXPERTS):
                start = ends_ref[e - 1] if e > 0 else 0

                @pl.when(ends_ref[e] > start)
                def _(e=e, wait=wait):
                    last = pl.multiple_of(ends_ref[e] - TM, TM)
                    cp = pltpu.make_async_copy(zero_buf, xs_ref.at[pl.ds(last, TM)], sem_zero)
                    if wait:
                        cp.wait()
                    else:
                        cp.start()

            def tail(t, carry, wait=wait):
                cp = pltpu.make_async_copy(zero_buf, xs_ref.at[pl.ds(pl.multiple_of(t * TM, TM), TM)], sem_zero)
                if wait:
                    cp.wait()
                else:
                    cp.start()
                return carry

            lax.fori_loop(ends_ref[N_EXPERTS - 1] // TM, R_TILES, tail, 0)

    cp = pltpu.make_async_copy(pos_ref, pos_smem, sem_idx)
    cp.start()
    cp.wait()

    def issue(r, carry):
        for k in range(TOP_K):
            p = pos_smem[k, r]
            pltpu.make_async_copy(h_ref.at[pl.ds(r, 1)], xs_ref.at[pl.ds(p, 1)], sem.at[k]).start(priority=k % 2)
        return carry

    lax.fori_loop(0, TM, issue, 0, unroll=ISSUE_UNROLL)
    for k in range(TOP_K):
        pltpu.make_async_copy(h_ref, xs_ref.at[pl.ds(0, TM)], sem.at[k]).wait()


def _dispatch(ends, pos, h):
    grid_spec = pltpu.PrefetchScalarGridSpec(
        num_scalar_prefetch=1,
        grid=(N_TILES,),
        in_specs=[
            pl.BlockSpec((TOPK_ROWS, TM), lambda i, ends: (0, i)),
            pl.BlockSpec((TM, D_MODEL), lambda i, ends: (i, 0)),
        ],
        out_specs=pl.BlockSpec(memory_space=pl.ANY),
        scratch_shapes=[
            pltpu.VMEM((TM, D_MODEL), F32),
            pltpu.SMEM((TOPK_ROWS, TM), jnp.int32),
            pltpu.SemaphoreType.DMA,
            pltpu.SemaphoreType.DMA((TOP_K,)),
            pltpu.SemaphoreType.DMA,
        ],
    )
    return pl.pallas_call(
        _dispatch_kernel,
        grid_spec=grid_spec,
        out_shape=jax.ShapeDtypeStruct((R_MAX, D_MODEL), F32),
        compiler_params=pltpu.CompilerParams(
            dimension_semantics=("arbitrary",), vmem_limit_bytes=VMEM_LIMIT_BYTES),
        name="moe_dispatch",
    )(ends, pos, h)


def _ffn_kernel(te_ref, nu_ref, nx_ref, xs_ref, wgu_hbm, bgu_ref, wd_hbm, bd_ref, ys_ref,
                wgu_f32, wd_f32, wgu_bf, wd_bf, sem, *, layer):
    i = pl.program_id(0)

    def weight_copies(e):
        return (pltpu.make_async_copy(wgu_hbm.at[layer, e], wgu_f32, sem.at[0]),
                pltpu.make_async_copy(wd_hbm.at[layer, e], wd_f32, sem.at[1]))

    @pl.when(i < nu_ref[0])
    def _():
        expert = te_ref[i]
        new_expert = jnp.logical_or(i == 0, expert != te_ref[jnp.maximum(i - 1, 0)])

        @pl.when(i == 0)
        def _():
            for cp in weight_copies(expert):
                cp.start()

        @pl.when(new_expert)
        def _():
            for cp in weight_copies(expert):
                cp.wait()
            wgu_bf[...] = wgu_f32[...].astype(BF16)
            wd_bf[...] = wd_f32[...].astype(BF16)
            nxt = nx_ref[expert]

            @pl.when(nxt >= 0)
            def _():
                for cp in weight_copies(nxt):
                    cp.start()

        x = xs_ref[...].astype(BF16)
        gu = jnp.dot(x, wgu_bf[...], preferred_element_type=F32) + bgu_ref[...]
        g = jnp.minimum(gu[:, :D_FF], SWIGLU_LIMIT)
        u = jnp.clip(gu[:, D_FF:], -SWIGLU_LIMIT, SWIGLU_LIMIT)
        act = g * jax.nn.sigmoid(SWIGLU_ALPHA * g) * (u + 1.0)
        ys_ref[...] = jnp.dot(act.astype(BF16), wd_bf[...], preferred_element_type=F32) + bd_ref[...]

    @pl.when(i >= nu_ref[0])
    def _():
        ys_ref[...] = jnp.zeros_like(ys_ref)


def _ffn(layer, tile_expert, n_used, next_expert, xs, w_gate_up, b_gate_up, w_down, b_down):
    def row_map(i, te, nu, nx):
        return (jnp.maximum(jnp.minimum(i, nu[0] - 1), 0), 0)

    def b_map(i, te, nu, nx):
        return (layer, te[i], 0, 0)

    grid_spec = pltpu.PrefetchScalarGridSpec(
        num_scalar_prefetch=3,
        grid=(R_TILES,),
        in_specs=[
            pl.BlockSpec((TM, D_MODEL), row_map),
            pl.BlockSpec(memory_space=pl.ANY),
            pl.BlockSpec((None, None, 1, 2 * D_FF), b_map),
            pl.BlockSpec(memory_space=pl.ANY),
            pl.BlockSpec((None, None, 1, D_MODEL), b_map),
        ],
        out_specs=pl.BlockSpec((TM, D_MODEL), lambda i, te, nu, nx: (i, 0)),
        scratch_shapes=[
            pltpu.VMEM((D_MODEL, 2 * D_FF), F32),
            pltpu.VMEM((D_FF, D_MODEL), F32),
            pltpu.VMEM((D_MODEL, 2 * D_FF), BF16),
            pltpu.VMEM((D_FF, D_MODEL), BF16),
            pltpu.SemaphoreType.DMA((2,)),
        ],
    )
    return pl.pallas_call(
        functools.partial(_ffn_kernel, layer=layer),
        grid_spec=grid_spec,
        out_shape=jax.ShapeDtypeStruct((R_MAX, D_MODEL), F32),
        compiler_params=pltpu.CompilerParams(
            dimension_semantics=("arbitrary",), vmem_limit_bytes=VMEM_LIMIT_BYTES),
        name="moe_ffn",
    )(tile_expert, n_used, next_expert, xs, w_gate_up,
      b_gate_up.reshape(DEPTH, N_EXPERTS, 1, 2 * D_FF), w_down,
      b_down.reshape(DEPTH, N_EXPERTS, 1, D_MODEL))


def _combine_kernel(pos_ref, pos_next_ref, ys_ref, x_ref, gate_ref, mod_ref, out_ref, buf, pos_smem, sem_idx, sem):
    i = pl.program_id(0)
    slot = i % 2

    def gather_tile(tile_pos_ref, s):
        cp = pltpu.make_async_copy(tile_pos_ref, pos_smem, sem_idx)
        cp.start()
        cp.wait()

        def issue(r, carry):
            for k in range(TOP_K):
                p = pos_smem[k, r]
                pltpu.make_async_copy(ys_ref.at[pl.ds(p, 1)], buf.at[s, k, pl.ds(r, 1)],
                                      sem.at[s, k]).start(priority=k % 2)
            return carry

        lax.fori_loop(0, TM, issue, 0, unroll=ISSUE_UNROLL)

    @pl.when(i == 0)
    def _():
        gather_tile(pos_ref, 0)

    @pl.when(i + 1 < N_TILES)
    def _():
        gather_tile(pos_next_ref, 1 - slot)

    acc = jnp.zeros((TM, D_MODEL), F32)
    for k in range(TOP_K):
        pltpu.make_async_copy(ys_ref.at[pl.ds(0, TM)], buf.at[slot, k], sem.at[slot, k]).wait()
        acc = acc + gate_ref[:, k:k + 1] * buf[slot, k]
    out_ref[...] = x_ref[...] + mod_ref[0, 5:6, :] * acc


def _combine(pos, ys, x_all, gate, mod_tab):
    return pl.pallas_call(
        _combine_kernel,
        grid=(N_TILES,),
        in_specs=[
            pl.BlockSpec((TOPK_ROWS, TM), lambda i: (0, i)),
            pl.BlockSpec((TOPK_ROWS, TM), lambda i: (0, jnp.minimum(i + 1, N_TILES - 1))),
            pl.BlockSpec(memory_space=pl.ANY),
            pl.BlockSpec((TM, D_MODEL), lambda i: (i, 0)),
            pl.BlockSpec((TM, TOPK_ROWS), lambda i: (i, 0)),
            pl.BlockSpec((1, 6, D_MODEL), lambda i: (_mod_row(i), 0, 0)),
        ],
        out_specs=pl.BlockSpec((TM, D_MODEL), lambda i: (i, 0)),
        out_shape=jax.ShapeDtypeStruct((N_TOK, D_MODEL), F32),
        scratch_shapes=[
            pltpu.VMEM((2, TOP_K, TM, D_MODEL), F32),
            pltpu.SMEM((TOPK_ROWS, TM), jnp.int32),
            pltpu.SemaphoreType.DMA,
            pltpu.SemaphoreType.DMA((2, TOP_K)),
        ],
        compiler_params=pltpu.CompilerParams(
            dimension_semantics=("arbitrary",), vmem_limit_bytes=VMEM_LIMIT_BYTES),
        name="moe_combine",
    )(pos, pos, ys, x_all, gate, mod_tab)


def _moe_layer(layer, x_all, mod_tab, g_norm2, w_router, b_router, w_gate_up, b_gate_up, w_down, b_down):
    h, topi, gate, rank, counts = _router(x_all, g_norm2[layer], mod_tab, w_router[layer], b_router[layer])
    counts = counts[:, 0]
    padded = ((counts + TM - 1) // TM) * TM
    ends = jnp.cumsum(padded)
    starts = ends - padded
    order = jnp.arange(N_EXPERTS, dtype=jnp.int32)
    first_row = jnp.sum(jnp.where(topi[None] == order[:, None, None], starts[:, None, None], 0), axis=0)
    pos = (first_row + rank).astype(jnp.int32)
    gate = gate.T
    n_used = (ends[-1] // TM).astype(jnp.int32)
    later = jnp.where((padded[None, :] > 0) & (order[None, :] > order[:, None]), order[None, :], N_EXPERTS)
    next_expert = jnp.min(later, axis=1)
    next_expert = jnp.where(next_expert == N_EXPERTS, -1, next_expert).astype(jnp.int32)
    tile_start = jnp.arange(R_TILES, dtype=jnp.int32) * TM
    tile_start = jnp.minimum(tile_start, ends[-1] - 1)
    tile_expert = jnp.sum((ends[None, :] <= tile_start[:, None]).astype(jnp.int32), axis=1)
    tile_expert = jnp.minimum(tile_expert, N_EXPERTS - 1).astype(jnp.int32)
    xs = _dispatch(ends.astype(jnp.int32), pos, h)
    ys = _ffn(layer, tile_expert, n_used.reshape(1), next_expert, xs, w_gate_up, b_gate_up, w_down, b_down)
    return _combine(pos, ys, x_all, gate, mod_tab)


S5_ROWS = 512
S5_HALF_W = S5_WIDTH // 2
S5_HALF_STATES = (S5_GROUPS // 2) * S5_STATE
S5_COL_CHUNK = 512


def _s5_scan_kernel(u_ref, bmat_ref, cmat_ref, a_ref, h0_ref, y_ref, fin_ref, bu_ref, h_ref, *, bsz, steps):
    d = pl.program_id(0)
    c = pl.program_id(1)
    hs = S5_HALF_STATES

    @pl.when(c == 0)
    def _():
        h_ref[...] = h0_ref[...]

    u = u_ref[...].astype(BF16)
    for hf in range(2):
        bu_ref[...] = jnp.dot(u[:, hf * S5_HALF_W:(hf + 1) * S5_HALF_W], bmat_ref[hf],
                              preferred_element_type=F32)
        for j in range(hs // S5_COL_CHUNK):
            re0 = j * S5_COL_CHUNK
            im0 = hs + j * S5_COL_CHUNK
            ar = jnp.broadcast_to(a_ref[hf, 0:1, re0:re0 + S5_COL_CHUNK], (bsz, S5_COL_CHUNK))
            ai = jnp.broadcast_to(a_ref[hf, 1:2, re0:re0 + S5_COL_CHUNK], (bsz, S5_COL_CHUNK))

            def step(t, carry, re0=re0, im0=im0, ar=ar, ai=ai):
                hr, hi = carry
                te = jnp.where(d == 0, t, steps - 1 - t)
                r0 = pl.multiple_of(te * bsz, bsz)
                br = bu_ref[pl.ds(r0, bsz), re0:re0 + S5_COL_CHUNK]
                bi = bu_ref[pl.ds(r0, bsz), im0:im0 + S5_COL_CHUNK]
                nr = ar * hr - ai * hi + br
                ni = ar * hi + ai * hr + bi
                bu_ref[pl.ds(r0, bsz), re0:re0 + S5_COL_CHUNK] = nr
                bu_ref[pl.ds(r0, bsz), im0:im0 + S5_COL_CHUNK] = ni
                return nr, ni

            hr, hi = lax.fori_loop(
                0, steps, step,
                (h_ref[hf, :, re0:re0 + S5_COL_CHUNK], h_ref[hf, :, im0:im0 + S5_COL_CHUNK]), unroll=4)
            h_ref[hf, :, re0:re0 + S5_COL_CHUNK] = hr
            h_ref[hf, :, im0:im0 + S5_COL_CHUNK] = hi
        y_ref[:, hf * S5_HALF_W:(hf + 1) * S5_HALF_W] = jnp.dot(
            bu_ref[...].astype(BF16), cmat_ref[hf], preferred_element_type=F32)

    @pl.when(c == pl.num_programs(1) - 1)
    def _():
        fin_ref[...] = h_ref[...]


def _s5_scan(u_tm, bmat, cmat, acoef, h0, bsz):
    rows = u_tm.shape[0]
    steps = S5_ROWS // bsz
    n_chunks = rows // S5_ROWS

    def chunk_map(d, c):
        return jnp.where(d == 0, c, n_chunks - 1 - c)

    return pl.pallas_call(
        functools.partial(_s5_scan_kernel, bsz=bsz, steps=steps),
        grid=(2, n_chunks),
        in_specs=[
            pl.BlockSpec((S5_ROWS, S5_WIDTH), lambda d, c: (chunk_map(d, c), 0)),
            pl.BlockSpec((None, 2, S5_HALF_W, 2 * S5_HALF_STATES), lambda d, c: (d, 0, 0, 0)),
            pl.BlockSpec((None, 2, 2 * S5_HALF_STATES, S5_HALF_W), lambda d, c: (d, 0, 0, 0)),
            pl.BlockSpec((None, 2, 2, S5_HALF_STATES), lambda d, c: (d, 0, 0, 0)),
            pl.BlockSpec((None, 2, bsz, 2 * S5_HALF_STATES), lambda d, c: (d, 0, 0, 0)),
        ],
        out_specs=[
            pl.BlockSpec((None, S5_ROWS, S5_WIDTH), lambda d, c: (d, chunk_map(d, c), 0)),
            pl.BlockSpec((None, 2, bsz, 2 * S5_HALF_STATES), lambda d, c: (d, 0, 0, 0)),
        ],
        out_shape=[
            jax.ShapeDtypeStruct((2, rows, S5_WIDTH), F32),
            jax.ShapeDtypeStruct((2, 2, bsz, 2 * S5_HALF_STATES), F32),
        ],
        scratch_shapes=[
            pltpu.VMEM((S5_ROWS, 2 * S5_HALF_STATES), F32),
            pltpu.VMEM((2, bsz, 2 * S5_HALF_STATES), F32),
        ],
        compiler_params=pltpu.CompilerParams(
            dimension_semantics=("arbitrary", "arbitrary"), vmem_limit_bytes=VMEM_LIMIT_BYTES),
        name="s5_scan",
    )(u_tm, bmat, cmat, acoef, h0)


def _s5_discretize(lam_re, lam_im, log_dt, b_re, b_im, c_re, c_im):
    eye = jnp.eye(S5_GROUPS // 2, dtype=F32)
    bmats, cmats, acoefs = [], [], []
    for dr in range(2):
        lr = jnp.minimum(lam_re[dr].astype(F32), -1e-4)
        li = lam_im[dr].astype(F32)
        dt = jnp.exp(log_dt[dr].astype(F32))[:, None]
        mag = jnp.exp(lr * dt)
        ar, ai = mag * jnp.cos(li * dt), mag * jnp.sin(li * dt)
        den = lr * lr + li * li
        fr = ((ar - 1.0) * lr + ai * li) / den
        fi = (ai * lr - (ar - 1.0) * li) / den
        br_ = b_re[dr].astype(F32)
        bi_ = b_im[dr].astype(F32)
        bbr = fr[..., None] * br_ - fi[..., None] * bi_
        bbi = fr[..., None] * bi_ + fi[..., None] * br_
        bm, cm, am = [], [], []
        for hf in range(2):
            g = slice(hf * S5_GROUPS // 2, (hf + 1) * S5_GROUPS // 2)

            def bdiag_in(w):
                return jnp.einsum('ab,aph->ahbp', eye, w[g]).reshape(S5_HALF_W, S5_HALF_STATES)

            def bdiag_out(w):
                return jnp.einsum('ab,ahp->apbh', eye, w[g]).reshape(S5_HALF_STATES, S5_HALF_W)

            bm.append(jnp.concatenate([bdiag_in(bbr), bdiag_in(bbi)], axis=1))
            cm.append(jnp.concatenate([bdiag_out(c_re[dr].astype(F32)),
                                       -bdiag_out(c_im[dr].astype(F32))], axis=0))
            am.append(jnp.stack([ar[g].reshape(-1), ai[g].reshape(-1)]))
        bmats.append(jnp.stack(bm))
        cmats.append(jnp.stack(cm))
        acoefs.append(jnp.stack(am))
    return jnp.stack(bmats).astype(BF16), jnp.stack(cmats).astype(BF16), jnp.stack(acoefs)


def _s5_state_to_kernel(h0):
    bsz = h0.shape[0]
    h = h0.astype(F32).reshape(bsz, 2, 2, 2, S5_HALF_STATES)
    return h.transpose(1, 3, 0, 2, 4).reshape(2, 2, bsz, 2 * S5_HALF_STATES)


def _s5_state_from_kernel(fin):
    bsz = fin.shape[2]
    h = fin.reshape(2, 2, bsz, 2, S5_HALF_STATES).transpose(2, 0, 3, 1, 4)
    return h.reshape(bsz, 2, 2, S5_GROUPS, S5_STATE)


TQ = 256
ATT_TQ = 1024
HEAD_LANES = 128
MLA_SCALE = (MLA_NOPE + MLA_ROPE) ** -0.5
DIFF_SCALE = DIFF_HD ** -0.5


def _dot(a, b):
    return jnp.dot(a, b, preferred_element_type=F32)


def _dot_t(a, b):
    return lax.dot_general(a, b, (((1,), (1,)), ((), ())), preferred_element_type=F32)


def _rms_rows(x, g):
    return x * lax.rsqrt(jnp.mean(x * x, axis=-1, keepdims=True) + EPS) * g


def _group(bsz, length, row0, mod_base, mod_stride):
    return dict(bsz=bsz, length=length, nt=length // TQ, tile0=row0 // TQ,
                mod_base=mod_base, mod_stride=mod_stride)


def _params(n_axes):
    return pltpu.CompilerParams(dimension_semantics=("arbitrary",) * n_axes,
                                vmem_limit_bytes=VMEM_LIMIT_BYTES)


def _axial_rope(length, dim):
    rows = length // GRID_W
    row = jnp.repeat(jnp.arange(rows, dtype=F32), GRID_W)
    col = jnp.tile(jnp.arange(GRID_W, dtype=F32), rows)
    n_freq = dim // 4
    inv = ROPE_THETA ** (-jnp.arange(n_freq, dtype=F32) / n_freq)
    ang = jnp.concatenate([row[:, None] * inv, col[:, None] * inv], axis=-1)
    return jnp.cos(ang), jnp.sin(ang)


def _rope_tables(length, dim, lead, reps):
    cos, sin = _axial_rope(length, dim)
    cos_r = jnp.repeat(cos, 2, axis=-1)
    sin_r = jnp.repeat(sin, 2, axis=-1) * jnp.tile(jnp.array([-1.0, 1.0], F32), dim // 2)
    part = HEAD_LANES // reps
    pad = ((0, 0), (lead, part - lead - dim))
    cos_t = jnp.tile(jnp.pad(cos_r, pad, constant_values=1.0), (1, reps))
    sin_t = jnp.tile(jnp.pad(sin_r, pad), (1, reps))
    return cos_t, sin_t


def _swap_pairs(w):
    return w[:, jnp.arange(w.shape[1]) ^ 1]


def _even_in_kernel(*refs, rope):
    if rope:
        (x_ref, g_ref, mod_ref, win_ref, gq_ref, wuq_ref, gkv_ref, cos_ref, sin_ref,
         u_ref, q_ref, ckv_ref, kr_ref) = refs
    else:
        (x_ref, g_ref, mod_ref, win_ref, gq_ref, wuq_ref, gkv_ref,
         u_ref, q_ref, ckv_ref, kr_ref) = refs
    o1 = S5_WIDTH
    o2 = o1 + MLA_Q_LORA
    o3 = o2 + MLA_KV_LORA
    o4 = o3 + HEAD_LANES
    n_in = o4 + HEAD_LANES if rope else o4
    n_q = MLA_HEADS * HEAD_LANES
    h = _rms_rows(x_ref[...], g_ref[...]) * (1.0 + mod_ref[0, 1:2, :]) + mod_ref[0, 0:1, :]
    z = _dot(h.astype(BF16), win_ref[:, :n_in])
    u_ref[...] = z[:, :o1]
    ckv_ref[...] = _rms_rows(z[:, o2:o3], gkv_ref[...])
    qn = _rms_rows(z[:, o1:o2], gq_ref[...]).astype(BF16)
    if rope:
        q2 = _dot(qn, wuq_ref[...])
        cos = cos_ref[...]
        sin = sin_ref[...]
        for hd in range(MLA_HEADS):
            a = hd * HEAD_LANES
            q_ref[:, a:a + HEAD_LANES] = ((q2[:, a:a + HEAD_LANES] * cos
                                           + q2[:, n_q + a:n_q + a + HEAD_LANES] * sin) * MLA_SCALE
                                          ).astype(q_ref.dtype)
        kr_ref[...] = z[:, o3:o4] * cos + z[:, o4:o4 + HEAD_LANES] * sin
    else:
        q_ref[...] = (_dot(qn, wuq_ref[:, :n_q]) * MLA_SCALE).astype(q_ref.dtype)
        kr_ref[...] = z[:, o3:o4]


def _even_in(x_all, grp, g1, mod_tab, win_aug, g_q, wuq2, g_kv, tables):
    bsz, length, nt = grp["bsz"], grp["length"], grp["nt"]
    rope = tables is not None
    rows = bsz * length

    def tok(b, t):
        return (b * nt + t, 0)

    in_specs = [
        pl.BlockSpec((TQ, D_MODEL), lambda b, t: (grp["tile0"] + b * nt + t, 0)),
        pl.BlockSpec((1, D_MODEL), lambda b, t: (0, 0)),
        pl.BlockSpec((1, 6, D_MODEL), lambda b, t: (grp["mod_base"] + b * grp["mod_stride"], 0, 0)),
        pl.BlockSpec(win_aug.shape, lambda b, t: (0, 0)),
        pl.BlockSpec((1, MLA_Q_LORA), lambda b, t: (0, 0)),
        pl.BlockSpec(wuq2.shape, lambda b, t: (0, 0)),
        pl.BlockSpec((1, MLA_KV_LORA), lambda b, t: (0, 0)),
    ]
    args = [x_all, g1.reshape(1, D_MODEL), mod_tab, win_aug, g_q.reshape(1, -1), wuq2, g_kv.reshape(1, -1)]
    if rope:
        in_specs += [pl.BlockSpec((TQ, HEAD_LANES), lambda b, t: (t, 0))] * 2
        args += list(tables)
    return pl.pallas_call(
        functools.partial(_even_in_kernel, rope=rope),
        grid=(bsz, nt),
        in_specs=in_specs,
        out_specs=[
            pl.BlockSpec((TQ, S5_WIDTH), lambda b, t: (t, b)),
            pl.BlockSpec((TQ, MLA_HEADS * HEAD_LANES), tok),
            pl.BlockSpec((TQ, MLA_KV_LORA), tok),
            pl.BlockSpec((TQ, HEAD_LANES), tok),
        ],
        out_shape=[
            jax.ShapeDtypeStruct((length, bsz * S5_WIDTH), F32),
            jax.ShapeDtypeStruct((rows, MLA_HEADS * HEAD_LANES), BF16),
            jax.ShapeDtypeStruct((rows, MLA_KV_LORA), F32),
            jax.ShapeDtypeStruct((rows, HEAD_LANES), F32),
        ],
        compiler_params=_params(2),
        name="even_in",
    )(*args)


def _kv_expand_kernel(x_ref, kr_ref, wk_ref, wv_ref, k_ref, v_ref):
    x = x_ref[...].astype(BF16)
    k = _dot(x, wk_ref[...])
    kr = kr_ref[...]
    for hd in range(MLA_HEADS):
        a = hd * HEAD_LANES
        k_ref[:, a:a + HEAD_LANES] = (k[:, a:a + HEAD_LANES] + kr).astype(k_ref.dtype)
    v_ref[...] = _dot(x, wv_ref[...]).astype(v_ref.dtype)


def _kv_expand(ckv, kr, wk, wv):
    rows = ckv.shape[0]
    tm = 512
    width = MLA_HEADS * HEAD_LANES
    return pl.pallas_call(
        _kv_expand_kernel,
        grid=(rows // tm,),
        in_specs=[pl.BlockSpec((tm, MLA_KV_LORA), lambda i: (i, 0)),
                  pl.BlockSpec((tm, HEAD_LANES), lambda i: (i, 0)),
                  pl.BlockSpec(wk.shape, lambda i: (0, 0)),
                  pl.BlockSpec(wv.shape, lambda i: (0, 0))],
        out_specs=[pl.BlockSpec((tm, width), lambda i: (i, 0))] * 2,
        out_shape=[jax.ShapeDtypeStruct((rows, width), BF16)] * 2,
        compiler_params=_params(1),
        name="kv_expand",
    )(ckv, kr, wk, wv)


def _exp_parts(scores):
    m = functools.reduce(jnp.maximum, [jnp.max(s, axis=-1, keepdims=True) for s in scores])
    es = [jnp.exp(s - m) for s in scores]
    den = functools.reduce(jnp.add, [jnp.sum(e, axis=-1, keepdims=True) for e in es])
    return es, 1.0 / den


def _weighted_values(es, vs):
    o = _dot(es[0].astype(BF16), vs[0])
    for e, v in zip(es[1:], vs[1:]):
        o = o + _dot(e.astype(BF16), v)
    return o


def _mla_attn_kernel(*refs, n_seg):
    q = refs[0][...]
    o_ref = refs[-1]
    ks = [refs[1 + 2 * s][...] for s in range(n_seg)]
    vs = [refs[2 + 2 * s][...] for s in range(n_seg)]
    es, inv = _exp_parts([_dot_t(q, k) for k in ks])
    o_ref[...] = (_weighted_values(es, vs) * inv).astype(o_ref.dtype)


def _mla_attn(q, segs, bsz, length):
    tq = TQ
    nq = length // tq
    in_specs = [pl.BlockSpec((tq, HEAD_LANES), lambda b, h, i: (b * nq + i, h))]
    args = [q]
    for k, v, lk in segs:
        in_specs += [pl.BlockSpec((lk, HEAD_LANES), lambda b, h, i: (b, h))] * 2
        args += [k, v]
    return pl.pallas_call(
        functools.partial(_mla_attn_kernel, n_seg=len(segs)),
        grid=(bsz, MLA_HEADS, nq),
        in_specs=in_specs,
        out_specs=pl.BlockSpec((tq, HEAD_LANES), lambda b, h, i: (b * nq + i, h)),
        out_shape=jax.ShapeDtypeStruct((bsz * length, MLA_HEADS * HEAD_LANES), BF16),
        compiler_params=_params(3),
        name="mla_attn",
    )(*args)


def _even_out_kernel(u_ref, y_ref, o_ref, x_ref, mod_ref, d_ref, wglu_ref, bglu_ref, ws5_ref, wmla_ref,
                     out_ref):
    y =jax.nn.gelu(d_ref[...] * u_ref[...] + y_ref[0] + y_ref[1])
    s5 = y * jax.nn.sigmoid(_dot(y.astype(BF16), wglu_ref[...]) + bglu_ref[...])
    mix = _dot(s5.astype(BF16), ws5_ref[...]) + _dot(o_ref[...].astype(BF16), wmla_ref[...])
    out_ref[...] = x_ref[...] + mod_ref[0, 2:3, :] * mix


def _even_out(x_all, grp, mod_tab, u_tm, y_dir, o_mla, d_skip, w_glu, b_glu, w_out_s5, w_out_mla):
    bsz, length, nt = grp["bsz"], grp["length"], grp["nt"]

    def xrow(b, t):
        return (grp["tile0"] + b * nt + t, 0)

    full = lambda b, t: (0, 0)
    return pl.pallas_call(
        _even_out_kernel,
        grid=(bsz, nt),
        in_specs=[
            pl.BlockSpec((TQ, S5_WIDTH), lambda b, t: (t, b)),
            pl.BlockSpec((2, TQ, S5_WIDTH), lambda b, t: (0, t, b)),
            pl.BlockSpec((TQ, MLA_HEADS * HEAD_LANES), lambda b, t: (b * nt + t, 0)),
            pl.BlockSpec((TQ, D_MODEL), xrow),
            pl.BlockSpec((1, 6, D_MODEL), lambda b, t: (grp["mod_base"] + b * grp["mod_stride"], 0, 0)),
            pl.BlockSpec((1, S5_WIDTH), full),
            pl.BlockSpec(w_glu.shape, full),
            pl.BlockSpec((1, S5_WIDTH), full),
            pl.BlockSpec(w_out_s5.shape, full),
            pl.BlockSpec(w_out_mla.shape, full),
        ],
        out_specs=pl.BlockSpec((TQ, D_MODEL), xrow),
        out_shape=jax.ShapeDtypeStruct(x_all.shape, F32),
        input_output_aliases={3: 0},
        compiler_params=_params(2),
        name="even_out",
    )(u_tm, y_dir.reshape(2, length, bsz * S5_WIDTH), o_mla, x_all, mod_tab,
      d_skip.reshape(1, S5_WIDTH), w_glu, b_glu.reshape(1, S5_WIDTH), w_out_s5, w_out_mla)


def _odd_in_kernel(*refs, rope):
    if rope:
        x_ref, g_ref, mod_ref, w_ref, cos_ref, sin_ref, q_ref, k_ref, v_ref = refs
    else:
        x_ref, g_ref, mod_ref, w_ref, q_ref, k_ref, v_ref = refs
    w3 = 3 * DIFF_WIDTH
    h = _rms_rows(x_ref[...], g_ref[...]) * (1.0 + mod_ref[0, 1:2, :]) + mod_ref[0, 0:1, :]
    z = _dot(h.astype(BF16), w_ref[...] if rope else w_ref[:, :w3])
    v_ref[...] = z[:, 2 * DIFF_WIDTH:w3].astype(v_ref.dtype)
    if rope:
        cos = cos_ref[...]
        sin = sin_ref[...]
        for hd in range(DIFF_HEADS):
            a = hd * HEAD_LANES
            q_ref[:, a:a + HEAD_LANES] = ((z[:, a:a + HEAD_LANES] * cos
                                           + z[:, w3 + a:w3 + a + HEAD_LANES] * sin) * DIFF_SCALE
                                          ).astype(q_ref.dtype)
            b = DIFF_WIDTH + a
            k_ref[:, a:a + HEAD_LANES] = (z[:, b:b + HEAD_LANES] * cos
                                          + z[:, w3 + b:w3 + b + HEAD_LANES] * sin).astype(k_ref.dtype)
    else:
        q_ref[...] = (z[:, :DIFF_WIDTH] * DIFF_SCALE).astype(q_ref.dtype)
        k_ref[...] = z[:, DIFF_WIDTH:2 * DIFF_WIDTH].astype(k_ref.dtype)


def _odd_in(x_all, grp, g1, mod_tab, w_aug, tables, kv_dtype):
    bsz, length, nt = grp["bsz"], grp["length"], grp["nt"]
    rope = tables is not None
    rows = bsz * length

    def tok(b, t):
        return (b * nt + t, 0)

    in_specs = [
        pl.BlockSpec((TQ, D_MODEL), lambda b, t: (grp["tile0"] + b * nt + t, 0)),
        pl.BlockSpec((1, D_MODEL), lambda b, t: (0, 0)),
        pl.BlockSpec((1, 6, D_MODEL), lambda b, t: (grp["mod_base"] + b * grp["mod_stride"], 0, 0)),
        pl.BlockSpec(w_aug.shape, lambda b, t: (0, 0)),
    ]
    args = [x_all, g1.reshape(1, D_MODEL), mod_tab, w_aug]
    if rope:
        in_specs += [pl.BlockSpec((TQ, HEAD_LANES), lambda b, t: (t, 0))] * 2
        args += list(tables)
    return pl.pallas_call(
        functools.partial(_odd_in_kernel, rope=rope),
        grid=(bsz, nt),
        in_specs=in_specs,
        out_specs=[pl.BlockSpec((TQ, DIFF_WIDTH), tok)] * 3,
        out_shape=[
            jax.ShapeDtypeStruct((rows, DIFF_WIDTH), BF16),
            jax.ShapeDtypeStruct((rows, DIFF_WIDTH), kv_dtype),
            jax.ShapeDtypeStruct((rows, DIFF_WIDTH), kv_dtype),
        ],
        compiler_params=_params(2),
        name="odd_in",
    )(*args)


def _diff_attn_kernel(*refs, n_seg, post_scale):
    lam_ref, q_ref = refs[0], refs[1]
    g_ref, o_ref = refs[-2], refs[-1]
    q = q_ref[...].astype(F32)
    lane = lax.broadcasted_iota(jnp.int32, q.shape, 1)
    q0 = jnp.where(lane < DIFF_HD, q, 0.0).astype(BF16)
    q1 = jnp.where(lane >= DIFF_HD, q, 0.0).astype(BF16)
    ks = [refs[2 + 2 * s][...].astype(BF16) for s in range(n_seg)]
    vs = [refs[3 + 2 * s][...].astype(BF16) for s in range(n_seg)]
    e0, inv0 = _exp_parts([_dot_t(q0, k) for k in ks])
    e1, inv1 = _exp_parts([_dot_t(q1, k) for k in ks])
    o = _weighted_values(e0, vs) * inv0 - lam_ref[0] * (_weighted_values(e1, vs) * inv1)
    o_ref[...] = (_rms_rows(o, g_ref[...]) * post_scale).astype(o_ref.dtype)


def _diff_attn(lam_full, q, segs, g_sub, post_scale, bsz, length):
    tq = min(ATT_TQ, length)
    nq = length // tq
    in_specs = [pl.BlockSpec(memory_space=pltpu.SMEM),
                pl.BlockSpec((tq, HEAD_LANES), lambda b, h, i: (b * nq + i, h))]
    args = [lam_full.reshape(1).astype(F32), q]
    for k, v, lk in segs:
        in_specs += [pl.BlockSpec((lk, HEAD_LANES), lambda b, h, i: (b, h))] * 2
        args += [k, v]
    in_specs.append(pl.BlockSpec((1, HEAD_LANES), lambda b, h, i: (0, 0)))
    args.append(g_sub.reshape(1, HEAD_LANES))
    return pl.pallas_call(
        functools.partial(_diff_attn_kernel, n_seg=len(segs), post_scale=post_scale),
        grid=(bsz, DIFF_HEADS, nq),
        in_specs=in_specs,
        out_specs=pl.BlockSpec((tq, HEAD_LANES), lambda b, h, i: (b * nq + i, h)),
        out_shape=jax.ShapeDtypeStruct((bsz * length, DIFF_WIDTH), BF16),
        compiler_params=_params(3),
        name="diff_attn",
    )(*args)


def _odd_out_kernel(o_ref, x_ref, mod_ref, w_ref, out_ref):
    out_ref[...] = x_ref[...] + mod_ref[0, 2:3, :] * _dot(o_ref[...], w_ref[...])


def _odd_out(x_all, grp, mod_tab, o, w_out):
    bsz, nt = grp["bsz"], grp["nt"]

    def xrow(b, t):
        return (grp["tile0"] + b * nt + t, 0)

    return pl.pallas_call(
        _odd_out_kernel,
        grid=(bsz, nt),
        in_specs=[
            pl.BlockSpec((TQ, DIFF_WIDTH), lambda b, t: (b * nt + t, 0)),
            pl.BlockSpec((TQ, D_MODEL), xrow),
            pl.BlockSpec((1, 6, D_MODEL), lambda b, t: (grp["mod_base"] + b * grp["mod_stride"], 0, 0)),
            pl.BlockSpec(w_out.shape, lambda b, t: (0, 0)),
        ],
        out_specs=pl.BlockSpec((TQ, D_MODEL), xrow),
        out_shape=jax.ShapeDtypeStruct(x_all.shape, F32),
        input_output_aliases={1: 0},
        compiler_params=_params(2),
        name="odd_out",
    )(o, x_all, mod_tab, w_out)


def _final_norm_kernel(x_ref, g_ref, o_ref):
    o_ref[...] = _rms_rows(x_ref[...], g_ref[...])


def _final_norm(x_all, g, row0, rows):
    tm = 512
    return pl.pallas_call(
        _final_norm_kernel,
        grid=(rows // tm,),
        in_specs=[pl.BlockSpec((tm, D_MODEL), lambda i: (row0 // tm + i, 0)),
                  pl.BlockSpec((1, D_MODEL), lambda i: (0, 0))],
        out_specs=pl.BlockSpec((tm, D_MODEL), lambda i: (i, 0)),
        out_shape=jax.ShapeDtypeStruct((rows, D_MODEL), F32),
        compiler_params=_params(1),
        name="final_norm",
    )(x_all, g.reshape(1, D_MODEL))


MOD_ROWS = 16
MOD_COL_TILE = 1536


def _adaln_kernel(c_ref, w_ref, b_ref, o_ref):
    cond = jax.nn.silu(c_ref[...])
    o_ref[...] = jnp.dot(cond, w_ref[...], preferred_element_type=F32,
                         precision=lax.Precision.HIGHEST) + b_ref[...]


def _adaln(cond_rows, w_mod, b_mod):
    return pl.pallas_call(
        _adaln_kernel,
        grid=(DEPTH, 6 * D_MODEL // MOD_COL_TILE),
        in_specs=[pl.BlockSpec((MOD_ROWS, D_MODEL), lambda l, j: (0, 0)),
                  pl.BlockSpec((None, D_MODEL, MOD_COL_TILE), lambda l, j: (l, 0, j)),
                  pl.BlockSpec((None, 1, MOD_COL_TILE), lambda l, j: (l, 0, j))],
        out_specs=pl.BlockSpec((None, MOD_ROWS, MOD_COL_TILE), lambda l, j: (l, 0, j)),
        out_shape=jax.ShapeDtypeStruct((DEPTH, MOD_ROWS, 6 * D_MODEL), F32),
        compiler_params=_params(2),
        name="adaln",
    )(cond_rows, w_mod, b_mod.reshape(DEPTH, 1, 6 * D_MODEL))


def _pad_head_lanes(x, lead):
    return jnp.pad(x, ((0, 0), (lead, HEAD_LANES - lead - x.shape[1])))


def _even_weights(w_in, w_out, w_uq, w_ukv, w_glu):
    o3 = S5_WIDTH + MLA_Q_LORA + MLA_KV_LORA
    w_kr = w_in[:, o3:]
    win_aug = jnp.concatenate(
        [w_in[:, :o3], _pad_head_lanes(w_kr, MLA_NOPE), _pad_head_lanes(_swap_pairs(w_kr), MLA_NOPE)], axis=1)
    dq = MLA_NOPE + MLA_ROPE
    plain, swapped = [], []
    for hd in range(MLA_HEADS):
        wn = w_uq[:, hd * dq:hd * dq + MLA_NOPE]
        wr = w_uq[:, hd * dq + MLA_NOPE:(hd + 1) * dq]
        plain.append(jnp.pad(jnp.concatenate([wn, wr], axis=1), ((0, 0), (0, HEAD_LANES - dq))))
        swapped.append(_pad_head_lanes(_swap_pairs(wr), MLA_NOPE))
    wuq2 = jnp.concatenate(plain + swapped, axis=1)
    w_mla = w_out[S5_WIDTH:].reshape(MLA_HEADS, MLA_V, D_MODEL)
    w_out_mla = jnp.pad(w_mla, ((0, 0), (0, HEAD_LANES - MLA_V), (0, 0))).reshape(MLA_HEADS * HEAD_LANES, D_MODEL)
    w_kv = w_ukv.reshape(MLA_KV_LORA, MLA_HEADS, MLA_NOPE + MLA_V)
    wk = jnp.pad(w_kv[:, :, :MLA_NOPE], ((0, 0), (0, 0), (0, HEAD_LANES - MLA_NOPE)))
    wv = jnp.pad(w_kv[:, :, MLA_NOPE:], ((0, 0), (0, 0), (0, HEAD_LANES - MLA_V)))
    w_kv = (wk.reshape(MLA_KV_LORA, -1).astype(BF16), wv.reshape(MLA_KV_LORA, -1).astype(BF16))
    return (win_aug.astype(BF16), wuq2.astype(BF16), w_kv, w_glu.astype(BF16),
            w_out[:S5_WIDTH].astype(BF16), w_out_mla.astype(BF16))


def _even_layer(x_all, grp, g1, mod_tab, ew, s5m, g_q, g_kv, d_skip, b_glu, h0, ctx, tables):
    win_aug, wuq2, w_kv, w_glu, w_out_s5, w_out_mla = ew
    bmat, cmat, acoef = s5m
    bsz, length = grp["bsz"], grp["length"]
    u_tm, q, ckv, kr = _even_in(x_all, grp, g1, mod_tab, win_aug, g_q, wuq2, g_kv, tables)
    y_dir, fin = _s5_scan(u_tm.reshape(length * bsz, S5_WIDTH), bmat, cmat, acoef, h0, bsz)
    segs = [(*_kv_expand(ckv, kr, *w_kv), length)]
    if ctx is not None:
        segs.append((*_kv_expand(*ctx, *w_kv), PAST_LEN))
    o_mla = _mla_attn(q, segs, bsz, length)
    x_all = _even_out(x_all, grp, mod_tab, u_tm, y_dir, o_mla, d_skip, w_glu, b_glu, w_out_s5, w_out_mla)
    return x_all, fin, ckv, kr


def _odd_layer(x_all, grp, g1, mod_tab, w_aug, w_out, lam_full, g_sub, post_scale, ctx, tables, kv_dtype):
    bsz, length = grp["bsz"], grp["length"]
    q, k, v = _odd_in(x_all, grp, g1, mod_tab, w_aug, tables, kv_dtype)
    segs = [(k, v, length)]
    if ctx is not None:
        segs.append((ctx[0], ctx[1], PAST_LEN))
    o = _diff_attn(lam_full, q, segs, g_sub, post_scale, bsz, length)
    return _odd_out(x_all, grp, mod_tab, o, w_out), k, v


def kernel(x_prompt, x_sample, state_s5, cache_mla, cache_diff_k, cache_diff_v, c, c_ctx, w_mod, b_mod, g_norm1, g_norm2, g_final, w_in_even, w_out_even, s5_lam_re, s5_lam_im, s5_log_dt, s5_b_re, s5_b_im, s5_c_re, s5_c_im, s5_d, s5_w_glu, s5_b_glu, mla_g_q, mla_w_uq, mla_g_kv, mla_w_ukv, w_in_odd, w_out_odd, diff_lam, diff_g_sub, w_router, b_router, w_gate_up, b_gate_up, w_down, b_down):
    tab_mla = _rope_tables(DEC_SEQ, MLA_ROPE, MLA_NOPE, 1)
    tab_diff = _rope_tables(DEC_SEQ, DIFF_HD, 0, 2)
    grp_p = _group(BATCH, SEQ, 0, 0, 0)
    grp_s = _group(DEC_BATCH, DEC_SEQ, N_PROMPT, 1, 1)
    x_all = jnp.concatenate([x_prompt.reshape(N_PROMPT, D_MODEL), x_sample.reshape(N_SAMPLE, D_MODEL)], axis=0)
    cond = jnp.concatenate([c_ctx[None], c, jnp.zeros((MOD_ROWS - 1 - DEC_BATCH, D_MODEL), c.dtype)], axis=0)
    mods = _adaln(cond.astype(F32), w_mod, b_mod)
    new_s5, new_mla, new_k, new_v = [], [], [], []
    for l in range(DEPTH):
        mod_tab = mods[l].reshape(MOD_ROWS, 6, D_MODEL)
        i = l // 2
        if l % 2 == 0:
            ew = _even_weights(w_in_even[i], w_out_even[i], mla_w_uq[i], mla_w_ukv[i], s5_w_glu[i])
            s5m = _s5_discretize(s5_lam_re[i], s5_lam_im[i], s5_log_dt[i], s5_b_re[i], s5_b_im[i],
                                 s5_c_re[i], s5_c_im[i])
            common = (ew, s5m, mla_g_q[i], mla_g_kv[i], s5_d[i], s5_b_glu[i])
            h0_p = jnp.zeros((2, 2, BATCH, 2 * S5_HALF_STATES), F32)
            x_all, fin, ckv, kr = _even_layer(x_all, grp_p, g_norm1[l], mod_tab, *common, h0_p, None, None)
            new_s5.append(_s5_state_from_kernel(fin))
            new_mla.append(jnp.concatenate([ckv, kr[:, MLA_NOPE:MLA_NOPE + MLA_ROPE]], axis=1)
                           .reshape(BATCH, SEQ, MLA_KV_LORA + MLA_ROPE))
            lat_ctx = cache_mla[:, i].astype(F32).reshape(DEC_BATCH * PAST_LEN, MLA_KV_LORA + MLA_ROPE)
            ctx = (lat_ctx[:, :MLA_KV_LORA], _pad_head_lanes(lat_ctx[:, MLA_KV_LORA:], MLA_NOPE))
            x_all, _, _, _ = _even_layer(x_all, grp_s, g_norm1[l], mod_tab, *common,
                                         _s5_state_to_kernel(state_s5[:, i]), ctx, tab_mla)
        else:
            lam_init = 0.8 - 0.6 * math.exp(-0.3 * l)
            lamf = diff_lam[i].astype(F32)
            lam_full = jnp.exp(jnp.sum(lamf[0] * lamf[1])) - jnp.exp(jnp.sum(lamf[2] * lamf[3])) + lam_init
            w_qk = w_in_odd[i][:, :2 * DIFF_WIDTH]
            w_aug = jnp.concatenate([w_in_odd[i], _swap_pairs(w_qk)], axis=1).astype(BF16)
            w_out = w_out_odd[i].astype(BF16)
            odd = (w_aug, w_out, lam_full, diff_g_sub[i], 1.0 - lam_init)
            x_all, kp, vp = _odd_layer(x_all, grp_p, g_norm1[l], mod_tab, *odd, None, None, F32)
            new_k.append(kp.reshape(BATCH, SEQ, DIFF_HEADS, 2, DIFF_HD))
            new_v.append(vp.reshape(BATCH, SEQ, DIFF_HEADS, 2 * DIFF_HD))
            ctx = (cache_diff_k[:, i].reshape(DEC_BATCH * PAST_LEN, DIFF_WIDTH).astype(BF16),
                   cache_diff_v[:, i].reshape(DEC_BATCH * PAST_LEN, DIFF_WIDTH).astype(BF16))
            x_all, _, _ = _odd_layer(x_all, grp_s, g_norm1[l], mod_tab, *odd, ctx, tab_diff, BF16)
        x_all = _moe_layer(l, x_all, mod_tab, g_norm2, w_router, b_router,
                           w_gate_up, b_gate_up, w_down, b_down)
    y_prompt = _final_norm(x_all, g_final, 0, N_PROMPT)
    y_sample = _final_norm(x_all, g_final, N_PROMPT, N_SAMPLE)
    return (y_prompt.reshape(BATCH, SEQ, D_MODEL), y_sample.reshape(DEC_BATCH, DEC_SEQ, D_MODEL),
            jnp.stack(new_s5, axis=1), jnp.stack(new_mla, axis=1),
            jnp.stack(new_k, axis=1), jnp.stack(new_v, axis=1))
```

```python
import functools
import math

import jax
import jax.numpy as jnp
from jax import lax
from jax.experimental import pallas as pl
from jax.experimental.pallas import tpu as pltpu

D_MODEL = 1024
BATCH = 16
SEQ = 256
DEPTH = 4
DEC_BATCH = 8
DEC_SEQ = 1024
PAST_LEN = 512
GRID_W = 64
N_EVEN = (DEPTH + 1) // 2
N_ODD = DEPTH // 2
S5_WIDTH = D_MODEL // 2
S5_GROUP = 16
S5_GROUPS = S5_WIDTH // S5_GROUP
S5_STATE = 64
MLA_HEADS = 8
MLA_NOPE = 64
MLA_ROPE = 32
MLA_V = 64
MLA_Q_LORA = D_MODEL // 4
MLA_KV_LORA = D_MODEL // 8
MLA_WIDTH = MLA_HEADS * MLA_V
EVEN_IN = S5_WIDTH + MLA_Q_LORA + MLA_KV_LORA + MLA_ROPE
EVEN_OUT = S5_WIDTH + MLA_WIDTH
DIFF_HEADS = 8
DIFF_HD = D_MODEL // (2 * DIFF_HEADS)
DIFF_WIDTH = DIFF_HEADS * 2 * DIFF_HD
N_EXPERTS = 32
TOP_K = 4
D_FF = D_MODEL
SWIGLU_LIMIT = 7.0
SWIGLU_ALPHA = 1.702
ROPE_THETA = 10000.0
Q_BLOCK = 128
EPS = 1e-6

N_PROMPT = BATCH * SEQ
N_SAMPLE = DEC_BATCH * DEC_SEQ
N_TOK = N_PROMPT + N_SAMPLE

LANES = 128
VMEM_LIMIT_BYTES = 56 * 1024 * 1024

TM = 256
N_TILES = N_TOK // TM
R_TILES = N_TOK * TOP_K // TM + N_EXPERTS
R_MAX = R_TILES * TM
TOPK_ROWS = 8

F32 = jnp.float32
BF16 = jnp.bfloat16


def _mod_row(i):
    t0 = i * TM
    return jnp.where(t0 < N_PROMPT, 0, 1 + (t0 - N_PROMPT) // DEC_SEQ)


def _router_kernel(x_ref, g_ref, mod_ref, wr_ref, br_ref,
                   h_ref, topi_ref, gate_ref, rank_ref, counts_ref, carry_ref):
    i = pl.program_id(0)

    @pl.when(i == 0)
    def _():
        carry_ref[...] = jnp.zeros_like(carry_ref)

    x = x_ref[...]
    ms = jnp.mean(x * x, axis=-1, keepdims=True)
    y = x * lax.rsqrt(ms + EPS) * g_ref[...]
    shift = mod_ref[0, 3:4, :]
    scale = mod_ref[0, 4:5, :]
    h = y * (1.0 + scale) + shift
    h_ref[...] = h

    hi = h.astype(BF16)
    lo = (h - hi.astype(F32)).astype(BF16)
    w_hi = wr_ref[0]
    logits = _dot_t(w_hi, hi) + (_dot_t(wr_ref[1], hi) + _dot_t(w_hi, lo)) + br_ref[...]
    sub_e = lax.broadcasted_iota(jnp.int32, logits.shape, 0)
    work = logits
    vals, hits = [], []
    sel = jnp.zeros(logits.shape, F32)
    for _ in range(TOP_K):
        m = jnp.max(work, axis=0, keepdims=True)
        idx = jnp.min(jnp.where(work == m, sub_e, N_EXPERTS), axis=0, keepdims=True)
        hit = sub_e == idx
        vals.append(m)
        hits.append((hit, idx))
        sel = jnp.where(hit, 1.0, sel)
        work = jnp.where(hit, -jnp.inf, work)
    es = [jnp.exp(v - vals[0]) for v in vals]
    inv = 1.0 / (es[0] + es[1] + es[2] + es[3])

    row = lax.broadcasted_iota(jnp.int32, (TM, TM), 0)
    col = lax.broadcasted_iota(jnp.int32, (TM, TM), 1)
    earlier = jnp.where(row < col, 1.0, 0.0).astype(BF16)
    before = _dot(sel.astype(BF16), earlier) + carry_ref[...]
    carry_ref[...] += jnp.sum(sel, axis=1, keepdims=True)
    counts_ref[...] = carry_ref[...].astype(jnp.int32)

    sub_k = lax.broadcasted_iota(jnp.int32, (TOPK_ROWS, TM), 0)
    topi = jnp.zeros((TOPK_ROWS, TM), jnp.int32)
    gate = jnp.zeros((TOPK_ROWS, TM), F32)
    rank = jnp.zeros((TOPK_ROWS, TM), jnp.int32)
    for k in range(TOP_K):
        hit, idx = hits[k]
        rk = jnp.sum(jnp.where(hit, before, 0.0), axis=0, keepdims=True)
        topi = jnp.where(sub_k == k, idx, topi)
        gate = jnp.where(sub_k == k, es[k] * inv, gate)
        rank = jnp.where(sub_k == k, rk.astype(jnp.int32), rank)
    topi_ref[...] = topi
    gate_ref[...] = gate
    rank_ref[...] = rank


def _router(x_all, g, mod_tab, w_router, b_router):
    w_t = w_router.astype(F32).T
    w_hi = w_t.astype(BF16)
    w_split = jnp.stack([w_hi, (w_t - w_hi.astype(F32)).astype(BF16)])
    return pl.pallas_call(
        _router_kernel,
        grid=(N_TILES,),
        in_specs=[
            pl.BlockSpec((TM, D_MODEL), lambda i: (i, 0)),
            pl.BlockSpec((1, D_MODEL), lambda i: (0, 0)),
            pl.BlockSpec((1, 6, D_MODEL), lambda i: (_mod_row(i), 0, 0)),
            pl.BlockSpec((2, N_EXPERTS, D_MODEL), lambda i: (0, 0, 0)),
            pl.BlockSpec((N_EXPERTS, 1), lambda i: (0, 0)),
        ],
        out_specs=[
            pl.BlockSpec((TM, D_MODEL), lambda i: (i, 0)),
            pl.BlockSpec((TOPK_ROWS, TM), lambda i: (0, i)),
            pl.BlockSpec((TOPK_ROWS, TM), lambda i: (0, i)),
            pl.BlockSpec((TOPK_ROWS, TM), lambda i: (0, i)),
            pl.BlockSpec((N_EXPERTS, 1), lambda i: (0, 0)),
        ],
        out_shape=[
            jax.ShapeDtypeStruct((N_TOK, D_MODEL), F32),
            jax.ShapeDtypeStruct((TOPK_ROWS, N_TOK), jnp.int32),
            jax.ShapeDtypeStruct((TOPK_ROWS, N_TOK), F32),
            jax.ShapeDtypeStruct((TOPK_ROWS, N_TOK), jnp.int32),
            jax.ShapeDtypeStruct((N_EXPERTS, 1), jnp.int32),
        ],
        scratch_shapes=[pltpu.VMEM((N_EXPERTS, 1), F32)],
        compiler_params=pltpu.CompilerParams(
            dimension_semantics=("arbitrary",), vmem_limit_bytes=VMEM_LIMIT_BYTES),
        name="moe_router",
    )(x_all, g.reshape(1, D_MODEL), mod_tab, w_split, b_router.reshape(N_EXPERTS, 1))


ISSUE_UNROLL = 4


def _dispatch_kernel(ends_ref, pos_ref, h_ref, xs_ref, zero_buf, pos_smem, sem_idx, sem, sem_zero):
    i = pl.program_id(0)

    @pl.when(i == 0)
    def _():
        zero_buf[...] = jnp.zeros_like(zero_buf)

        for wait in (False, True):
            for e in range(N_EXPERTS):
                start = ends_ref[e - 1] if e > 0 else 0

                @pl.when(ends_ref[e] > start)
                def _(e=e, wait=wait):
                    last = pl.multiple_of(ends_ref[e] - TM, TM)
                    cp = pltpu.make_async_copy(zero_buf, xs_ref.at[pl.ds(last, TM)], sem_zero)
                    if wait:
                        cp.wait()
                    else:
                        cp.start()

            def tail(t, carry, wait=wait):
                cp = pltpu.make_async_copy(zero_buf, xs_ref.at[pl.ds(pl.multiple_of(t * TM, TM), TM)], sem_zero)
                if wait:
                    cp.wait()
                else:
                    cp.start()
                return carry

            lax.fori_loop(ends_ref[N_EXPERTS - 1] // TM, R_TILES, tail, 0)

    cp = pltpu.make_async_copy(pos_ref, pos_smem, sem_idx)
    cp.start()
    cp.wait()

    def issue(r, carry):
        for k in range(TOP_K):
            p = pos_smem[k, r]
            pltpu.make_async_copy(h_ref.at[pl.ds(r, 1)], xs_ref.at[pl.ds(p, 1)], sem.at[k]).start(priority=k % 2)
        return carry

    lax.fori_loop(0, TM, issue, 0, unroll=ISSUE_UNROLL)
    for k in range(TOP_K):
        pltpu.make_async_copy(h_ref, xs_ref.at[pl.ds(0, TM)], sem.at[k]).wait()


def _dispatch(ends, pos, h):
    grid_spec = pltpu.PrefetchScalarGridSpec(
        num_scalar_prefetch=1,
        grid=(N_TILES,),
        in_specs=[
            pl.BlockSpec((TOPK_ROWS, TM), lambda i, ends: (0, i)),
            pl.BlockSpec((TM, D_MODEL), lambda i, ends: (i, 0)),
        ],
        out_specs=pl.BlockSpec(memory_space=pl.ANY),
        scratch_shapes=[
            pltpu.VMEM((TM, D_MODEL), F32),
            pltpu.SMEM((TOPK_ROWS, TM), jnp.int32),
            pltpu.SemaphoreType.DMA,
            pltpu.SemaphoreType.DMA((TOP_K,)),
            pltpu.SemaphoreType.DMA,
        ],
    )
    return pl.pallas_call(
        _dispatch_kernel,
        grid_spec=grid_spec,
        out_shape=jax.ShapeDtypeStruct((R_MAX, D_MODEL), F32),
        compiler_params=pltpu.CompilerParams(
            dimension_semantics=("arbitrary",), vmem_limit_bytes=VMEM_LIMIT_BYTES),
        name="moe_dispatch",
    )(ends, pos, h)


def _ffn_kernel(te_ref, nu_ref, nx_ref, xs_ref, wgu_hbm, bgu_ref, wd_hbm, bd_ref, ys_ref,
                wgu_f32, wd_f32, wgu_bf, wd_bf, sem, *, layer):
    i = pl.program_id(0)

    def weight_copies(e):
        return (pltpu.make_async_copy(wgu_hbm.at[layer, e], wgu_f32, sem.at[0]),
                pltpu.make_async_copy(wd_hbm.at[layer, e], wd_f32, sem.at[1]))

    @pl.when(i < nu_ref[0])
    def _():
        expert = te_ref[i]
        new_expert = jnp.logical_or(i == 0, expert != te_ref[jnp.maximum(i - 1, 0)])

        @pl.when(i == 0)
        def _():
            for cp in weight_copies(expert):
                cp.start()

        @pl.when(new_expert)
        def _():
            for cp in weight_copies(expert):
                cp.wait()
            wgu_bf[...] = wgu_f32[...].astype(BF16)
            wd_bf[...] = wd_f32[...].astype(BF16)
            nxt = nx_ref[expert]

            @pl.when(nxt >= 0)
            def _():
                for cp in weight_copies(nxt):
                    cp.start()

        x = xs_ref[...].astype(BF16)
        gu = jnp.dot(x, wgu_bf[...], preferred_element_type=F32) + bgu_ref[...]
        g = jnp.minimum(gu[:, :D_FF], SWIGLU_LIMIT)
        u = jnp.clip(gu[:, D_FF:], -SWIGLU_LIMIT, SWIGLU_LIMIT)
        act = g * jax.nn.sigmoid(SWIGLU_ALPHA * g) * (u + 1.0)
        ys_ref[...] = jnp.dot(act.astype(BF16), wd_bf[...], preferred_element_type=F32) + bd_ref[...]

    @pl.when(i >= nu_ref[0])
    def _():
        ys_ref[...] = jnp.zeros_like(ys_ref)


def _ffn(layer, tile_expert, n_used, next_expert, xs, w_gate_up, b_gate_up, w_down, b_down):
    def row_map(i, te, nu, nx):
        return (jnp.maximum(jnp.minimum(i, nu[0] - 1), 0), 0)

    def b_map(i, te, nu, nx):
        return (layer, te[i], 0, 0)

    grid_spec = pltpu.PrefetchScalarGridSpec(
        num_scalar_prefetch=3,
        grid=(R_TILES,),
        in_specs=[
            pl.BlockSpec((TM, D_MODEL), row_map),
            pl.BlockSpec(memory_space=pl.ANY),
            pl.BlockSpec((None, None, 1, 2 * D_FF), b_map),
            pl.BlockSpec(memory_space=pl.ANY),
            pl.BlockSpec((None, None, 1, D_MODEL), b_map),
        ],
        out_specs=pl.BlockSpec((TM, D_MODEL), lambda i, te, nu, nx: (i, 0)),
        scratch_shapes=[
            pltpu.VMEM((D_MODEL, 2 * D_FF), F32),
            pltpu.VMEM((D_FF, D_MODEL), F32),
            pltpu.VMEM((D_MODEL, 2 * D_FF), BF16),
            pltpu.VMEM((D_FF, D_MODEL), BF16),
            pltpu.SemaphoreType.DMA((2,)),
        ],
    )
    return pl.pallas_call(
        functools.partial(_ffn_kernel, layer=layer),
        grid_spec=grid_spec,
        out_shape=jax.ShapeDtypeStruct((R_MAX, D_MODEL), F32),
        compiler_params=pltpu.CompilerParams(
            dimension_semantics=("arbitrary",), vmem_limit_bytes=VMEM_LIMIT_BYTES),
        name="moe_ffn",
    )(tile_expert, n_used, next_expert, xs, w_gate_up,
      b_gate_up.reshape(DEPTH, N_EXPERTS, 1, 2 * D_FF), w_down,
      b_down.reshape(DEPTH, N_EXPERTS, 1, D_MODEL))


def _combine_kernel(pos_ref, pos_next_ref, ys_ref, x_ref, gate_ref, mod_ref, out_ref, buf, pos_smem, sem_idx, sem):
    i = pl.program_id(0)
    slot = i % 2

    def gather_tile(tile_pos_ref, s):
        cp = pltpu.make_async_copy(tile_pos_ref, pos_smem, sem_idx)
        cp.start()
        cp.wait()

        def issue(r, carry):
            for k in range(TOP_K):
                p = pos_smem[k, r]
                pltpu.make_async_copy(ys_ref.at[pl.ds(p, 1)], buf.at[s, k, pl.ds(r, 1)],
                                      sem.at[s, k]).start(priority=k % 2)
            return carry

        lax.fori_loop(0, TM, issue, 0, unroll=ISSUE_UNROLL)

    @pl.when(i == 0)
    def _():
        gather_tile(pos_ref, 0)

    @pl.when(i + 1 < N_TILES)
    def _():
        gather_tile(pos_next_ref, 1 - slot)

    acc = jnp.zeros((TM, D_MODEL), F32)
    for k in range(TOP_K):
        pltpu.make_async_copy(ys_ref.at[pl.ds(0, TM)], buf.at[slot, k], sem.at[slot, k]).wait()
        acc = acc + gate_ref[:, k:k + 1] * buf[slot, k]
    out_ref[...] = x_ref[...] + mod_ref[0, 5:6, :] * acc


def _combine(pos, ys, x_all, gate, mod_tab):
    return pl.pallas_call(
        _combine_kernel,
        grid=(N_TILES,),
        in_specs=[
            pl.BlockSpec((TOPK_ROWS, TM), lambda i: (0, i)),
            pl.BlockSpec((TOPK_ROWS, TM), lambda i: (0, jnp.minimum(i + 1, N_TILES - 1))),
            pl.BlockSpec(memory_space=pl.ANY),
            pl.BlockSpec((TM, D_MODEL), lambda i: (i, 0)),
            pl.BlockSpec((TM, TOPK_ROWS), lambda i: (i, 0)),
            pl.BlockSpec((1, 6, D_MODEL), lambda i: (_mod_row(i), 0, 0)),
        ],
        out_specs=pl.BlockSpec((TM, D_MODEL), lambda i: (i, 0)),
        out_shape=jax.ShapeDtypeStruct((N_TOK, D_MODEL), F32),
        scratch_shapes=[
            pltpu.VMEM((2, TOP_K, TM, D_MODEL), F32),
            pltpu.SMEM((TOPK_ROWS, TM), jnp.int32),
            pltpu.SemaphoreType.DMA,
            pltpu.SemaphoreType.DMA((2, TOP_K)),
        ],
        compiler_params=pltpu.CompilerParams(
            dimension_semantics=("arbitrary",), vmem_limit_bytes=VMEM_LIMIT_BYTES),
        name="moe_combine",
    )(pos, pos, ys, x_all, gate, mod_tab)


def _moe_layer(layer, x_all, mod_tab, g_norm2, w_router, b_router, w_gate_up, b_gate_up, w_down, b_down):
    h, topi, gate, rank, counts = _router(x_all, g_norm2[layer], mod_tab, w_router[layer], b_router[layer])
    counts = counts[:, 0]
    padded = ((counts + TM - 1) // TM) * TM
    ends = jnp.cumsum(padded)
    starts = ends - padded
    order = jnp.arange(N_EXPERTS, dtype=jnp.int32)
    first_row = jnp.sum(jnp.where(topi[None] == order[:, None, None], starts[:, None, None], 0), axis=0)
    pos = (first_row + rank).astype(jnp.int32)
    gate = gate.T
    n_used = (ends[-1] // TM).astype(jnp.int32)
    later = jnp.where((padded[None, :] > 0) & (order[None, :] > order[:, None]), order[None, :], N_EXPERTS)
    next_expert = jnp.min(later, axis=1)
    next_expert = jnp.where(next_expert == N_EXPERTS, -1, next_expert).astype(jnp.int32)
    tile_start = jnp.arange(R_TILES, dtype=jnp.int32) * TM
    tile_start = jnp.minimum(tile_start, ends[-1] - 1)
    tile_expert = jnp.sum((ends[None, :] <= tile_start[:, None]).astype(jnp.int32), axis=1)
    tile_expert = jnp.minimum(tile_expert, N_EXPERTS - 1).astype(jnp.int32)
    xs = _dispatch(ends.astype(jnp.int32), pos, h)
    ys = _ffn(layer, tile_expert, n_used.reshape(1), next_expert, xs, w_gate_up, b_gate_up, w_down, b_down)
    return _combine(pos, ys, x_all, gate, mod_tab)


S5_ROWS = 512
S5_HALF_W = S5_WIDTH // 2
S5_HALF_STATES = (S5_GROUPS // 2) * S5_STATE
S5_COL_CHUNK = 512


def _s5_scan_kernel(u_ref, bmat_ref, cmat_ref, a_ref, h0_ref, y_ref, fin_ref, bu_ref, h_ref, *, bsz, steps):
    d = pl.program_id(0)
    c = pl.program_id(1)
    hs = S5_HALF_STATES

    @pl.when(c == 0)
    def _():
        h_ref[...] = h0_ref[...]

    u = u_ref[...].astype(BF16)
    for hf in range(2):
        bu_ref[...] = jnp.dot(u[:, hf * S5_HALF_W:(hf + 1) * S5_HALF_W], bmat_ref[hf],
                              preferred_element_type=F32)
        for j in range(hs // S5_COL_CHUNK):
            re0 = j * S5_COL_CHUNK
            im0 = hs + j * S5_COL_CHUNK
            ar = jnp.broadcast_to(a_ref[hf, 0:1, re0:re0 + S5_COL_CHUNK], (bsz, S5_COL_CHUNK))
            ai = jnp.broadcast_to(a_ref[hf, 1:2, re0:re0 + S5_COL_CHUNK], (bsz, S5_COL_CHUNK))

            def step(t, carry, re0=re0, im0=im0, ar=ar, ai=ai):
                hr, hi = carry
                te = jnp.where(d == 0, t, steps - 1 - t)
                r0 = pl.multiple_of(te * bsz, bsz)
                br = bu_ref[pl.ds(r0, bsz), re0:re0 + S5_COL_CHUNK]
                bi = bu_ref[pl.ds(r0, bsz), im0:im0 + S5_COL_CHUNK]
                nr = ar * hr - ai * hi + br
                ni = ar * hi + ai * hr + bi
                bu_ref[pl.ds(r0, bsz), re0:re0 + S5_COL_CHUNK] = nr
                bu_ref[pl.ds(r0, bsz), im0:im0 + S5_COL_CHUNK] = ni
                return nr, ni

            hr, hi = lax.fori_loop(
                0, steps, step,
                (h_ref[hf, :, re0:re0 + S5_COL_CHUNK], h_ref[hf, :, im0:im0 + S5_COL_CHUNK]), unroll=4)
            h_ref[hf, :, re0:re0 + S5_COL_CHUNK] = hr
            h_ref[hf, :, im0:im0 + S5_COL_CHUNK] = hi
        y_ref[:, hf * S5_HALF_W:(hf + 1) * S5_HALF_W] = jnp.dot(
            bu_ref[...].astype(BF16), cmat_ref[hf], preferred_element_type=F32)

    @pl.when(c == pl.num_programs(1) - 1)
    def _():
        fin_ref[...] = h_ref[...]


def _s5_scan(u_tm, bmat, cmat, acoef, h0, bsz):
    rows = u_tm.shape[0]
    steps = S5_ROWS // bsz
    n_chunks = rows // S5_ROWS

    def chunk_map(d, c):
        return jnp.where(d == 0, c, n_chunks - 1 - c)

    return pl.pallas_call(
        functools.partial(_s5_scan_kernel, bsz=bsz, steps=steps),
        grid=(2, n_chunks),
        in_specs=[
            pl.BlockSpec((S5_ROWS, S5_WIDTH), lambda d, c: (chunk_map(d, c), 0)),
            pl.BlockSpec((None, 2, S5_HALF_W, 2 * S5_HALF_STATES), lambda d, c: (d, 0, 0, 0)),
            pl.BlockSpec((None, 2, 2 * S5_HALF_STATES, S5_HALF_W), lambda d, c: (d, 0, 0, 0)),
            pl.BlockSpec((None, 2, 2, S5_HALF_STATES), lambda d, c: (d, 0, 0, 0)),
            pl.BlockSpec((None, 2, bsz, 2 * S5_HALF_STATES), lambda d, c: (d, 0, 0, 0)),
        ],
        out_specs=[
            pl.BlockSpec((None, S5_ROWS, S5_WIDTH), lambda d, c: (d, chunk_map(d, c), 0)),
            pl.BlockSpec((None, 2, bsz, 2 * S5_HALF_STATES), lambda d, c: (d, 0, 0, 0)),
        ],
        out_shape=[
            jax.ShapeDtypeStruct((2, rows, S5_WIDTH), F32),
            jax.ShapeDtypeStruct((2, 2, bsz, 2 * S5_HALF_STATES), F32),
        ],
        scratch_shapes=[
            pltpu.VMEM((S5_ROWS, 2 * S5_HALF_STATES), F32),
            pltpu.VMEM((2, bsz, 2 * S5_HALF_STATES), F32),
        ],
        compiler_params=pltpu.CompilerParams(
            dimension_semantics=("arbitrary", "arbitrary"), vmem_limit_bytes=VMEM_LIMIT_BYTES),
        name="s5_scan",
    )(u_tm, bmat, cmat, acoef, h0)


def _s5_discretize(lam_re, lam_im, log_dt, b_re, b_im, c_re, c_im):
    eye = jnp.eye(S5_GROUPS // 2, dtype=F32)
    bmats, cmats, acoefs = [], [], []
    for dr in range(2):
        lr = jnp.minimum(lam_re[dr].astype(F32), -1e-4)
        li = lam_im[dr].astype(F32)
        dt = jnp.exp(log_dt[dr].astype(F32))[:, None]
        mag = jnp.exp(lr * dt)
        ar, ai = mag * jnp.cos(li * dt), mag * jnp.sin(li * dt)
        den = lr * lr + li * li
        fr = ((ar - 1.0) * lr + ai * li) / den
        fi = (ai * lr - (ar - 1.0) * li) / den
        br_ = b_re[dr].astype(F32)
        bi_ = b_im[dr].astype(F32)
        bbr = fr[..., None] * br_ - fi[..., None] * bi_
        bbi = fr[..., None] * bi_ + fi[..., None] * br_
        bm, cm, am = [], [], []
        for hf in range(2):
            g = slice(hf * S5_GROUPS // 2, (hf + 1) * S5_GROUPS // 2)

            def bdiag_in(w):
                return jnp.einsum('ab,aph->ahbp', eye, w[g]).reshape(S5_HALF_W, S5_HALF_STATES)

            def bdiag_out(w):
                return jnp.einsum('ab,ahp->apbh', eye, w[g]).reshape(S5_HALF_STATES, S5_HALF_W)

            bm.append(jnp.concatenate([bdiag_in(bbr), bdiag_in(bbi)], axis=1))
            cm.append(jnp.concatenate([bdiag_out(c_re[dr].astype(F32)),
                                       -bdiag_out(c_im[dr].astype(F32))], axis=0))
            am.append(jnp.stack([ar[g].reshape(-1), ai[g].reshape(-1)]))
        bmats.append(jnp.stack(bm))
        cmats.append(jnp.stack(cm))
        acoefs.append(jnp.stack(am))
    return jnp.stack(bmats).astype(BF16), jnp.stack(cmats).astype(BF16), jnp.stack(acoefs)


def _s5_state_to_kernel(h0):
    bsz = h0.shape[0]
    h = h0.astype(F32).reshape(bsz, 2, 2, 2, S5_HALF_STATES)
    return h.transpose(1, 3, 0, 2, 4).reshape(2, 2, bsz, 2 * S5_HALF_STATES)


def _s5_state_from_kernel(fin):
    bsz = fin.shape[2]
    h = fin.reshape(2, 2, bsz, 2, S5_HALF_STATES).transpose(2, 0, 3, 1, 4)
    return h.reshape(bsz, 2, 2, S5_GROUPS, S5_STATE)


TQ = 256
MLA_HEADS_PER_STEP = 8
DIFF_HEADS_PER_STEP = 2
ATT_TQ = 1024
HEAD_LANES = 128
MLA_SCALE = (MLA_NOPE + MLA_ROPE) ** -0.5
DIFF_SCALE = DIFF_HD ** -0.5


def _dot(a, b):
    return jnp.dot(a, b, preferred_element_type=F32)


def _dot_t(a, b):
    return lax.dot_general(a, b, (((1,), (1,)), ((), ())), preferred_element_type=F32)


def _rms_rows(x, g):
    return x * lax.rsqrt(jnp.mean(x * x, axis=-1, keepdims=True) + EPS) * g


def _group(bsz, length, row0, mod_base, mod_stride):
    return dict(bsz=bsz, length=length, nt=length // TQ, tile0=row0 // TQ,
                mod_base=mod_base, mod_stride=mod_stride)


def _params(n_axes):
    return pltpu.CompilerParams(dimension_semantics=("arbitrary",) * n_axes,
                                vmem_limit_bytes=VMEM_LIMIT_BYTES)


def _axial_rope(length, dim):
    rows = length // GRID_W
    row = jnp.repeat(jnp.arange(rows, dtype=F32), GRID_W)
    col = jnp.tile(jnp.arange(GRID_W, dtype=F32), rows)
    n_freq = dim // 4
    inv = ROPE_THETA ** (-jnp.arange(n_freq, dtype=F32) / n_freq)
    ang = jnp.concatenate([row[:, None] * inv, col[:, None] * inv], axis=-1)
    return jnp.cos(ang), jnp.sin(ang)


def _rope_tables(length, dim, lead, reps):
    cos, sin = _axial_rope(length, dim)
    cos_r = jnp.repeat(cos, 2, axis=-1)
    sin_r = jnp.repeat(sin, 2, axis=-1) * jnp.tile(jnp.array([-1.0, 1.0], F32), dim // 2)
    part = HEAD_LANES // reps
    pad = ((0, 0), (lead, part - lead - dim))
    cos_t = jnp.tile(jnp.pad(cos_r, pad, constant_values=1.0), (1, reps))
    sin_t = jnp.tile(jnp.pad(sin_r, pad), (1, reps))
    return cos_t, sin_t


def _swap_pairs(w):
    return w[:, jnp.arange(w.shape[1]) ^ 1]


def _even_in_kernel(*refs, rope):
    if rope:
        (x_ref, g_ref, mod_ref, win_ref, gq_ref, wuq_ref, gkv_ref, cos_ref, sin_ref,
         u_ref, q_ref, ckv_ref, kr_ref) = refs
    else:
        (x_ref, g_ref, mod_ref, win_ref, gq_ref, wuq_ref, gkv_ref,
         u_ref, q_ref, ckv_ref, kr_ref) = refs
    o1 = S5_WIDTH
    o2 = o1 + MLA_Q_LORA
    o3 = o2 + MLA_KV_LORA
    o4 = o3 + HEAD_LANES
    n_in = o4 + HEAD_LANES if rope else o4
    n_q = MLA_HEADS * HEAD_LANES
    h = _rms_rows(x_ref[...], g_ref[...]) * (1.0 + mod_ref[0, 1:2, :]) + mod_ref[0, 0:1, :]
    z = _dot(h.astype(BF16), win_ref[:, :n_in])
    u_ref[...] = z[:, :o1]
    ckv_ref[...] = _rms_rows(z[:, o2:o3], gkv_ref[...])
    qn = _rms_rows(z[:, o1:o2], gq_ref[...]).astype(BF16)
    if rope:
        q2 = _dot(qn, wuq_ref[...])
        cos = cos_ref[...]
        sin = sin_ref[...]
        for hd in range(MLA_HEADS):
            a = hd * HEAD_LANES
            q_ref[:, a:a + HEAD_LANES] = ((q2[:, a:a + HEAD_LANES] * cos
                                           + q2[:, n_q + a:n_q + a + HEAD_LANES] * sin) * MLA_SCALE
                                          ).astype(q_ref.dtype)
        kr_ref[...] = z[:, o3:o4] * cos + z[:, o4:o4 + HEAD_LANES] * sin
    else:
        q_ref[...] = (_dot(qn, wuq_ref[:, :n_q]) * MLA_SCALE).astype(q_ref.dtype)
        kr_ref[...] = z[:, o3:o4]


def _even_in(x_all, grp, g1, mod_tab, win_aug, g_q, wuq2, g_kv, tables):
    bsz, length, nt = grp["bsz"], grp["length"], grp["nt"]
    rope = tables is not None
    rows = bsz * length

    def tok(b, t):
        return (b * nt + t, 0)

    in_specs = [
        pl.BlockSpec((TQ, D_MODEL), lambda b, t: (grp["tile0"] + b * nt + t, 0)),
        pl.BlockSpec((1, D_MODEL), lambda b, t: (0, 0)),
        pl.BlockSpec((1, 6, D_MODEL), lambda b, t: (grp["mod_base"] + b * grp["mod_stride"], 0, 0)),
        pl.BlockSpec(win_aug.shape, lambda b, t: (0, 0)),
        pl.BlockSpec((1, MLA_Q_LORA), lambda b, t: (0, 0)),
        pl.BlockSpec(wuq2.shape, lambda b, t: (0, 0)),
        pl.BlockSpec((1, MLA_KV_LORA), lambda b, t: (0, 0)),
    ]
    args = [x_all, g1.reshape(1, D_MODEL), mod_tab, win_aug, g_q.reshape(1, -1), wuq2, g_kv.reshape(1, -1)]
    if rope:
        in_specs += [pl.BlockSpec((TQ, HEAD_LANES), lambda b, t: (t, 0))] * 2
        args += list(tables)
    return pl.pallas_call(
        functools.partial(_even_in_kernel, rope=rope),
        grid=(bsz, nt),
        in_specs=in_specs,
        out_specs=[
            pl.BlockSpec((TQ, S5_WIDTH), lambda b, t: (t, b)),
            pl.BlockSpec((TQ, MLA_HEADS * HEAD_LANES), tok),
            pl.BlockSpec((TQ, MLA_KV_LORA), tok),
            pl.BlockSpec((TQ, HEAD_LANES), tok),
        ],
        out_shape=[
            jax.ShapeDtypeStruct((length, bsz * S5_WIDTH), F32),
            jax.ShapeDtypeStruct((rows, MLA_HEADS * HEAD_LANES), BF16),
            jax.ShapeDtypeStruct((rows, MLA_KV_LORA), F32),
            jax.ShapeDtypeStruct((rows, HEAD_LANES), F32),
        ],
        compiler_params=_params(2),
        name="even_in",
    )(*args)


def _kv_expand_kernel(x_ref, kr_ref, wk_ref, wv_ref, k_ref, v_ref):
    x = x_ref[...].astype(BF16)
    k = _dot(x, wk_ref[...])
    kr = kr_ref[...]
    for hd in range(MLA_HEADS):
        a = hd * HEAD_LANES
        k_ref[:, a:a + HEAD_LANES] = (k[:, a:a + HEAD_LANES] + kr).astype(k_ref.dtype)
    v_ref[...] = _dot(x, wv_ref[...]).astype(v_ref.dtype)


def _kv_expand(ckv, kr, wk, wv):
    rows = ckv.shape[0]
    tm = 512
    width = MLA_HEADS * HEAD_LANES
    return pl.pallas_call(
        _kv_expand_kernel,
        grid=(rows // tm,),
        in_specs=[pl.BlockSpec((tm, MLA_KV_LORA), lambda i: (i, 0)),
                  pl.BlockSpec((tm, HEAD_LANES), lambda i: (i, 0)),
                  pl.BlockSpec(wk.shape, lambda i: (0, 0)),
                  pl.BlockSpec(wv.shape, lambda i: (0, 0))],
        out_specs=[pl.BlockSpec((tm, width), lambda i: (i, 0))] * 2,
        out_shape=[jax.ShapeDtypeStruct((rows, width), BF16)] * 2,
        compiler_params=_params(1),
        name="kv_expand",
    )(ckv, kr, wk, wv)


def _exp_parts(scores):
    m = functools.reduce(jnp.maximum, [jnp.max(s, axis=-1, keepdims=True) for s in scores])
    es = [jnp.exp(s - m) for s in scores]
    den = functools.reduce(jnp.add, [jnp.sum(e, axis=-1, keepdims=True) for e in es])
    return es, 1.0 / den


def _weighted_values(es, vs):
    o = _dot(es[0].astype(BF16), vs[0])
    for e, v in zip(es[1:], vs[1:]):
        o = o + _dot(e.astype(BF16), v)
    return o


def _mla_attn_kernel(*refs, n_seg):
    q_ref, o_ref = refs[0], refs[-1]
    for hd in range(MLA_HEADS_PER_STEP):
        lanes = slice(hd * HEAD_LANES, (hd + 1) * HEAD_LANES)
        q = q_ref[:, lanes]
        ks = [refs[1 + 2 * s][:, lanes] for s in range(n_seg)]
        vs = [refs[2 + 2 * s][:, lanes] for s in range(n_seg)]
        es, inv = _exp_parts([_dot_t(q, k) for k in ks])
        o_ref[:, lanes] = (_weighted_values(es, vs) * inv).astype(o_ref.dtype)


def _mla_attn(q, segs, bsz, length):
    tq = TQ
    nq = length // tq
    width = MLA_HEADS_PER_STEP * HEAD_LANES
    in_specs = [pl.BlockSpec((tq, width), lambda b, h, i: (b * nq + i, h))]
    args = [q]
    for k, v, lk in segs:
        in_specs += [pl.BlockSpec((lk, width), lambda b, h, i: (b, h))] * 2
        args += [k, v]
    return pl.pallas_call(
        functools.partial(_mla_attn_kernel, n_seg=len(segs)),
        grid=(bsz, MLA_HEADS // MLA_HEADS_PER_STEP, nq),
        in_specs=in_specs,
        out_specs=pl.BlockSpec((tq, width), lambda b, h, i: (b * nq + i, h)),
        out_shape=jax.ShapeDtypeStruct((bsz * length, MLA_HEADS * HEAD_LANES), BF16),
        compiler_params=_params(3),
        name="mla_attn",
    )(*args)


def _even_out_kernel(u_ref, y_ref, o_ref, x_ref, mod_ref, d_ref, wglu_ref, bglu_ref, ws5_ref, wmla_ref,
                     out_ref):
    y =jax.nn.gelu(d_ref[...] * u_ref[...] + y_ref[0] + y_ref[1])
    s5 = y * jax.nn.sigmoid(_dot(y.astype(BF16), wglu_ref[...]) + bglu_ref[...])
    mix = _dot(s5.astype(BF16), ws5_ref[...]) + _dot(o_ref[...].astype(BF16), wmla_ref[...])
    out_ref[...] = x_ref[...] + mod_ref[0, 2:3, :] * mix


def _even_out(x_all, grp, mod_tab, u_tm, y_dir, o_mla, d_skip, w_glu, b_glu, w_out_s5, w_out_mla):
    bsz, length, nt = grp["bsz"], grp["length"], grp["nt"]

    def xrow(b, t):
        return (grp["tile0"] + b * nt + t, 0)

    full = lambda b, t: (0, 0)
    return pl.pallas_call(
        _even_out_kernel,
        grid=(bsz, nt),
        in_specs=[
            pl.BlockSpec((TQ, S5_WIDTH), lambda b, t: (t, b)),
            pl.BlockSpec((2, TQ, S5_WIDTH), lambda b, t: (0, t, b)),
            pl.BlockSpec((TQ, MLA_HEADS * HEAD_LANES), lambda b, t: (b * nt + t, 0)),
            pl.BlockSpec((TQ, D_MODEL), xrow),
            pl.BlockSpec((1, 6, D_MODEL), lambda b, t: (grp["mod_base"] + b * grp["mod_stride"], 0, 0)),
            pl.BlockSpec((1, S5_WIDTH), full),
            pl.BlockSpec(w_glu.shape, full),
            pl.BlockSpec((1, S5_WIDTH), full),
            pl.BlockSpec(w_out_s5.shape, full),
            pl.BlockSpec(w_out_mla.shape, full),
        ],
        out_specs=pl.BlockSpec((TQ, D_MODEL), xrow),
        out_shape=jax.ShapeDtypeStruct(x_all.shape, F32),
        input_output_aliases={3: 0},
        compiler_params=_params(2),
        name="even_out",
    )(u_tm, y_dir.reshape(2, length, bsz * S5_WIDTH), o_mla, x_all, mod_tab,
      d_skip.reshape(1, S5_WIDTH), w_glu, b_glu.reshape(1, S5_WIDTH), w_out_s5, w_out_mla)


def _odd_in_kernel(*refs, rope):
    if rope:
        x_ref, g_ref, mod_ref, w_ref, cos_ref, sin_ref, q_ref, k_ref, v_ref = refs
    else:
        x_ref, g_ref, mod_ref, w_ref, q_ref, k_ref, v_ref = refs
    w3 = 3 * DIFF_WIDTH
    h = _rms_rows(x_ref[...], g_ref[...]) * (1.0 + mod_ref[0, 1:2, :]) + mod_ref[0, 0:1, :]
    z = _dot(h.astype(BF16), w_ref[...] if rope else w_ref[:, :w3])
    v_ref[...] = z[:, 2 * DIFF_WIDTH:w3].astype(v_ref.dtype)
    if rope:
        cos = cos_ref[...]
        sin = sin_ref[...]
        for hd in range(DIFF_HEADS):
            a = hd * HEAD_LANES
            q_ref[:, a:a + HEAD_LANES] = ((z[:, a:a + HEAD_LANES] * cos
                                           + z[:, w3 + a:w3 + a + HEAD_LANES] * sin) * DIFF_SCALE
                                          ).astype(q_ref.dtype)
            b = DIFF_WIDTH + a
            k_ref[:, a:a + HEAD_LANES] = (z[:, b:b + HEAD_LANES] * cos
                                          + z[:, w3 + b:w3 + b + HEAD_LANES] * sin).astype(k_ref.dtype)
    else:
        q_ref[...] = (z[:, :DIFF_WIDTH] * DIFF_SCALE).astype(q_ref.dtype)
        k_ref[...] = z[:, DIFF_WIDTH:2 * DIFF_WIDTH].astype(k_ref.dtype)


def _odd_in(x_all, grp, g1, mod_tab, w_aug, tables, kv_dtype):
    bsz, length, nt = grp["bsz"], grp["length"], grp["nt"]
    rope = tables is not None
    rows = bsz * length

    def tok(b, t):
        return (b * nt + t, 0)

    in_specs = [
        pl.BlockSpec((TQ, D_MODEL), lambda b, t: (grp["tile0"] + b * nt + t, 0)),
        pl.BlockSpec((1, D_MODEL), lambda b, t: (0, 0)),
        pl.BlockSpec((1, 6, D_MODEL), lambda b, t: (grp["mod_base"] + b * grp["mod_stride"], 0, 0)),
        pl.BlockSpec(w_aug.shape, lambda b, t: (0, 0)),
    ]
    args = [x_all, g1.reshape(1, D_MODEL), mod_tab, w_aug]
    if rope:
        in_specs += [pl.BlockSpec((TQ, HEAD_LANES), lambda b, t: (t, 0))] * 2
        args += list(tables)
    return pl.pallas_call(
        functools.partial(_odd_in_kernel, rope=rope),
        grid=(bsz, nt),
        in_specs=in_specs,
        out_specs=[pl.BlockSpec((TQ, DIFF_WIDTH), tok)] * 3,
        out_shape=[
            jax.ShapeDtypeStruct((rows, DIFF_WIDTH), BF16),
            jax.ShapeDtypeStruct((rows, DIFF_WIDTH), kv_dtype),
            jax.ShapeDtypeStruct((rows, DIFF_WIDTH), kv_dtype),
        ],
        compiler_params=_params(2),
        name="odd_in",
    )(*args)


def _diff_attn_kernel(*refs, n_seg, post_scale):
    lam_ref, q_ref = refs[0], refs[1]
    g_ref, o_ref = refs[-2], refs[-1]
    for hd in range(DIFF_HEADS_PER_STEP):
        lanes = slice(hd * HEAD_LANES, (hd + 1) * HEAD_LANES)
        q = q_ref[:, lanes].astype(F32)
        lane = lax.broadcasted_iota(jnp.int32, q.shape, 1)
        q0 = jnp.where(lane < DIFF_HD, q, 0.0).astype(BF16)
        q1 = jnp.where(lane >= DIFF_HD, q, 0.0).astype(BF16)
        ks = [refs[2 + 2 * s][:, lanes].astype(BF16) for s in range(n_seg)]
        vs = [refs[3 + 2 * s][:, lanes].astype(BF16) for s in range(n_seg)]
        e0, inv0 = _exp_parts([_dot_t(q0, k) for k in ks])
        e1, inv1 = _exp_parts([_dot_t(q1, k) for k in ks])
        o = _weighted_values(e0, vs) * inv0 - lam_ref[0] * (_weighted_values(e1, vs) * inv1)
        o_ref[:, lanes] = (_rms_rows(o, g_ref[...]) * post_scale).astype(o_ref.dtype)


def _diff_attn(lam_full, q, segs, g_sub, post_scale, bsz, length):
    tq = min(ATT_TQ, length)
    nq = length // tq
    width = DIFF_HEADS_PER_STEP * HEAD_LANES
    in_specs = [pl.BlockSpec(memory_space=pltpu.SMEM),
                pl.BlockSpec((tq, width), lambda b, h, i: (b * nq + i, h))]
    args = [lam_full.reshape(1).astype(F32), q]
    for k, v, lk in segs:
        in_specs += [pl.BlockSpec((lk, width), lambda b, h, i: (b, h))] * 2
        args += [k, v]
    in_specs.append(pl.BlockSpec((1, HEAD_LANES), lambda b, h, i: (0, 0)))
    args.append(g_sub.reshape(1, HEAD_LANES))
    return pl.pallas_call(
        functools.partial(_diff_attn_kernel, n_seg=len(segs), post_scale=post_scale),
        grid=(bsz, DIFF_HEADS // DIFF_HEADS_PER_STEP, nq),
        in_specs=in_specs,
        out_specs=pl.BlockSpec((tq, width), lambda b, h, i: (b * nq + i, h)),
        out_shape=jax.ShapeDtypeStruct((bsz * length, DIFF_WIDTH), BF16),
        compiler_params=_params(3),
        name="diff_attn",
    )(*args)


def _odd_out_kernel(o_ref, x_ref, mod_ref, w_ref, out_ref):
    out_ref[...] = x_ref[...] + mod_ref[0, 2:3, :] * _dot(o_ref[...], w_ref[...])


def _odd_out(x_all, grp, mod_tab, o, w_out):
    bsz, nt = grp["bsz"], grp["nt"]

    def xrow(b, t):
        return (grp["tile0"] + b * nt + t, 0)

    return pl.pallas_call(
        _odd_out_kernel,
        grid=(bsz, nt),
        in_specs=[
            pl.BlockSpec((TQ, DIFF_WIDTH), lambda b, t: (b * nt + t, 0)),
            pl.BlockSpec((TQ, D_MODEL), xrow),
            pl.BlockSpec((1, 6, D_MODEL), lambda b, t: (grp["mod_base"] + b * grp["mod_stride"], 0, 0)),
            pl.BlockSpec(w_out.shape, lambda b, t: (0, 0)),
        ],
        out_specs=pl.BlockSpec((TQ, D_MODEL), xrow),
        out_shape=jax.ShapeDtypeStruct(x_all.shape, F32),
        input_output_aliases={1: 0},
        compiler_params=_params(2),
        name="odd_out",
    )(o, x_all, mod_tab, w_out)


def _final_norm_kernel(x_ref, g_ref, o_ref):
    o_ref[...] = _rms_rows(x_ref[...], g_ref[...])


def _final_norm(x_all, g, row0, rows):
    tm = 512
    return pl.pallas_call(
        _final_norm_kernel,
        grid=(rows // tm,),
        in_specs=[pl.BlockSpec((tm, D_MODEL), lambda i: (row0 // tm + i, 0)),
                  pl.BlockSpec((1, D_MODEL), lambda i: (0, 0))],
        out_specs=pl.BlockSpec((tm, D_MODEL), lambda i: (i, 0)),
        out_shape=jax.ShapeDtypeStruct((rows, D_MODEL), F32),
        compiler_params=_params(1),
        name="final_norm",
    )(x_all, g.reshape(1, D_MODEL))


MOD_ROWS = 16
MOD_COL_TILE = 1536


def _adaln_kernel(c_ref, w_ref, b_ref, o_ref):
    cond = jax.nn.silu(c_ref[...])
    o_ref[...] = jnp.dot(cond, w_ref[...], preferred_element_type=F32,
                         precision=lax.Precision.HIGHEST) + b_ref[...]


def _adaln(cond_rows, w_mod, b_mod):
    return pl.pallas_call(
        _adaln_kernel,
        grid=(DEPTH, 6 * D_MODEL // MOD_COL_TILE),
        in_specs=[pl.BlockSpec((MOD_ROWS, D_MODEL), lambda l, j: (0, 0)),
                  pl.BlockSpec((None, D_MODEL, MOD_COL_TILE), lambda l, j: (l, 0, j)),
                  pl.BlockSpec((None, 1, MOD_COL_TILE), lambda l, j: (l, 0, j))],
        out_specs=pl.BlockSpec((None, MOD_ROWS, MOD_COL_TILE), lambda l, j: (l, 0, j)),
        out_shape=jax.ShapeDtypeStruct((DEPTH, MOD_ROWS, 6 * D_MODEL), F32),
        compiler_params=_params(2),
        name="adaln",
    )(cond_rows, w_mod, b_mod.reshape(DEPTH, 1, 6 * D_MODEL))


def _pad_head_lanes(x, lead):
    return jnp.pad(x, ((0, 0), (lead, HEAD_LANES - lead - x.shape[1])))


def _even_weights(w_in, w_out, w_uq, w_ukv, w_glu):
    o3 = S5_WIDTH + MLA_Q_LORA + MLA_KV_LORA
    w_kr = w_in[:, o3:]
    win_aug = jnp.concatenate(
        [w_in[:, :o3], _pad_head_lanes(w_kr, MLA_NOPE), _pad_head_lanes(_swap_pairs(w_kr), MLA_NOPE)], axis=1)
    dq = MLA_NOPE + MLA_ROPE
    plain, swapped = [], []
    for hd in range(MLA_HEADS):
        wn = w_uq[:, hd * dq:hd * dq + MLA_NOPE]
        wr = w_uq[:, hd * dq + MLA_NOPE:(hd + 1) * dq]
        plain.append(jnp.pad(jnp.concatenate([wn, wr], axis=1), ((0, 0), (0, HEAD_LANES - dq))))
        swapped.append(_pad_head_lanes(_swap_pairs(wr), MLA_NOPE))
    wuq2 = jnp.concatenate(plain + swapped, axis=1)
    w_mla = w_out[S5_WIDTH:].reshape(MLA_HEADS, MLA_V, D_MODEL)
    w_out_mla = jnp.pad(w_mla, ((0, 0), (0, HEAD_LANES - MLA_V), (0, 0))).reshape(MLA_HEADS * HEAD_LANES, D_MODEL)
    w_kv = w_ukv.reshape(MLA_KV_LORA, MLA_HEADS, MLA_NOPE + MLA_V)
    wk = jnp.pad(w_kv[:, :, :MLA_NOPE], ((0, 0), (0, 0), (0, HEAD_LANES - MLA_NOPE)))
    wv = jnp.pad(w_kv[:, :, MLA_NOPE:], ((0, 0), (0, 0), (0, HEAD_LANES - MLA_V)))
    w_kv = (wk.reshape(MLA_KV_LORA, -1).astype(BF16), wv.reshape(MLA_KV_LORA, -1).astype(BF16))
    return (win_aug.astype(BF16), wuq2.astype(BF16), w_kv, w_glu.astype(BF16),
            w_out[:S5_WIDTH].astype(BF16), w_out_mla.astype(BF16))


def _even_layer(x_all, grp, g1, mod_tab, ew, s5m, g_q, g_kv, d_skip, b_glu, h0, ctx, tables):
    win_aug, wuq2, w_kv, w_glu, w_out_s5, w_out_mla = ew
    bmat, cmat, acoef = s5m
    bsz, length = grp["bsz"], grp["length"]
    u_tm, q, ckv, kr = _even_in(x_all, grp, g1, mod_tab, win_aug, g_q, wuq2, g_kv, tables)
    y_dir, fin = _s5_scan(u_tm.reshape(length * bsz, S5_WIDTH), bmat, cmat, acoef, h0, bsz)
    segs = [(*_kv_expand(ckv, kr, *w_kv), length)]
    if ctx is not None:
        segs.append((*_kv_expand(*ctx, *w_kv), PAST_LEN))
    o_mla = _mla_attn(q, segs, bsz, length)
    x_all = _even_out(x_all, grp, mod_tab, u_tm, y_dir, o_mla, d_skip, w_glu, b_glu, w_out_s5, w_out_mla)
    return x_all, fin, ckv, kr


def _odd_layer(x_all, grp, g1, mod_tab, w_aug, w_out, lam_full, g_sub, post_scale, ctx, tables, kv_dtype):
    bsz, length = grp["bsz"], grp["length"]
    q, k, v = _odd_in(x_all, grp, g1, mod_tab, w_aug, tables, kv_dtype)
    segs = [(k, v, length)]
    if ctx is not None:
        segs.append((ctx[0], ctx[1], PAST_LEN))
    o = _diff_attn(lam_full, q, segs, g_sub, post_scale, bsz, length)
    return _odd_out(x_all, grp, mod_tab, o, w_out), k, v


def kernel(x_prompt, x_sample, state_s5, cache_mla, cache_diff_k, cache_diff_v, c, c_ctx, w_mod, b_mod, g_norm1, g_norm2, g_final, w_in_even, w_out_even, s5_lam_re, s5_lam_im, s5_log_dt, s5_b_re, s5_b_im, s5_c_re, s5_c_im, s5_d, s5_w_glu, s5_b_glu, mla_g_q, mla_w_uq, mla_g_kv, mla_w_ukv, w_in_odd, w_out_odd, diff_lam, diff_g_sub, w_router, b_router, w_gate_up, b_gate_up, w_down, b_down):
    tab_mla = _rope_tables(DEC_SEQ, MLA_ROPE, MLA_NOPE, 1)
    tab_diff = _rope_tables(DEC_SEQ, DIFF_HD, 0, 2)
    grp_p = _group(BATCH, SEQ, 0, 0, 0)
    grp_s = _group(DEC_BATCH, DEC_SEQ, N_PROMPT, 1, 1)
    x_all = jnp.concatenate([x_prompt.reshape(N_PROMPT, D_MODEL), x_sample.reshape(N_SAMPLE, D_MODEL)], axis=0)
    cond = jnp.concatenate([c_ctx[None], c, jnp.zeros((MOD_ROWS - 1 - DEC_BATCH, D_MODEL), c.dtype)], axis=0)
    mods = _adaln(cond.astype(F32), w_mod, b_mod)
    new_s5, new_mla, new_k, new_v = [], [], [], []
    for l in range(DEPTH):
        mod_tab = mods[l].reshape(MOD_ROWS, 6, D_MODEL)
        i = l // 2
        if l % 2 == 0:
            ew = _even_weights(w_in_even[i], w_out_even[i], mla_w_uq[i], mla_w_ukv[i], s5_w_glu[i])
            s5m = _s5_discretize(s5_lam_re[i], s5_lam_im[i], s5_log_dt[i], s5_b_re[i], s5_b_im[i],
                                 s5_c_re[i], s5_c_im[i])
            common = (ew, s5m, mla_g_q[i], mla_g_kv[i], s5_d[i], s5_b_glu[i])
            h0_p = jnp.zeros((2, 2, BATCH, 2 * S5_HALF_STATES), F32)
            x_all, fin, ckv, kr = _even_layer(x_all, grp_p, g_norm1[l], mod_tab, *common, h0_p, None, None)
            new_s5.append(_s5_state_from_kernel(fin))
            new_mla.append(jnp.concatenate([ckv, kr[:, MLA_NOPE:MLA_NOPE + MLA_ROPE]], axis=1)
                           .reshape(BATCH, SEQ, MLA_KV_LORA + MLA_ROPE))
            lat_ctx = cache_mla[:, i].astype(F32).reshape(DEC_BATCH * PAST_LEN, MLA_KV_LORA + MLA_ROPE)
            ctx = (lat_ctx[:, :MLA_KV_LORA], _pad_head_lanes(lat_ctx[:, MLA_KV_LORA:], MLA_NOPE))
            x_all, _, _, _ = _even_layer(x_all, grp_s, g_norm1[l], mod_tab, *common,
                                         _s5_state_to_kernel(state_s5[:, i]), ctx, tab_mla)
        else:
            lam_init = 0.8 - 0.6 * math.exp(-0.3 * l)
            lamf = diff_lam[i].astype(F32)
            lam_full = jnp.exp(jnp.sum(lamf[0] * lamf[1])) - jnp.exp(jnp.sum(lamf[2] * lamf[3])) + lam_init
            w_qk = w_in_odd[i][:, :2 * DIFF_WIDTH]
            w_aug = jnp.concatenate([w_in_odd[i], _swap_pairs(w_qk)], axis=1).astype(BF16)
            w_out = w_out_odd[i].astype(BF16)
            odd = (w_aug, w_out, lam_full, diff_g_sub[i], 1.0 - lam_init)
            x_all, kp, vp = _odd_layer(x_all, grp_p, g_norm1[l], mod_tab, *odd, None, None, F32)
            new_k.append(kp.reshape(BATCH, SEQ, DIFF_HEADS, 2, DIFF_HD))
            new_v.append(vp.reshape(BATCH, SEQ, DIFF_HEADS, 2 * DIFF_HD))
            ctx = (cache_diff_k[:, i].reshape(DEC_BATCH * PAST_LEN, DIFF_WIDTH).astype(BF16),
                   cache_diff_v[:, i].reshape(DEC_BATCH * PAST_LEN, DIFF_WIDTH).astype(BF16))
            x_all, _, _ = _odd_layer(x_all, grp_s, g_norm1[l], mod_tab, *odd, ctx, tab_diff, BF16)
        x_all = _moe_layer(l, x_all, mod_tab, g_norm2, w_router, b_router,
                           w_gate_up, b_gate_up, w_down, b_down)
    y_prompt = _final_norm(x_all, g_final, 0, N_PROMPT)
    y_sample = _final_norm(x_all, g_final, N_PROMPT, N_SAMPLE)
    return (y_prompt.reshape(BATCH, SEQ, D_MODEL), y_sample.reshape(DEC_BATCH, DEC_SEQ, D_MODEL),
            jnp.stack(new_s5, axis=1), jnp.stack(new_mla, axis=1),
            jnp.stack(new_k, axis=1), jnp.stack(new_v, axis=1))
```

```python
import functools
import math

import jax
import jax.numpy as jnp
from jax import lax
from jax.experimental import pallas as pl
from jax.experimental.pallas import tpu as pltpu

D_MODEL = 1024
BATCH = 16
SEQ = 256
DEPTH = 4
DEC_BATCH = 8
DEC_SEQ = 1024
PAST_LEN = 512
GRID_W = 64
N_EVEN = (DEPTH + 1) // 2
N_ODD = DEPTH // 2
S5_WIDTH = D_MODEL // 2
S5_GROUP = 16
S5_GROUPS = S5_WIDTH // S5_GROUP
S5_STATE = 64
MLA_HEADS = 8
MLA_NOPE = 64
MLA_ROPE = 32
MLA_V = 64
MLA_Q_LORA = D_MODEL // 4
MLA_KV_LORA = D_MODEL // 8
MLA_WIDTH = MLA_HEADS * MLA_V
EVEN_IN = S5_WIDTH + MLA_Q_LORA + MLA_KV_LORA + MLA_ROPE
EVEN_OUT = S5_WIDTH + MLA_WIDTH
DIFF_HEADS = 8
DIFF_HD = D_MODEL // (2 * DIFF_HEADS)
DIFF_WIDTH = DIFF_HEADS * 2 * DIFF_HD
N_EXPERTS = 32
TOP_K = 4
D_FF = D_MODEL
SWIGLU_LIMIT = 7.0
SWIGLU_ALPHA = 1.702
ROPE_THETA = 10000.0
Q_BLOCK = 128
EPS = 1e-6

N_PROMPT = BATCH * SEQ
N_SAMPLE = DEC_BATCH * DEC_SEQ
N_TOK = N_PROMPT + N_SAMPLE

LANES = 128
VMEM_LIMIT_BYTES = 56 * 1024 * 1024

TM = 256
N_TILES = N_TOK // TM
R_TILES = N_TOK * TOP_K // TM + N_EXPERTS
R_MAX = R_TILES * TM
TOPK_ROWS = 8

F32 = jnp.float32
BF16 = jnp.bfloat16


def _mod_row(i):
    t0 = i * TM
    return jnp.where(t0 < N_PROMPT, 0, 1 + (t0 - N_PROMPT) // DEC_SEQ)


def _moe_input(x_ref, g_ref, mod_ref):
    x = x_ref[...]
    ms = jnp.mean(x * x, axis=-1, keepdims=True)
    y = x * lax.rsqrt(ms + EPS) * g_ref[...]
    return y * (1.0 + mod_ref[0, 4:5, :]) + mod_ref[0, 3:4, :]


def _router_kernel(x_ref, g_ref, mod_ref, wr_ref, br_ref,
                   topi_ref, gate_ref, rank_ref, counts_ref, carry_ref):
    i = pl.program_id(0)

    @pl.when(i == 0)
    def _():
        carry_ref[...] = jnp.zeros_like(carry_ref)

    h = _moe_input(x_ref, g_ref, mod_ref)

    hi = h.astype(BF16)
    lo = (h - hi.astype(F32)).astype(BF16)
    w_hi = wr_ref[0]
    logits = _dot_t(w_hi, hi) + (_dot_t(wr_ref[1], hi) + _dot_t(w_hi, lo)) + br_ref[...]
    sub_e = lax.broadcasted_iota(jnp.int32, logits.shape, 0)
    work = logits
    vals, hits = [], []
    sel = jnp.zeros(logits.shape, F32)
    for _ in range(TOP_K):
        m = jnp.max(work, axis=0, keepdims=True)
        idx = jnp.min(jnp.where(work == m, sub_e, N_EXPERTS), axis=0, keepdims=True)
        hit = sub_e == idx
        vals.append(m)
        hits.append((hit, idx))
        sel = jnp.where(hit, 1.0, sel)
        work = jnp.where(hit, -jnp.inf, work)
    es = [jnp.exp(v - vals[0]) for v in vals]
    inv = 1.0 / (es[0] + es[1] + es[2] + es[3])

    row = lax.broadcasted_iota(jnp.int32, (TM, TM), 0)
    col = lax.broadcasted_iota(jnp.int32, (TM, TM), 1)
    earlier = jnp.where(row < col, 1.0, 0.0).astype(BF16)
    before = _dot(sel.astype(BF16), earlier) + carry_ref[...]
    carry_ref[...] += jnp.sum(sel, axis=1, keepdims=True)
    counts_ref[...] = carry_ref[...].astype(jnp.int32)

    sub_k = lax.broadcasted_iota(jnp.int32, (TOPK_ROWS, TM), 0)
    topi = jnp.zeros((TOPK_ROWS, TM), jnp.int32)
    gate = jnp.zeros((TOPK_ROWS, TM), F32)
    rank = jnp.zeros((TOPK_ROWS, TM), jnp.int32)
    for k in range(TOP_K):
        hit, idx = hits[k]
        rk = jnp.sum(jnp.where(hit, before, 0.0), axis=0, keepdims=True)
        topi = jnp.where(sub_k == k, idx, topi)
        gate = jnp.where(sub_k == k, es[k] * inv, gate)
        rank = jnp.where(sub_k == k, rk.astype(jnp.int32), rank)
    topi_ref[...] = topi
    gate_ref[...] = gate
    rank_ref[...] = rank


def _router(x_all, g, mod_tab, w_router, b_router):
    w_t = w_router.astype(F32).T
    w_hi = w_t.astype(BF16)
    w_split = jnp.stack([w_hi, (w_t - w_hi.astype(F32)).astype(BF16)])
    return pl.pallas_call(
        _router_kernel,
        grid=(N_TILES,),
        in_specs=[
            pl.BlockSpec((TM, D_MODEL), lambda i: (i, 0)),
            pl.BlockSpec((1, D_MODEL), lambda i: (0, 0)),
            pl.BlockSpec((1, 6, D_MODEL), lambda i: (_mod_row(i), 0, 0)),
            pl.BlockSpec((2, N_EXPERTS, D_MODEL), lambda i: (0, 0, 0)),
            pl.BlockSpec((N_EXPERTS, 1), lambda i: (0, 0)),
        ],
        out_specs=[
            pl.BlockSpec((TOPK_ROWS, TM), lambda i: (0, i)),
            pl.BlockSpec((TOPK_ROWS, TM), lambda i: (0, i)),
            pl.BlockSpec((TOPK_ROWS, TM), lambda i: (0, i)),
            pl.BlockSpec((N_EXPERTS, 1), lambda i: (0, 0)),
        ],
        out_shape=[
            jax.ShapeDtypeStruct((TOPK_ROWS, N_TOK), jnp.int32),
            jax.ShapeDtypeStruct((TOPK_ROWS, N_TOK), F32),
            jax.ShapeDtypeStruct((TOPK_ROWS, N_TOK), jnp.int32),
            jax.ShapeDtypeStruct((N_EXPERTS, 1), jnp.int32),
        ],
        scratch_shapes=[pltpu.VMEM((N_EXPERTS, 1), F32)],
        compiler_params=pltpu.CompilerParams(
            dimension_semantics=("arbitrary",), vmem_limit_bytes=VMEM_LIMIT_BYTES),
        name="moe_router",
    )(x_all, g.reshape(1, D_MODEL), mod_tab, w_split, b_router.reshape(N_EXPERTS, 1))


ISSUE_UNROLL = 4


def _dispatch_kernel(ends_ref, pos_ref, x_ref, g_ref, mod_ref, xs_ref, zero_buf, h_ref, pos_smem, sem_idx, sem,
                     sem_zero):
    i = pl.program_id(0)

    @pl.when(i == 0)
    def _():
        zero_buf[...] = jnp.zeros_like(zero_buf)

        for wait in (False, True):
            for e in range(N_EXPERTS):
                start = ends_ref[e - 1] if e > 0 else 0

                @pl.when(ends_ref[e] > start)
                def _(e=e, wait=wait):
                    last = pl.multiple_of(ends_ref[e] - TM, TM)
                    cp = pltpu.make_async_copy(zero_buf, xs_ref.at[pl.ds(last, TM)], sem_zero)
                    if wait:
                        cp.wait()
                    else:
                        cp.start()

            def tail(t, carry, wait=wait):
                cp = pltpu.make_async_copy(zero_buf, xs_ref.at[pl.ds(pl.multiple_of(t * TM, TM), TM)], sem_zero)
                if wait:
                    cp.wait()
                else:
                    cp.start()
                return carry

            lax.fori_loop(ends_ref[N_EXPERTS - 1] // TM, R_TILES, tail, 0)

    cp = pltpu.make_async_copy(pos_ref, pos_smem, sem_idx)
    cp.start()
    h_ref[...] = _moe_input(x_ref, g_ref, mod_ref)
    cp.wait()

    def issue(r, carry):
        for k in range(TOP_K):
            p = pos_smem[k, r]
            pltpu.make_async_copy(h_ref.at[pl.ds(r, 1)], xs_ref.at[pl.ds(p, 1)], sem.at[k]).start(priority=k % 2)
        return carry

    lax.fori_loop(0, TM, issue, 0, unroll=ISSUE_UNROLL)
    for k in range(TOP_K):
        pltpu.make_async_copy(h_ref, xs_ref.at[pl.ds(0, TM)], sem.at[k]).wait()


def _dispatch(ends, pos, x_all, g, mod_tab):
    grid_spec = pltpu.PrefetchScalarGridSpec(
        num_scalar_prefetch=1,
        grid=(N_TILES,),
        in_specs=[
            pl.BlockSpec((TOPK_ROWS, TM), lambda i, ends: (0, i)),
            pl.BlockSpec((TM, D_MODEL), lambda i, ends: (i, 0)),
            pl.BlockSpec((1, D_MODEL), lambda i, ends: (0, 0)),
            pl.BlockSpec((1, 6, D_MODEL), lambda i, ends: (_mod_row(i), 0, 0)),
        ],
        out_specs=pl.BlockSpec(memory_space=pl.ANY),
        scratch_shapes=[
            pltpu.VMEM((TM, D_MODEL), F32),
            pltpu.VMEM((TM, D_MODEL), F32),
            pltpu.SMEM((TOPK_ROWS, TM), jnp.int32),
            pltpu.SemaphoreType.DMA,
            pltpu.SemaphoreType.DMA((TOP_K,)),
            pltpu.SemaphoreType.DMA,
        ],
    )
    return pl.pallas_call(
        _dispatch_kernel,
        grid_spec=grid_spec,
        out_shape=jax.ShapeDtypeStruct((R_MAX, D_MODEL), F32),
        compiler_params=pltpu.CompilerParams(
            dimension_semantics=("arbitrary",), vmem_limit_bytes=VMEM_LIMIT_BYTES),
        name="moe_dispatch",
    )(ends, pos, x_all, g.reshape(1, D_MODEL), mod_tab)


def _ffn_kernel(te_ref, nu_ref, nx_ref, xs_ref, wgu_hbm, bgu_ref, wd_hbm, bd_ref, ys_ref,
                wgu_f32, wd_f32, wgu_bf, wd_bf, sem, *, layer):
    i = pl.program_id(0)

    def weight_copies(e):
        return (pltpu.make_async_copy(wgu_hbm.at[layer, e], wgu_f32, sem.at[0]),
                pltpu.make_async_copy(wd_hbm.at[layer, e], wd_f32, sem.at[1]))

    @pl.when(i < nu_ref[0])
    def _():
        expert = te_ref[i]
        new_expert = jnp.logical_or(i == 0, expert != te_ref[jnp.maximum(i - 1, 0)])

        @pl.when(i == 0)
        def _():
            for cp in weight_copies(expert):
                cp.start()

        @pl.when(new_expert)
        def _():
            for cp in weight_copies(expert):
                cp.wait()
            wgu_bf[...] = wgu_f32[...].astype(BF16)
            wd_bf[...] = wd_f32[...].astype(BF16)
            nxt = nx_ref[expert]

            @pl.when(nxt >= 0)
            def _():
                for cp in weight_copies(nxt):
                    cp.start()

        x = xs_ref[...].astype(BF16)
        gu = jnp.dot(x, wgu_bf[...], preferred_element_type=F32) + bgu_ref[...]
        g = jnp.minimum(gu[:, :D_FF], SWIGLU_LIMIT)
        u = jnp.clip(gu[:, D_FF:], -SWIGLU_LIMIT, SWIGLU_LIMIT)
        act = g * jax.nn.sigmoid(SWIGLU_ALPHA * g) * (u + 1.0)
        ys_ref[...] = jnp.dot(act.astype(BF16), wd_bf[...], preferred_element_type=F32) + bd_ref[...]

    @pl.when(i >= nu_ref[0])
    def _():
        ys_ref[...] = jnp.zeros_like(ys_ref)


def _ffn(layer, tile_expert, n_used, next_expert, xs, w_gate_up, b_gate_up, w_down, b_down):
    def row_map(i, te, nu, nx):
        return (jnp.maximum(jnp.minimum(i, nu[0] - 1), 0), 0)

    def b_map(i, te, nu, nx):
        return (layer, te[i], 0, 0)

    grid_spec = pltpu.PrefetchScalarGridSpec(
        num_scalar_prefetch=3,
        grid=(R_TILES,),
        in_specs=[
            pl.BlockSpec((TM, D_MODEL), row_map),
            pl.BlockSpec(memory_space=pl.ANY),
            pl.BlockSpec((None, None, 1, 2 * D_FF), b_map),
            pl.BlockSpec(memory_space=pl.ANY),
            pl.BlockSpec((None, None, 1, D_MODEL), b_map),
        ],
        out_specs=pl.BlockSpec((TM, D_MODEL), lambda i, te, nu, nx: (i, 0)),
        scratch_shapes=[
            pltpu.VMEM((D_MODEL, 2 * D_FF), F32),
            pltpu.VMEM((D_FF, D_MODEL), F32),
            pltpu.VMEM((D_MODEL, 2 * D_FF), BF16),
            pltpu.VMEM((D_FF, D_MODEL), BF16),
            pltpu.SemaphoreType.DMA((2,)),
        ],
    )
    return pl.pallas_call(
        functools.partial(_ffn_kernel, layer=layer),
        grid_spec=grid_spec,
        out_shape=jax.ShapeDtypeStruct((R_MAX, D_MODEL), F32),
        compiler_params=pltpu.CompilerParams(
            dimension_semantics=("arbitrary",), vmem_limit_bytes=VMEM_LIMIT_BYTES),
        name="moe_ffn",
    )(tile_expert, n_used, next_expert, xs, w_gate_up,
      b_gate_up.reshape(DEPTH, N_EXPERTS, 1, 2 * D_FF), w_down,
      b_down.reshape(DEPTH, N_EXPERTS, 1, D_MODEL))


def _combine_kernel(pos_ref, pos_next_ref, ys_ref, x_ref, gate_ref, mod_ref, out_ref, buf, pos_smem, sem_idx, sem):
    i = pl.program_id(0)
    slot = i % 2

    def gather_tile(tile_pos_ref, s):
        cp = pltpu.make_async_copy(tile_pos_ref, pos_smem, sem_idx)
        cp.start()
        cp.wait()

        def issue(r, carry):
            for k in range(TOP_K):
                p = pos_smem[k, r]
                pltpu.make_async_copy(ys_ref.at[pl.ds(p, 1)], buf.at[s, k, pl.ds(r, 1)],
                                      sem.at[s, k]).start(priority=k % 2)
            return carry

        lax.fori_loop(0, TM, issue, 0, unroll=ISSUE_UNROLL)

    @pl.when(i == 0)
    def _():
        gather_tile(pos_ref, 0)

    @pl.when(i + 1 < N_TILES)
    def _():
        gather_tile(pos_next_ref, 1 - slot)

    acc = jnp.zeros((TM, D_MODEL), F32)
    for k in range(TOP_K):
        pltpu.make_async_copy(ys_ref.at[pl.ds(0, TM)], buf.at[slot, k], sem.at[slot, k]).wait()
        acc = acc + gate_ref[:, k:k + 1] * buf[slot, k]
    out_ref[...] = x_ref[...] + mod_ref[0, 5:6, :] * acc


def _combine(pos, ys, x_all, gate, mod_tab):
    return pl.pallas_call(
        _combine_kernel,
        grid=(N_TILES,),
        in_specs=[
            pl.BlockSpec((TOPK_ROWS, TM), lambda i: (0, i)),
            pl.BlockSpec((TOPK_ROWS, TM), lambda i: (0, jnp.minimum(i + 1, N_TILES - 1))),
            pl.BlockSpec(memory_space=pl.ANY),
            pl.BlockSpec((TM, D_MODEL), lambda i: (i, 0)),
            pl.BlockSpec((TM, TOPK_ROWS), lambda i: (i, 0)),
            pl.BlockSpec((1, 6, D_MODEL), lambda i: (_mod_row(i), 0, 0)),
        ],
        out_specs=pl.BlockSpec((TM, D_MODEL), lambda i: (i, 0)),
        out_shape=jax.ShapeDtypeStruct((N_TOK, D_MODEL), F32),
        scratch_shapes=[
            pltpu.VMEM((2, TOP_K, TM, D_MODEL), F32),
            pltpu.SMEM((TOPK_ROWS, TM), jnp.int32),
            pltpu.SemaphoreType.DMA,
            pltpu.SemaphoreType.DMA((2, TOP_K)),
        ],
        compiler_params=pltpu.CompilerParams(
            dimension_semantics=("arbitrary",), vmem_limit_bytes=VMEM_LIMIT_BYTES),
        name="moe_combine",
    )(pos, pos, ys, x_all, gate, mod_tab)


def _moe_layer(layer, x_all, mod_tab, g_norm2, w_router, b_router, w_gate_up, b_gate_up, w_down, b_down):
    topi, gate, rank, counts = _router(x_all, g_norm2[layer], mod_tab, w_router[layer], b_router[layer])
    counts = counts[:, 0]
    padded = ((counts + TM - 1) // TM) * TM
    ends = jnp.cumsum(padded)
    starts = ends - padded
    order = jnp.arange(N_EXPERTS, dtype=jnp.int32)
    first_row = jnp.sum(jnp.where(topi[None] == order[:, None, None], starts[:, None, None], 0), axis=0)
    pos = (first_row + rank).astype(jnp.int32)
    gate = gate.T
    n_used = (ends[-1] // TM).astype(jnp.int32)
    later = jnp.where((padded[None, :] > 0) & (order[None, :] > order[:, None]), order[None, :], N_EXPERTS)
    next_expert = jnp.min(later, axis=1)
    next_expert = jnp.where(next_expert == N_EXPERTS, -1, next_expert).astype(jnp.int32)
    tile_start = jnp.arange(R_TILES, dtype=jnp.int32) * TM
    tile_start = jnp.minimum(tile_start, ends[-1] - 1)
    tile_expert = jnp.sum((ends[None, :] <= tile_start[:, None]).astype(jnp.int32), axis=1)
    tile_expert = jnp.minimum(tile_expert, N_EXPERTS - 1).astype(jnp.int32)
    xs = _dispatch(ends.astype(jnp.int32), pos, x_all, g_norm2[layer], mod_tab)
    ys = _ffn(layer, tile_expert, n_used.reshape(1), next_expert, xs, w_gate_up, b_gate_up, w_down, b_down)
    return _combine(pos, ys, x_all, gate, mod_tab)


S5_ROWS = 512
S5_HALF_W = S5_WIDTH // 2
S5_HALF_STATES = (S5_GROUPS // 2) * S5_STATE
S5_COL_CHUNK = 512


def _s5_scan_kernel(u_ref, bmat_ref, cmat_ref, a_ref, h0_ref, y_ref, fin_ref, bu_ref, h_ref, *, bsz, steps):
    d = pl.program_id(0)
    c = pl.program_id(1)
    hs = S5_HALF_STATES

    @pl.when(c == 0)
    def _():
        h_ref[...] = h0_ref[...]

    u = u_ref[...].astype(BF16)
    for hf in range(2):
        bu_ref[...] = jnp.dot(u[:, hf * S5_HALF_W:(hf + 1) * S5_HALF_W], bmat_ref[hf],
                              preferred_element_type=F32)
        for j in range(hs // S5_COL_CHUNK):
            re0 = j * S5_COL_CHUNK
            im0 = hs + j * S5_COL_CHUNK
            ar = jnp.broadcast_to(a_ref[hf, 0:1, re0:re0 + S5_COL_CHUNK], (bsz, S5_COL_CHUNK))
            ai = jnp.broadcast_to(a_ref[hf, 1:2, re0:re0 + S5_COL_CHUNK], (bsz, S5_COL_CHUNK))

            def step(t, carry, re0=re0, im0=im0, ar=ar, ai=ai):
                hr, hi = carry
                te = jnp.where(d == 0, t, steps - 1 - t)
                r0 = pl.multiple_of(te * bsz, bsz)
                br = bu_ref[pl.ds(r0, bsz), re0:re0 + S5_COL_CHUNK]
                bi = bu_ref[pl.ds(r0, bsz), im0:im0 + S5_COL_CHUNK]
                nr = ar * hr - ai * hi + br
                ni = ar * hi + ai * hr + bi
                bu_ref[pl.ds(r0, bsz), re0:re0 + S5_COL_CHUNK] = nr
                bu_ref[pl.ds(r0, bsz), im0:im0 + S5_COL_CHUNK] = ni
                return nr, ni

            hr, hi = lax.fori_loop(
                0, steps, step,
                (h_ref[hf, :, re0:re0 + S5_COL_CHUNK], h_ref[hf, :, im0:im0 + S5_COL_CHUNK]), unroll=4)
            h_ref[hf, :, re0:re0 + S5_COL_CHUNK] = hr
            h_ref[hf, :, im0:im0 + S5_COL_CHUNK] = hi
        y_ref[:, hf * S5_HALF_W:(hf + 1) * S5_HALF_W] = jnp.dot(
            bu_ref[...].astype(BF16), cmat_ref[hf], preferred_element_type=F32)

    @pl.when(c == pl.num_programs(1) - 1)
    def _():
        fin_ref[...] = h_ref[...]


def _s5_scan(u_tm, bmat, cmat, acoef, h0, bsz):
    rows = u_tm.shape[0]
    steps = S5_ROWS // bsz
    n_chunks = rows // S5_ROWS

    def chunk_map(d, c):
        return jnp.where(d == 0, c, n_chunks - 1 - c)

    return pl.pallas_call(
        functools.partial(_s5_scan_kernel, bsz=bsz, steps=steps),
        grid=(2, n_chunks),
        in_specs=[
            pl.BlockSpec((S5_ROWS, S5_WIDTH), lambda d, c: (chunk_map(d, c), 0)),
            pl.BlockSpec((None, 2, S5_HALF_W, 2 * S5_HALF_STATES), lambda d, c: (d, 0, 0, 0)),
            pl.BlockSpec((None, 2, 2 * S5_HALF_STATES, S5_HALF_W), lambda d, c: (d, 0, 0, 0)),
            pl.BlockSpec((None, 2, 2, S5_HALF_STATES), lambda d, c: (d, 0, 0, 0)),
            pl.BlockSpec((None, 2, bsz, 2 * S5_HALF_STATES), lambda d, c: (d, 0, 0, 0)),
        ],
        out_specs=[
            pl.BlockSpec((None, S5_ROWS, S5_WIDTH), lambda d, c: (d, chunk_map(d, c), 0)),
            pl.BlockSpec((None, 2, bsz, 2 * S5_HALF_STATES), lambda d, c: (d, 0, 0, 0)),
        ],
        out_shape=[
            jax.ShapeDtypeStruct((2, rows, S5_WIDTH), F32),
            jax.ShapeDtypeStruct((2, 2, bsz, 2 * S5_HALF_STATES), F32),
        ],
        scratch_shapes=[
            pltpu.VMEM((S5_ROWS, 2 * S5_HALF_STATES), F32),
            pltpu.VMEM((2, bsz, 2 * S5_HALF_STATES), F32),
        ],
        compiler_params=pltpu.CompilerParams(
            dimension_semantics=("arbitrary", "arbitrary"), vmem_limit_bytes=VMEM_LIMIT_BYTES),
        name="s5_scan",
    )(u_tm, bmat, cmat, acoef, h0)


def _s5_discretize(lam_re, lam_im, log_dt, b_re, b_im, c_re, c_im):
    eye = jnp.eye(S5_GROUPS // 2, dtype=F32)
    bmats, cmats, acoefs = [], [], []
    for dr in range(2):
        lr = jnp.minimum(lam_re[dr].astype(F32), -1e-4)
        li = lam_im[dr].astype(F32)
        dt = jnp.exp(log_dt[dr].astype(F32))[:, None]
        mag = jnp.exp(lr * dt)
        ar, ai = mag * jnp.cos(li * dt), mag * jnp.sin(li * dt)
        den = lr * lr + li * li
        fr = ((ar - 1.0) * lr + ai * li) / den
        fi = (ai * lr - (ar - 1.0) * li) / den
        br_ = b_re[dr].astype(F32)
        bi_ = b_im[dr].astype(F32)
        bbr = fr[..., None] * br_ - fi[..., None] * bi_
        bbi = fr[..., None] * bi_ + fi[..., None] * br_
        bm, cm, am = [], [], []
        for hf in range(2):
            g = slice(hf * S5_GROUPS // 2, (hf + 1) * S5_GROUPS // 2)

            def bdiag_in(w):
                return jnp.einsum('ab,aph->ahbp', eye, w[g]).reshape(S5_HALF_W, S5_HALF_STATES)

            def bdiag_out(w):
                return jnp.einsum('ab,ahp->apbh', eye, w[g]).reshape(S5_HALF_STATES, S5_HALF_W)

            bm.append(jnp.concatenate([bdiag_in(bbr), bdiag_in(bbi)], axis=1))
            cm.append(jnp.concatenate([bdiag_out(c_re[dr].astype(F32)),
                                       -bdiag_out(c_im[dr].astype(F32))], axis=0))
            am.append(jnp.stack([ar[g].reshape(-1), ai[g].reshape(-1)]))
        bmats.append(jnp.stack(bm))
        cmats.append(jnp.stack(cm))
        acoefs.append(jnp.stack(am))
    return jnp.stack(bmats).astype(BF16), jnp.stack(cmats).astype(BF16), jnp.stack(acoefs)


def _s5_state_to_kernel(h0):
    bsz = h0.shape[0]
    h = h0.astype(F32).reshape(bsz, 2, 2, 2, S5_HALF_STATES)
    return h.transpose(1, 3, 0, 2, 4).reshape(2, 2, bsz, 2 * S5_HALF_STATES)


def _s5_state_from_kernel(fin):
    bsz = fin.shape[2]
    h = fin.reshape(2, 2, bsz, 2, S5_HALF_STATES).transpose(2, 0, 3, 1, 4)
    return h.reshape(bsz, 2, 2, S5_GROUPS, S5_STATE)


TQ = 256
MLA_HEADS_PER_STEP = 8
DIFF_HEADS_PER_STEP = 2
ATT_TQ = 1024
HEAD_LANES = 128
MLA_SCALE = (MLA_NOPE + MLA_ROPE) ** -0.5
DIFF_SCALE = DIFF_HD ** -0.5


def _dot(a, b):
    return jnp.dot(a, b, preferred_element_type=F32)


def _dot_t(a, b):
    return lax.dot_general(a, b, (((1,), (1,)), ((), ())), preferred_element_type=F32)


def _rms_rows(x, g):
    return x * lax.rsqrt(jnp.mean(x * x, axis=-1, keepdims=True) + EPS) * g


def _group(bsz, length, row0, mod_base, mod_stride):
    return dict(bsz=bsz, length=length, nt=length // TQ, tile0=row0 // TQ,
                mod_base=mod_base, mod_stride=mod_stride)


def _params(n_axes):
    return pltpu.CompilerParams(dimension_semantics=("arbitrary",) * n_axes,
                                vmem_limit_bytes=VMEM_LIMIT_BYTES)


def _axial_rope(length, dim):
    rows = length // GRID_W
    row = jnp.repeat(jnp.arange(rows, dtype=F32), GRID_W)
    col = jnp.tile(jnp.arange(GRID_W, dtype=F32), rows)
    n_freq = dim // 4
    inv = ROPE_THETA ** (-jnp.arange(n_freq, dtype=F32) / n_freq)
    ang = jnp.concatenate([row[:, None] * inv, col[:, None] * inv], axis=-1)
    return jnp.cos(ang), jnp.sin(ang)


def _rope_tables(length, dim, lead, reps):
    cos, sin = _axial_rope(length, dim)
    cos_r = jnp.repeat(cos, 2, axis=-1)
    sin_r = jnp.repeat(sin, 2, axis=-1) * jnp.tile(jnp.array([-1.0, 1.0], F32), dim // 2)
    part = HEAD_LANES // reps
    pad = ((0, 0), (lead, part - lead - dim))
    cos_t = jnp.tile(jnp.pad(cos_r, pad, constant_values=1.0), (1, reps))
    sin_t = jnp.tile(jnp.pad(sin_r, pad), (1, reps))
    return cos_t, sin_t


def _swap_pairs(w):
    return w[:, jnp.arange(w.shape[1]) ^ 1]


def _even_in_kernel(*refs, rope):
    if rope:
        (x_ref, g_ref, mod_ref, win_ref, gq_ref, wuq_ref, gkv_ref, cos_ref, sin_ref,
         u_ref, q_ref, ckv_ref, kr_ref) = refs
    else:
        (x_ref, g_ref, mod_ref, win_ref, gq_ref, wuq_ref, gkv_ref,
         u_ref, q_ref, ckv_ref, kr_ref) = refs
    o1 = S5_WIDTH
    o2 = o1 + MLA_Q_LORA
    o3 = o2 + MLA_KV_LORA
    o4 = o3 + HEAD_LANES
    n_in = o4 + HEAD_LANES if rope else o4
    n_q = MLA_HEADS * HEAD_LANES
    h = _rms_rows(x_ref[...], g_ref[...]) * (1.0 + mod_ref[0, 1:2, :]) + mod_ref[0, 0:1, :]
    z = _dot(h.astype(BF16), win_ref[:, :n_in])
    u_ref[...] = z[:, :o1]
    ckv_ref[...] = _rms_rows(z[:, o2:o3], gkv_ref[...])
    qn = _rms_rows(z[:, o1:o2], gq_ref[...]).astype(BF16)
    if rope:
        q2 = _dot(qn, wuq_ref[...])
        cos = cos_ref[...]
        sin = sin_ref[...]
        for hd in range(MLA_HEADS):
            a = hd * HEAD_LANES
            q_ref[:, a:a + HEAD_LANES] = ((q2[:, a:a + HEAD_LANES] * cos
                                           + q2[:, n_q + a:n_q + a + HEAD_LANES] * sin) * MLA_SCALE
                                          ).astype(q_ref.dtype)
        kr_ref[...] = z[:, o3:o4] * cos + z[:, o4:o4 + HEAD_LANES] * sin
    else:
        q_ref[...] = (_dot(qn, wuq_ref[:, :n_q]) * MLA_SCALE).astype(q_ref.dtype)
        kr_ref[...] = z[:, o3:o4]


def _even_in(x_all, grp, g1, mod_tab, win_aug, g_q, wuq2, g_kv, tables):
    bsz, length, nt = grp["bsz"], grp["length"], grp["nt"]
    rope = tables is not None
    rows = bsz * length

    def tok(b, t):
        return (b * nt + t, 0)

    in_specs = [
        pl.BlockSpec((TQ, D_MODEL), lambda b, t: (grp["tile0"] + b * nt + t, 0)),
        pl.BlockSpec((1, D_MODEL), lambda b, t: (0, 0)),
        pl.BlockSpec((1, 6, D_MODEL), lambda b, t: (grp["mod_base"] + b * grp["mod_stride"], 0, 0)),
        pl.BlockSpec(win_aug.shape, lambda b, t: (0, 0)),
        pl.BlockSpec((1, MLA_Q_LORA), lambda b, t: (0, 0)),
        pl.BlockSpec(wuq2.shape, lambda b, t: (0, 0)),
        pl.BlockSpec((1, MLA_KV_LORA), lambda b, t: (0, 0)),
    ]
    args = [x_all, g1.reshape(1, D_MODEL), mod_tab, win_aug, g_q.reshape(1, -1), wuq2, g_kv.reshape(1, -1)]
    if rope:
        in_specs += [pl.BlockSpec((TQ, HEAD_LANES), lambda b, t: (t, 0))] * 2
        args += list(tables)
    return pl.pallas_call(
        functools.partial(_even_in_kernel, rope=rope),
        grid=(bsz, nt),
        in_specs=in_specs,
        out_specs=[
            pl.BlockSpec((TQ, S5_WIDTH), lambda b, t: (t, b)),
            pl.BlockSpec((TQ, MLA_HEADS * HEAD_LANES), tok),
            pl.BlockSpec((TQ, MLA_KV_LORA), tok),
            pl.BlockSpec((TQ, HEAD_LANES), tok),
        ],
        out_shape=[
            jax.ShapeDtypeStruct((length, bsz * S5_WIDTH), F32),
            jax.ShapeDtypeStruct((rows, MLA_HEADS * HEAD_LANES), BF16),
            jax.ShapeDtypeStruct((rows, MLA_KV_LORA), F32),
            jax.ShapeDtypeStruct((rows, HEAD_LANES), F32),
        ],
        compiler_params=_params(2),
        name="even_in",
    )(*args)


def _kv_expand_kernel(x_ref, kr_ref, wk_ref, wv_ref, k_ref, v_ref):
    x = x_ref[...].astype(BF16)
    k = _dot(x, wk_ref[...])
    kr = kr_ref[...]
    for hd in range(MLA_HEADS):
        a = hd * HEAD_LANES
        k_ref[:, a:a + HEAD_LANES] = (k[:, a:a + HEAD_LANES] + kr).astype(k_ref.dtype)
    v_ref[...] = _dot(x, wv_ref[...]).astype(v_ref.dtype)


def _kv_expand(ckv, kr, wk, wv):
    rows = ckv.shape[0]
    tm = 512
    width = MLA_HEADS * HEAD_LANES
    return pl.pallas_call(
        _kv_expand_kernel,
        grid=(rows // tm,),
        in_specs=[pl.BlockSpec((tm, MLA_KV_LORA), lambda i: (i, 0)),
                  pl.BlockSpec((tm, HEAD_LANES), lambda i: (i, 0)),
                  pl.BlockSpec(wk.shape, lambda i: (0, 0)),
                  pl.BlockSpec(wv.shape, lambda i: (0, 0))],
        out_specs=[pl.BlockSpec((tm, width), lambda i: (i, 0))] * 2,
        out_shape=[jax.ShapeDtypeStruct((rows, width), BF16)] * 2,
        compiler_params=_params(1),
        name="kv_expand",
    )(ckv, kr, wk, wv)


def _exp_parts(scores):
    m = functools.reduce(jnp.maximum, [jnp.max(s, axis=-1, keepdims=True) for s in scores])
    es = [jnp.exp(s - m) for s in scores]
    den = functools.reduce(jnp.add, [jnp.sum(e, axis=-1, keepdims=True) for e in es])
    return es, 1.0 / den


def _weighted_values(es, vs):
    o = _dot(es[0].astype(BF16), vs[0])
    for e, v in zip(es[1:], vs[1:]):
        o = o + _dot(e.astype(BF16), v)
    return o


def _mla_attn_kernel(*refs, n_seg):
    q_ref, o_ref = refs[0], refs[-1]
    for hd in range(MLA_HEADS_PER_STEP):
        lanes = slice(hd * HEAD_LANES, (hd + 1) * HEAD_LANES)
        q = q_ref[:, lanes]
        ks = [refs[1 + 2 * s][:, lanes] for s in range(n_seg)]
        vs = [refs[2 + 2 * s][:, lanes] for s in range(n_seg)]
        es, inv = _exp_parts([_dot_t(q, k) for k in ks])
        o_ref[:, lanes] = (_weighted_values(es, vs) * inv).astype(o_ref.dtype)


def _mla_attn(q, segs, bsz, length):
    tq = TQ
    nq = length // tq
    width = MLA_HEADS_PER_STEP * HEAD_LANES
    in_specs = [pl.BlockSpec((tq, width), lambda b, h, i: (b * nq + i, h))]
    args = [q]
    for k, v, lk in segs:
        in_specs += [pl.BlockSpec((lk, width), lambda b, h, i: (b, h))] * 2
        args += [k, v]
    return pl.pallas_call(
        functools.partial(_mla_attn_kernel, n_seg=len(segs)),
        grid=(bsz, MLA_HEADS // MLA_HEADS_PER_STEP, nq),
        in_specs=in_specs,
        out_specs=pl.BlockSpec((tq, width), lambda b, h, i: (b * nq + i, h)),
        out_shape=jax.ShapeDtypeStruct((bsz * length, MLA_HEADS * HEAD_LANES), BF16),
        compiler_params=_params(3),
        name="mla_attn",
    )(*args)


def _even_out_kernel(u_ref, y_ref, o_ref, x_ref, mod_ref, d_ref, wglu_ref, bglu_ref, ws5_ref, wmla_ref,
                     out_ref):
    y =jax.nn.gelu(d_ref[...] * u_ref[...] + y_ref[0] + y_ref[1])
    s5 = y * jax.nn.sigmoid(_dot(y.astype(BF16), wglu_ref[...]) + bglu_ref[...])
    mix = _dot(s5.astype(BF16), ws5_ref[...]) + _dot(o_ref[...].astype(BF16), wmla_ref[...])
    out_ref[...] = x_ref[...] + mod_ref[0, 2:3, :] * mix


def _even_out(x_all, grp, mod_tab, u_tm, y_dir, o_mla, d_skip, w_glu, b_glu, w_out_s5, w_out_mla):
    bsz, length, nt = grp["bsz"], grp["length"], grp["nt"]

    def xrow(b, t):
        return (grp["tile0"] + b * nt + t, 0)

    full = lambda b, t: (0, 0)
    return pl.pallas_call(
        _even_out_kernel,
        grid=(bsz, nt),
        in_specs=[
            pl.BlockSpec((TQ, S5_WIDTH), lambda b, t: (t, b)),
            pl.BlockSpec((2, TQ, S5_WIDTH), lambda b, t: (0, t, b)),
            pl.BlockSpec((TQ, MLA_HEADS * HEAD_LANES), lambda b, t: (b * nt + t, 0)),
            pl.BlockSpec((TQ, D_MODEL), xrow),
            pl.BlockSpec((1, 6, D_MODEL), lambda b, t: (grp["mod_base"] + b * grp["mod_stride"], 0, 0)),
            pl.BlockSpec((1, S5_WIDTH), full),
            pl.BlockSpec(w_glu.shape, full),
            pl.BlockSpec((1, S5_WIDTH), full),
            pl.BlockSpec(w_out_s5.shape, full),
            pl.BlockSpec(w_out_mla.shape, full),
        ],
        out_specs=pl.BlockSpec((TQ, D_MODEL), xrow),
        out_shape=jax.ShapeDtypeStruct(x_all.shape, F32),
        input_output_aliases={3: 0},
        compiler_params=_params(2),
        name="even_out",
    )(u_tm, y_dir.reshape(2, length, bsz * S5_WIDTH), o_mla, x_all, mod_tab,
      d_skip.reshape(1, S5_WIDTH), w_glu, b_glu.reshape(1, S5_WIDTH), w_out_s5, w_out_mla)


def _odd_in_kernel(*refs, rope):
    if rope:
        x_ref, g_ref, mod_ref, w_ref, cos_ref, sin_ref, q_ref, k_ref, v_ref = refs
    else:
        x_ref, g_ref, mod_ref, w_ref, q_ref, k_ref, v_ref = refs
    w3 = 3 * DIFF_WIDTH
    h = _rms_rows(x_ref[...], g_ref[...]) * (1.0 + mod_ref[0, 1:2, :]) + mod_ref[0, 0:1, :]
    z = _dot(h.astype(BF16), w_ref[...] if rope else w_ref[:, :w3])
    v_ref[...] = z[:, 2 * DIFF_WIDTH:w3].astype(v_ref.dtype)
    if rope:
        cos = cos_ref[...]
        sin = sin_ref[...]
        for hd in range(DIFF_HEADS):
            a = hd * HEAD_LANES
            q_ref[:, a:a + HEAD_LANES] = ((z[:, a:a + HEAD_LANES] * cos
                                           + z[:, w3 + a:w3 + a + HEAD_LANES] * sin) * DIFF_SCALE
                                          ).astype(q_ref.dtype)
            b = DIFF_WIDTH + a
            k_ref[:, a:a + HEAD_LANES] = (z[:, b:b + HEAD_LANES] * cos
                                          + z[:, w3 + b:w3 + b + HEAD_LANES] * sin).astype(k_ref.dtype)
    else:
        q_ref[...] = (z[:, :DIFF_WIDTH] * DIFF_SCALE).astype(q_ref.dtype)
        k_ref[...] = z[:, DIFF_WIDTH:2 * DIFF_WIDTH].astype(k_ref.dtype)


def _odd_in(x_all, grp, g1, mod_tab, w_aug, tables, kv_dtype):
    bsz, length, nt = grp["bsz"], grp["length"], grp["nt"]
    rope = tables is not None
    rows = bsz * length

    def tok(b, t):
        return (b * nt + t, 0)

    in_specs = [
        pl.BlockSpec((TQ, D_MODEL), lambda b, t: (grp["tile0"] + b * nt + t, 0)),
        pl.BlockSpec((1, D_MODEL), lambda b, t: (0, 0)),
        pl.BlockSpec((1, 6, D_MODEL), lambda b, t: (grp["mod_base"] + b * grp["mod_stride"], 0, 0)),
        pl.BlockSpec(w_aug.shape, lambda b, t: (0, 0)),
    ]
    args = [x_all, g1.reshape(1, D_MODEL), mod_tab, w_aug]
    if rope:
        in_specs += [pl.BlockSpec((TQ, HEAD_LANES), lambda b, t: (t, 0))] * 2
        args += list(tables)
    return pl.pallas_call(
        functools.partial(_odd_in_kernel, rope=rope),
        grid=(bsz, nt),
        in_specs=in_specs,
        out_specs=[pl.BlockSpec((TQ, DIFF_WIDTH), tok)] * 3,
        out_shape=[
            jax.ShapeDtypeStruct((rows, DIFF_WIDTH), BF16),
            jax.ShapeDtypeStruct((rows, DIFF_WIDTH), kv_dtype),
            jax.ShapeDtypeStruct((rows, DIFF_WIDTH), kv_dtype),
        ],
        compiler_params=_params(2),
        name="odd_in",
    )(*args)


def _diff_attn_kernel(*refs, n_seg, post_scale):
    lam_ref, q_ref = refs[0], refs[1]
    g_ref, o_ref = refs[-2], refs[-1]
    for hd in range(DIFF_HEADS_PER_STEP):
        lanes = slice(hd * HEAD_LANES, (hd + 1) * HEAD_LANES)
        q = q_ref[:, lanes].astype(F32)
        lane = lax.broadcasted_iota(jnp.int32, q.shape, 1)
        q0 = jnp.where(lane < DIFF_HD, q, 0.0).astype(BF16)
        q1 = jnp.where(lane >= DIFF_HD, q, 0.0).astype(BF16)
        ks = [refs[2 + 2 * s][:, lanes].astype(BF16) for s in range(n_seg)]
        vs = [refs[3 + 2 * s][:, lanes].astype(BF16) for s in range(n_seg)]
        e0, inv0 = _exp_parts([_dot_t(q0, k) for k in ks])
        e1, inv1 = _exp_parts([_dot_t(q1, k) for k in ks])
        o = _weighted_values(e0, vs) * inv0 - lam_ref[0] * (_weighted_values(e1, vs) * inv1)
        o_ref[:, lanes] = (_rms_rows(o, g_ref[...]) * post_scale).astype(o_ref.dtype)


def _diff_attn(lam_full, q, segs, g_sub, post_scale, bsz, length):
    tq = min(ATT_TQ, length)
    nq = length // tq
    width = DIFF_HEADS_PER_STEP * HEAD_LANES
    in_specs = [pl.BlockSpec(memory_space=pltpu.SMEM),
                pl.BlockSpec((tq, width), lambda b, h, i: (b * nq + i, h))]
    args = [lam_full.reshape(1).astype(F32), q]
    for k, v, lk in segs:
        in_specs += [pl.BlockSpec((lk, width), lambda b, h, i: (b, h))] * 2
        args += [k, v]
    in_specs.append(pl.BlockSpec((1, HEAD_LANES), lambda b, h, i: (0, 0)))
    args.append(g_sub.reshape(1, HEAD_LANES))
    return pl.pallas_call(
        functools.partial(_diff_attn_kernel, n_seg=len(segs), post_scale=post_scale),
        grid=(bsz, DIFF_HEADS // DIFF_HEADS_PER_STEP, nq),
        in_specs=in_specs,
        out_specs=pl.BlockSpec((tq, width), lambda b, h, i: (b * nq + i, h)),
        out_shape=jax.ShapeDtypeStruct((bsz * length, DIFF_WIDTH), BF16),
        compiler_params=_params(3),
        name="diff_attn",
    )(*args)


def _odd_out_kernel(o_ref, x_ref, mod_ref, w_ref, out_ref):
    out_ref[...] = x_ref[...] + mod_ref[0, 2:3, :] * _dot(o_ref[...], w_ref[...])


def _odd_out(x_all, grp, mod_tab, o, w_out):
    bsz, nt = grp["bsz"], grp["nt"]

    def xrow(b, t):
        return (grp["tile0"] + b * nt + t, 0)

    return pl.pallas_call(
        _odd_out_kernel,
        grid=(bsz, nt),
        in_specs=[
            pl.BlockSpec((TQ, DIFF_WIDTH), lambda b, t: (b * nt + t, 0)),
            pl.BlockSpec((TQ, D_MODEL), xrow),
            pl.BlockSpec((1, 6, D_MODEL), lambda b, t: (grp["mod_base"] + b * grp["mod_stride"], 0, 0)),
            pl.BlockSpec(w_out.shape, lambda b, t: (0, 0)),
        ],
        out_specs=pl.BlockSpec((TQ, D_MODEL), xrow),
        out_shape=jax.ShapeDtypeStruct(x_all.shape, F32),
        input_output_aliases={1: 0},
        compiler_params=_params(2),
        name="odd_out",
    )(o, x_all, mod_tab, w_out)


def _final_norm_kernel(x_ref, g_ref, o_ref):
    o_ref[...] = _rms_rows(x_ref[...], g_ref[...])


def _final_norm(x_all, g, row0, rows):
    tm = 512
    return pl.pallas_call(
        _final_norm_kernel,
        grid=(rows // tm,),
        in_specs=[pl.BlockSpec((tm, D_MODEL), lambda i: (row0 // tm + i, 0)),
                  pl.BlockSpec((1, D_MODEL), lambda i: (0, 0))],
        out_specs=pl.BlockSpec((tm, D_MODEL), lambda i: (i, 0)),
        out_shape=jax.ShapeDtypeStruct((rows, D_MODEL), F32),
        compiler_params=_params(1),
        name="final_norm",
    )(x_all, g.reshape(1, D_MODEL))


MOD_ROWS = 16
MOD_COL_TILE = 1536


def _adaln_kernel(c_ref, w_ref, b_ref, o_ref):
    cond = jax.nn.silu(c_ref[...])
    o_ref[...] = jnp.dot(cond, w_ref[...], preferred_element_type=F32,
                         precision=lax.Precision.HIGHEST) + b_ref[...]


def _adaln(cond_rows, w_mod, b_mod):
    return pl.pallas_call(
        _adaln_kernel,
        grid=(DEPTH, 6 * D_MODEL // MOD_COL_TILE),
        in_specs=[pl.BlockSpec((MOD_ROWS, D_MODEL), lambda l, j: (0, 0)),
                  pl.BlockSpec((None, D_MODEL, MOD_COL_TILE), lambda l, j: (l, 0, j)),
                  pl.BlockSpec((None, 1, MOD_COL_TILE), lambda l, j: (l, 0, j))],
        out_specs=pl.BlockSpec((None, MOD_ROWS, MOD_COL_TILE), lambda l, j: (l, 0, j)),
        out_shape=jax.ShapeDtypeStruct((DEPTH, MOD_ROWS, 6 * D_MODEL), F32),
        compiler_params=_params(2),
        name="adaln",
    )(cond_rows, w_mod, b_mod.reshape(DEPTH, 1, 6 * D_MODEL))


def _pad_head_lanes(x, lead):
    return jnp.pad(x, ((0, 0), (lead, HEAD_LANES - lead - x.shape[1])))


def _even_weights(w_in, w_out, w_uq, w_ukv, w_glu):
    o3 = S5_WIDTH + MLA_Q_LORA + MLA_KV_LORA
    w_kr = w_in[:, o3:]
    win_aug = jnp.concatenate(
        [w_in[:, :o3], _pad_head_lanes(w_kr, MLA_NOPE), _pad_head_lanes(_swap_pairs(w_kr), MLA_NOPE)], axis=1)
    dq = MLA_NOPE + MLA_ROPE
    plain, swapped = [], []
    for hd in range(MLA_HEADS):
        wn = w_uq[:, hd * dq:hd * dq + MLA_NOPE]
        wr = w_uq[:, hd * dq + MLA_NOPE:(hd + 1) * dq]
        plain.append(jnp.pad(jnp.concatenate([wn, wr], axis=1), ((0, 0), (0, HEAD_LANES - dq))))
        swapped.append(_pad_head_lanes(_swap_pairs(wr), MLA_NOPE))
    wuq2 = jnp.concatenate(plain + swapped, axis=1)
    w_mla = w_out[S5_WIDTH:].reshape(MLA_HEADS, MLA_V, D_MODEL)
    w_out_mla = jnp.pad(w_mla, ((0, 0), (0, HEAD_LANES - MLA_V), (0, 0))).reshape(MLA_HEADS * HEAD_LANES, D_MODEL)
    w_kv = w_ukv.reshape(MLA_KV_LORA, MLA_HEADS, MLA_NOPE + MLA_V)
    wk = jnp.pad(w_kv[:, :, :MLA_NOPE], ((0, 0), (0, 0), (0, HEAD_LANES - MLA_NOPE)))
    wv = jnp.pad(w_kv[:, :, MLA_NOPE:], ((0, 0), (0, 0), (0, HEAD_LANES - MLA_V)))
    w_kv = (wk.reshape(MLA_KV_LORA, -1).astype(BF16), wv.reshape(MLA_KV_LORA, -1).astype(BF16))
    return (win_aug.astype(BF16), wuq2.astype(BF16), w_kv, w_glu.astype(BF16),
            w_out[:S5_WIDTH].astype(BF16), w_out_mla.astype(BF16))


def _even_layer(x_all, grp, g1, mod_tab, ew, s5m, g_q, g_kv, d_skip, b_glu, h0, ctx, tables):
    win_aug, wuq2, w_kv, w_glu, w_out_s5, w_out_mla = ew
    bmat, cmat, acoef = s5m
    bsz, length = grp["bsz"], grp["length"]
    u_tm, q, ckv, kr = _even_in(x_all, grp, g1, mod_tab, win_aug, g_q, wuq2, g_kv, tables)
    y_dir, fin = _s5_scan(u_tm.reshape(length * bsz, S5_WIDTH), bmat, cmat, acoef, h0, bsz)
    segs = [(*_kv_expand(ckv, kr, *w_kv), length)]
    if ctx is not None:
        segs.append((*_kv_expand(*ctx, *w_kv), PAST_LEN))
    o_mla = _mla_attn(q, segs, bsz, length)
    x_all = _even_out(x_all, grp, mod_tab, u_tm, y_dir, o_mla, d_skip, w_glu, b_glu, w_out_s5, w_out_mla)
    return x_all, fin, ckv, kr


def _odd_layer(x_all, grp, g1, mod_tab, w_aug, w_out, lam_full, g_sub, post_scale, ctx, tables, kv_dtype):
    bsz, length = grp["bsz"], grp["length"]
    q, k, v = _odd_in(x_all, grp, g1, mod_tab, w_aug, tables, kv_dtype)
    segs = [(k, v, length)]
    if ctx is not None:
        segs.append((ctx[0], ctx[1], PAST_LEN))
    o = _diff_attn(lam_full, q, segs, g_sub, post_scale, bsz, length)
    return _odd_out(x_all, grp, mod_tab, o, w_out), k, v


def kernel(x_prompt, x_sample, state_s5, cache_mla, cache_diff_k, cache_diff_v, c, c_ctx, w_mod, b_mod, g_norm1, g_norm2, g_final, w_in_even, w_out_even, s5_lam_re, s5_lam_im, s5_log_dt, s5_b_re, s5_b_im, s5_c_re, s5_c_im, s5_d, s5_w_glu, s5_b_glu, mla_g_q, mla_w_uq, mla_g_kv, mla_w_ukv, w_in_odd, w_out_odd, diff_lam, diff_g_sub, w_router, b_router, w_gate_up, b_gate_up, w_down, b_down):
    tab_mla = _rope_tables(DEC_SEQ, MLA_ROPE, MLA_NOPE, 1)
    tab_diff = _rope_tables(DEC_SEQ, DIFF_HD, 0, 2)
    grp_p = _group(BATCH, SEQ, 0, 0, 0)
    grp_s = _group(DEC_BATCH, DEC_SEQ, N_PROMPT, 1, 1)
    x_all = jnp.concatenate([x_prompt.reshape(N_PROMPT, D_MODEL), x_sample.reshape(N_SAMPLE, D_MODEL)], axis=0)
    cond = jnp.concatenate([c_ctx[None], c, jnp.zeros((MOD_ROWS - 1 - DEC_BATCH, D_MODEL), c.dtype)], axis=0)
    mods = _adaln(cond.astype(F32), w_mod, b_mod)
    new_s5, new_mla, new_k, new_v = [], [], [], []
    for l in range(DEPTH):
        mod_tab = mods[l].reshape(MOD_ROWS, 6, D_MODEL)
        i = l // 2
        if l % 2 == 0:
            ew = _even_weights(w_in_even[i], w_out_even[i], mla_w_uq[i], mla_w_ukv[i], s5_w_glu[i])
            s5m = _s5_discretize(s5_lam_re[i], s5_lam_im[i], s5_log_dt[i], s5_b_re[i], s5_b_im[i],
                                 s5_c_re[i], s5_c_im[i])
            common = (ew, s5m, mla_g_q[i], mla_g_kv[i], s5_d[i], s5_b_glu[i])
            h0_p = jnp.zeros((2, 2, BATCH, 2 * S5_HALF_STATES), F32)
            x_all, fin, ckv, kr = _even_layer(x_all, grp_p, g_norm1[l], mod_tab, *common, h0_p, None, None)
            new_s5.append(_s5_state_from_kernel(fin))
            new_mla.append(jnp.concatenate([ckv, kr[:, MLA_NOPE:MLA_NOPE + MLA_ROPE]], axis=1)
                           .reshape(BATCH, SEQ, MLA_KV_LORA + MLA_ROPE))
            lat_ctx = cache_mla[:, i].astype(F32).reshape(DEC_BATCH * PAST_LEN, MLA_KV_LORA + MLA_ROPE)
            ctx = (lat_ctx[:, :MLA_KV_LORA], _pad_head_lanes(lat_ctx[:, MLA_KV_LORA:], MLA_NOPE))
            x_all, _, _, _ = _even_layer(x_all, grp_s, g_norm1[l], mod_tab, *common,
                                         _s5_state_to_kernel(state_s5[:, i]), ctx, tab_mla)
        else:
            lam_init = 0.8 - 0.6 * math.exp(-0.3 * l)
            lamf = diff_lam[i].astype(F32)
            lam_full = jnp.exp(jnp.sum(lamf[0] * lamf[1])) - jnp.exp(jnp.sum(lamf[2] * lamf[3])) + lam_init
            w_qk = w_in_odd[i][:, :2 * DIFF_WIDTH]
            w_aug = jnp.concatenate([w_in_odd[i], _swap_pairs(w_qk)], axis=1).astype(BF16)
            w_out = w_out_odd[i].astype(BF16)
            odd = (w_aug, w_out, lam_full, diff_g_sub[i], 1.0 - lam_init)
            x_all, kp, vp = _odd_layer(x_all, grp_p, g_norm1[l], mod_tab, *odd, None, None, F32)
            new_k.append(kp.reshape(BATCH, SEQ, DIFF_HEADS, 2, DIFF_HD))
            new_v.append(vp.reshape(BATCH, SEQ, DIFF_HEADS, 2 * DIFF_HD))
            ctx = (cache_diff_k[:, i].reshape(DEC_BATCH * PAST_LEN, DIFF_WIDTH).astype(BF16),
                   cache_diff_v[:, i].reshape(DEC_BATCH * PAST_LEN, DIFF_WIDTH).astype(BF16))
            x_all, _, _ = _odd_layer(x_all, grp_s, g_norm1[l], mod_tab, *odd, ctx, tab_diff, BF16)
        x_all = _moe_layer(l, x_all, mod_tab, g_norm2, w_router, b_router,
                           w_gate_up, b_gate_up, w_down, b_down)
    y_prompt = _final_norm(x_all, g_final, 0, N_PROMPT)
    y_sample = _final_norm(x_all, g_final, N_PROMPT, N_SAMPLE)
    return (y_prompt.reshape(BATCH, SEQ, D_MODEL), y_sample.reshape(DEC_BATCH, DEC_SEQ, D_MODEL),
            jnp.stack(new_s5, axis=1), jnp.stack(new_mla, axis=1),
            jnp.stack(new_k, axis=1), jnp.stack(new_v, axis=1))
```

```python
import functools
import math

import jax
import jax.numpy as jnp
from jax import lax
from jax.experimental import pallas as pl
from jax.experimental.pallas import tpu as pltpu

D_MODEL = 1024
BATCH = 16
SEQ = 256
DEPTH = 4
DEC_BATCH = 8
DEC_SEQ = 1024
PAST_LEN = 512
GRID_W = 64
N_EVEN = (DEPTH + 1) // 2
N_ODD = DEPTH // 2
S5_WIDTH = D_MODEL // 2
S5_GROUP = 16
S5_GROUPS = S5_WIDTH // S5_GROUP
S5_STATE = 64
MLA_HEADS = 8
MLA_NOPE = 64
MLA_ROPE = 32
MLA_V = 64
MLA_Q_LORA = D_MODEL // 4
MLA_KV_LORA = D_MODEL // 8
MLA_WIDTH = MLA_HEADS * MLA_V
EVEN_IN = S5_WIDTH + MLA_Q_LORA + MLA_KV_LORA + MLA_ROPE
EVEN_OUT = S5_WIDTH + MLA_WIDTH
DIFF_HEADS = 8
DIFF_HD = D_MODEL // (2 * DIFF_HEADS)
DIFF_WIDTH = DIFF_HEADS * 2 * DIFF_HD
N_EXPERTS = 32
TOP_K = 4
D_FF = D_MODEL
SWIGLU_LIMIT = 7.0
SWIGLU_ALPHA = 1.702
ROPE_THETA = 10000.0
Q_BLOCK = 128
EPS = 1e-6

N_PROMPT = BATCH * SEQ
N_SAMPLE = DEC_BATCH * DEC_SEQ
N_TOK = N_PROMPT + N_SAMPLE

LANES = 128
VMEM_LIMIT_BYTES = 56 * 1024 * 1024

TM = 256
N_TILES = N_TOK // TM
R_TILES = N_TOK * TOP_K // TM + N_EXPERTS
R_MAX = R_TILES * TM
TOPK_ROWS = 8

F32 = jnp.float32
BF16 = jnp.bfloat16


def _mod_row(i):
    t0 = i * TM
    return jnp.where(t0 < N_PROMPT, 0, 1 + (t0 - N_PROMPT) // DEC_SEQ)


def _moe_input(x_ref, g_ref, mod_ref):
    x = x_ref[...]
    ms = jnp.mean(x * x, axis=-1, keepdims=True)
    y = x * lax.rsqrt(ms + EPS) * g_ref[...]
    return y * (1.0 + mod_ref[0, 4:5, :]) + mod_ref[0, 3:4, :]


def _router_kernel(x_ref, g_ref, mod_ref, wr_ref, br_ref,
                   topi_ref, gate_ref, rank_ref, counts_ref, carry_ref):
    i = pl.program_id(0)

    @pl.when(i == 0)
    def _():
        carry_ref[...] = jnp.zeros_like(carry_ref)

    h = _moe_input(x_ref, g_ref, mod_ref)

    hi = h.astype(BF16)
    lo = (h - hi.astype(F32)).astype(BF16)
    w_hi = wr_ref[0]
    logits = _dot_t(w_hi, hi) + (_dot_t(wr_ref[1], hi) + _dot_t(w_hi, lo)) + br_ref[...]
    sub_e = lax.broadcasted_iota(jnp.int32, logits.shape, 0)
    work = logits
    vals, hits = [], []
    sel = jnp.zeros(logits.shape, F32)
    for _ in range(TOP_K):
        m = jnp.max(work, axis=0, keepdims=True)
        idx = jnp.min(jnp.where(work == m, sub_e, N_EXPERTS), axis=0, keepdims=True)
        hit = sub_e == idx
        vals.append(m)
        hits.append((hit, idx))
        sel = jnp.where(hit, 1.0, sel)
        work = jnp.where(hit, -jnp.inf, work)
    es = [jnp.exp(v - vals[0]) for v in vals]
    inv = 1.0 / (es[0] + es[1] + es[2] + es[3])

    row = lax.broadcasted_iota(jnp.int32, (TM, TM), 0)
    col = lax.broadcasted_iota(jnp.int32, (TM, TM), 1)
    earlier = jnp.where(row < col, 1.0, 0.0).astype(BF16)
    before = _dot(sel.astype(BF16), earlier) + carry_ref[...]
    carry_ref[...] += jnp.sum(sel, axis=1, keepdims=True)
    counts_ref[...] = carry_ref[...].astype(jnp.int32)

    sub_k = lax.broadcasted_iota(jnp.int32, (TOPK_ROWS, TM), 0)
    topi = jnp.zeros((TOPK_ROWS, TM), jnp.int32)
    gate = jnp.zeros((TOPK_ROWS, TM), F32)
    rank = jnp.zeros((TOPK_ROWS, TM), jnp.int32)
    for k in range(TOP_K):
        hit, idx = hits[k]
        rk = jnp.sum(jnp.where(hit, before, 0.0), axis=0, keepdims=True)
        topi = jnp.where(sub_k == k, idx, topi)
        gate = jnp.where(sub_k == k, es[k] * inv, gate)
        rank = jnp.where(sub_k == k, rk.astype(jnp.int32), rank)
    topi_ref[...] = topi
    gate_ref[...] = gate
    rank_ref[...] = rank


def _router(x_all, g, mod_tab, w_router, b_router):
    w_t = w_router.astype(F32).T
    w_hi = w_t.astype(BF16)
    w_split = jnp.stack([w_hi, (w_t - w_hi.astype(F32)).astype(BF16)])
    return pl.pallas_call(
        _router_kernel,
        grid=(N_TILES,),
        in_specs=[
            pl.BlockSpec((TM, D_MODEL), lambda i: (i, 0)),
            pl.BlockSpec((1, D_MODEL), lambda i: (0, 0)),
            pl.BlockSpec((1, 6, D_MODEL), lambda i: (_mod_row(i), 0, 0)),
            pl.BlockSpec((2, N_EXPERTS, D_MODEL), lambda i: (0, 0, 0)),
            pl.BlockSpec((N_EXPERTS, 1), lambda i: (0, 0)),
        ],
        out_specs=[
            pl.BlockSpec((TOPK_ROWS, TM), lambda i: (0, i)),
            pl.BlockSpec((TOPK_ROWS, TM), lambda i: (0, i)),
            pl.BlockSpec((TOPK_ROWS, TM), lambda i: (0, i)),
            pl.BlockSpec((N_EXPERTS, 1), lambda i: (0, 0)),
        ],
        out_shape=[
            jax.ShapeDtypeStruct((TOPK_ROWS, N_TOK), jnp.int32),
            jax.ShapeDtypeStruct((TOPK_ROWS, N_TOK), F32),
            jax.ShapeDtypeStruct((TOPK_ROWS, N_TOK), jnp.int32),
            jax.ShapeDtypeStruct((N_EXPERTS, 1), jnp.int32),
        ],
        scratch_shapes=[pltpu.VMEM((N_EXPERTS, 1), F32)],
        compiler_params=pltpu.CompilerParams(
            dimension_semantics=("arbitrary",), vmem_limit_bytes=VMEM_LIMIT_BYTES),
        name="moe_router",
    )(x_all, g.reshape(1, D_MODEL), mod_tab, w_split, b_router.reshape(N_EXPERTS, 1))


ISSUE_UNROLL = 4


def _dispatch_kernel(ends_ref, pos_ref, x_ref, g_ref, mod_ref, xs_ref, zero_buf, h_ref, pos_smem, sem_idx, sem,
                     sem_zero):
    i = pl.program_id(0)

    @pl.when(i == 0)
    def _():
        zero_buf[...] = jnp.zeros_like(zero_buf)

        for wait in (False, True):
            for e in range(N_EXPERTS):
                start = ends_ref[e - 1] if e > 0 else 0

                @pl.when(ends_ref[e] > start)
                def _(e=e, wait=wait):
                    last = pl.multiple_of(ends_ref[e] - TM, TM)
                    cp = pltpu.make_async_copy(zero_buf, xs_ref.at[pl.ds(last, TM)], sem_zero)
                    if wait:
                        cp.wait()
                    else:
                        cp.start()

            def tail(t, carry, wait=wait):
                cp = pltpu.make_async_copy(zero_buf, xs_ref.at[pl.ds(pl.multiple_of(t * TM, TM), TM)], sem_zero)
                if wait:
                    cp.wait()
                else:
                    cp.start()
                return carry

            lax.fori_loop(ends_ref[N_EXPERTS - 1] // TM, R_TILES, tail, 0)

    cp = pltpu.make_async_copy(pos_ref, pos_smem, sem_idx)
    cp.start()
    h_ref[...] = _moe_input(x_ref, g_ref, mod_ref)
    cp.wait()

    def issue(r, carry):
        for k in range(TOP_K):
            p = pos_smem[k, r]
            pltpu.make_async_copy(h_ref.at[pl.ds(r, 1)], xs_ref.at[pl.ds(p, 1)], sem.at[k]).start(priority=k % 2)
        return carry

    lax.fori_loop(0, TM, issue, 0, unroll=ISSUE_UNROLL)
    for k in range(TOP_K):
        pltpu.make_async_copy(h_ref, xs_ref.at[pl.ds(0, TM)], sem.at[k]).wait()


def _dispatch(ends, pos, x_all, g, mod_tab):
    grid_spec = pltpu.PrefetchScalarGridSpec(
        num_scalar_prefetch=1,
        grid=(N_TILES,),
        in_specs=[
            pl.BlockSpec((TOPK_ROWS, TM), lambda i, ends: (0, i)),
            pl.BlockSpec((TM, D_MODEL), lambda i, ends: (i, 0)),
            pl.BlockSpec((1, D_MODEL), lambda i, ends: (0, 0)),
            pl.BlockSpec((1, 6, D_MODEL), lambda i, ends: (_mod_row(i), 0, 0)),
        ],
        out_specs=pl.BlockSpec(memory_space=pl.ANY),
        scratch_shapes=[
            pltpu.VMEM((TM, D_MODEL), F32),
            pltpu.VMEM((TM, D_MODEL), F32),
            pltpu.SMEM((TOPK_ROWS, TM), jnp.int32),
            pltpu.SemaphoreType.DMA,
            pltpu.SemaphoreType.DMA((TOP_K,)),
            pltpu.SemaphoreType.DMA,
        ],
    )
    return pl.pallas_call(
        _dispatch_kernel,
        grid_spec=grid_spec,
        out_shape=jax.ShapeDtypeStruct((R_MAX, D_MODEL), F32),
        compiler_params=pltpu.CompilerParams(
            dimension_semantics=("arbitrary",), vmem_limit_bytes=VMEM_LIMIT_BYTES),
        name="moe_dispatch",
    )(ends, pos, x_all, g.reshape(1, D_MODEL), mod_tab)


def _ffn_kernel(te_ref, nu_ref, nx_ref, xs_ref, wgu_hbm, bgu_ref, wd_hbm, bd_ref, ys_ref,
                wgu_f32, wd_f32, wgu_bf, wd_bf, sem, *, layer):
    i = pl.program_id(0)

    def weight_copies(e):
        return (pltpu.make_async_copy(wgu_hbm.at[layer, e], wgu_f32, sem.at[0]),
                pltpu.make_async_copy(wd_hbm.at[layer, e], wd_f32, sem.at[1]))

    @pl.when(i < nu_ref[0])
    def _():
        expert = te_ref[i]
        new_expert = jnp.logical_or(i == 0, expert != te_ref[jnp.maximum(i - 1, 0)])

        @pl.when(i == 0)
        def _():
            for cp in weight_copies(expert):
                cp.start()

        @pl.when(new_expert)
        def _():
            for cp in weight_copies(expert):
                cp.wait()
            wgu_bf[...] = wgu_f32[...].astype(BF16)
            wd_bf[...] = wd_f32[...].astype(BF16)
            nxt = nx_ref[expert]

            @pl.when(nxt >= 0)
            def _():
                for cp in weight_copies(nxt):
                    cp.start()

        x = xs_ref[...].astype(BF16)
        gu = jnp.dot(x, wgu_bf[...], preferred_element_type=F32) + bgu_ref[...]
        g = jnp.minimum(gu[:, :D_FF], SWIGLU_LIMIT)
        u = jnp.clip(gu[:, D_FF:], -SWIGLU_LIMIT, SWIGLU_LIMIT)
        act = g * jax.nn.sigmoid(SWIGLU_ALPHA * g) * (u + 1.0)
        ys_ref[...] = jnp.dot(act.astype(BF16), wd_bf[...], preferred_element_type=F32) + bd_ref[...]

    @pl.when(i >= nu_ref[0])
    def _():
        ys_ref[...] = jnp.zeros_like(ys_ref)


def _ffn(layer, tile_expert, n_used, next_expert, xs, w_gate_up, b_gate_up, w_down, b_down):
    def row_map(i, te, nu, nx):
        return (jnp.maximum(jnp.minimum(i, nu[0] - 1), 0), 0)

    def b_map(i, te, nu, nx):
        return (layer, te[i], 0, 0)

    grid_spec = pltpu.PrefetchScalarGridSpec(
        num_scalar_prefetch=3,
        grid=(R_TILES,),
        in_specs=[
            pl.BlockSpec((TM, D_MODEL), row_map),
            pl.BlockSpec(memory_space=pl.ANY),
            pl.BlockSpec((None, None, 1, 2 * D_FF), b_map),
            pl.BlockSpec(memory_space=pl.ANY),
            pl.BlockSpec((None, None, 1, D_MODEL), b_map),
        ],
        out_specs=pl.BlockSpec((TM, D_MODEL), lambda i, te, nu, nx: (i, 0)),
        scratch_shapes=[
            pltpu.VMEM((D_MODEL, 2 * D_FF), F32),
            pltpu.VMEM((D_FF, D_MODEL), F32),
            pltpu.VMEM((D_MODEL, 2 * D_FF), BF16),
            pltpu.VMEM((D_FF, D_MODEL), BF16),
            pltpu.SemaphoreType.DMA((2,)),
        ],
    )
    return pl.pallas_call(
        functools.partial(_ffn_kernel, layer=layer),
        grid_spec=grid_spec,
        out_shape=jax.ShapeDtypeStruct((R_MAX, D_MODEL), F32),
        compiler_params=pltpu.CompilerParams(
            dimension_semantics=("arbitrary",), vmem_limit_bytes=VMEM_LIMIT_BYTES),
        name="moe_ffn",
    )(tile_expert, n_used, next_expert, xs, w_gate_up,
      b_gate_up.reshape(DEPTH, N_EXPERTS, 1, 2 * D_FF), w_down,
      b_down.reshape(DEPTH, N_EXPERTS, 1, D_MODEL))


def _combine_kernel(pos_ref, pos_next_ref, ys_ref, x_ref, gate_ref, mod_ref, out_ref, buf, pos_smem, sem_idx, sem):
    i = pl.program_id(0)
    slot = i % 2

    def gather_tile(tile_pos_ref, s):
        cp = pltpu.make_async_copy(tile_pos_ref, pos_smem, sem_idx)
        cp.start()
        cp.wait()

        def issue(r, carry):
            for k in range(TOP_K):
                p = pos_smem[k, r]
                pltpu.make_async_copy(ys_ref.at[pl.ds(p, 1)], buf.at[s, k, pl.ds(r, 1)],
                                      sem.at[s, k]).start(priority=k % 2)
            return carry

        lax.fori_loop(0, TM, issue, 0, unroll=ISSUE_UNROLL)

    @pl.when(i == 0)
    def _():
        gather_tile(pos_ref, 0)

    @pl.when(i + 1 < N_TILES)
    def _():
        gather_tile(pos_next_ref, 1 - slot)

    acc = jnp.zeros((TM, D_MODEL), F32)
    for k in range(TOP_K):
        pltpu.make_async_copy(ys_ref.at[pl.ds(0, TM)], buf.at[slot, k], sem.at[slot, k]).wait()
        acc = acc + gate_ref[:, k:k + 1] * buf[slot, k]
    out_ref[...] = x_ref[...] + mod_ref[0, 5:6, :] * acc


def _combine(pos, ys, x_all, gate, mod_tab):
    return pl.pallas_call(
        _combine_kernel,
        grid=(N_TILES,),
        in_specs=[
            pl.BlockSpec((TOPK_ROWS, TM), lambda i: (0, i)),
            pl.BlockSpec((TOPK_ROWS, TM), lambda i: (0, jnp.minimum(i + 1, N_TILES - 1))),
            pl.BlockSpec(memory_space=pl.ANY),
            pl.BlockSpec((TM, D_MODEL), lambda i: (i, 0)),
            pl.BlockSpec((TM, TOPK_ROWS), lambda i: (i, 0)),
            pl.BlockSpec((1, 6, D_MODEL), lambda i: (_mod_row(i), 0, 0)),
        ],
        out_specs=pl.BlockSpec((TM, D_MODEL), lambda i: (i, 0)),
        out_shape=jax.ShapeDtypeStruct((N_TOK, D_MODEL), F32),
        scratch_shapes=[
            pltpu.VMEM((2, TOP_K, TM, D_MODEL), F32),
            pltpu.SMEM((TOPK_ROWS, TM), jnp.int32),
            pltpu.SemaphoreType.DMA,
            pltpu.SemaphoreType.DMA((2, TOP_K)),
        ],
        compiler_params=pltpu.CompilerParams(
            dimension_semantics=("arbitrary",), vmem_limit_bytes=VMEM_LIMIT_BYTES),
        name="moe_combine",
    )(pos, pos, ys, x_all, gate, mod_tab)


def _moe_layer(layer, x_all, mod_tab, g_norm2, w_router, b_router, w_gate_up, b_gate_up, w_down, b_down):
    topi, gate, rank, counts = _router(x_all, g_norm2[layer], mod_tab, w_router[layer], b_router[layer])
    counts = counts[:, 0]
    padded = ((counts + TM - 1) // TM) * TM
    ends = jnp.cumsum(padded)
    starts = ends - padded
    order = jnp.arange(N_EXPERTS, dtype=jnp.int32)
    first_row = jnp.sum(jnp.where(topi[None] == order[:, None, None], starts[:, None, None], 0), axis=0)
    pos = (first_row + rank).astype(jnp.int32)
    gate = gate.T
    n_used = (ends[-1] // TM).astype(jnp.int32)
    later = jnp.where((padded[None, :] > 0) & (order[None, :] > order[:, None]), order[None, :], N_EXPERTS)
    next_expert = jnp.min(later, axis=1)
    next_expert = jnp.where(next_expert == N_EXPERTS, -1, next_expert).astype(jnp.int32)
    tile_start = jnp.arange(R_TILES, dtype=jnp.int32) * TM
    tile_start = jnp.minimum(tile_start, ends[-1] - 1)
    tile_expert = jnp.sum((ends[None, :] <= tile_start[:, None]).astype(jnp.int32), axis=1)
    tile_expert = jnp.minimum(tile_expert, N_EXPERTS - 1).astype(jnp.int32)
    xs = _dispatch(ends.astype(jnp.int32), pos, x_all, g_norm2[layer], mod_tab)
    ys = _ffn(layer, tile_expert, n_used.reshape(1), next_expert, xs, w_gate_up, b_gate_up, w_down, b_down)
    return _combine(pos, ys, x_all, gate, mod_tab)


S5_ROWS = 512
S5_HALF_W = S5_WIDTH // 2
S5_HALF_STATES = (S5_GROUPS // 2) * S5_STATE
S5_COL_CHUNK = 512


def _s5_scan_kernel(u_ref, bmat_ref, cmat_ref, a_ref, h0_ref, y_ref, fin_ref, bu_ref, h_ref, *, bsz, steps):
    d = pl.program_id(0)
    c = pl.program_id(1)
    hs = S5_HALF_STATES

    @pl.when(c == 0)
    def _():
        h_ref[...] = h0_ref[...]

    u = u_ref[...].astype(BF16)
    for hf in range(2):
        bu_ref[...] = jnp.dot(u[:, hf * S5_HALF_W:(hf + 1) * S5_HALF_W], bmat_ref[hf],
                              preferred_element_type=F32)
        for j in range(hs // S5_COL_CHUNK):
            re0 = j * S5_COL_CHUNK
            im0 = hs + j * S5_COL_CHUNK
            ar = jnp.broadcast_to(a_ref[hf, 0:1, re0:re0 + S5_COL_CHUNK], (bsz, S5_COL_CHUNK))
            ai = jnp.broadcast_to(a_ref[hf, 1:2, re0:re0 + S5_COL_CHUNK], (bsz, S5_COL_CHUNK))

            def step(t, carry, re0=re0, im0=im0, ar=ar, ai=ai):
                hr, hi = carry
                te = jnp.where(d == 0, t, steps - 1 - t)
                r0 = pl.multiple_of(te * bsz, bsz)
                br = bu_ref[pl.ds(r0, bsz), re0:re0 + S5_COL_CHUNK]
                bi = bu_ref[pl.ds(r0, bsz), im0:im0 + S5_COL_CHUNK]
                nr = ar * hr - ai * hi + br
                ni = ar * hi + ai * hr + bi
                bu_ref[pl.ds(r0, bsz), re0:re0 + S5_COL_CHUNK] = nr
                bu_ref[pl.ds(r0, bsz), im0:im0 + S5_COL_CHUNK] = ni
                return nr, ni

            hr, hi = lax.fori_loop(
                0, steps, step,
                (h_ref[hf, :, re0:re0 + S5_COL_CHUNK], h_ref[hf, :, im0:im0 + S5_COL_CHUNK]), unroll=4)
            h_ref[hf, :, re0:re0 + S5_COL_CHUNK] = hr
            h_ref[hf, :, im0:im0 + S5_COL_CHUNK] = hi
        y_ref[:, hf * S5_HALF_W:(hf + 1) * S5_HALF_W] = jnp.dot(
            bu_ref[...].astype(BF16), cmat_ref[hf], preferred_element_type=F32)

    @pl.when(c == pl.num_programs(1) - 1)
    def _():
        fin_ref[...] = h_ref[...]


def _s5_scan(u_tm, bmat, cmat, acoef, h0, bsz):
    rows = u_tm.shape[0]
    steps = S5_ROWS // bsz
    n_chunks = rows // S5_ROWS

    def chunk_map(d, c):
        return jnp.where(d == 0, c, n_chunks - 1 - c)

    return pl.pallas_call(
        functools.partial(_s5_scan_kernel, bsz=bsz, steps=steps),
        grid=(2, n_chunks),
        in_specs=[
            pl.BlockSpec((S5_ROWS, S5_WIDTH), lambda d, c: (chunk_map(d, c), 0)),
            pl.BlockSpec((None, 2, S5_HALF_W, 2 * S5_HALF_STATES), lambda d, c: (d, 0, 0, 0)),
            pl.BlockSpec((None, 2, 2 * S5_HALF_STATES, S5_HALF_W), lambda d, c: (d, 0, 0, 0)),
            pl.BlockSpec((None, 2, 2, S5_HALF_STATES), lambda d, c: (d, 0, 0, 0)),
            pl.BlockSpec((None, 2, bsz, 2 * S5_HALF_STATES), lambda d, c: (d, 0, 0, 0)),
        ],
        out_specs=[
            pl.BlockSpec((None, S5_ROWS, S5_WIDTH), lambda d, c: (d, chunk_map(d, c), 0)),
            pl.BlockSpec((None, 2, bsz, 2 * S5_HALF_STATES), lambda d, c: (d, 0, 0, 0)),
        ],
        out_shape=[
            jax.ShapeDtypeStruct((2, rows, S5_WIDTH), F32),
            jax.ShapeDtypeStruct((2, 2, bsz, 2 * S5_HALF_STATES), F32),
        ],
        scratch_shapes=[
            pltpu.VMEM((S5_ROWS, 2 * S5_HALF_STATES), F32),
            pltpu.VMEM((2, bsz, 2 * S5_HALF_STATES), F32),
        ],
        compiler_params=pltpu.CompilerParams(
            dimension_semantics=("arbitrary", "arbitrary"), vmem_limit_bytes=VMEM_LIMIT_BYTES),
        name="s5_scan",
    )(u_tm, bmat, cmat, acoef, h0)


def _s5_discretize(lam_re, lam_im, log_dt, b_re, b_im, c_re, c_im):
    eye = jnp.eye(S5_GROUPS // 2, dtype=F32)
    bmats, cmats, acoefs = [], [], []
    for dr in range(2):
        lr = jnp.minimum(lam_re[dr].astype(F32), -1e-4)
        li = lam_im[dr].astype(F32)
        dt = jnp.exp(log_dt[dr].astype(F32))[:, None]
        mag = jnp.exp(lr * dt)
        ar, ai = mag * jnp.cos(li * dt), mag * jnp.sin(li * dt)
        den = lr * lr + li * li
        fr = ((ar - 1.0) * lr + ai * li) / den
        fi = (ai * lr - (ar - 1.0) * li) / den
        br_ = b_re[dr].astype(F32)
        bi_ = b_im[dr].astype(F32)
        bbr = fr[..., None] * br_ - fi[..., None] * bi_
        bbi = fr[..., None] * bi_ + fi[..., None] * br_
        bm, cm, am = [], [], []
        for hf in range(2):
            g = slice(hf * S5_GROUPS // 2, (hf + 1) * S5_GROUPS // 2)

            def bdiag_in(w):
                return jnp.einsum('ab,aph->ahbp', eye, w[g]).reshape(S5_HALF_W, S5_HALF_STATES)

            def bdiag_out(w):
                return jnp.einsum('ab,ahp->apbh', eye, w[g]).reshape(S5_HALF_STATES, S5_HALF_W)

            bm.append(jnp.concatenate([bdiag_in(bbr), bdiag_in(bbi)], axis=1))
            cm.append(jnp.concatenate([bdiag_out(c_re[dr].astype(F32)),
                                       -bdiag_out(c_im[dr].astype(F32))], axis=0))
            am.append(jnp.stack([ar[g].reshape(-1), ai[g].reshape(-1)]))
        bmats.append(jnp.stack(bm))
        cmats.append(jnp.stack(cm))
        acoefs.append(jnp.stack(am))
    return jnp.stack(bmats).astype(BF16), jnp.stack(cmats).astype(BF16), jnp.stack(acoefs)


def _s5_state_to_kernel(h0):
    bsz = h0.shape[0]
    h = h0.astype(F32).reshape(bsz, 2, 2, 2, S5_HALF_STATES)
    return h.transpose(1, 3, 0, 2, 4).reshape(2, 2, bsz, 2 * S5_HALF_STATES)


def _s5_state_from_kernel(fin):
    bsz = fin.shape[2]
    h = fin.reshape(2, 2, bsz, 2, S5_HALF_STATES).transpose(2, 0, 3, 1, 4)
    return h.reshape(bsz, 2, 2, S5_GROUPS, S5_STATE)


TQ = 256
MLA_HEADS_PER_STEP = 8
DIFF_HEADS_PER_STEP = 2
ATT_TQ = 1024
HEAD_LANES = 128
MLA_SCALE = (MLA_NOPE + MLA_ROPE) ** -0.5
DIFF_SCALE = DIFF_HD ** -0.5


def _dot(a, b):
    return jnp.dot(a, b, preferred_element_type=F32)


def _dot_t(a, b):
    return lax.dot_general(a, b, (((1,), (1,)), ((), ())), preferred_element_type=F32)


def _rms_rows(x, g):
    return x * lax.rsqrt(jnp.mean(x * x, axis=-1, keepdims=True) + EPS) * g


def _group(bsz, length, row0, mod_base, mod_stride):
    return dict(bsz=bsz, length=length, nt=length // TQ, tile0=row0 // TQ,
                mod_base=mod_base, mod_stride=mod_stride)


def _params(n_axes):
    return pltpu.CompilerParams(dimension_semantics=("arbitrary",) * n_axes,
                                vmem_limit_bytes=VMEM_LIMIT_BYTES)


def _axial_rope(length, dim):
    rows = length // GRID_W
    row = jnp.repeat(jnp.arange(rows, dtype=F32), GRID_W)
    col = jnp.tile(jnp.arange(GRID_W, dtype=F32), rows)
    n_freq = dim // 4
    inv = ROPE_THETA ** (-jnp.arange(n_freq, dtype=F32) / n_freq)
    ang = jnp.concatenate([row[:, None] * inv, col[:, None] * inv], axis=-1)
    return jnp.cos(ang), jnp.sin(ang)


def _rope_tables(length, dim, lead, reps):
    cos, sin = _axial_rope(length, dim)
    cos_r = jnp.repeat(cos, 2, axis=-1)
    sin_r = jnp.repeat(sin, 2, axis=-1) * jnp.tile(jnp.array([-1.0, 1.0], F32), dim // 2)
    part = HEAD_LANES // reps
    pad = ((0, 0), (lead, part - lead - dim))
    cos_t = jnp.tile(jnp.pad(cos_r, pad, constant_values=1.0), (1, reps))
    sin_t = jnp.tile(jnp.pad(sin_r, pad), (1, reps))
    return cos_t, sin_t


def _swap_pairs(w):
    return w[:, jnp.arange(w.shape[1]) ^ 1]


def _even_in_kernel(*refs, rope):
    if rope:
        (x_ref, g_ref, mod_ref, win_ref, gq_ref, wuq_ref, gkv_ref, cos_ref, sin_ref,
         u_ref, q_ref, ckv_ref, kr_ref) = refs
    else:
        (x_ref, g_ref, mod_ref, win_ref, gq_ref, wuq_ref, gkv_ref,
         u_ref, q_ref, ckv_ref, kr_ref) = refs
    o1 = S5_WIDTH
    o2 = o1 + MLA_Q_LORA
    o3 = o2 + MLA_KV_LORA
    o4 = o3 + HEAD_LANES
    n_in = o4 + HEAD_LANES if rope else o4
    n_q = MLA_HEADS * HEAD_LANES
    h = _rms_rows(x_ref[...], g_ref[...]) * (1.0 + mod_ref[0, 1:2, :]) + mod_ref[0, 0:1, :]
    z = _dot(h.astype(BF16), win_ref[:, :n_in])
    u_ref[...] = z[:, :o1]
    ckv_ref[...] = _rms_rows(z[:, o2:o3], gkv_ref[...])
    qn = _rms_rows(z[:, o1:o2], gq_ref[...]).astype(BF16)
    if rope:
        q2 = _dot(qn, wuq_ref[...])
        cos = cos_ref[...]
        sin = sin_ref[...]
        for hd in range(MLA_HEADS):
            a = hd * HEAD_LANES
            q_ref[:, a:a + HEAD_LANES] = ((q2[:, a:a + HEAD_LANES] * cos
                                           + q2[:, n_q + a:n_q + a + HEAD_LANES] * sin) * MLA_SCALE
                                          ).astype(q_ref.dtype)
        kr_ref[...] = z[:, o3:o4] * cos + z[:, o4:o4 + HEAD_LANES] * sin
    else:
        q_ref[...] = (_dot(qn, wuq_ref[:, :n_q]) * MLA_SCALE).astype(q_ref.dtype)
        kr_ref[...] = z[:, o3:o4]


def _even_in(x_all, grp, g1, mod_tab, win_aug, g_q, wuq2, g_kv, tables):
    bsz, length, nt = grp["bsz"], grp["length"], grp["nt"]
    rope = tables is not None
    rows = bsz * length

    def tok(b, t):
        return (b * nt + t, 0)

    in_specs = [
        pl.BlockSpec((TQ, D_MODEL), lambda b, t: (grp["tile0"] + b * nt + t, 0)),
        pl.BlockSpec((1, D_MODEL), lambda b, t: (0, 0)),
        pl.BlockSpec((1, 6, D_MODEL), lambda b, t: (grp["mod_base"] + b * grp["mod_stride"], 0, 0)),
        pl.BlockSpec(win_aug.shape, lambda b, t: (0, 0)),
        pl.BlockSpec((1, MLA_Q_LORA), lambda b, t: (0, 0)),
        pl.BlockSpec(wuq2.shape, lambda b, t: (0, 0)),
        pl.BlockSpec((1, MLA_KV_LORA), lambda b, t: (0, 0)),
    ]
    args = [x_all, g1.reshape(1, D_MODEL), mod_tab, win_aug, g_q.reshape(1, -1), wuq2, g_kv.reshape(1, -1)]
    if rope:
        in_specs += [pl.BlockSpec((TQ, HEAD_LANES), lambda b, t: (t, 0))] * 2
        args += list(tables)
    return pl.pallas_call(
        functools.partial(_even_in_kernel, rope=rope),
        grid=(bsz, nt),
        in_specs=in_specs,
        out_specs=[
            pl.BlockSpec((TQ, S5_WIDTH), lambda b, t: (t, b)),
            pl.BlockSpec((TQ, MLA_HEADS * HEAD_LANES), tok),
            pl.BlockSpec((TQ, MLA_KV_LORA), tok),
            pl.BlockSpec((TQ, HEAD_LANES), tok),
        ],
        out_shape=[
            jax.ShapeDtypeStruct((length, bsz * S5_WIDTH), F32),
            jax.ShapeDtypeStruct((rows, MLA_HEADS * HEAD_LANES), BF16),
            jax.ShapeDtypeStruct((rows, MLA_KV_LORA), F32),
            jax.ShapeDtypeStruct((rows, HEAD_LANES), F32),
        ],
        compiler_params=_params(2),
        name="even_in",
    )(*args)


def _kv_expand_kernel(x_ref, kr_ref, wk_ref, wv_ref, k_ref, v_ref):
    x = x_ref[...].astype(BF16)
    k = _dot(x, wk_ref[...])
    kr = kr_ref[...]
    for hd in range(MLA_HEADS):
        a = hd * HEAD_LANES
        k_ref[:, a:a + HEAD_LANES] = (k[:, a:a + HEAD_LANES] + kr).astype(k_ref.dtype)
    v_ref[...] = _dot(x, wv_ref[...]).astype(v_ref.dtype)


def _kv_expand(ckv, kr, wk, wv):
    rows = ckv.shape[0]
    tm = 512
    width = MLA_HEADS * HEAD_LANES
    return pl.pallas_call(
        _kv_expand_kernel,
        grid=(rows // tm,),
        in_specs=[pl.BlockSpec((tm, MLA_KV_LORA), lambda i: (i, 0)),
                  pl.BlockSpec((tm, HEAD_LANES), lambda i: (i, 0)),
                  pl.BlockSpec(wk.shape, lambda i: (0, 0)),
                  pl.BlockSpec(wv.shape, lambda i: (0, 0))],
        out_specs=[pl.BlockSpec((tm, width), lambda i: (i, 0))] * 2,
        out_shape=[jax.ShapeDtypeStruct((rows, width), BF16)] * 2,
        compiler_params=_params(1),
        name="kv_expand",
    )(ckv, kr, wk, wv)


def _exp_parts(scores):
    m = functools.reduce(jnp.maximum, [jnp.max(s, axis=-1, keepdims=True) for s in scores])
    es = [jnp.exp(s - m) for s in scores]
    den = functools.reduce(jnp.add, [jnp.sum(e, axis=-1, keepdims=True) for e in es])
    return es, 1.0 / den


def _weighted_values(es, vs):
    o = _dot(es[0].astype(BF16), vs[0])
    for e, v in zip(es[1:], vs[1:]):
        o = o + _dot(e.astype(BF16), v)
    return o


def _mla_attn_kernel(*refs, n_seg):
    q_ref, o_ref = refs[0], refs[-1]
    for hd in range(MLA_HEADS_PER_STEP):
        lanes = slice(hd * HEAD_LANES, (hd + 1) * HEAD_LANES)
        q = q_ref[:, lanes]
        ks = [refs[1 + 2 * s][:, lanes] for s in range(n_seg)]
        vs = [refs[2 + 2 * s][:, lanes] for s in range(n_seg)]
        es, inv = _exp_parts([_dot_t(q, k) for k in ks])
        o_ref[:, lanes] = (_weighted_values(es, vs) * inv).astype(o_ref.dtype)


def _mla_attn(q, segs, bsz, length):
    tq = TQ
    nq = length // tq
    width = MLA_HEADS_PER_STEP * HEAD_LANES
    in_specs = [pl.BlockSpec((tq, width), lambda b, h, i: (b * nq + i, h))]
    args = [q]
    for k, v, lk in segs:
        in_specs += [pl.BlockSpec((lk, width), lambda b, h, i: (b, h))] * 2
        args += [k, v]
    return pl.pallas_call(
        functools.partial(_mla_attn_kernel, n_seg=len(segs)),
        grid=(bsz, MLA_HEADS // MLA_HEADS_PER_STEP, nq),
        in_specs=in_specs,
        out_specs=pl.BlockSpec((tq, width), lambda b, h, i: (b * nq + i, h)),
        out_shape=jax.ShapeDtypeStruct((bsz * length, MLA_HEADS * HEAD_LANES), BF16),
        compiler_params=_params(3),
        name="mla_attn",
    )(*args)


def _even_out_kernel(u_ref, y_ref, o_ref, x_ref, mod_ref, d_ref, wglu_ref, bglu_ref, ws5_ref, wmla_ref,
                     out_ref):
    y =jax.nn.gelu(d_ref[...] * u_ref[...] + y_ref[0] + y_ref[1])
    s5 = y * jax.nn.sigmoid(_dot(y.astype(BF16), wglu_ref[...]) + bglu_ref[...])
    mix = _dot(s5.astype(BF16), ws5_ref[...]) + _dot(o_ref[...].astype(BF16), wmla_ref[...])
    out_ref[...] = x_ref[...] + mod_ref[0, 2:3, :] * mix


def _even_out(x_all, grp, mod_tab, u_tm, y_dir, o_mla, d_skip, w_glu, b_glu, w_out_s5, w_out_mla):
    bsz, length, nt = grp["bsz"], grp["length"], grp["nt"]

    def xrow(b, t):
        return (grp["tile0"] + b * nt + t, 0)

    full = lambda b, t: (0, 0)
    return pl.pallas_call(
        _even_out_kernel,
        grid=(bsz, nt),
        in_specs=[
            pl.BlockSpec((TQ, S5_WIDTH), lambda b, t: (t, b)),
            pl.BlockSpec((2, TQ, S5_WIDTH), lambda b, t: (0, t, b)),
            pl.BlockSpec((TQ, MLA_HEADS * HEAD_LANES), lambda b, t: (b * nt + t, 0)),
            pl.BlockSpec((TQ, D_MODEL), xrow),
            pl.BlockSpec((1, 6, D_MODEL), lambda b, t: (grp["mod_base"] + b * grp["mod_stride"], 0, 0)),
            pl.BlockSpec((1, S5_WIDTH), full),
            pl.BlockSpec(w_glu.shape, full),
            pl.BlockSpec((1, S5_WIDTH), full),
            pl.BlockSpec(w_out_s5.shape, full),
            pl.BlockSpec(w_out_mla.shape, full),
        ],
        out_specs=pl.BlockSpec((TQ, D_MODEL), xrow),
        out_shape=jax.ShapeDtypeStruct(x_all.shape, F32),
        input_output_aliases={3: 0},
        compiler_params=_params(2),
        name="even_out",
    )(u_tm, y_dir.reshape(2, length, bsz * S5_WIDTH), o_mla, x_all, mod_tab,
      d_skip.reshape(1, S5_WIDTH), w_glu, b_glu.reshape(1, S5_WIDTH), w_out_s5, w_out_mla)


def _odd_in_kernel(*refs, rope):
    if rope:
        x_ref, g_ref, mod_ref, w_ref, cos_ref, sin_ref, q_ref, k_ref, v_ref = refs
    else:
        x_ref, g_ref, mod_ref, w_ref, q_ref, k_ref, v_ref = refs
    w3 = 3 * DIFF_WIDTH
    h = _rms_rows(x_ref[...], g_ref[...]) * (1.0 + mod_ref[0, 1:2, :]) + mod_ref[0, 0:1, :]
    z = _dot(h.astype(BF16), w_ref[:, :w3])
    v_ref[...] = z[:, 2 * DIFF_WIDTH:w3].astype(v_ref.dtype)
    if rope:
        cos = cos_ref[...]
        sin = sin_ref[...]
        even = lax.broadcasted_iota(jnp.int32, cos.shape, 1) % 2 == 0

        def swap_pairs(t):
            return jnp.where(even, pltpu.roll(t, HEAD_LANES - 1, 1), pltpu.roll(t, 1, 1))

        for hd in range(DIFF_HEADS):
            a = hd * HEAD_LANES
            zq = z[:, a:a + HEAD_LANES]
            q_ref[:, a:a + HEAD_LANES] = ((zq * cos + swap_pairs(zq) * sin) * DIFF_SCALE).astype(q_ref.dtype)
            zk = z[:, DIFF_WIDTH + a:DIFF_WIDTH + a + HEAD_LANES]
            k_ref[:, a:a + HEAD_LANES] = (zk * cos + swap_pairs(zk) * sin).astype(k_ref.dtype)
    else:
        q_ref[...] = (z[:, :DIFF_WIDTH] * DIFF_SCALE).astype(q_ref.dtype)
        k_ref[...] = z[:, DIFF_WIDTH:2 * DIFF_WIDTH].astype(k_ref.dtype)


def _odd_in(x_all, grp, g1, mod_tab, w_aug, tables, kv_dtype):
    bsz, length, nt = grp["bsz"], grp["length"], grp["nt"]
    rope = tables is not None
    rows = bsz * length

    def tok(b, t):
        return (b * nt + t, 0)

    in_specs = [
        pl.BlockSpec((TQ, D_MODEL), lambda b, t: (grp["tile0"] + b * nt + t, 0)),
        pl.BlockSpec((1, D_MODEL), lambda b, t: (0, 0)),
        pl.BlockSpec((1, 6, D_MODEL), lambda b, t: (grp["mod_base"] + b * grp["mod_stride"], 0, 0)),
        pl.BlockSpec(w_aug.shape, lambda b, t: (0, 0)),
    ]
    args = [x_all, g1.reshape(1, D_MODEL), mod_tab, w_aug]
    if rope:
        in_specs += [pl.BlockSpec((TQ, HEAD_LANES), lambda b, t: (t, 0))] * 2
        args += list(tables)
    return pl.pallas_call(
        functools.partial(_odd_in_kernel, rope=rope),
        grid=(bsz, nt),
        in_specs=in_specs,
        out_specs=[pl.BlockSpec((TQ, DIFF_WIDTH), tok)] * 3,
        out_shape=[
            jax.ShapeDtypeStruct((rows, DIFF_WIDTH), BF16),
            jax.ShapeDtypeStruct((rows, DIFF_WIDTH), kv_dtype),
            jax.ShapeDtypeStruct((rows, DIFF_WIDTH), kv_dtype),
        ],
        compiler_params=_params(2),
        name="odd_in",
    )(*args)


def _diff_attn_kernel(*refs, n_seg, post_scale):
    lam_ref, q_ref = refs[0], refs[1]
    g_ref, o_ref = refs[-2], refs[-1]
    for hd in range(DIFF_HEADS_PER_STEP):
        lanes = slice(hd * HEAD_LANES, (hd + 1) * HEAD_LANES)
        q = q_ref[:, lanes].astype(F32)
        lane = lax.broadcasted_iota(jnp.int32, q.shape, 1)
        q0 = jnp.where(lane < DIFF_HD, q, 0.0).astype(BF16)
        q1 = jnp.where(lane >= DIFF_HD, q, 0.0).astype(BF16)
        ks = [refs[2 + 2 * s][:, lanes].astype(BF16) for s in range(n_seg)]
        vs = [refs[3 + 2 * s][:, lanes].astype(BF16) for s in range(n_seg)]
        e0, inv0 = _exp_parts([_dot_t(q0, k) for k in ks])
        e1, inv1 = _exp_parts([_dot_t(q1, k) for k in ks])
        o = _weighted_values(e0, vs) * inv0 - lam_ref[0] * (_weighted_values(e1, vs) * inv1)
        o_ref[:, lanes] = (_rms_rows(o, g_ref[...]) * post_scale).astype(o_ref.dtype)


def _diff_attn(lam_full, q, segs, g_sub, post_scale, bsz, length):
    tq = min(ATT_TQ, length)
    nq = length // tq
    width = DIFF_HEADS_PER_STEP * HEAD_LANES
    in_specs = [pl.BlockSpec(memory_space=pltpu.SMEM),
                pl.BlockSpec((tq, width), lambda b, h, i: (b * nq + i, h))]
    args = [lam_full.reshape(1).astype(F32), q]
    for k, v, lk in segs:
        in_specs += [pl.BlockSpec((lk, width), lambda b, h, i: (b, h))] * 2
        args += [k, v]
    in_specs.append(pl.BlockSpec((1, HEAD_LANES), lambda b, h, i: (0, 0)))
    args.append(g_sub.reshape(1, HEAD_LANES))
    return pl.pallas_call(
        functools.partial(_diff_attn_kernel, n_seg=len(segs), post_scale=post_scale),
        grid=(bsz, DIFF_HEADS // DIFF_HEADS_PER_STEP, nq),
        in_specs=in_specs,
        out_specs=pl.BlockSpec((tq, width), lambda b, h, i: (b * nq + i, h)),
        out_shape=jax.ShapeDtypeStruct((bsz * length, DIFF_WIDTH), BF16),
        compiler_params=_params(3),
        name="diff_attn",
    )(*args)


def _odd_out_kernel(o_ref, x_ref, mod_ref, w_ref, out_ref):
    out_ref[...] = x_ref[...] + mod_ref[0, 2:3, :] * _dot(o_ref[...], w_ref[...])


def _odd_out(x_all, grp, mod_tab, o, w_out):
    bsz, nt = grp["bsz"], grp["nt"]

    def xrow(b, t):
        return (grp["tile0"] + b * nt + t, 0)

    return pl.pallas_call(
        _odd_out_kernel,
        grid=(bsz, nt),
        in_specs=[
            pl.BlockSpec((TQ, DIFF_WIDTH), lambda b, t: (b * nt + t, 0)),
            pl.BlockSpec((TQ, D_MODEL), xrow),
            pl.BlockSpec((1, 6, D_MODEL), lambda b, t: (grp["mod_base"] + b * grp["mod_stride"], 0, 0)),
            pl.BlockSpec(w_out.shape, lambda b, t: (0, 0)),
        ],
        out_specs=pl.BlockSpec((TQ, D_MODEL), xrow),
        out_shape=jax.ShapeDtypeStruct(x_all.shape, F32),
        input_output_aliases={1: 0},
        compiler_params=_params(2),
        name="odd_out",
    )(o, x_all, mod_tab, w_out)


def _final_norm_kernel(x_ref, g_ref, o_ref):
    o_ref[...] = _rms_rows(x_ref[...], g_ref[...])


def _final_norm(x_all, g, row0, rows):
    tm = 512
    return pl.pallas_call(
        _final_norm_kernel,
        grid=(rows // tm,),
        in_specs=[pl.BlockSpec((tm, D_MODEL), lambda i: (row0 // tm + i, 0)),
                  pl.BlockSpec((1, D_MODEL), lambda i: (0, 0))],
        out_specs=pl.BlockSpec((tm, D_MODEL), lambda i: (i, 0)),
        out_shape=jax.ShapeDtypeStruct((rows, D_MODEL), F32),
        compiler_params=_params(1),
        name="final_norm",
    )(x_all, g.reshape(1, D_MODEL))


MOD_ROWS = 16
MOD_COL_TILE = 1536


def _adaln_kernel(c_ref, w_ref, b_ref, o_ref):
    cond = jax.nn.silu(c_ref[...])
    o_ref[...] = jnp.dot(cond, w_ref[...], preferred_element_type=F32,
                         precision=lax.Precision.HIGHEST) + b_ref[...]


def _adaln(cond_rows, w_mod, b_mod):
    return pl.pallas_call(
        _adaln_kernel,
        grid=(DEPTH, 6 * D_MODEL // MOD_COL_TILE),
        in_specs=[pl.BlockSpec((MOD_ROWS, D_MODEL), lambda l, j: (0, 0)),
                  pl.BlockSpec((None, D_MODEL, MOD_COL_TILE), lambda l, j: (l, 0, j)),
                  pl.BlockSpec((None, 1, MOD_COL_TILE), lambda l, j: (l, 0, j))],
        out_specs=pl.BlockSpec((None, MOD_ROWS, MOD_COL_TILE), lambda l, j: (l, 0, j)),
        out_shape=jax.ShapeDtypeStruct((DEPTH, MOD_ROWS, 6 * D_MODEL), F32),
        compiler_params=_params(2),
        name="adaln",
    )(cond_rows, w_mod, b_mod.reshape(DEPTH, 1, 6 * D_MODEL))


def _pad_head_lanes(x, lead):
    return jnp.pad(x, ((0, 0), (lead, HEAD_LANES - lead - x.shape[1])))


def _even_weights(w_in, w_out, w_uq, w_ukv, w_glu):
    o3 = S5_WIDTH + MLA_Q_LORA + MLA_KV_LORA
    w_kr = w_in[:, o3:]
    win_aug = jnp.concatenate(
        [w_in[:, :o3], _pad_head_lanes(w_kr, MLA_NOPE), _pad_head_lanes(_swap_pairs(w_kr), MLA_NOPE)], axis=1)
    dq = MLA_NOPE + MLA_ROPE
    plain, swapped = [], []
    for hd in range(MLA_HEADS):
        wn = w_uq[:, hd * dq:hd * dq + MLA_NOPE]
        wr = w_uq[:, hd * dq + MLA_NOPE:(hd + 1) * dq]
        plain.append(jnp.pad(jnp.concatenate([wn, wr], axis=1), ((0, 0), (0, HEAD_LANES - dq))))
        swapped.append(_pad_head_lanes(_swap_pairs(wr), MLA_NOPE))
    wuq2 = jnp.concatenate(plain + swapped, axis=1)
    w_mla = w_out[S5_WIDTH:].reshape(MLA_HEADS, MLA_V, D_MODEL)
    w_out_mla = jnp.pad(w_mla, ((0, 0), (0, HEAD_LANES - MLA_V), (0, 0))).reshape(MLA_HEADS * HEAD_LANES, D_MODEL)
    w_kv = w_ukv.reshape(MLA_KV_LORA, MLA_HEADS, MLA_NOPE + MLA_V)
    wk = jnp.pad(w_kv[:, :, :MLA_NOPE], ((0, 0), (0, 0), (0, HEAD_LANES - MLA_NOPE)))
    wv = jnp.pad(w_kv[:, :, MLA_NOPE:], ((0, 0), (0, 0), (0, HEAD_LANES - MLA_V)))
    w_kv = (wk.reshape(MLA_KV_LORA, -1).astype(BF16), wv.reshape(MLA_KV_LORA, -1).astype(BF16))
    return (win_aug.astype(BF16), wuq2.astype(BF16), w_kv, w_glu.astype(BF16),
            w_out[:S5_WIDTH].astype(BF16), w_out_mla.astype(BF16))


def _even_layer(x_all, grp, g1, mod_tab, ew, s5m, g_q, g_kv, d_skip, b_glu, h0, ctx, tables):
    win_aug, wuq2, w_kv, w_glu, w_out_s5, w_out_mla = ew
    bmat, cmat, acoef = s5m
    bsz, length = grp["bsz"], grp["length"]
    u_tm, q, ckv, kr = _even_in(x_all, grp, g1, mod_tab, win_aug, g_q, wuq2, g_kv, tables)
    y_dir, fin = _s5_scan(u_tm.reshape(length * bsz, S5_WIDTH), bmat, cmat, acoef, h0, bsz)
    segs = [(*_kv_expand(ckv, kr, *w_kv), length)]
    if ctx is not None:
        segs.append((*_kv_expand(*ctx, *w_kv), PAST_LEN))
    o_mla = _mla_attn(q, segs, bsz, length)
    x_all = _even_out(x_all, grp, mod_tab, u_tm, y_dir, o_mla, d_skip, w_glu, b_glu, w_out_s5, w_out_mla)
    return x_all, fin, ckv, kr


def _odd_layer(x_all, grp, g1, mod_tab, w_aug, w_out, lam_full, g_sub, post_scale, ctx, tables, kv_dtype):
    bsz, length = grp["bsz"], grp["length"]
    q, k, v = _odd_in(x_all, grp, g1, mod_tab, w_aug, tables, kv_dtype)
    segs = [(k, v, length)]
    if ctx is not None:
        segs.append((ctx[0], ctx[1], PAST_LEN))
    o = _diff_attn(lam_full, q, segs, g_sub, post_scale, bsz, length)
    return _odd_out(x_all, grp, mod_tab, o, w_out), k, v


def kernel(x_prompt, x_sample, state_s5, cache_mla, cache_diff_k, cache_diff_v, c, c_ctx, w_mod, b_mod, g_norm1, g_norm2, g_final, w_in_even, w_out_even, s5_lam_re, s5_lam_im, s5_log_dt, s5_b_re, s5_b_im, s5_c_re, s5_c_im, s5_d, s5_w_glu, s5_b_glu, mla_g_q, mla_w_uq, mla_g_kv, mla_w_ukv, w_in_odd, w_out_odd, diff_lam, diff_g_sub, w_router, b_router, w_gate_up, b_gate_up, w_down, b_down):
    tab_mla = _rope_tables(DEC_SEQ, MLA_ROPE, MLA_NOPE, 1)
    tab_diff = _rope_tables(DEC_SEQ, DIFF_HD, 0, 2)
    grp_p = _group(BATCH, SEQ, 0, 0, 0)
    grp_s = _group(DEC_BATCH, DEC_SEQ, N_PROMPT, 1, 1)
    x_all = jnp.concatenate([x_prompt.reshape(N_PROMPT, D_MODEL), x_sample.reshape(N_SAMPLE, D_MODEL)], axis=0)
    cond = jnp.concatenate([c_ctx[None], c, jnp.zeros((MOD_ROWS - 1 - DEC_BATCH, D_MODEL), c.dtype)], axis=0)
    mods = _adaln(cond.astype(F32), w_mod, b_mod)
    new_s5, new_mla, new_k, new_v = [], [], [], []
    for l in range(DEPTH):
        mod_tab = mods[l].reshape(MOD_ROWS, 6, D_MODEL)
        i = l // 2
        if l % 2 == 0:
            ew = _even_weights(w_in_even[i], w_out_even[i], mla_w_uq[i], mla_w_ukv[i], s5_w_glu[i])
            s5m = _s5_discretize(s5_lam_re[i], s5_lam_im[i], s5_log_dt[i], s5_b_re[i], s5_b_im[i],
                                 s5_c_re[i], s5_c_im[i])
            common = (ew, s5m, mla_g_q[i], mla_g_kv[i], s5_d[i], s5_b_glu[i])
            h0_p = jnp.zeros((2, 2, BATCH, 2 * S5_HALF_STATES), F32)
            x_all, fin, ckv, kr = _even_layer(x_all, grp_p, g_norm1[l], mod_tab, *common, h0_p, None, None)
            new_s5.append(_s5_state_from_kernel(fin))
            new_mla.append(jnp.concatenate([ckv, kr[:, MLA_NOPE:MLA_NOPE + MLA_ROPE]], axis=1)
                           .reshape(BATCH, SEQ, MLA_KV_LORA + MLA_ROPE))
            lat_ctx = cache_mla[:, i].astype(F32).reshape(DEC_BATCH * PAST_LEN, MLA_KV_LORA + MLA_ROPE)
            ctx = (lat_ctx[:, :MLA_KV_LORA], _pad_head_lanes(lat_ctx[:, MLA_KV_LORA:], MLA_NOPE))
            x_all, _, _, _ = _even_layer(x_all, grp_s, g_norm1[l], mod_tab, *common,
                                         _s5_state_to_kernel(state_s5[:, i]), ctx, tab_mla)
        else:
            lam_init = 0.8 - 0.6 * math.exp(-0.3 * l)
            lamf = diff_lam[i].astype(F32)
            lam_full = jnp.exp(jnp.sum(lamf[0] * lamf[1])) - jnp.exp(jnp.sum(lamf[2] * lamf[3])) + lam_init
            w_qk = w_in_odd[i][:, :2 * DIFF_WIDTH]
            w_aug = jnp.concatenate([w_in_odd[i], _swap_pairs(w_qk)], axis=1).astype(BF16)
            w_out = w_out_odd[i].astype(BF16)
            odd = (w_aug, w_out, lam_full, diff_g_sub[i], 1.0 - lam_init)
            x_all, kp, vp = _odd_layer(x_all, grp_p, g_norm1[l], mod_tab, *odd, None, None, F32)
            new_k.append(kp.reshape(BATCH, SEQ, DIFF_HEADS, 2, DIFF_HD))
            new_v.append(vp.reshape(BATCH, SEQ, DIFF_HEADS, 2 * DIFF_HD))
            ctx = (cache_diff_k[:, i].reshape(DEC_BATCH * PAST_LEN, DIFF_WIDTH).astype(BF16),
                   cache_diff_v[:, i].reshape(DEC_BATCH * PAST_LEN, DIFF_WIDTH).astype(BF16))
            x_all, _, _ = _odd_layer(x_all, grp_s, g_norm1[l], mod_tab, *odd, ctx, tab_diff, BF16)
        x_all = _moe_layer(l, x_all, mod_tab, g_norm2, w_router, b_router,
                           w_gate_up, b_gate_up, w_down, b_down)
    y_prompt = _final_norm(x_all, g_final, 0, N_PROMPT)
    y_sample = _final_norm(x_all, g_final, N_PROMPT, N_SAMPLE)
    return (y_prompt.reshape(BATCH, SEQ, D_MODEL), y_sample.reshape(DEC_BATCH, DEC_SEQ, D_MODEL),
            jnp.stack(new_s5, axis=1), jnp.stack(new_mla, axis=1),
            jnp.stack(new_k, axis=1), jnp.stack(new_v, axis=1))
```
